```python
import math
import jax, jax.numpy as jnp
from jax import lax
import numpy as np

D_MODEL = 2048
BATCH = 2
SEQ = 4096
DEPTH = 1

GRID_W = 64
CTX_LEN = 256
EPS = 1e-6
RWKV_HEADS = 16
RWKV_HEAD = 64
RWKV_DIM = RWKV_HEADS * RWKV_HEAD
DECAY_LORA = 96
AAA_LORA = 96
GATE_LORA = 256
N_DIR = 2
DECAY_SCALE = math.exp(-0.5)
LNX_EPS = 64e-5
MLA_HEADS = 16
Q_LORA = 512
KV_LORA = 512
QK_NOPE = 128
QK_ROPE = 64
V_HEAD = 128
ROPE_THETA = 10000.0
ATTN_SCALE = (QK_NOPE + QK_ROPE) ** -0.5
Q_BLOCK = 128
D_FF = 5632
CONV_W = 3
N_BRANCH = 2
RWKV_IN = 3 * RWKV_DIM + N_DIR * DECAY_LORA + N_DIR * AAA_LORA + GATE_LORA
MLA_IN = Q_LORA + KV_LORA + QK_ROPE
GATE_IN = N_BRANCH * D_MODEL
D_IN = RWKV_IN + MLA_IN + GATE_IN
RWKV_SPLITS = [RWKV_DIM, 2 * RWKV_DIM, 3 * RWKV_DIM,
               3 * RWKV_DIM + N_DIR * DECAY_LORA,
               3 * RWKV_DIM + N_DIR * (DECAY_LORA + AAA_LORA)]

kernel_name = "hybrid_rwkv7_mla_convffn_dit_block"


def rms_norm(x, g, eps=EPS):
    xf = x.astype(jnp.float32)
    y = xf * lax.rsqrt(jnp.mean(xf * xf, axis=-1, keepdims=True) + eps)
    return (y * g.astype(jnp.float32)).astype(x.dtype)


def modulate(h, shift, scale):
    return h * (1.0 + scale) + shift


def centred_shift(x):
    xp = jnp.pad(x, ((0, 0), (1, 1), (0, 0)))
    return 0.5 * (xp[:, :-2] + xp[:, 2:])


def depthwise_conv(x, w, b):
    T = x.shape[1]
    pad = CONV_W // 2
    xp = jnp.pad(x, ((0, 0), (pad, pad), (0, 0)))
    y = b
    for j in range(CONV_W):
        y = y + xp[:, j:j + T] * w[j]
    return y


def axial_angles(n_tok):
    rows = n_tok // GRID_W
    row = jnp.repeat(jnp.arange(rows), GRID_W).astype(jnp.float32)
    col = jnp.tile(jnp.arange(GRID_W), rows).astype(jnp.float32)
    half = QK_ROPE // 2
    freqs = ROPE_THETA ** (-jnp.arange(0, half, 2, dtype=jnp.float32) / half)
    return row[:, None] * freqs, col[:, None] * freqs


def rotate(x, ang):
    x1, x2 = jnp.split(x, 2, axis=-1)
    cs, sn = jnp.cos(ang).astype(x.dtype), jnp.sin(ang).astype(x.dtype)
    return jnp.concatenate([x1 * cs - x2 * sn, x1 * sn + x2 * cs], axis=-1)


def rope_2d(x, ang_row, ang_col):
    xr, xc = jnp.split(x, 2, axis=-1)
    return jnp.concatenate([rotate(xr, ang_row), rotate(xc, ang_col)], axis=-1)


def wkv7_scan(r, w, k, v, kk, a, s0, reverse, emit):
    def step(S, inp):
        r_t, w_t, k_t, v_t, kk_t, a_t = inp
        sa = jnp.einsum('bhvk,bhk->bhv', S, -kk_t)
        S = (S * w_t[:, :, None, :] + sa[..., None] * (kk_t * a_t)[:, :, None, :]
             + v_t[..., None] * k_t[:, :, None, :])
        y = jnp.einsum('bhvk,bhk->bhv', S, r_t) if emit else None
        return S, y
    xs = tuple(jnp.moveaxis(t, 1, 0) for t in (r, w, k, v, kk, a))
    s_fin, ys = lax.scan(step, s0, xs, reverse=reverse)
    return (jnp.moveaxis(ys, 0, 1) if emit else None), s_fin


def rwkv_branch(p, s0, P, emit):
    B, T, _ = p.shape
    f32 = jnp.float32
    heads = lambda t: t.reshape(B, T, RWKV_HEADS, RWKV_HEAD).astype(f32)
    p = p + P['rwkv_mu'] * (centred_shift(p) - p)
    r, k, v, wl, al, gl = jnp.split(p, RWKV_SPLITS, axis=-1)
    wl = wl.reshape(B, T, N_DIR, DECAY_LORA)
    al = al.reshape(B, T, N_DIR, AAA_LORA)
    w_raw = P['rwkv_w0'] + jnp.einsum('btdr,drc->btdc', jnp.tanh(wl), P['rwkv_w2'])
    decay = jnp.exp(-DECAY_SCALE * jax.nn.sigmoid(w_raw.astype(f32)))
    a = jax.nn.sigmoid((P['rwkv_a0'] + jnp.einsum('btdr,drc->btdc', al, P['rwkv_a2'])).astype(f32))
    kk = heads(k * P['rwkv_k_k'])
    kk = kk / jnp.maximum(jnp.sqrt(jnp.sum(kk * kk, axis=-1, keepdims=True)), 1e-12)
    k_dir = k[:, :, None, :].astype(f32) * (1.0 + (a - 1.0) * P['rwkv_k_a'].astype(f32))
    r_h, v_h = heads(r), heads(v)
    ys, states = [], []
    for d in range(N_DIR):
        y_d, s_d = wkv7_scan(r_h, heads(decay[:, :, d]), heads(k_dir[:, :, d]), v_h, kk,
                             heads(a[:, :, d]), s0[d], reverse=(d == 1), emit=emit)
        ys.append(y_d)
        states.append(s_d)
    s_fin = jnp.stack(states)
    if not emit:
        return None, s_fin
    y = ys[0] + ys[1]
    mu = jnp.mean(y, axis=-1, keepdims=True)
    var = jnp.mean(jnp.square(y - mu), axis=-1, keepdims=True)
    yn = ((y - mu) * lax.rsqrt(var + LNX_EPS)).reshape(B, T, RWKV_DIM)
    yn = yn * P['rwkv_lnx_w'] + P['rwkv_lnx_b']
    k_bar = heads(0.5 * (k_dir[:, :, 0] + k_dir[:, :, 1]))
    bonus = (jnp.sum(r_h * k_bar * P['rwkv_r_k'].astype(f32), axis=-1, keepdims=True) * v_h).reshape(B, T, RWKV_DIM)
    g = jax.nn.sigmoid(gl) @ P['rwkv_g2']
    return ((yn + bonus).astype(p.dtype) * g), s_fin


def mla_kv(p_mla, P, ang):
    B, T, _ = p_mla.shape
    ckv = rms_norm(p_mla[..., Q_LORA:Q_LORA + KV_LORA], P['mla_kv_norm'])
    kpe = p_mla[..., Q_LORA + KV_LORA:]
    kv = (ckv @ P['mla_w_ukv']).reshape(B, T, MLA_HEADS, QK_NOPE + V_HEAD)
    k_nope, v = kv[..., :QK_NOPE], kv[..., QK_NOPE:]
    if ang is not None:
        kpe = rope_2d(kpe, ang[0], ang[1])
    k = jnp.concatenate([k_nope, jnp.broadcast_to(kpe[:, :, None, :], (B, T, MLA_HEADS, QK_ROPE))], axis=-1)
    return k, v


def mla_q(p_mla, P, ang):
    B, T, _ = p_mla.shape
    cq = rms_norm(p_mla[..., :Q_LORA], P['mla_q_norm'])
    q = (cq @ P['mla_w_uq']).reshape(B, T, MLA_HEADS, QK_NOPE + QK_ROPE)
    if ang is not None:
        q = jnp.concatenate([q[..., :QK_NOPE], rope_2d(q[..., QK_NOPE:], ang[0][:, None], ang[1][:, None])], axis=-1)
    return q


def softmax_attend(q, k, v):
    s = jnp.einsum('bqhd,bkhd->bhqk', q, k).astype(jnp.float32) * ATTN_SCALE
    pr = jax.nn.softmax(s, axis=-1).astype(v.dtype)
    return jnp.einsum('bhqk,bkhd->bqhd', pr, v)


def blocked_attention(q, k, v):
    B, T, H, Dk = q.shape
    nb = T // Q_BLOCK
    qb = jnp.moveaxis(q.reshape(B, nb, Q_BLOCK, H, Dk), 1, 0)
    ob = lax.map(lambda qi: softmax_attend(qi, k, v), qb)
    return jnp.moveaxis(ob, 0, 1).reshape(B, T, H, v.shape[-1])


def token_mixer(h, s0, ang, kv_prefix, P, emit):
    B, T, _ = h.shape
    p = h @ P['w_in']
    p_rwkv, p_mla, p_gate = jnp.split(p, [RWKV_IN, RWKV_IN + MLA_IN], axis=-1)
    y_rwkv, s_fin = rwkv_branch(p_rwkv, s0, P, emit)
    k, v = mla_kv(p_mla, P, ang)
    if not emit:
        return None, s_fin, (k, v)
    q = mla_q(p_mla, P, ang)
    if kv_prefix is not None:
        k_all = jnp.concatenate([kv_prefix[0], k], axis=1)
        v_all = jnp.concatenate([kv_prefix[1], v], axis=1)
    else:
        k_all, v_all = k, v
    o = blocked_attention(q, k_all, v_all).reshape(B, T, MLA_HEADS * V_HEAD)
    g_rwkv, g_mla = jnp.split(jax.nn.sigmoid(p_gate + P['gate_b']), N_BRANCH, axis=-1)
    merged = g_rwkv * (y_rwkv @ P['w_rwkv_proj']) + g_mla * (o @ P['w_mla_proj'])
    return merged @ P['w_out'], s_fin, (k, v)


def conv_ffn(h, P):
    u = depthwise_conv(h @ P['ffn_w_gate'], P['ffn_conv_w'], P['ffn_conv_b'])
    return (jax.nn.gelu(u, approximate=True) * (h @ P['ffn_w_val'])) @ P['ffn_w_down']


def setup_inputs(seed: int = 0) -> dict:
    key = jax.random.key(seed)
    ks = jax.random.split(key, 40)
    L, D, C, H = DEPTH, D_MODEL, RWKV_DIM, MLA_HEADS
    nrm = lambda k, shape, s: jax.random.normal(k, shape, jnp.float32) * s
    gain = lambda k, shape: 1.0 + 0.02 * jax.random.normal(k, shape, jnp.float32)
    return {
        'x': nrm(ks[0], (BATCH, SEQ, D), 1.0),
        'c': nrm(ks[1], (BATCH, D), 1.0),
        'ctx': nrm(ks[2], (BATCH, CTX_LEN, D), 1.0),
        'c_ctx': nrm(ks[3], (D,), 1.0),
        'w_ada': nrm(ks[4], (L, D, 6 * D), 0.5 * D ** -0.5),
        'b_ada': nrm(ks[5], (L, 6 * D), 0.02),
        'norm_mix_pre': gain(ks[6], (L, D)),
        'norm_mix_post': gain(ks[7], (L, D)),
        'norm_ffn_pre': gain(ks[8], (L, D)),
        'norm_ffn_post': gain(ks[9], (L, D)),
        'w_in': nrm(ks[10], (L, D, D_IN), D ** -0.5),
        'rwkv_mu': jax.random.uniform(ks[11], (L, RWKV_IN), jnp.float32),
        'rwkv_w0': nrm(ks[12], (L, N_DIR, C), 0.5),
        'rwkv_w2': nrm(ks[13], (L, N_DIR, DECAY_LORA, C), 0.5 * DECAY_LORA ** -0.5),
        'rwkv_a0': nrm(ks[14], (L, N_DIR, C), 0.5),
        'rwkv_a2': nrm(ks[15], (L, N_DIR, AAA_LORA, C), 0.5 * AAA_LORA ** -0.5),
        'rwkv_k_k': 0.85 + 0.05 * jax.random.normal(ks[16], (L, C), jnp.float32),
        'rwkv_k_a': gain(ks[17], (L, C)),
        'rwkv_r_k': nrm(ks[18], (L, RWKV_HEADS, RWKV_HEAD), 0.1),
        'rwkv_lnx_w': gain(ks[19], (L, C)),
        'rwkv_lnx_b': nrm(ks[20], (L, C), 0.02),
        'rwkv_g2': nrm(ks[21], (L, GATE_LORA, C), GATE_LORA ** -0.5),
        'w_rwkv_proj': nrm(ks[22], (L, C, D), C ** -0.5),
        'mla_q_norm': gain(ks[23], (L, Q_LORA)),
        'mla_kv_norm': gain(ks[24], (L, KV_LORA)),
        'mla_w_uq': nrm(ks[25], (L, Q_LORA, H * (QK_NOPE + QK_ROPE)), Q_LORA ** -0.5),
        'mla_w_ukv': nrm(ks[26], (L, KV_LORA, H * (QK_NOPE + V_HEAD)), KV_LORA ** -0.5),
        'w_mla_proj': nrm(ks[27], (L, H * V_HEAD, D), (H * V_HEAD) ** -0.5),
        'gate_b': nrm(ks[28], (L, GATE_IN), 0.02),
        'w_out': nrm(ks[29], (L, D, D), D ** -0.5),
        'ffn_w_gate': nrm(ks[30], (L, D, D_FF), D ** -0.5),
        'ffn_w_val': nrm(ks[31], (L, D, D_FF), D ** -0.5),
        'ffn_conv_w': nrm(ks[32], (L, CONV_W, D_FF), CONV_W ** -0.5),
        'ffn_conv_b': nrm(ks[33], (L, D_FF), 0.02),
        'ffn_w_down': nrm(ks[34], (L, D_FF, D), D_FF ** -0.5),
    }


def reference(x, c, ctx, c_ctx, w_ada, b_ada, norm_mix_pre, norm_mix_post, norm_ffn_pre, norm_ffn_post,
              w_in, rwkv_mu, rwkv_w0, rwkv_w2, rwkv_a0, rwkv_a2, rwkv_k_k, rwkv_k_a, rwkv_r_k,
              rwkv_lnx_w, rwkv_lnx_b, rwkv_g2, w_rwkv_proj, mla_q_norm, mla_kv_norm, mla_w_uq, mla_w_ukv,
              w_mla_proj, gate_b, w_out, ffn_w_gate, ffn_w_val, ffn_conv_w, ffn_conv_b, ffn_w_down):
    B = x.shape[0]
    ang = axial_angles(x.shape[1])
    xc = ctx
    for l in range(DEPTH):
        last = l == DEPTH - 1
        P = {
            'w_in': w_in[l], 'rwkv_mu': rwkv_mu[l], 'rwkv_w0': rwkv_w0[l], 'rwkv_w2': rwkv_w2[l],
            'rwkv_a0': rwkv_a0[l], 'rwkv_a2': rwkv_a2[l], 'rwkv_k_k': rwkv_k_k[l], 'rwkv_k_a': rwkv_k_a[l],
            'rwkv_r_k': rwkv_r_k[l], 'rwkv_lnx_w': rwkv_lnx_w[l], 'rwkv_lnx_b': rwkv_lnx_b[l],
            'rwkv_g2': rwkv_g2[l], 'w_rwkv_proj': w_rwkv_proj[l], 'mla_q_norm': mla_q_norm[l],
            'mla_kv_norm': mla_kv_norm[l], 'mla_w_uq': mla_w_uq[l], 'mla_w_ukv': mla_w_ukv[l],
            'w_mla_proj': w_mla_proj[l], 'gate_b': gate_b[l], 'w_out': w_out[l],
            'ffn_w_gate': ffn_w_gate[l], 'ffn_w_val': ffn_w_val[l], 'ffn_conv_w': ffn_conv_w[l],
            'ffn_conv_b': ffn_conv_b[l], 'ffn_w_down': ffn_w_down[l],
        }
        sh1, sc1, g1, sh2, sc2, g2 = [m[:, None, :] for m in jnp.split(jax.nn.silu(c) @ w_ada[l] + b_ada[l], 6, axis=-1)]
        ch1, cs1, cg1, ch2, cs2, cg2 = jnp.split(jax.nn.silu(c_ctx) @ w_ada[l] + b_ada[l], 6, axis=-1)
        h_ctx = modulate(rms_norm(xc, norm_mix_pre[l]), ch1, cs1)
        s0 = jnp.zeros((N_DIR, B, RWKV_HEADS, RWKV_HEAD, RWKV_HEAD), jnp.float32)
        out_ctx, s_ctx, kv_ctx = token_mixer(h_ctx, s0, None, None, P, emit=not last)
        h_lat = modulate(rms_norm(x, norm_mix_pre[l]), sh1, sc1)
        out_lat, _, _ = token_mixer(h_lat, s_ctx, ang, kv_ctx, P, emit=True)
        x = x + g1 * rms_norm(out_lat, norm_mix_post[l])
        h = modulate(rms_norm(x, norm_ffn_pre[l]), sh2, sc2)
        x = x + g2 * rms_norm(conv_ffn(h, P), norm_ffn_post[l])
        if not last:
            xc = xc + cg1 * rms_norm(out_ctx, norm_mix_post[l])
            hc = modulate(rms_norm(xc, norm_ffn_pre[l]), ch2, cs2)
            xc = xc + cg2 * rms_norm(conv_ffn(hc, P), norm_ffn_post[l])
    return x
```

```python
import functools
import math

import jax
import jax.numpy as jnp
from jax import lax
from jax.experimental import pallas as pl
from jax.experimental.pallas import tpu as pltpu

D_MODEL = 2048
BATCH = 2
SEQ = 4096
GRID_W = 64
CTX_LEN = 256
S_ALL = SEQ + CTX_LEN
EPS = 1e-6
RWKV_HEADS = 16
RWKV_HEAD = 64
RWKV_DIM = RWKV_HEADS * RWKV_HEAD
DECAY_LORA = 96
AAA_LORA = 96
GATE_LORA = 256
N_DIR = 2
DECAY_SCALE = math.exp(-0.5)
LNX_EPS = 64e-5
MLA_HEADS = 16
Q_LORA = 512
KV_LORA = 512
QK_NOPE = 128
QK_ROPE = 64
V_HEAD = 128
ROPE_THETA = 10000.0
ATTN_SCALE = (QK_NOPE + QK_ROPE) ** -0.5
D_FF = 5632
RWKV_IN = 3 * RWKV_DIM + N_DIR * DECAY_LORA + N_DIR * AAA_LORA + GATE_LORA
MLA_IN = Q_LORA + KV_LORA + QK_ROPE

LANES = 128
LORA_PAD = 128
VMEM_LIMIT = 56 * 1024 * 1024

C_R, C_K, C_V = 0, RWKV_DIM, 2 * RWKV_DIM
C_WL = 3 * RWKV_DIM
C_AL = C_WL + N_DIR * LORA_PAD
C_GL = C_AL + N_DIR * LORA_PAD
C_KPE = C_GL + GATE_LORA
C_PAD = C_KPE + 2 * QK_ROPE
C_CQ = C_PAD + LANES
C_CKV = C_CQ + Q_LORA
C_GATE = C_CKV + KV_LORA
P_COLS = C_GATE + 2 * D_MODEL
RWKV_COLS = C_KPE

WKV_CHUNK = 64
N_CHUNKS = S_ALL // WKV_CHUNK
CTX_CHUNKS = CTX_LEN // WKV_CHUNK
LAT_CHUNKS = SEQ // WKV_CHUNK

F32 = jnp.float32
BF16 = jnp.bfloat16


def _cparams(sem):
    return pltpu.CompilerParams(dimension_semantics=sem, vmem_limit_bytes=VMEM_LIMIT)


def _bdot(a, b):
    return jnp.dot(a.astype(BF16), b.astype(BF16), preferred_element_type=F32)


def _ada_kernel(s_ref, w_ref, b_ref, o_ref):
    w = w_ref[...]
    for r in range(s_ref.shape[0]):
        o_ref[r:r + 1, :] = jnp.sum(s_ref[r] * w, axis=0, keepdims=True) + b_ref[...]


def _ada(s, w, b, tn=1024):
    R, K = s.shape
    N = w.shape[1]
    return pl.pallas_call(
        _ada_kernel, grid=(N // tn,),
        in_specs=[pl.BlockSpec((R, K, 1), lambda j: (0, 0, 0)),
                  pl.BlockSpec((K, tn), lambda j: (0, j)),
                  pl.BlockSpec((1, tn), lambda j: (0, j))],
        out_specs=pl.BlockSpec((R, tn), lambda j: (0, j)),
        out_shape=jax.ShapeDtypeStruct((R, N), F32),
        compiler_params=_cparams(("parallel",)), name="ada",
    )(s[:, :, None], w, b[None, :])


def _norm_mod_kernel(x_ref, g_ref, m_ref, o_ref):
    x = x_ref[0]
    n = x * lax.rsqrt(jnp.mean(x * x, axis=-1, keepdims=True) + EPS) * g_ref[...]
    m = m_ref[0, 0]
    o_ref[0] = (n * (1.0 + m[1:2]) + m[0:1]).astype(o_ref.dtype)


def _norm_mod(xcat, gain, mods, tm=256):
    B, S, D = xcat.shape
    lat_tiles = SEQ // tm
    return pl.pallas_call(
        _norm_mod_kernel, grid=(B, S // tm),
        in_specs=[pl.BlockSpec((1, tm, D), lambda b, i: (b, i, 0)),
                  pl.BlockSpec((1, D), lambda b, i: (0, 0)),
                  pl.BlockSpec((1, 1, 2, D), lambda b, i: (b, i // lat_tiles, 0, 0))],
        out_specs=pl.BlockSpec((1, tm, D), lambda b, i: (b, i, 0)),
        out_shape=jax.ShapeDtypeStruct((B, S, D), BF16),
        compiler_params=_cparams(("parallel", "parallel")), name="norm_mod",
    )(xcat, gain[None, :], mods)


def _mm_kernel(a_ref, w_ref, o_ref):
    o_ref[...] = _bdot(a_ref[...], w_ref[...]).astype(o_ref.dtype)


def _matmul(a, w, tm, tn, out_dtype=F32, name="matmul"):
    M, K = a.shape
    N = w.shape[1]
    return pl.pallas_call(
        _mm_kernel, grid=(M // tm, N // tn),
        in_specs=[pl.BlockSpec((tm, K), lambda i, j: (i, 0)),
                  pl.BlockSpec((K, tn), lambda i, j: (0, j))],
        out_specs=pl.BlockSpec((tm, tn), lambda i, j: (i, j)),
        out_shape=jax.ShapeDtypeStruct((M, N), out_dtype),
        compiler_params=_cparams(("parallel", "parallel")), name=name,
    )(a, w)


def _wkv_kernel(r_ref, v_ref, kk_ref, lw_ref, a_ref, k_ref, y_ref, s_ref):
    d = pl.program_id(1)
    C = WKV_CHUNK

    @pl.when(pl.program_id(2) == 0)
    def _():
        s_ref[...] = jnp.zeros_like(s_ref)

    r = r_ref[0]
    v = v_ref[0]
    kk = kk_ref[0]
    lw = lw_ref[0, 0]
    a = a_ref[0, 0]
    k = k_ref[0, 0]
    H = r.shape[0]

    row = lax.broadcasted_iota(jnp.int32, (C, C), 0)
    col = lax.broadcasted_iota(jnp.int32, (C, C), 1)
    lag = (row - col) * (1 - 2 * d)
    incl = lag >= 0
    strict = lag > 0

    def blk(m):
        return (row // m) == (col // m)

    def bmm(x, y):
        return jnp.einsum('hab,hbc->hac', x.astype(BF16), y.astype(BF16), preferred_element_type=F32)

    def bmm_nt(x, y):
        return jnp.einsum('han,hbn->hab', x.astype(BF16), y.astype(BF16), preferred_element_type=F32)

    def bmm_tn(x, y):
        return jnp.einsum('hca,hcb->hab', x.astype(BF16), y.astype(BF16), preferred_element_type=F32)

    tri = jnp.broadcast_to(incl.astype(BF16)[None], (H, C, C))
    lw_hi = lw.astype(BF16)
    lw_lo = (lw - lw_hi.astype(F32)).astype(BF16)
    cum = (jnp.einsum('hts,hsn->htn', tri, lw_hi, preferred_element_type=F32)
           + jnp.einsum('hts,hsn->htn', tri, lw_lo, preferred_element_type=F32))
    tot = jnp.sum(lw, axis=1, keepdims=True)

    e_cum = jnp.exp(cum)
    e_neg = jnp.exp(-cum)
    b = kk * a
    at = -kk * jnp.exp(cum - lw)
    rt = r * e_cum
    bt = b * e_neg
    kt = k * e_neg
    e_rem = jnp.exp(tot - cum)
    bdec = b * e_rem
    kdec = k * e_rem

    a_ab = bmm_nt(at, bt)
    a_ak = bmm_nt(at, kt)
    a_rb = bmm_nt(rt, bt)
    a_rk = bmm_nt(rt, kt)

    zero = jnp.zeros((), F32)
    L = jnp.where(strict[None], a_ab, zero)
    Lb = jnp.where(blk(8)[None], L, zero)
    L2 = bmm(Lb, Lb)
    L4 = bmm(L2, L2)
    S1 = Lb + L2 + bmm(Lb, L2)
    Nm = S1 + L4 + bmm(L4, S1)
    for m in (8, 16, 32):
        Lo = jnp.where((blk(2 * m) & jnp.logical_not(blk(m)))[None], L, zero)
        P = Lo + bmm(Nm, Lo)
        Nm = Nm + P + bmm(P, Nm)

    s0 = s_ref[...]
    rhs = bmm_nt(at, s0) + bmm(jnp.where(strict[None], a_ak, zero), v)
    u = rhs + bmm(Nm, rhs)
    y = (bmm_nt(rt, s0) + bmm(jnp.where(incl[None], a_rb, zero), u)
         + bmm(jnp.where(incl[None], a_rk, zero), v))
    y_ref[0, 0] = y
    s_ref[...] = s0 * jnp.exp(tot) + bmm_tn(u, bdec) + bmm_tn(v, kdec)


def _wkv(r, v, kk, lw, a, k):
    B, H, S, N = r.shape
    C = WKV_CHUNK

    def cidx(d, i):
        return jnp.where(d == 0, (i + LAT_CHUNKS) % N_CHUNKS, N_CHUNKS - 1 - i)

    shared = pl.BlockSpec((1, H, C, N), lambda b, d, i: (b, 0, cidx(d, i), 0))
    perdir = pl.BlockSpec((1, 1, H, C, N), lambda b, d, i: (b, d, 0, cidx(d, i), 0))
    return pl.pallas_call(
        _wkv_kernel, grid=(B, N_DIR, N_CHUNKS),
        in_specs=[shared, shared, shared, perdir, perdir, perdir],
        out_specs=perdir,
        out_shape=jax.ShapeDtypeStruct((B, N_DIR, H, S, N), F32),
        scratch_shapes=[pltpu.VMEM((H, N, N), F32)],
        compiler_params=_cparams(("parallel", "parallel", "arbitrary")), name="wkv7",
    )(r, v, kk, lw, a, k)


def _rms(x, g):
    return x * lax.rsqrt(jnp.mean(x * x, axis=-1, keepdims=True) + EPS) * g


def _qproj_kernel(p_ref, g_ref, w_ref, cos_ref, sin_ref, q_ref):
    cq = _rms(p_ref[0], g_ref[...]).astype(BF16)
    cs, sn = cos_ref[...], sin_ref[...]
    for h in range(MLA_HEADS):
        y = jnp.dot(cq, w_ref[h], preferred_element_type=F32)
        rope = y[:, QK_NOPE:QK_NOPE + QK_ROPE] * cs + y[:, QK_NOPE + QK_ROPE:] * sn
        q = jnp.concatenate([y[:, :QK_NOPE], rope], axis=-1) * ATTN_SCALE
        q_ref[0, h] = q.astype(q_ref.dtype)


def _qproj(p, gain, wq, cos, sin, tm=512):
    B = p.shape[0]
    return pl.pallas_call(
        _qproj_kernel, grid=(B, SEQ // tm),
        in_specs=[pl.BlockSpec((1, tm, Q_LORA), lambda b, i: (b, i, C_CQ // Q_LORA)),
                  pl.BlockSpec((1, Q_LORA), lambda b, i: (0, 0)),
                  pl.BlockSpec((MLA_HEADS, Q_LORA, 2 * LANES), lambda b, i: (0, 0, 0)),
                  pl.BlockSpec((tm, QK_ROPE), lambda b, i: (i, 0)),
                  pl.BlockSpec((tm, QK_ROPE), lambda b, i: (i, 0))],
        out_specs=pl.BlockSpec((1, MLA_HEADS, tm, QK_NOPE + QK_ROPE), lambda b, i: (b, 0, i, 0)),
        out_shape=jax.ShapeDtypeStruct((B, MLA_HEADS, SEQ, QK_NOPE + QK_ROPE), BF16),
        compiler_params=_cparams(("parallel", "parallel")), name="mla_q",
    )(p, gain[None, :], wq, cos, sin)


def _kvproj_kernel(p_ref, pe_ref, g_ref, w_ref, cos_ref, sin_ref, k_ref, v_ref):
    ckv = _rms(p_ref[0], g_ref[...]).astype(BF16)
    pe = pe_ref[0]
    kpe = pe[:, :QK_ROPE] * cos_ref[...] + pe[:, QK_ROPE:] * sin_ref[...]
    for h in range(MLA_HEADS):
        y = jnp.dot(ckv, w_ref[h], preferred_element_type=F32)
        k_ref[0, h] = jnp.concatenate([y[:, :QK_NOPE], kpe], axis=-1).astype(k_ref.dtype)
        v_ref[0, h] = y[:, QK_NOPE:].astype(v_ref.dtype)


def _kvproj(p, gain, wkv, cos, sin, tm=256):
    B = p.shape[0]
    return pl.pallas_call(
        _kvproj_kernel, grid=(B, S_ALL // tm),
        in_specs=[pl.BlockSpec((1, tm, KV_LORA), lambda b, i: (b, i, C_CKV // KV_LORA)),
                  pl.BlockSpec((1, tm, 2 * QK_ROPE), lambda b, i: (b, i, C_KPE // (2 * QK_ROPE))),
                  pl.BlockSpec((1, KV_LORA), lambda b, i: (0, 0)),
                  pl.BlockSpec((MLA_HEADS, KV_LORA, 2 * LANES), lambda b, i: (0, 0, 0)),
                  pl.BlockSpec((tm, QK_ROPE), lambda b, i: (i, 0)),
                  pl.BlockSpec((tm, QK_ROPE), lambda b, i: (i, 0))],
        out_specs=[pl.BlockSpec((1, MLA_HEADS, tm, QK_NOPE + QK_ROPE), lambda b, i: (b, 0, i, 0)),
                   pl.BlockSpec((1, MLA_HEADS, tm, V_HEAD), lambda b, i: (b, 0, i, 0))],
        out_shape=[jax.ShapeDtypeStruct((B, MLA_HEADS, S_ALL, QK_NOPE + QK_ROPE), BF16),
                   jax.ShapeDtypeStruct((B, MLA_HEADS, S_ALL, V_HEAD), BF16)],
        compiler_params=_cparams(("parallel", "parallel")), name="mla_kv",
    )(p, p, gain[None, :], wkv, cos, sin)


def _attn_kernel(q_ref, k_ref, v_ref, o_ref, *, tk):
    q = q_ref[0, 0]
    tq = q.shape[0]
    n_kv = k_ref.shape[2] // tk

    def body(j, carry):
        m, l, acc = carry
        off = pl.multiple_of(j * tk, tk)
        kj = k_ref[0, 0, pl.ds(off, tk), :]
        vj = v_ref[0, 0, pl.ds(off, tk), :]
        s = lax.dot_general(q, kj, (((1,), (1,)), ((), ())), preferred_element_type=F32)
        m_new = jnp.maximum(m, jnp.max(s, axis=-1, keepdims=True))
        alpha = jnp.exp(m - m_new)
        p = jnp.exp(s - m_new)
        l = alpha * l + jnp.sum(p, axis=-1, keepdims=True)
        acc = alpha * acc + jnp.dot(p.astype(BF16), vj, preferred_element_type=F32)
        return m_new, l, acc

    m0 = jnp.full((tq, 1), -jnp.inf, F32)
    l0 = jnp.zeros((tq, 1), F32)
    a0 = jnp.zeros((tq, V_HEAD), F32)
    _, l, acc = lax.fori_loop(0, n_kv, body, (m0, l0, a0))
    o_ref[0] = (acc / l).astype(o_ref.dtype)


def _attention(q, k, v, tq=512, tk=256):
    B, H, T, Dk = q.shape
    S = k.shape[2]
    return pl.pallas_call(
        functools.partial(_attn_kernel, tk=tk), grid=(B, H, T // tq),
        in_specs=[pl.BlockSpec((1, 1, tq, Dk), lambda b, h, i: (b, h, i, 0)),
                  pl.BlockSpec((1, 1, S, Dk), lambda b, h, i: (b, h, 0, 0)),
                  pl.BlockSpec((1, 1, S, V_HEAD), lambda b, h, i: (b, h, 0, 0))],
        out_specs=pl.BlockSpec((1, tq, V_HEAD), lambda b, h, i: (b, i, h)),
        out_shape=jax.ShapeDtypeStruct((B, T, H * V_HEAD), BF16),
        compiler_params=_cparams(("parallel", "parallel", "parallel")), name="mla_attn",
    )(q, k, v)


def _merge_kernel(y_ref, o_ref, wr_ref, wm_ref, pr_ref, pm_ref, br_ref, bm_ref, out_ref):
    gr = jax.nn.sigmoid(pr_ref[0] + br_ref[...])
    gm = jax.nn.sigmoid(pm_ref[0] + bm_ref[...])
    out_ref[0] = (gr * jnp.dot(y_ref[0], wr_ref[...], preferred_element_type=F32)
                  + gm * jnp.dot(o_ref[0], wm_ref[...], preferred_element_type=F32)).astype(out_ref.dtype)


def _merge(y, o, w_rp, w_mp, p, gate_b, tm=512, tn=512):
    B = y.shape[0]
    D = D_MODEL
    gcol = C_GATE // tn
    return pl.pallas_call(
        _merge_kernel, grid=(B, SEQ // tm, D // tn),
        in_specs=[pl.BlockSpec((1, tm, RWKV_DIM), lambda b, i, j: (b, i, 0)),
                  pl.BlockSpec((1, tm, D), lambda b, i, j: (b, i, 0)),
                  pl.BlockSpec((RWKV_DIM, tn), lambda b, i, j: (0, j)),
                  pl.BlockSpec((D, tn), lambda b, i, j: (0, j)),
                  pl.BlockSpec((1, tm, tn), lambda b, i, j: (b, i, gcol + j)),
                  pl.BlockSpec((1, tm, tn), lambda b, i, j: (b, i, gcol + D // tn + j)),
                  pl.BlockSpec((1, tn), lambda b, i, j: (0, j)),
                  pl.BlockSpec((1, tn), lambda b, i, j: (0, D // tn + j))],
        out_specs=pl.BlockSpec((1, tm, tn), lambda b, i, j: (b, i, j)),
        out_shape=jax.ShapeDtypeStruct((B, SEQ, D), BF16),
        compiler_params=_cparams(("parallel", "parallel", "parallel")), name="merge",
    )(y, o, w_rp, w_mp, p, p, gate_b[None, :], gate_b[None, :])


def _outproj_kernel(m_ref, w_ref, x_ref, gpost_ref, gate_ref, gpre_ref, mod_ref, x1_ref, h_ref):
    out = jnp.dot(m_ref[0], w_ref[...], preferred_element_type=F32)
    x1 = x_ref[0] + gate_ref[0] * _rms(out, gpost_ref[...])
    x1_ref[0] = x1
    mod = mod_ref[0]
    h_ref[0] = (_rms(x1, gpre_ref[...]) * (1.0 + mod[1:2]) + mod[0:1]).astype(h_ref.dtype)


def _outproj(merged, w_out, x, g_post, gate, g_pre, mod2, tm=256):
    B = x.shape[0]
    D = D_MODEL
    return pl.pallas_call(
        _outproj_kernel, grid=(B, SEQ // tm),
        in_specs=[pl.BlockSpec((1, tm, D), lambda b, i: (b, i, 0)),
                  pl.BlockSpec((D, D), lambda b, i: (0, 0)),
                  pl.BlockSpec((1, tm, D), lambda b, i: (b, i, 0)),
                  pl.BlockSpec((1, D), lambda b, i: (0, 0)),
                  pl.BlockSpec((1, 1, D), lambda b, i: (b, 0, 0)),
                  pl.BlockSpec((1, D), lambda b, i: (0, 0)),
                  pl.BlockSpec((1, 2, D), lambda b, i: (b, 0, 0))],
        out_specs=[pl.BlockSpec((1, tm, D), lambda b, i: (b, i, 0)),
                   pl.BlockSpec((1, tm, D), lambda b, i: (b, i, 0))],
        out_shape=[jax.ShapeDtypeStruct((B, SEQ, D), F32),
                   jax.ShapeDtypeStruct((B, SEQ, D), BF16)],
        compiler_params=_cparams(("parallel", "parallel")), name="outproj",
    )(merged, w_out, x, g_post[None, :], gate[:, None, :], g_pre[None, :], mod2)


HALO = 16


def _ffn_kernel(h_ref, hp_ref, hn_ref, wg_ref, wv_ref, wd_ref, cw_ref, cb_ref, x_ref, gate_ref, gpost_ref,
                o_ref, acc_ref):
    i = pl.program_id(1)
    f = pl.program_id(2)
    tm = h_ref.shape[1]

    @pl.when(f == 0)
    def _():
        acc_ref[...] = jnp.zeros_like(acc_ref)

    h = h_ref[0]
    wg = wg_ref[...]
    g = jnp.dot(h, wg, preferred_element_type=F32)
    g_prev = jnp.dot(hp_ref[0], wg, preferred_element_type=F32)[HALO - 1:HALO]
    g_next = jnp.dot(hn_ref[0], wg, preferred_element_type=F32)[0:1]
    g_prev = jnp.where(i == 0, 0.0, g_prev)
    g_next = jnp.where(i == pl.num_programs(1) - 1, 0.0, g_next)
    rows = lax.broadcasted_iota(jnp.int32, g.shape, 0)
    up = jnp.where(rows == 0, g_prev, pltpu.roll(g, 1, 0))
    dn = jnp.where(rows == tm - 1, g_next, pltpu.roll(g, tm - 1, 0))
    cw = cw_ref[...]
    u = cb_ref[...] + up * cw[0:1] + g * cw[1:2] + dn * cw[2:3]
    val = jnp.dot(h, wv_ref[...], preferred_element_type=F32)
    act = jax.nn.gelu(u, approximate=True) * val
    acc_ref[...] += jnp.dot(act.astype(BF16), wd_ref[...], preferred_element_type=F32)

    @pl.when(f == pl.num_programs(2) - 1)
    def _():
        o_ref[0] = x_ref[0] + gate_ref[0] * _rms(acc_ref[...], gpost_ref[...])


def _ffn(h, wg, wv, wd, cw, cb, x1, gate, g_post, tm=512, tf=512):
    B = h.shape[0]
    D = D_MODEL
    hb = tm // HALO
    last = SEQ // HALO - 1
    return pl.pallas_call(
        _ffn_kernel, grid=(B, SEQ // tm, D_FF // tf),
        in_specs=[pl.BlockSpec((1, tm, D), lambda b, i, f: (b, i, 0)),
                  pl.BlockSpec((1, HALO, D), lambda b, i, f: (b, jnp.maximum(i * hb - 1, 0), 0)),
                  pl.BlockSpec((1, HALO, D), lambda b, i, f: (b, jnp.minimum((i + 1) * hb, last), 0)),
                  pl.BlockSpec((D, tf), lambda b, i, f: (0, f)),
                  pl.BlockSpec((D, tf), lambda b, i, f: (0, f)),
                  pl.BlockSpec((tf, D), lambda b, i, f: (f, 0)),
                  pl.BlockSpec((3, tf), lambda b, i, f: (0, f)),
                  pl.BlockSpec((1, tf), lambda b, i, f: (0, f)),
                  pl.BlockSpec((1, tm, D), lambda b, i, f: (b, i, 0)),
                  pl.BlockSpec((1, 1, D), lambda b, i, f: (b, 0, 0)),
                  pl.BlockSpec((1, D), lambda b, i, f: (0, 0))],
        out_specs=pl.BlockSpec((1, tm, D), lambda b, i, f: (b, i, 0)),
        out_shape=jax.ShapeDtypeStruct((B, SEQ, D), F32),
        scratch_shapes=[pltpu.VMEM((tm, D), F32)],
        compiler_params=_cparams(("parallel", "parallel", "arbitrary")), name="convffn",
    )(h, h, h, wg, wv, wd, cw, cb[None, :], x1, gate[:, None, :], g_post[None, :])


def _rope_partner(w):
    q = QK_ROPE // 4
    return jnp.concatenate([-w[..., q:2 * q], w[..., :q], -w[..., 3 * q:], w[..., 2 * q:3 * q]], axis=-1)


def _pad_lora_cols(w):
    K = w.shape[0]
    w = w.reshape(K, N_DIR, -1)
    return jnp.pad(w, ((0, 0), (0, 0), (0, LORA_PAD - w.shape[-1]))).reshape(K, N_DIR * LORA_PAD)


def _pack_w_in(w):
    o = 3 * RWKV_DIM
    wl = w[:, o:o + N_DIR * DECAY_LORA]
    al = w[:, o + N_DIR * DECAY_LORA:o + N_DIR * (DECAY_LORA + AAA_LORA)]
    gl = w[:, o + N_DIR * (DECAY_LORA + AAA_LORA):RWKV_IN]
    m = w[:, RWKV_IN:RWKV_IN + MLA_IN]
    kpe = m[:, Q_LORA + KV_LORA:]
    return jnp.concatenate(
        [w[:, :o], _pad_lora_cols(wl), _pad_lora_cols(al), gl, kpe, _rope_partner(kpe),
         jnp.zeros((w.shape[0], LANES), w.dtype), m[:, :Q_LORA], m[:, Q_LORA:Q_LORA + KV_LORA],
         w[:, RWKV_IN + MLA_IN:]], axis=1)


def _pack_mu(mu):
    o = 3 * RWKV_DIM
    wl = mu[None, o:o + N_DIR * DECAY_LORA]
    al = mu[None, o + N_DIR * DECAY_LORA:o + N_DIR * (DECAY_LORA + AAA_LORA)]
    return jnp.concatenate([mu[:o], _pad_lora_cols(wl)[0], _pad_lora_cols(al)[0],
                            mu[o + N_DIR * (DECAY_LORA + AAA_LORA):RWKV_IN]])


def _rope_tables():
    rows = SEQ // GRID_W
    row = jnp.repeat(jnp.arange(rows), GRID_W).astype(F32)
    col = jnp.tile(jnp.arange(GRID_W), rows).astype(F32)
    half = QK_ROPE // 2
    freqs = ROPE_THETA ** (-jnp.arange(0, half, 2, dtype=F32) / half)
    ar, ac = row[:, None] * freqs, col[:, None] * freqs
    cos = jnp.concatenate([jnp.cos(ar), jnp.cos(ar), jnp.cos(ac), jnp.cos(ac)], axis=-1)
    sin = jnp.concatenate([jnp.sin(ar), jnp.sin(ar), jnp.sin(ac), jnp.sin(ac)], axis=-1)
    cos_all = jnp.concatenate([cos, jnp.ones((CTX_LEN, QK_ROPE), F32)], axis=0)
    sin_all = jnp.concatenate([sin, jnp.zeros((CTX_LEN, QK_ROPE), F32)], axis=0)
    return cos_all, sin_all


def _heads(t):
    B, S, _ = t.shape
    return t.reshape(B, S, RWKV_HEADS, RWKV_HEAD).transpose(0, 2, 1, 3)


def _heads_dir(t):
    B, S = t.shape[0], t.shape[1]
    return t.reshape(B, S, N_DIR, RWKV_HEADS, RWKV_HEAD).transpose(0, 2, 3, 1, 4)


def kernel(x, c, ctx, c_ctx, w_ada, b_ada, norm_mix_pre, norm_mix_post, norm_ffn_pre, norm_ffn_post, w_in, rwkv_mu, rwkv_w0, rwkv_w2, rwkv_a0, rwkv_a2, rwkv_k_k, rwkv_k_a, rwkv_r_k, rwkv_lnx_w, rwkv_lnx_b, rwkv_g2, w_rwkv_proj, mla_q_norm, mla_kv_norm, mla_w_uq, mla_w_ukv, w_mla_proj, gate_b, w_out, ffn_w_gate, ffn_w_val, ffn_conv_w, ffn_conv_b, ffn_w_down):
    B = x.shape[0]
    D = D_MODEL
    C = RWKV_DIM
    l = 0

    s = jnp.concatenate([jax.nn.silu(c), jax.nn.silu(c_ctx)[None, :]], axis=0)
    mods = _ada(s, w_ada[l], b_ada[l])
    lat = mods[:B].reshape(B, 6, D)
    cm = mods[B].reshape(6, D)
    sh1, sc1, g1, sh2, sc2, g2 = [lat[:, j] for j in range(6)]
    mods1 = jnp.stack([jnp.stack([sh1, sc1], axis=1),
                       jnp.broadcast_to(jnp.stack([cm[0], cm[1]])[None], (B, 2, D))], axis=1)
    mods2 = jnp.stack([sh2, sc2], axis=1)

    xcat = jnp.concatenate([x, ctx], axis=1)
    h = _norm_mod(xcat, norm_mix_pre[l], mods1)
    w_in_p = _pack_w_in(w_in[l]).astype(BF16)
    p = _matmul(h.reshape(B * S_ALL, D), w_in_p, tm=1088, tn=1024, name="w_in").reshape(B, S_ALL, P_COLS)

    pr = p[..., :RWKV_COLS]
    zero_row = jnp.zeros((B, 1, RWKV_COLS), F32)
    prev = jnp.concatenate([zero_row, pr[:, :SEQ - 1], zero_row, pr[:, SEQ:S_ALL - 1]], axis=1)
    nxt = jnp.concatenate([pr[:, 1:SEQ], zero_row, pr[:, SEQ + 1:], zero_row], axis=1)
    pr = pr + _pack_mu(rwkv_mu[l]) * (0.5 * (prev + nxt) - pr)
    r, k, v = pr[..., C_R:C_R + C], pr[..., C_K:C_K + C], pr[..., C_V:C_V + C]
    wl = pr[..., C_WL:C_WL + N_DIR * LORA_PAD].reshape(B * S_ALL, N_DIR, LORA_PAD)
    al = pr[..., C_AL:C_AL + N_DIR * LORA_PAD].reshape(B * S_ALL, N_DIR, LORA_PAD)
    gl = pr[..., C_GL:C_GL + GATE_LORA].reshape(B * S_ALL, GATE_LORA)
    pad_rows = lambda w: jnp.pad(w, ((0, 0), (0, LORA_PAD - w.shape[1]), (0, 0))).astype(BF16)
    w2p, a2p = pad_rows(rwkv_w2[l]), pad_rows(rwkv_a2[l])
    tanh_wl = jnp.tanh(wl).astype(BF16)
    al_b = al.astype(BF16)
    w_raw = jnp.stack([rwkv_w0[l, d] + _matmul(tanh_wl[:, d], w2p[d], 1088, 1024, name="decay_lora")
                       for d in range(N_DIR)], axis=1).reshape(B, S_ALL, N_DIR, C)
    a_raw = jnp.stack([rwkv_a0[l, d] + _matmul(al_b[:, d], a2p[d], 1088, 1024, name="a_lora")
                       for d in range(N_DIR)], axis=1).reshape(B, S_ALL, N_DIR, C)
    lw = -DECAY_SCALE * jax.nn.sigmoid(w_raw)
    a = jax.nn.sigmoid(a_raw)
    kk = (k * rwkv_k_k[l]).reshape(B, S_ALL, RWKV_HEADS, RWKV_HEAD)
    kk = (kk / jnp.maximum(jnp.sqrt(jnp.sum(kk * kk, axis=-1, keepdims=True)), 1e-12)).reshape(B, S_ALL, C)
    k_dir = k[:, :, None, :] * (1.0 + (a - 1.0) * rwkv_k_a[l])

    y_dir = _wkv(_heads(r), _heads(v), _heads(kk), _heads_dir(lw), _heads_dir(a), _heads_dir(k_dir))
    y = (y_dir[:, 0] + y_dir[:, 1])[:, :, :SEQ]
    mu = jnp.mean(y, axis=-1, keepdims=True)
    var = jnp.mean(jnp.square(y - mu), axis=-1, keepdims=True)
    yn = ((y - mu) * lax.rsqrt(var + LNX_EPS)).transpose(0, 2, 1, 3).reshape(B, SEQ, C)
    yn = yn * rwkv_lnx_w[l] + rwkv_lnx_b[l]
    r_l, v_l = r[:, :SEQ], v[:, :SEQ]
    k_bar = 0.5 * (k_dir[:, :SEQ, 0] + k_dir[:, :SEQ, 1])
    bonus = jnp.sum((r_l * k_bar * rwkv_r_k[l].reshape(C)).reshape(B, SEQ, RWKV_HEADS, RWKV_HEAD),
                    axis=-1, keepdims=True)
    bonus = (bonus * v_l.reshape(B, SEQ, RWKV_HEADS, RWKV_HEAD)).reshape(B, SEQ, C)
    g = _matmul(jax.nn.sigmoid(gl).astype(BF16), rwkv_g2[l].astype(BF16), 1088, 1024,
                name="gate_lora").reshape(B, S_ALL, C)[:, :SEQ]
    y_rwkv = ((yn + bonus) * g).astype(BF16)

    cos, sin = _rope_tables()
    uq = mla_w_uq[l].reshape(Q_LORA, MLA_HEADS, QK_NOPE + QK_ROPE)
    wq = jnp.concatenate([uq, _rope_partner(uq[..., QK_NOPE:])], axis=-1).transpose(1, 0, 2).astype(BF16)
    wkv = mla_w_ukv[l].reshape(KV_LORA, MLA_HEADS, QK_NOPE + V_HEAD).transpose(1, 0, 2).astype(BF16)
    q = _qproj(p, mla_q_norm[l], wq, cos[:SEQ], sin[:SEQ])
    k_all, v_all = _kvproj(p, mla_kv_norm[l], wkv, cos, sin)
    o = _attention(q, k_all, v_all)

    merged = _merge(y_rwkv, o, w_rwkv_proj[l].astype(BF16), w_mla_proj[l].astype(BF16), p, gate_b[l])
    x1, h2 = _outproj(merged, w_out[l].astype(BF16), x, norm_mix_post[l], g1, norm_ffn_pre[l], mods2)
    return _ffn(h2, ffn_w_gate[l].astype(BF16), ffn_w_val[l].astype(BF16), ffn_w_down[l].astype(BF16),
                ffn_conv_w[l], ffn_conv_b[l], x1, g2, norm_ffn_post[l])
```

```python
import functools
import math

import jax
import jax.numpy as jnp
from jax import lax
from jax.experimental import pallas as pl
from jax.experimental.pallas import tpu as pltpu

D_MODEL = 2048
BATCH = 2
SEQ = 4096
GRID_W = 64
CTX_LEN = 256
S_ALL = SEQ + CTX_LEN
EPS = 1e-6
RWKV_HEADS = 16
RWKV_HEAD = 64
RWKV_DIM = RWKV_HEADS * RWKV_HEAD
DECAY_LORA = 96
AAA_LORA = 96
GATE_LORA = 256
N_DIR = 2
DECAY_SCALE = math.exp(-0.5)
LNX_EPS = 64e-5
MLA_HEADS = 16
Q_LORA = 512
KV_LORA = 512
QK_NOPE = 128
QK_ROPE = 64
V_HEAD = 128
ROPE_THETA = 10000.0
ATTN_SCALE = (QK_NOPE + QK_ROPE) ** -0.5
D_FF = 5632
RWKV_IN = 3 * RWKV_DIM + N_DIR * DECAY_LORA + N_DIR * AAA_LORA + GATE_LORA
MLA_IN = Q_LORA + KV_LORA + QK_ROPE

LANES = 128
SUBLANES = 8
LORA_PAD = LANES
VMEM_LIMIT = 56 * 1024 * 1024

C_R, C_K, C_V = 0, RWKV_DIM, 2 * RWKV_DIM
C_WL = 3 * RWKV_DIM
C_AL = C_WL + N_DIR * LORA_PAD
C_GL = C_AL + N_DIR * LORA_PAD
C_KPE = C_GL + GATE_LORA
C_PAD = C_KPE + 2 * QK_ROPE
C_CQ = C_PAD + LANES
C_CKV = C_CQ + Q_LORA
C_GATE = C_CKV + KV_LORA
P_COLS = C_GATE + 2 * D_MODEL
RWKV_COLS = C_KPE

WKV_CHUNK = 64
N_CHUNKS = S_ALL // WKV_CHUNK
LAT_CHUNKS = SEQ // WKV_CHUNK
HEAD_PAIRS = RWKV_DIM // LANES

F32 = jnp.float32
BF16 = jnp.bfloat16


def _cparams(sem):
    return pltpu.CompilerParams(dimension_semantics=sem, vmem_limit_bytes=VMEM_LIMIT)


def _bdot(a, b):
    return jnp.dot(a.astype(BF16), b.astype(BF16), preferred_element_type=F32)


def _hilo_dot(x, w):
    hi = x.astype(BF16)
    lo = (x - hi.astype(F32)).astype(BF16)
    return jnp.dot(hi, w, preferred_element_type=F32) + jnp.dot(lo, w, preferred_element_type=F32)


def _head_sum(x, ones2):
    return jnp.concatenate([_hilo_dot(x[:, c * LANES:(c + 1) * LANES], ones2) for c in range(HEAD_PAIRS)], axis=-1)


def _ada_kernel(s_ref, w_ref, b_ref, o_ref):
    w = w_ref[...]
    for r in range(s_ref.shape[0]):
        o_ref[r:r + 1, :] = jnp.sum(s_ref[r] * w, axis=0, keepdims=True) + b_ref[...]


def _ada(s, w, b, tn=1024):
    R, K = s.shape
    N = w.shape[1]
    return pl.pallas_call(
        _ada_kernel, grid=(N // tn,),
        in_specs=[pl.BlockSpec((R, K, 1), lambda j: (0, 0, 0)),
                  pl.BlockSpec((K, tn), lambda j: (0, j)),
                  pl.BlockSpec((1, tn), lambda j: (0, j))],
        out_specs=pl.BlockSpec((R, tn), lambda j: (0, j)),
        out_shape=jax.ShapeDtypeStruct((R, N), F32),
        compiler_params=_cparams(("parallel",)), name="ada",
    )(s[:, :, None], w, b[None, :])


def _norm_mod_kernel(x_ref, g_ref, m_ref, o_ref):
    x = x_ref[0]
    n = x * lax.rsqrt(jnp.mean(x * x, axis=-1, keepdims=True) + EPS) * g_ref[...]
    m = m_ref[0, 0]
    o_ref[0] = (n * (1.0 + m[1:2]) + m[0:1]).astype(o_ref.dtype)


def _norm_mod(xcat, gain, mods, tm=256):
    B, S, D = xcat.shape
    lat_tiles = SEQ // tm
    return pl.pallas_call(
        _norm_mod_kernel, grid=(B, S // tm),
        in_specs=[pl.BlockSpec((1, tm, D), lambda b, i: (b, i, 0)),
                  pl.BlockSpec((1, D), lambda b, i: (0, 0)),
                  pl.BlockSpec((1, 1, 2, D), lambda b, i: (b, i // lat_tiles, 0, 0))],
        out_specs=pl.BlockSpec((1, tm, D), lambda b, i: (b, i, 0)),
        out_shape=jax.ShapeDtypeStruct((B, S, D), BF16),
        compiler_params=_cparams(("parallel", "parallel")), name="norm_mod",
    )(xcat, gain[None, :], mods)


def _mm_kernel(a_ref, w_ref, o_ref):
    o_ref[...] = _bdot(a_ref[...], w_ref[...]).astype(o_ref.dtype)


def _matmul(a, w, tm, tn, out_dtype=F32, name="matmul"):
    M, K = a.shape
    N = w.shape[1]
    return pl.pallas_call(
        _mm_kernel, grid=(M // tm, N // tn),
        in_specs=[pl.BlockSpec((tm, K), lambda i, j: (i, 0)),
                  pl.BlockSpec((K, tn), lambda i, j: (0, j))],
        out_specs=pl.BlockSpec((tm, tn), lambda i, j: (i, j)),
        out_shape=jax.ShapeDtypeStruct((M, N), out_dtype),
        compiler_params=_cparams(("parallel", "parallel")), name=name,
    )(a, w)


def _prep_kernel(p_ref, pp_ref, pn_ref, mu_ref, w0_ref, w2_ref, a0_ref, a2_ref, kk_ref, ka_ref, rk_ref, g2_ref,
                 ones_ref, r_o, k_o, v_o, kkn_o, lw_o, a_o, g_o, bg_o):
    i = pl.program_id(1)
    x = p_ref[0]
    tm = x.shape[0]
    lat_tiles = SEQ // tm
    first = jnp.logical_or(i == 0, i == lat_tiles)
    last = jnp.logical_or(i == lat_tiles - 1, i == pl.num_programs(1) - 1)
    x_before = jnp.where(first, 0.0, pp_ref[0][SUBLANES - 1:SUBLANES])
    x_after = jnp.where(last, 0.0, pn_ref[0][0:1])
    rows = lax.broadcasted_iota(jnp.int32, (tm, 1), 0)
    prev = jnp.where(rows == 0, x_before, pltpu.roll(x, 1, 0))
    nxt = jnp.where(rows == tm - 1, x_after, pltpu.roll(x, tm - 1, 0))
    xs = x + mu_ref[...] * (0.5 * (prev + nxt) - x)

    C = RWKV_DIM
    r, k, v = xs[:, C_R:C_R + C], xs[:, C_K:C_K + C], xs[:, C_V:C_V + C]
    a_sum = None
    for d in range(N_DIR):
        wl = xs[:, C_WL + d * LORA_PAD:C_WL + (d + 1) * LORA_PAD]
        al = xs[:, C_AL + d * LORA_PAD:C_AL + (d + 1) * LORA_PAD]
        w_raw = w0_ref[d:d + 1] + _bdot(jnp.tanh(wl), w2_ref[d])
        lw_o[0, d] = -DECAY_SCALE * jax.nn.sigmoid(w_raw)
        a_d = jax.nn.sigmoid(a0_ref[d:d + 1] + _bdot(al, a2_ref[d]))
        a_o[0, d] = a_d
        a_sum = a_d if a_sum is None else a_sum + a_d
    ones2 = ones_ref[...]
    kkx = k * kk_ref[...]
    kkn_o[0] = kkx / jnp.maximum(jnp.sqrt(_head_sum(kkx * kkx, ones2)), 1e-12)
    g = _bdot(jax.nn.sigmoid(xs[:, C_GL:C_GL + GATE_LORA]), g2_ref[...])
    k_bar = k * (1.0 + (0.5 * a_sum - 1.0) * ka_ref[...])
    bonus = _head_sum(r * k_bar * rk_ref[...], ones2) * v
    r_o[0] = r
    k_o[0] = k
    v_o[0] = v
    g_o[0] = g
    bg_o[0] = bonus * g


def _prep(p, mu, w0, w2p, a0, a2p, k_k, k_a, r_k, g2, ones2, tm=256):
    B = p.shape[0]
    C = RWKV_DIM
    hb = tm // SUBLANES
    last = S_ALL // SUBLANES - 1
    row = lambda n: pl.BlockSpec((1, n), lambda b, i: (0, 0))
    tok = pl.BlockSpec((1, tm, C), lambda b, i: (b, i, 0))
    tok_dir = pl.BlockSpec((1, N_DIR, tm, C), lambda b, i: (b, 0, i, 0))
    shp = jax.ShapeDtypeStruct((B, S_ALL, C), F32)
    shp_dir = jax.ShapeDtypeStruct((B, N_DIR, S_ALL, C), F32)
    return pl.pallas_call(
        _prep_kernel, grid=(B, S_ALL // tm),
        in_specs=[pl.BlockSpec((1, tm, RWKV_COLS), lambda b, i: (b, i, 0)),
                  pl.BlockSpec((1, SUBLANES, RWKV_COLS), lambda b, i: (b, jnp.maximum(i * hb - 1, 0), 0)),
                  pl.BlockSpec((1, SUBLANES, RWKV_COLS), lambda b, i: (b, jnp.minimum((i + 1) * hb, last), 0)),
                  row(RWKV_COLS),
                  pl.BlockSpec((N_DIR, C), lambda b, i: (0, 0)),
                  pl.BlockSpec((N_DIR, LORA_PAD, C), lambda b, i: (0, 0, 0)),
                  pl.BlockSpec((N_DIR, C), lambda b, i: (0, 0)),
                  pl.BlockSpec((N_DIR, LORA_PAD, C), lambda b, i: (0, 0, 0)),
                  row(C), row(C), row(C),
                  pl.BlockSpec((GATE_LORA, C), lambda b, i: (0, 0)),
                  pl.BlockSpec((LANES, LANES), lambda b, i: (0, 0))],
        out_specs=[tok, tok, tok, tok, tok_dir, tok_dir, tok, tok],
        out_shape=[shp, shp, shp, shp, shp_dir, shp_dir, shp, shp],
        compiler_params=_cparams(("parallel", "parallel")), name="rwkv_prep",
    )(p, p, p, mu[None, :], w0, w2p, a0, a2p, k_k[None, :], k_a[None, :], r_k[None, :], g2, ones2)


def _wkv_kernel(r_ref, k_ref, v_ref, kk_ref, lw_ref, a_ref, ka_ref, y_ref, s_ref):
    d = pl.program_id(1)
    C = WKV_CHUNK
    C2 = 2 * C
    P = HEAD_PAIRS

    @pl.when(pl.program_id(2) == 0)
    def _():
        s_ref[...] = jnp.zeros_like(s_ref)

    r = r_ref[0]
    v = v_ref[0]
    kk = kk_ref[0]
    lw = lw_ref[0, 0]
    a = a_ref[0, 0]
    k = k_ref[0] * (1.0 + (a - 1.0) * ka_ref[...])
    sign = 1 - 2 * d

    trow = lax.broadcasted_iota(jnp.int32, (C, C), 0)
    tcol = lax.broadcasted_iota(jnp.int32, (C, C), 1)
    tri = ((trow - tcol) * sign >= 0).astype(BF16)
    cum = _hilo_dot_left(tri, lw)
    tot = jnp.sum(lw, axis=0, keepdims=True)

    e_cum = jnp.exp(cum)
    e_neg = jnp.exp(-cum)
    e_rem = jnp.exp(tot - cum)
    b = kk * a
    at = -kk * jnp.exp(cum - lw)
    rt = r * e_cum
    bt = b * e_neg
    kt = k * e_neg
    bdec = b * e_rem
    kdec = k * e_rem
    dec = jnp.exp(tot)

    lane = lax.broadcasted_iota(jnp.int32, (C, LANES), 1)
    head0 = lane < RWKV_HEAD
    zero = jnp.zeros((), F32)

    def expand(x):
        x = x.astype(BF16)
        zb = jnp.zeros((), BF16)
        return jnp.stack([jnp.concatenate([jnp.where(head0, x[:, j * LANES:(j + 1) * LANES], zb),
                                           jnp.where(head0, zb, x[:, j * LANES:(j + 1) * LANES])], axis=0)
                          for j in range(P)])

    def bmm(x, y):
        return jnp.einsum('hab,hbc->hac', x.astype(BF16), y.astype(BF16), preferred_element_type=F32)

    def bmm_nt(x, y):
        return jnp.einsum('han,hbn->hab', x.astype(BF16), y.astype(BF16), preferred_element_type=F32)

    def bmm_tn(x, y):
        return jnp.einsum('hca,hcb->hab', x.astype(BF16), y.astype(BF16), preferred_element_type=F32)

    AT, RT, BT, KT, BD, KD, V = (expand(t) for t in (at, rt, bt, kt, bdec, kdec, v))

    row = lax.broadcasted_iota(jnp.int32, (C2, C2), 0)
    col = lax.broadcasted_iota(jnp.int32, (C2, C2), 1)
    same = (row // C) == (col // C)
    lag = (row - col) * sign
    incl = jnp.logical_and(same, lag >= 0)[None]
    strict = jnp.logical_and(same, lag > 0)[None]

    def blk(m):
        return (row // m) == (col // m)

    G = bmm_nt(jnp.concatenate([AT, RT], axis=1), jnp.concatenate([BT, KT], axis=1))
    a_ab, a_ak = G[:, :C2, :C2], G[:, :C2, C2:]
    a_rb, a_rk = G[:, C2:, :C2], G[:, C2:, C2:]

    L = jnp.where(strict, a_ab, zero)
    Lb = jnp.where(blk(8)[None], L, zero)
    L2 = bmm(Lb, Lb)
    L4 = bmm(L2, L2)
    S1 = Lb + L2 + bmm(Lb, L2)
    Nm = S1 + L4 + bmm(L4, S1)
    for m in (8, 16, 32):
        Lo = jnp.where(jnp.logical_and(blk(2 * m), jnp.logical_not(blk(m)))[None], L, zero)
        Pm = Lo + bmm(Nm, Lo)
        Nm = Nm + Pm + bmm(Pm, Nm)

    s0 = s_ref[...]
    rhs = bmm_nt(AT, s0) + bmm(jnp.where(strict, a_ak, zero), V)
    u = rhs + bmm(Nm, rhs)
    uv = jnp.concatenate([u.astype(BF16), V], axis=1)
    y2 = bmm_nt(RT, s0) + bmm(jnp.concatenate([jnp.where(incl, a_rb, zero), jnp.where(incl, a_rk, zero)], axis=2), uv)
    y = y2[:, :C] + y2[:, C:]
    y_ref[0, 0] = jnp.concatenate([y[j] for j in range(P)], axis=-1)
    dec3 = jnp.stack([dec[:, j * LANES:(j + 1) * LANES] for j in range(P)])
    s_ref[...] = s0 * dec3 + bmm_tn(uv, jnp.concatenate([BD, KD], axis=1))


def _hilo_dot_left(w, x):
    hi = x.astype(BF16)
    lo = (x - hi.astype(F32)).astype(BF16)
    return jnp.dot(w, hi, preferred_element_type=F32) + jnp.dot(w, lo, preferred_element_type=F32)


def _wkv(r, k, v, kk, lw, a, k_a):
    B, S, Cd = r.shape
    C = WKV_CHUNK

    def cidx(d, i):
        return jnp.where(d == 0, (i + LAT_CHUNKS) % N_CHUNKS, N_CHUNKS - 1 - i)

    shared = pl.BlockSpec((1, C, Cd), lambda b, d, i: (b, cidx(d, i), 0))
    perdir = pl.BlockSpec((1, 1, C, Cd), lambda b, d, i: (b, d, cidx(d, i), 0))
    return pl.pallas_call(
        _wkv_kernel, grid=(B, N_DIR, N_CHUNKS),
        in_specs=[shared, shared, shared, shared, perdir, perdir, pl.BlockSpec((1, Cd), lambda b, d, i: (0, 0))],
        out_specs=perdir,
        out_shape=jax.ShapeDtypeStruct((B, N_DIR, S, Cd), F32),
        scratch_shapes=[pltpu.VMEM((HEAD_PAIRS, LANES, LANES), F32)],
        compiler_params=_cparams(("parallel", "parallel", "arbitrary")), name="wkv7",
    )(r, k, v, kk, lw, a, k_a[None, :])


def _rms(x, g):
    return x * lax.rsqrt(jnp.mean(x * x, axis=-1, keepdims=True) + EPS) * g


def _qproj_kernel(p_ref, g_ref, w_ref, cos_ref, sin_ref, q_ref):
    cq = _rms(p_ref[0], g_ref[...]).astype(BF16)
    cs, sn = cos_ref[...], sin_ref[...]
    for h in range(MLA_HEADS):
        y = jnp.dot(cq, w_ref[h], preferred_element_type=F32)
        rope = y[:, QK_NOPE:QK_NOPE + QK_ROPE] * cs + y[:, QK_NOPE + QK_ROPE:] * sn
        q = jnp.concatenate([y[:, :QK_NOPE], rope], axis=-1) * ATTN_SCALE
        q_ref[0, h] = q.astype(q_ref.dtype)


def _qproj(p, gain, wq, cos, sin, tm=512):
    B = p.shape[0]
    return pl.pallas_call(
        _qproj_kernel, grid=(B, SEQ // tm),
        in_specs=[pl.BlockSpec((1, tm, Q_LORA), lambda b, i: (b, i, C_CQ // Q_LORA)),
                  pl.BlockSpec((1, Q_LORA), lambda b, i: (0, 0)),
                  pl.BlockSpec((MLA_HEADS, Q_LORA, 2 * LANES), lambda b, i: (0, 0, 0)),
                  pl.BlockSpec((tm, QK_ROPE), lambda b, i: (i, 0)),
                  pl.BlockSpec((tm, QK_ROPE), lambda b, i: (i, 0))],
        out_specs=pl.BlockSpec((1, MLA_HEADS, tm, QK_NOPE + QK_ROPE), lambda b, i: (b, 0, i, 0)),
        out_shape=jax.ShapeDtypeStruct((B, MLA_HEADS, SEQ, QK_NOPE + QK_ROPE), BF16),
        compiler_params=_cparams(("parallel", "parallel")), name="mla_q",
    )(p, gain[None, :], wq, cos, sin)


def _kvproj_kernel(p_ref, pe_ref, g_ref, w_ref, cos_ref, sin_ref, k_ref, v_ref):
    ckv = _rms(p_ref[0], g_ref[...]).astype(BF16)
    pe = pe_ref[0]
    kpe = pe[:, :QK_ROPE] * cos_ref[...] + pe[:, QK_ROPE:] * sin_ref[...]
    for h in range(MLA_HEADS):
        y = jnp.dot(ckv, w_ref[h], preferred_element_type=F32)
        k_ref[0, h] = jnp.concatenate([y[:, :QK_NOPE], kpe], axis=-1).astype(k_ref.dtype)
        v_ref[0, h] = y[:, QK_NOPE:].astype(v_ref.dtype)


def _kvproj(p, gain, wkv, cos, sin, tm=256):
    B = p.shape[0]
    return pl.pallas_call(
        _kvproj_kernel, grid=(B, S_ALL // tm),
        in_specs=[pl.BlockSpec((1, tm, KV_LORA), lambda b, i: (b, i, C_CKV // KV_LORA)),
                  pl.BlockSpec((1, tm, 2 * QK_ROPE), lambda b, i: (b, i, C_KPE // (2 * QK_ROPE))),
                  pl.BlockSpec((1, KV_LORA), lambda b, i: (0, 0)),
                  pl.BlockSpec((MLA_HEADS, KV_LORA, 2 * LANES), lambda b, i: (0, 0, 0)),
                  pl.BlockSpec((tm, QK_ROPE), lambda b, i: (i, 0)),
                  pl.BlockSpec((tm, QK_ROPE), lambda b, i: (i, 0))],
        out_specs=[pl.BlockSpec((1, MLA_HEADS, tm, QK_NOPE + QK_ROPE), lambda b, i: (b, 0, i, 0)),
                   pl.BlockSpec((1, MLA_HEADS, tm, V_HEAD), lambda b, i: (b, 0, i, 0))],
        out_shape=[jax.ShapeDtypeStruct((B, MLA_HEADS, S_ALL, QK_NOPE + QK_ROPE), BF16),
                   jax.ShapeDtypeStruct((B, MLA_HEADS, S_ALL, V_HEAD), BF16)],
        compiler_params=_cparams(("parallel", "parallel")), name="mla_kv",
    )(p, p, gain[None, :], wkv, cos, sin)


def _attn_kernel(q_ref, k_ref, v_ref, o_ref, *, tk):
    q = q_ref[0, 0]
    tq = q.shape[0]
    n_kv = k_ref.shape[2] // tk

    def body(j, carry):
        m, l, acc = carry
        off = pl.multiple_of(j * tk, tk)
        kj = k_ref[0, 0, pl.ds(off, tk), :]
        vj = v_ref[0, 0, pl.ds(off, tk), :]
        s = lax.dot_general(q, kj, (((1,), (1,)), ((), ())), preferred_element_type=F32)
        m_new = jnp.maximum(m, jnp.max(s, axis=-1, keepdims=True))
        alpha = jnp.exp(m - m_new)
        p = jnp.exp(s - m_new)
        l = alpha * l + jnp.sum(p, axis=-1, keepdims=True)
        acc = alpha * acc + jnp.dot(p.astype(BF16), vj, preferred_element_type=F32)
        return m_new, l, acc

    m0 = jnp.full((tq, 1), -jnp.inf, F32)
    l0 = jnp.zeros((tq, 1), F32)
    a0 = jnp.zeros((tq, V_HEAD), F32)
    _, l, acc = lax.fori_loop(0, n_kv, body, (m0, l0, a0))
    o_ref[0] = (acc / l).astype(o_ref.dtype)


def _attention(q, k, v, tq=512, tk=256):
    B, H, T, Dk = q.shape
    S = k.shape[2]
    return pl.pallas_call(
        functools.partial(_attn_kernel, tk=tk), grid=(B, H, T // tq),
        in_specs=[pl.BlockSpec((1, 1, tq, Dk), lambda b, h, i: (b, h, i, 0)),
                  pl.BlockSpec((1, 1, S, Dk), lambda b, h, i: (b, h, 0, 0)),
                  pl.BlockSpec((1, 1, S, V_HEAD), lambda b, h, i: (b, h, 0, 0))],
        out_specs=pl.BlockSpec((1, tq, V_HEAD), lambda b, h, i: (b, i, h)),
        out_shape=jax.ShapeDtypeStruct((B, T, H * V_HEAD), BF16),
        compiler_params=_cparams(("parallel", "parallel", "parallel")), name="mla_attn",
    )(q, k, v)


def _merge_kernel(y0_ref, y1_ref, g_ref, bg_ref, lnw_ref, lnb_ref, avg_ref, o_ref, wr_ref, wm_ref, pr_ref, pm_ref,
                  br_ref, bm_ref, out_ref, yr_ref):
    @pl.when(pl.program_id(2) == 0)
    def _():
        avg = avg_ref[...]
        y = y0_ref[0, 0] + y1_ref[0, 0]
        yc = y - _head_sum(y, avg)
        yn = yc * lax.rsqrt(_head_sum(yc * yc, avg) + LNX_EPS) * lnw_ref[...] + lnb_ref[...]
        yr_ref[...] = (yn * g_ref[0] + bg_ref[0]).astype(yr_ref.dtype)

    gr = jax.nn.sigmoid(pr_ref[0] + br_ref[...])
    gm = jax.nn.sigmoid(pm_ref[0] + bm_ref[...])
    out_ref[0] = (gr * jnp.dot(yr_ref[...], wr_ref[...], preferred_element_type=F32)
                  + gm * jnp.dot(o_ref[0], wm_ref[...], preferred_element_type=F32)).astype(out_ref.dtype)


def _merge(y_dir, g, bg, lnx_w, lnx_b, avg2, o, w_rp, w_mp, p, gate_b, tm=512, tn=512):
    B = o.shape[0]
    D = D_MODEL
    C = RWKV_DIM
    gcol = C_GATE // tn
    tok = pl.BlockSpec((1, tm, C), lambda b, i, j: (b, i, 0))
    row = lambda n: pl.BlockSpec((1, n), lambda b, i, j: (0, 0))
    return pl.pallas_call(
        _merge_kernel, grid=(B, SEQ // tm, D // tn),
        in_specs=[pl.BlockSpec((1, 1, tm, C), lambda b, i, j: (b, 0, i, 0)),
                  pl.BlockSpec((1, 1, tm, C), lambda b, i, j: (b, 1, i, 0)),
                  tok, tok, row(C), row(C),
                  pl.BlockSpec((LANES, LANES), lambda b, i, j: (0, 0)),
                  pl.BlockSpec((1, tm, D), lambda b, i, j: (b, i, 0)),
                  pl.BlockSpec((C, tn), lambda b, i, j: (0, j)),
                  pl.BlockSpec((D, tn), lambda b, i, j: (0, j)),
                  pl.BlockSpec((1, tm, tn), lambda b, i, j: (b, i, gcol + j)),
                  pl.BlockSpec((1, tm, tn), lambda b, i, j: (b, i, gcol + D // tn + j)),
                  pl.BlockSpec((1, tn), lambda b, i, j: (0, j)),
                  pl.BlockSpec((1, tn), lambda b, i, j: (0, D // tn + j))],
        out_specs=pl.BlockSpec((1, tm, tn), lambda b, i, j: (b, i, j)),
        out_shape=jax.ShapeDtypeStruct((B, SEQ, D), BF16),
        scratch_shapes=[pltpu.VMEM((tm, C), BF16)],
        compiler_params=_cparams(("parallel", "parallel", "arbitrary")), name="merge",
    )(y_dir, y_dir, g, bg, lnx_w[None, :], lnx_b[None, :], avg2, o, w_rp, w_mp, p, p,
      gate_b[None, :], gate_b[None, :])


def _outproj_kernel(m_ref, w_ref, x_ref, gpost_ref, gate_ref, gpre_ref, mod_ref, x1_ref, h_ref):
    out = jnp.dot(m_ref[0], w_ref[...], preferred_element_type=F32)
    x1 = x_ref[0] + gate_ref[0] * _rms(out, gpost_ref[...])
    x1_ref[0] = x1
    mod = mod_ref[0]
    h_ref[0] = (_rms(x1, gpre_ref[...]) * (1.0 + mod[1:2]) + mod[0:1]).astype(h_ref.dtype)


def _outproj(merged, w_out, x, g_post, gate, g_pre, mod2, tm=256):
    B = x.shape[0]
    D = D_MODEL
    return pl.pallas_call(
        _outproj_kernel, grid=(B, SEQ // tm),
        in_specs=[pl.BlockSpec((1, tm, D), lambda b, i: (b, i, 0)),
                  pl.BlockSpec((D, D), lambda b, i: (0, 0)),
                  pl.BlockSpec((1, tm, D), lambda b, i: (b, i, 0)),
                  pl.BlockSpec((1, D), lambda b, i: (0, 0)),
                  pl.BlockSpec((1, 1, D), lambda b, i: (b, 0, 0)),
                  pl.BlockSpec((1, D), lambda b, i: (0, 0)),
                  pl.BlockSpec((1, 2, D), lambda b, i: (b, 0, 0))],
        out_specs=[pl.BlockSpec((1, tm, D), lambda b, i: (b, i, 0)),
                   pl.BlockSpec((1, tm, D), lambda b, i: (b, i, 0))],
        out_shape=[jax.ShapeDtypeStruct((B, SEQ, D), F32),
                   jax.ShapeDtypeStruct((B, SEQ, D), BF16)],
        compiler_params=_cparams(("parallel", "parallel")), name="outproj",
    )(merged, w_out, x, g_post[None, :], gate[:, None, :], g_pre[None, :], mod2)


HALO = 16


def _ffn_kernel(h_ref, hp_ref, hn_ref, wg_ref, wv_ref, wd_ref, cw_ref, cb_ref, x_ref, gate_ref, gpost_ref,
                o_ref, acc_ref):
    i = pl.program_id(1)
    f = pl.program_id(2)
    tm = h_ref.shape[1]

    @pl.when(f == 0)
    def _():
        acc_ref[...] = jnp.zeros_like(acc_ref)

    h = h_ref[0]
    wg = wg_ref[...]
    g = jnp.dot(h, wg, preferred_element_type=F32)
    g_prev = jnp.dot(hp_ref[0], wg, preferred_element_type=F32)[HALO - 1:HALO]
    g_next = jnp.dot(hn_ref[0], wg, preferred_element_type=F32)[0:1]
    g_prev = jnp.where(i == 0, 0.0, g_prev)
    g_next = jnp.where(i == pl.num_programs(1) - 1, 0.0, g_next)
    rows = lax.broadcasted_iota(jnp.int32, g.shape, 0)
    up = jnp.where(rows == 0, g_prev, pltpu.roll(g, 1, 0))
    dn = jnp.where(rows == tm - 1, g_next, pltpu.roll(g, tm - 1, 0))
    cw = cw_ref[...]
    u = cb_ref[...] + up * cw[0:1] + g * cw[1:2] + dn * cw[2:3]
    val = jnp.dot(h, wv_ref[...], preferred_element_type=F32)
    act = jax.nn.gelu(u, approximate=True) * val
    acc_ref[...] += jnp.dot(act.astype(BF16), wd_ref[...], preferred_element_type=F32)

    @pl.when(f == pl.num_programs(2) - 1)
    def _():
        o_ref[0] = x_ref[0] + gate_ref[0] * _rms(acc_ref[...], gpost_ref[...])


def _ffn(h, wg, wv, wd, cw, cb, x1, gate, g_post, tm=512, tf=512):
    B = h.shape[0]
    D = D_MODEL
    hb = tm // HALO
    last = SEQ // HALO - 1
    return pl.pallas_call(
        _ffn_kernel, grid=(B, SEQ // tm, D_FF // tf),
        in_specs=[pl.BlockSpec((1, tm, D), lambda b, i, f: (b, i, 0)),
                  pl.BlockSpec((1, HALO, D), lambda b, i, f: (b, jnp.maximum(i * hb - 1, 0), 0)),
                  pl.BlockSpec((1, HALO, D), lambda b, i, f: (b, jnp.minimum((i + 1) * hb, last), 0)),
                  pl.BlockSpec((D, tf), lambda b, i, f: (0, f)),
                  pl.BlockSpec((D, tf), lambda b, i, f: (0, f)),
                  pl.BlockSpec((tf, D), lambda b, i, f: (f, 0)),
                  pl.BlockSpec((3, tf), lambda b, i, f: (0, f)),
                  pl.BlockSpec((1, tf), lambda b, i, f: (0, f)),
                  pl.BlockSpec((1, tm, D), lambda b, i, f: (b, i, 0)),
                  pl.BlockSpec((1, 1, D), lambda b, i, f: (b, 0, 0)),
                  pl.BlockSpec((1, D), lambda b, i, f: (0, 0))],
        out_specs=pl.BlockSpec((1, tm, D), lambda b, i, f: (b, i, 0)),
        out_shape=jax.ShapeDtypeStruct((B, SEQ, D), F32),
        scratch_shapes=[pltpu.VMEM((tm, D), F32)],
        compiler_params=_cparams(("parallel", "parallel", "arbitrary")), name="convffn",
    )(h, h, h, wg, wv, wd, cw, cb[None, :], x1, gate[:, None, :], g_post[None, :])


def _rope_partner(w):
    q = QK_ROPE // 4
    return jnp.concatenate([-w[..., q:2 * q], w[..., :q], -w[..., 3 * q:], w[..., 2 * q:3 * q]], axis=-1)


def _pad_lora_cols(w):
    K = w.shape[0]
    w = w.reshape(K, N_DIR, -1)
    return jnp.pad(w, ((0, 0), (0, 0), (0, LORA_PAD - w.shape[-1]))).reshape(K, N_DIR * LORA_PAD)


def _pack_w_in(w):
    o = 3 * RWKV_DIM
    wl = w[:, o:o + N_DIR * DECAY_LORA]
    al = w[:, o + N_DIR * DECAY_LORA:o + N_DIR * (DECAY_LORA + AAA_LORA)]
    gl = w[:, o + N_DIR * (DECAY_LORA + AAA_LORA):RWKV_IN]
    m = w[:, RWKV_IN:RWKV_IN + MLA_IN]
    kpe = m[:, Q_LORA + KV_LORA:]
    return jnp.concatenate(
        [w[:, :o], _pad_lora_cols(wl), _pad_lora_cols(al), gl, kpe, _rope_partner(kpe),
         jnp.zeros((w.shape[0], LANES), w.dtype), m[:, :Q_LORA], m[:, Q_LORA:Q_LORA + KV_LORA],
         w[:, RWKV_IN + MLA_IN:]], axis=1)


def _pack_mu(mu):
    o = 3 * RWKV_DIM
    wl = mu[None, o:o + N_DIR * DECAY_LORA]
    al = mu[None, o + N_DIR * DECAY_LORA:o + N_DIR * (DECAY_LORA + AAA_LORA)]
    return jnp.concatenate([mu[:o], _pad_lora_cols(wl)[0], _pad_lora_cols(al)[0],
                            mu[o + N_DIR * (DECAY_LORA + AAA_LORA):RWKV_IN]])


def _rope_tables():
    rows = SEQ // GRID_W
    row = jnp.repeat(jnp.arange(rows), GRID_W).astype(F32)
    col = jnp.tile(jnp.arange(GRID_W), rows).astype(F32)
    half = QK_ROPE // 2
    freqs = ROPE_THETA ** (-jnp.arange(0, half, 2, dtype=F32) / half)
    ar, ac = row[:, None] * freqs, col[:, None] * freqs
    cos = jnp.concatenate([jnp.cos(ar), jnp.cos(ar), jnp.cos(ac), jnp.cos(ac)], axis=-1)
    sin = jnp.concatenate([jnp.sin(ar), jnp.sin(ar), jnp.sin(ac), jnp.sin(ac)], axis=-1)
    cos_all = jnp.concatenate([cos, jnp.ones((CTX_LEN, QK_ROPE), F32)], axis=0)
    sin_all = jnp.concatenate([sin, jnp.zeros((CTX_LEN, QK_ROPE), F32)], axis=0)
    return cos_all, sin_all


def kernel(x, c, ctx, c_ctx, w_ada, b_ada, norm_mix_pre, norm_mix_post, norm_ffn_pre, norm_ffn_post, w_in, rwkv_mu, rwkv_w0, rwkv_w2, rwkv_a0, rwkv_a2, rwkv_k_k, rwkv_k_a, rwkv_r_k, rwkv_lnx_w, rwkv_lnx_b, rwkv_g2, w_rwkv_proj, mla_q_norm, mla_kv_norm, mla_w_uq, mla_w_ukv, w_mla_proj, gate_b, w_out, ffn_w_gate, ffn_w_val, ffn_conv_w, ffn_conv_b, ffn_w_down):
    B = x.shape[0]
    D = D_MODEL
    C = RWKV_DIM
    l = 0

    s = jnp.concatenate([jax.nn.silu(c), jax.nn.silu(c_ctx)[None, :]], axis=0)
    mods = _ada(s, w_ada[l], b_ada[l])
    lat = mods[:B].reshape(B, 6, D)
    cm = mods[B].reshape(6, D)
    sh1, sc1, g1, sh2, sc2, g2 = [lat[:, j] for j in range(6)]
    mods1 = jnp.stack([jnp.stack([sh1, sc1], axis=1),
                       jnp.broadcast_to(jnp.stack([cm[0], cm[1]])[None], (B, 2, D))], axis=1)
    mods2 = jnp.stack([sh2, sc2], axis=1)

    xcat = jnp.concatenate([x, ctx], axis=1)
    h = _norm_mod(xcat, norm_mix_pre[l], mods1)
    w_in_p = _pack_w_in(w_in[l]).astype(BF16)
    p = _matmul(h.reshape(B * S_ALL, D), w_in_p, tm=1088, tn=1024, name="w_in").reshape(B, S_ALL, P_COLS)

    pad_rows = lambda w: jnp.pad(w, ((0, 0), (0, LORA_PAD - w.shape[1]), (0, 0))).astype(BF16)
    lane_head = jnp.arange(LANES) // RWKV_HEAD
    ones2 = (lane_head[:, None] == lane_head[None, :]).astype(BF16)
    avg2 = ones2 * (1.0 / RWKV_HEAD)
    r, k, v, kk, lw, a, g, bg = _prep(p, _pack_mu(rwkv_mu[l]), rwkv_w0[l], pad_rows(rwkv_w2[l]), rwkv_a0[l],
                                      pad_rows(rwkv_a2[l]), rwkv_k_k[l], rwkv_k_a[l], rwkv_r_k[l].reshape(C),
                                      rwkv_g2[l].astype(BF16), ones2)
    y_dir = _wkv(r, k, v, kk, lw, a, rwkv_k_a[l])

    cos, sin = _rope_tables()
    uq = mla_w_uq[l].reshape(Q_LORA, MLA_HEADS, QK_NOPE + QK_ROPE)
    wq = jnp.concatenate([uq, _rope_partner(uq[..., QK_NOPE:])], axis=-1).transpose(1, 0, 2).astype(BF16)
    wkv = mla_w_ukv[l].reshape(KV_LORA, MLA_HEADS, QK_NOPE + V_HEAD).transpose(1, 0, 2).astype(BF16)
    q = _qproj(p, mla_q_norm[l], wq, cos[:SEQ], sin[:SEQ])
    k_all, v_all = _kvproj(p, mla_kv_norm[l], wkv, cos, sin)
    o = _attention(q, k_all, v_all)

    merged = _merge(y_dir, g, bg, rwkv_lnx_w[l], rwkv_lnx_b[l], avg2, o, w_rwkv_proj[l].astype(BF16),
                    w_mla_proj[l].astype(BF16), p, gate_b[l])
    x1, h2 = _outproj(merged, w_out[l].astype(BF16), x, norm_mix_post[l], g1, norm_ffn_pre[l], mods2)
    return _ffn(h2, ffn_w_gate[l].astype(BF16), ffn_w_val[l].astype(BF16), ffn_w_down[l].astype(BF16),
                ffn_conv_w[l], ffn_conv_b[l], x1, g2, norm_ffn_post[l])
```

```python
import functools
import math

import jax
import jax.numpy as jnp
from jax import lax
from jax.experimental import pallas as pl
from jax.experimental.pallas import tpu as pltpu

D_MODEL = 2048
BATCH = 2
SEQ = 4096
GRID_W = 64
CTX_LEN = 256
S_ALL = SEQ + CTX_LEN
EPS = 1e-6
RWKV_HEADS = 16
RWKV_HEAD = 64
RWKV_DIM = RWKV_HEADS * RWKV_HEAD
DECAY_LORA = 96
AAA_LORA = 96
GATE_LORA = 256
N_DIR = 2
DECAY_SCALE = math.exp(-0.5)
LNX_EPS = 64e-5
MLA_HEADS = 16
Q_LORA = 512
KV_LORA = 512
QK_NOPE = 128
QK_ROPE = 64
V_HEAD = 128
ROPE_THETA = 10000.0
ATTN_SCALE = (QK_NOPE + QK_ROPE) ** -0.5
LOG2_E = math.log2(math.e)
D_FF = 5632
RWKV_IN = 3 * RWKV_DIM + N_DIR * DECAY_LORA + N_DIR * AAA_LORA + GATE_LORA
MLA_IN = Q_LORA + KV_LORA + QK_ROPE

LANES = 128
SUBLANES = 8
LORA_PAD = LANES
VMEM_LIMIT = 56 * 1024 * 1024

C_R, C_K, C_V = 0, RWKV_DIM, 2 * RWKV_DIM
C_WL = 3 * RWKV_DIM
C_AL = C_WL + N_DIR * LORA_PAD
C_GL = C_AL + N_DIR * LORA_PAD
C_KPE = C_GL + GATE_LORA
C_PAD = C_KPE + 2 * QK_ROPE
C_CQ = C_PAD + LANES
C_CKV = C_CQ + Q_LORA
C_GATE = C_CKV + KV_LORA
P_COLS = C_GATE + 2 * D_MODEL
RWKV_COLS = C_KPE

WKV_CHUNK = 64
N_CHUNKS = S_ALL // WKV_CHUNK
LAT_CHUNKS = SEQ // WKV_CHUNK
HEAD_PAIRS = RWKV_DIM // LANES

F32 = jnp.float32
BF16 = jnp.bfloat16


def _cparams(sem):
    return pltpu.CompilerParams(dimension_semantics=sem, vmem_limit_bytes=VMEM_LIMIT)


def _bdot(a, b):
    return jnp.dot(a.astype(BF16), b.astype(BF16), preferred_element_type=F32)


def _hilo_dot(x, w):
    hi = x.astype(BF16)
    lo = (x - hi.astype(F32)).astype(BF16)
    return jnp.dot(hi, w, preferred_element_type=F32) + jnp.dot(lo, w, preferred_element_type=F32)


def _head_sum(x, ones2):
    return jnp.concatenate([_hilo_dot(x[:, c * LANES:(c + 1) * LANES], ones2) for c in range(HEAD_PAIRS)], axis=-1)


def _ada_kernel(s_ref, w_ref, b_ref, o_ref):
    w = w_ref[...]
    for r in range(s_ref.shape[0]):
        o_ref[r:r + 1, :] = jnp.sum(s_ref[r] * w, axis=0, keepdims=True) + b_ref[...]


def _ada(s, w, b, tn=1024):
    R, K = s.shape
    N = w.shape[1]
    return pl.pallas_call(
        _ada_kernel, grid=(N // tn,),
        in_specs=[pl.BlockSpec((R, K, 1), lambda j: (0, 0, 0)),
                  pl.BlockSpec((K, tn), lambda j: (0, j)),
                  pl.BlockSpec((1, tn), lambda j: (0, j))],
        out_specs=pl.BlockSpec((R, tn), lambda j: (0, j)),
        out_shape=jax.ShapeDtypeStruct((R, N), F32),
        compiler_params=_cparams(("parallel",)), name="ada",
    )(s[:, :, None], w, b[None, :])


def _norm_mod_kernel(x_ref, c_ref, g_ref, m_ref, o_ref, *, lat_tiles):
    def emit(x):
        n = x * lax.rsqrt(jnp.mean(x * x, axis=-1, keepdims=True) + EPS) * g_ref[...]
        m = m_ref[0, 0]
        o_ref[0] = (n * (1.0 + m[1:2]) + m[0:1]).astype(o_ref.dtype)

    is_latent = pl.program_id(1) < lat_tiles

    @pl.when(is_latent)
    def _():
        emit(x_ref[0])

    @pl.when(jnp.logical_not(is_latent))
    def _():
        emit(c_ref[0])


def _norm_mod(x, ctx, gain, mods, tm=256):
    B, _, D = x.shape
    assert ctx.shape[1] == tm
    lat_tiles = SEQ // tm
    return pl.pallas_call(
        functools.partial(_norm_mod_kernel, lat_tiles=lat_tiles), grid=(B, S_ALL // tm),
        in_specs=[pl.BlockSpec((1, tm, D), lambda b, i: (b, jnp.minimum(i, lat_tiles - 1), 0)),
                  pl.BlockSpec((1, tm, D), lambda b, i: (b, 0, 0)),
                  pl.BlockSpec((1, D), lambda b, i: (0, 0)),
                  pl.BlockSpec((1, 1, 2, D), lambda b, i: (b, i // lat_tiles, 0, 0))],
        out_specs=pl.BlockSpec((1, tm, D), lambda b, i: (b, i, 0)),
        out_shape=jax.ShapeDtypeStruct((B, S_ALL, D), BF16),
        compiler_params=_cparams(("parallel", "parallel")), name="norm_mod",
    )(x, ctx, gain[None, :], mods)


def _mm_kernel(a_ref, w_ref, o_ref):
    o_ref[...] = _bdot(a_ref[...], w_ref[...]).astype(o_ref.dtype)


def _matmul(a, w, tm, tn, out_dtype=F32, name="matmul"):
    M, K = a.shape
    N = w.shape[1]
    return pl.pallas_call(
        _mm_kernel, grid=(M // tm, N // tn),
        in_specs=[pl.BlockSpec((tm, K), lambda i, j: (i, 0)),
                  pl.BlockSpec((K, tn), lambda i, j: (0, j))],
        out_specs=pl.BlockSpec((tm, tn), lambda i, j: (i, j)),
        out_shape=jax.ShapeDtypeStruct((M, N), out_dtype),
        compiler_params=_cparams(("parallel", "parallel")), name=name,
    )(a, w)


def _prep_kernel(p_ref, pp_ref, pn_ref, mu_ref, w0_ref, w2_ref, a0_ref, a2_ref, kk_ref, ka_ref, rk_ref, g2_ref,
                 ones_ref, r_o, k_o, v_o, kkn_o, lw_o, a_o, g_o, bg_o):
    i = pl.program_id(1)
    x = p_ref[0]
    tm = x.shape[0]
    lat_tiles = SEQ // tm
    first = jnp.logical_or(i == 0, i == lat_tiles)
    last = jnp.logical_or(i == lat_tiles - 1, i == pl.num_programs(1) - 1)
    x_before = jnp.where(first, 0.0, pp_ref[0][SUBLANES - 1:SUBLANES])
    x_after = jnp.where(last, 0.0, pn_ref[0][0:1])
    rows = lax.broadcasted_iota(jnp.int32, (tm, 1), 0)
    prev = jnp.where(rows == 0, x_before, pltpu.roll(x, 1, 0))
    nxt = jnp.where(rows == tm - 1, x_after, pltpu.roll(x, tm - 1, 0))
    xs = x + mu_ref[...] * (0.5 * (prev + nxt) - x)

    C = RWKV_DIM
    r, k, v = xs[:, C_R:C_R + C], xs[:, C_K:C_K + C], xs[:, C_V:C_V + C]
    a_sum = None
    for d in range(N_DIR):
        wl = xs[:, C_WL + d * LORA_PAD:C_WL + (d + 1) * LORA_PAD]
        al = xs[:, C_AL + d * LORA_PAD:C_AL + (d + 1) * LORA_PAD]
        w_raw = w0_ref[d:d + 1] + _bdot(jnp.tanh(wl), w2_ref[d])
        lw_o[0, d] = -DECAY_SCALE * jax.nn.sigmoid(w_raw)
        a_d = jax.nn.sigmoid(a0_ref[d:d + 1] + _bdot(al, a2_ref[d]))
        a_o[0, d] = a_d
        a_sum = a_d if a_sum is None else a_sum + a_d
    ones2 = ones_ref[...]
    kkx = k * kk_ref[...]
    kkn_o[0] = kkx / jnp.maximum(jnp.sqrt(_head_sum(kkx * kkx, ones2)), 1e-12)
    g = _bdot(jax.nn.sigmoid(xs[:, C_GL:C_GL + GATE_LORA]), g2_ref[...])
    k_bar = k * (1.0 + (0.5 * a_sum - 1.0) * ka_ref[...])
    bonus = _head_sum(r * k_bar * rk_ref[...], ones2) * v
    r_o[0] = r
    k_o[0] = k
    v_o[0] = v
    g_o[0] = g
    bg_o[0] = bonus * g


def _prep(p, mu, w0, w2p, a0, a2p, k_k, k_a, r_k, g2, ones2, tm=256):
    B = p.shape[0]
    C = RWKV_DIM
    hb = tm // SUBLANES
    last = S_ALL // SUBLANES - 1
    row = lambda n: pl.BlockSpec((1, n), lambda b, i: (0, 0))
    tok = pl.BlockSpec((1, tm, C), lambda b, i: (b, i, 0))
    tok_dir = pl.BlockSpec((1, N_DIR, tm, C), lambda b, i: (b, 0, i, 0))
    shp = jax.ShapeDtypeStruct((B, S_ALL, C), F32)
    shp_dir = jax.ShapeDtypeStruct((B, N_DIR, S_ALL, C), F32)
    return pl.pallas_call(
        _prep_kernel, grid=(B, S_ALL // tm),
        in_specs=[pl.BlockSpec((1, tm, RWKV_COLS), lambda b, i: (b, i, 0)),
                  pl.BlockSpec((1, SUBLANES, RWKV_COLS), lambda b, i: (b, jnp.maximum(i * hb - 1, 0), 0)),
                  pl.BlockSpec((1, SUBLANES, RWKV_COLS), lambda b, i: (b, jnp.minimum((i + 1) * hb, last), 0)),
                  row(RWKV_COLS),
                  pl.BlockSpec((N_DIR, C), lambda b, i: (0, 0)),
                  pl.BlockSpec((N_DIR, LORA_PAD, C), lambda b, i: (0, 0, 0)),
                  pl.BlockSpec((N_DIR, C), lambda b, i: (0, 0)),
                  pl.BlockSpec((N_DIR, LORA_PAD, C), lambda b, i: (0, 0, 0)),
                  row(C), row(C), row(C),
                  pl.BlockSpec((GATE_LORA, C), lambda b, i: (0, 0)),
                  pl.BlockSpec((LANES, LANES), lambda b, i: (0, 0))],
        out_specs=[tok, tok, tok, tok, tok_dir, tok_dir, tok, tok],
        out_shape=[shp, shp, shp, shp, shp_dir, shp_dir, shp, shp],
        compiler_params=_cparams(("parallel", "parallel")), name="rwkv_prep",
    )(p, p, p, mu[None, :], w0, w2p, a0, a2p, k_k[None, :], k_a[None, :], r_k[None, :], g2, ones2)


def _wkv_kernel(r_ref, k_ref, v_ref, kk_ref, lw_ref, a_ref, ka_ref, y_ref, s_ref):
    d = pl.program_id(1)
    C = WKV_CHUNK
    C2 = 2 * C
    P = HEAD_PAIRS

    @pl.when(pl.program_id(2) == 0)
    def _():
        s_ref[...] = jnp.zeros_like(s_ref)

    r = r_ref[0]
    v = v_ref[0]
    kk = kk_ref[0]
    lw = lw_ref[0, 0]
    a = a_ref[0, 0]
    k = k_ref[0] * (1.0 + (a - 1.0) * ka_ref[...])
    sign = 1 - 2 * d

    trow = lax.broadcasted_iota(jnp.int32, (C, C), 0)
    tcol = lax.broadcasted_iota(jnp.int32, (C, C), 1)
    tri = ((trow - tcol) * sign >= 0).astype(BF16)
    cum = _hilo_dot_left(tri, lw)
    tot = jnp.sum(lw, axis=0, keepdims=True)

    e_cum = jnp.exp(cum)
    e_neg = jnp.exp(-cum)
    e_rem = jnp.exp(tot - cum)
    b = kk * a
    at = -kk * jnp.exp(cum - lw)
    rt = r * e_cum
    bt = b * e_neg
    kt = k * e_neg
    bdec = b * e_rem
    kdec = k * e_rem
    dec = jnp.exp(tot)

    lane = lax.broadcasted_iota(jnp.int32, (C, LANES), 1)
    head0 = lane < RWKV_HEAD
    zero = jnp.zeros((), F32)

    def expand(x):
        x = x.astype(BF16)
        zb = jnp.zeros((), BF16)
        return jnp.stack([jnp.concatenate([jnp.where(head0, x[:, j * LANES:(j + 1) * LANES], zb),
                                           jnp.where(head0, zb, x[:, j * LANES:(j + 1) * LANES])], axis=0)
                          for j in range(P)])

    def bmm(x, y):
        return jnp.einsum('hab,hbc->hac', x.astype(BF16), y.astype(BF16), preferred_element_type=F32)

    def bmm_nt(x, y):
        return jnp.einsum('han,hbn->hab', x.astype(BF16), y.astype(BF16), preferred_element_type=F32)

    def bmm_tn(x, y):
        return jnp.einsum('hca,hcb->hab', x.astype(BF16), y.astype(BF16), preferred_element_type=F32)

    AT, RT, BT, KT, BD, KD, V = (expand(t) for t in (at, rt, bt, kt, bdec, kdec, v))

    row = lax.broadcasted_iota(jnp.int32, (C2, C2), 0)
    col = lax.broadcasted_iota(jnp.int32, (C2, C2), 1)
    same = (row // C) == (col // C)
    lag = (row - col) * sign
    incl = jnp.logical_and(same, lag >= 0)[None]
    strict = jnp.logical_and(same, lag > 0)[None]

    def blk(m):
        return (row // m) == (col // m)

    G = bmm_nt(jnp.concatenate([AT, RT], axis=1), jnp.concatenate([BT, KT], axis=1))
    a_ab, a_ak = G[:, :C2, :C2], G[:, :C2, C2:]
    a_rb, a_rk = G[:, C2:, :C2], G[:, C2:, C2:]

    L = jnp.where(strict, a_ab, zero)
    Lb = jnp.where(blk(8)[None], L, zero)
    L2 = bmm(Lb, Lb)
    L4 = bmm(L2, L2)
    S1 = Lb + L2 + bmm(Lb, L2)
    Nm = S1 + L4 + bmm(L4, S1)
    for m in (8, 16, 32):
        Lo = jnp.where(jnp.logical_and(blk(2 * m), jnp.logical_not(blk(m)))[None], L, zero)
        Pm = Lo + bmm(Nm, Lo)
        Nm = Nm + Pm + bmm(Pm, Nm)

    s0 = s_ref[...]
    rhs = bmm_nt(AT, s0) + bmm(jnp.where(strict, a_ak, zero), V)
    u = rhs + bmm(Nm, rhs)
    uv = jnp.concatenate([u.astype(BF16), V], axis=1)
    y2 = bmm_nt(RT, s0) + bmm(jnp.concatenate([jnp.where(incl, a_rb, zero), jnp.where(incl, a_rk, zero)], axis=2), uv)
    y = y2[:, :C] + y2[:, C:]
    y_ref[0, 0] = jnp.concatenate([y[j] for j in range(P)], axis=-1)
    dec3 = jnp.stack([dec[:, j * LANES:(j + 1) * LANES] for j in range(P)])
    s_ref[...] = s0 * dec3 + bmm_tn(uv, jnp.concatenate([BD, KD], axis=1))


def _hilo_dot_left(w, x):
    hi = x.astype(BF16)
    lo = (x - hi.astype(F32)).astype(BF16)
    return jnp.dot(w, hi, preferred_element_type=F32) + jnp.dot(w, lo, preferred_element_type=F32)


def _wkv(r, k, v, kk, lw, a, k_a):
    B, S, Cd = r.shape
    C = WKV_CHUNK

    def cidx(d, i):
        return jnp.where(d == 0, (i + LAT_CHUNKS) % N_CHUNKS, N_CHUNKS - 1 - i)

    shared = pl.BlockSpec((1, C, Cd), lambda b, d, i: (b, cidx(d, i), 0))
    perdir = pl.BlockSpec((1, 1, C, Cd), lambda b, d, i: (b, d, cidx(d, i), 0))
    return pl.pallas_call(
        _wkv_kernel, grid=(B, N_DIR, N_CHUNKS),
        in_specs=[shared, shared, shared, shared, perdir, perdir, pl.BlockSpec((1, Cd), lambda b, d, i: (0, 0))],
        out_specs=perdir,
        out_shape=jax.ShapeDtypeStruct((B, N_DIR, S, Cd), F32),
        scratch_shapes=[pltpu.VMEM((HEAD_PAIRS, LANES, LANES), F32)],
        compiler_params=_cparams(("parallel", "parallel", "arbitrary")), name="wkv7",
    )(r, k, v, kk, lw, a, k_a[None, :])


def _rms(x, g):
    return x * lax.rsqrt(jnp.mean(x * x, axis=-1, keepdims=True) + EPS) * g


def _qproj_kernel(p_ref, g_ref, w_ref, cos_ref, sin_ref, q_ref):
    cq = _rms(p_ref[0], g_ref[...]).astype(BF16)
    cs, sn = cos_ref[...], sin_ref[...]
    for h in range(MLA_HEADS):
        y = jnp.dot(cq, w_ref[h], preferred_element_type=F32)
        rope = y[:, QK_NOPE:QK_NOPE + QK_ROPE] * cs + y[:, QK_NOPE + QK_ROPE:] * sn
        q = jnp.concatenate([y[:, :QK_NOPE], rope], axis=-1) * (ATTN_SCALE * LOG2_E)
        q_ref[0, h] = q.astype(q_ref.dtype)


def _qproj(p, gain, wq, cos, sin, tm=512):
    B = p.shape[0]
    return pl.pallas_call(
        _qproj_kernel, grid=(B, SEQ // tm),
        in_specs=[pl.BlockSpec((1, tm, Q_LORA), lambda b, i: (b, i, C_CQ // Q_LORA)),
                  pl.BlockSpec((1, Q_LORA), lambda b, i: (0, 0)),
                  pl.BlockSpec((MLA_HEADS, Q_LORA, 2 * LANES), lambda b, i: (0, 0, 0)),
                  pl.BlockSpec((tm, QK_ROPE), lambda b, i: (i, 0)),
                  pl.BlockSpec((tm, QK_ROPE), lambda b, i: (i, 0))],
        out_specs=pl.BlockSpec((1, MLA_HEADS, tm, QK_NOPE + QK_ROPE), lambda b, i: (b, 0, i, 0)),
        out_shape=jax.ShapeDtypeStruct((B, MLA_HEADS, SEQ, QK_NOPE + QK_ROPE), BF16),
        compiler_params=_cparams(("parallel", "parallel")), name="mla_q",
    )(p, gain[None, :], wq, cos, sin)


def _kvproj_kernel(p_ref, pe_ref, g_ref, w_ref, cos_ref, sin_ref, k_ref, v_ref):
    ckv = _rms(p_ref[0], g_ref[...]).astype(BF16)
    pe = pe_ref[0]
    kpe = pe[:, :QK_ROPE] * cos_ref[...] + pe[:, QK_ROPE:] * sin_ref[...]
    ones = jnp.ones((ckv.shape[0], V_HEAD), F32)
    for h in range(MLA_HEADS):
        y = jnp.dot(ckv, w_ref[h], preferred_element_type=F32)
        k_ref[0, h] = jnp.concatenate([y[:, :QK_NOPE], kpe], axis=-1).astype(k_ref.dtype)
        v_ref[0, h] = jnp.concatenate([y[:, QK_NOPE:], ones], axis=-1).astype(v_ref.dtype)


def _kvproj(p, gain, wkv, cos, sin, tm=256):
    B = p.shape[0]
    return pl.pallas_call(
        _kvproj_kernel, grid=(B, S_ALL // tm),
        in_specs=[pl.BlockSpec((1, tm, KV_LORA), lambda b, i: (b, i, C_CKV // KV_LORA)),
                  pl.BlockSpec((1, tm, 2 * QK_ROPE), lambda b, i: (b, i, C_KPE // (2 * QK_ROPE))),
                  pl.BlockSpec((1, KV_LORA), lambda b, i: (0, 0)),
                  pl.BlockSpec((MLA_HEADS, KV_LORA, 2 * LANES), lambda b, i: (0, 0, 0)),
                  pl.BlockSpec((tm, QK_ROPE), lambda b, i: (i, 0)),
                  pl.BlockSpec((tm, QK_ROPE), lambda b, i: (i, 0))],
        out_specs=[pl.BlockSpec((1, MLA_HEADS, tm, QK_NOPE + QK_ROPE), lambda b, i: (b, 0, i, 0)),
                   pl.BlockSpec((1, MLA_HEADS, tm, 2 * V_HEAD), lambda b, i: (b, 0, i, 0))],
        out_shape=[jax.ShapeDtypeStruct((B, MLA_HEADS, S_ALL, QK_NOPE + QK_ROPE), BF16),
                   jax.ShapeDtypeStruct((B, MLA_HEADS, S_ALL, 2 * V_HEAD), BF16)],
        compiler_params=_cparams(("parallel", "parallel")), name="mla_kv",
    )(p, p, gain[None, :], wkv, cos, sin)


def _attn_kernel(q_ref, k_ref, v_ref, o_ref, sa, sb, pa, pb, acc, m_ref, al_ref, *, tk):
    q = q_ref[0, 0]
    n_kv = k_ref.shape[2] // tk

    def scores(j):
        kj = k_ref[0, 0, pl.ds(pl.multiple_of(j * tk, tk), tk), :]
        return lax.dot_general(q, kj, (((1,), (1,)), ((), ())), preferred_element_type=F32)

    def accumulate(p_ref, j):
        vj = v_ref[0, 0, pl.ds(pl.multiple_of(j * tk, tk), tk), :]
        pv = jnp.dot(p_ref[...], vj, preferred_element_type=F32)
        al = al_ref[...]
        for c in range(0, 2 * V_HEAD, LANES):
            acc[:, c:c + LANES] = al * acc[:, c:c + LANES] + pv[:, c:c + LANES]

    def softmax(s_ref, p_ref):
        s = s_ref[...]
        m_old = m_ref[...]
        m_new = jnp.maximum(m_old, jnp.max(s, axis=-1, keepdims=True))
        al_ref[...] = jnp.exp2(m_old - m_new)
        m_ref[...] = m_new
        for c in range(0, tk, LANES):
            p_ref[:, c:c + LANES] = jnp.exp2(s[:, c:c + LANES] - m_new).astype(p_ref.dtype)

    s_bufs, p_bufs = (sa, sb), (pa, pb)
    sa[...] = scores(0)
    acc[...] = jnp.zeros_like(acc)
    m_ref[...] = jnp.full_like(m_ref, -jnp.inf)
    for j in range(n_kv):
        if j + 1 < n_kv:
            s_bufs[(j + 1) % 2][...] = scores(j + 1)
        if j >= 1:
            accumulate(p_bufs[(j - 1) % 2], j - 1)
        softmax(s_bufs[j % 2], p_bufs[j % 2])
    accumulate(p_bufs[(n_kv - 1) % 2], n_kv - 1)
    o_ref[0] = (acc[:, :V_HEAD] / acc[:, V_HEAD:]).astype(o_ref.dtype)


def _attention(q, k, v, tq=512, tk=256):
    B, H, T, Dk = q.shape
    S = k.shape[2]
    return pl.pallas_call(
        functools.partial(_attn_kernel, tk=tk), grid=(B, H, T // tq),
        in_specs=[pl.BlockSpec((1, 1, tq, Dk), lambda b, h, i: (b, h, i, 0)),
                  pl.BlockSpec((1, 1, S, Dk), lambda b, h, i: (b, h, 0, 0)),
                  pl.BlockSpec((1, 1, S, 2 * V_HEAD), lambda b, h, i: (b, h, 0, 0))],
        out_specs=pl.BlockSpec((1, tq, V_HEAD), lambda b, h, i: (b, i, h)),
        out_shape=jax.ShapeDtypeStruct((B, T, H * V_HEAD), BF16),
        scratch_shapes=[pltpu.VMEM((tq, tk), F32), pltpu.VMEM((tq, tk), F32),
                        pltpu.VMEM((tq, tk), BF16), pltpu.VMEM((tq, tk), BF16),
                        pltpu.VMEM((tq, 2 * V_HEAD), F32), pltpu.VMEM((tq, LANES), F32), pltpu.VMEM((tq, LANES), F32)],
        compiler_params=_cparams(("parallel", "parallel", "parallel")), name="mla_attn",
    )(q, k, v)


def _merge_kernel(y0_ref, y1_ref, g_ref, bg_ref, lnw_ref, lnb_ref, avg_ref, o_ref, wr_ref, wm_ref, pr_ref, pm_ref,
                  br_ref, bm_ref, out_ref, yr_ref):
    @pl.when(pl.program_id(2) == 0)
    def _():
        avg = avg_ref[...]
        y = y0_ref[0, 0] + y1_ref[0, 0]
        yc = y - _head_sum(y, avg)
        yn = yc * lax.rsqrt(_head_sum(yc * yc, avg) + LNX_EPS) * lnw_ref[...] + lnb_ref[...]
        yr_ref[...] = (yn * g_ref[0] + bg_ref[0]).astype(yr_ref.dtype)

    gr = jax.nn.sigmoid(pr_ref[0] + br_ref[...])
    gm = jax.nn.sigmoid(pm_ref[0] + bm_ref[...])
    out_ref[0] = (gr * jnp.dot(yr_ref[...], wr_ref[...], preferred_element_type=F32)
                  + gm * jnp.dot(o_ref[0], wm_ref[...], preferred_element_type=F32)).astype(out_ref.dtype)


def _merge(y_dir, g, bg, lnx_w, lnx_b, avg2, o, w_rp, w_mp, p, gate_b, tm=512, tn=512):
    B = o.shape[0]
    D = D_MODEL
    C = RWKV_DIM
    gcol = C_GATE // tn
    tok = pl.BlockSpec((1, tm, C), lambda b, i, j: (b, i, 0))
    row = lambda n: pl.BlockSpec((1, n), lambda b, i, j: (0, 0))
    return pl.pallas_call(
        _merge_kernel, grid=(B, SEQ // tm, D // tn),
        in_specs=[pl.BlockSpec((1, 1, tm, C), lambda b, i, j: (b, 0, i, 0)),
                  pl.BlockSpec((1, 1, tm, C), lambda b, i, j: (b, 1, i, 0)),
                  tok, tok, row(C), row(C),
                  pl.BlockSpec((LANES, LANES), lambda b, i, j: (0, 0)),
                  pl.BlockSpec((1, tm, D), lambda b, i, j: (b, i, 0)),
                  pl.BlockSpec((C, tn), lambda b, i, j: (0, j)),
                  pl.BlockSpec((D, tn), lambda b, i, j: (0, j)),
                  pl.BlockSpec((1, tm, tn), lambda b, i, j: (b, i, gcol + j)),
                  pl.BlockSpec((1, tm, tn), lambda b, i, j: (b, i, gcol + D // tn + j)),
                  pl.BlockSpec((1, tn), lambda b, i, j: (0, j)),
                  pl.BlockSpec((1, tn), lambda b, i, j: (0, D // tn + j))],
        out_specs=pl.BlockSpec((1, tm, tn), lambda b, i, j: (b, i, j)),
        out_shape=jax.ShapeDtypeStruct((B, SEQ, D), BF16),
        scratch_shapes=[pltpu.VMEM((tm, C), BF16)],
        compiler_params=_cparams(("parallel", "parallel", "arbitrary")), name="merge",
    )(y_dir, y_dir, g, bg, lnx_w[None, :], lnx_b[None, :], avg2, o, w_rp, w_mp, p, p,
      gate_b[None, :], gate_b[None, :])


def _outproj_kernel(m_ref, w_ref, x_ref, gpost_ref, gate_ref, gpre_ref, mod_ref, x1_ref, h_ref):
    out = jnp.dot(m_ref[0], w_ref[...], preferred_element_type=F32)
    x1 = x_ref[0] + gate_ref[0] * _rms(out, gpost_ref[...])
    x1_ref[0] = x1
    mod = mod_ref[0]
    h_ref[0] = (_rms(x1, gpre_ref[...]) * (1.0 + mod[1:2]) + mod[0:1]).astype(h_ref.dtype)


def _outproj(merged, w_out, x, g_post, gate, g_pre, mod2, tm=256):
    B = x.shape[0]
    D = D_MODEL
    return pl.pallas_call(
        _outproj_kernel, grid=(B, SEQ // tm),
        in_specs=[pl.BlockSpec((1, tm, D), lambda b, i: (b, i, 0)),
                  pl.BlockSpec((D, D), lambda b, i: (0, 0)),
                  pl.BlockSpec((1, tm, D), lambda b, i: (b, i, 0)),
                  pl.BlockSpec((1, D), lambda b, i: (0, 0)),
                  pl.BlockSpec((1, 1, D), lambda b, i: (b, 0, 0)),
                  pl.BlockSpec((1, D), lambda b, i: (0, 0)),
                  pl.BlockSpec((1, 2, D), lambda b, i: (b, 0, 0))],
        out_specs=[pl.BlockSpec((1, tm, D), lambda b, i: (b, i, 0)),
                   pl.BlockSpec((1, tm, D), lambda b, i: (b, i, 0))],
        out_shape=[jax.ShapeDtypeStruct((B, SEQ, D), F32),
                   jax.ShapeDtypeStruct((B, SEQ, D), BF16)],
        compiler_params=_cparams(("parallel", "parallel")), name="outproj",
    )(merged, w_out, x, g_post[None, :], gate[:, None, :], g_pre[None, :], mod2)


HALO = 16


def _ffn_kernel(h_ref, hp_ref, hn_ref, wg_ref, wv_ref, wd_ref, cw_ref, cb_ref, x_ref, gate_ref, gpost_ref,
                o_ref, acc_ref):
    i = pl.program_id(1)
    f = pl.program_id(2)
    tm = h_ref.shape[1]

    @pl.when(f == 0)
    def _():
        acc_ref[...] = jnp.zeros_like(acc_ref)

    h = h_ref[0]
    wg = wg_ref[...]
    g_ext = jnp.dot(jnp.concatenate([hp_ref[0], h, hn_ref[0]], axis=0), wg, preferred_element_type=F32)
    g = g_ext[HALO:HALO + tm]
    g_prev = g_ext[HALO - 1:HALO]
    g_next = g_ext[HALO + tm:HALO + tm + 1]
    g_prev = jnp.where(i == 0, 0.0, g_prev)
    g_next = jnp.where(i == pl.num_programs(1) - 1, 0.0, g_next)
    rows = lax.broadcasted_iota(jnp.int32, g.shape, 0)
    up = jnp.where(rows == 0, g_prev, pltpu.roll(g, 1, 0))
    dn = jnp.where(rows == tm - 1, g_next, pltpu.roll(g, tm - 1, 0))
    cw = cw_ref[...]
    u = cb_ref[...] + up * cw[0:1] + g * cw[1:2] + dn * cw[2:3]
    val = jnp.dot(h, wv_ref[...], preferred_element_type=F32)
    act = jax.nn.gelu(u, approximate=True) * val
    acc_ref[...] += jnp.dot(act.astype(BF16), wd_ref[...], preferred_element_type=F32)

    @pl.when(f == pl.num_programs(2) - 1)
    def _():
        o_ref[0] = x_ref[0] + gate_ref[0] * _rms(acc_ref[...], gpost_ref[...])


def _ffn(h, wg, wv, wd, cw, cb, x1, gate, g_post, tm=512, tf=512):
    B = h.shape[0]
    D = D_MODEL
    hb = tm // HALO
    last = SEQ // HALO - 1
    return pl.pallas_call(
        _ffn_kernel, grid=(B, SEQ // tm, D_FF // tf),
        in_specs=[pl.BlockSpec((1, tm, D), lambda b, i, f: (b, i, 0)),
                  pl.BlockSpec((1, HALO, D), lambda b, i, f: (b, jnp.maximum(i * hb - 1, 0), 0)),
                  pl.BlockSpec((1, HALO, D), lambda b, i, f: (b, jnp.minimum((i + 1) * hb, last), 0)),
                  pl.BlockSpec((D, tf), lambda b, i, f: (0, f)),
                  pl.BlockSpec((D, tf), lambda b, i, f: (0, f)),
                  pl.BlockSpec((tf, D), lambda b, i, f: (f, 0)),
                  pl.BlockSpec((3, tf), lambda b, i, f: (0, f)),
                  pl.BlockSpec((1, tf), lambda b, i, f: (0, f)),
                  pl.BlockSpec((1, tm, D), lambda b, i, f: (b, i, 0)),
                  pl.BlockSpec((1, 1, D), lambda b, i, f: (b, 0, 0)),
                  pl.BlockSpec((1, D), lambda b, i, f: (0, 0))],
        out_specs=pl.BlockSpec((1, tm, D), lambda b, i, f: (b, i, 0)),
        out_shape=jax.ShapeDtypeStruct((B, SEQ, D), F32),
        scratch_shapes=[pltpu.VMEM((tm, D), F32)],
        compiler_params=_cparams(("parallel", "parallel", "arbitrary")), name="convffn",
    )(h, h, h, wg, wv, wd, cw, cb[None, :], x1, gate[:, None, :], g_post[None, :])


def _rope_partner(w):
    q = QK_ROPE // 4
    return jnp.concatenate([-w[..., q:2 * q], w[..., :q], -w[..., 3 * q:], w[..., 2 * q:3 * q]], axis=-1)


def _pad_lora_cols(w):
    K = w.shape[0]
    w = w.reshape(K, N_DIR, -1)
    return jnp.pad(w, ((0, 0), (0, 0), (0, LORA_PAD - w.shape[-1]))).reshape(K, N_DIR * LORA_PAD)


def _pack_w_in(w):
    o = 3 * RWKV_DIM
    wl = w[:, o:o + N_DIR * DECAY_LORA]
    al = w[:, o + N_DIR * DECAY_LORA:o + N_DIR * (DECAY_LORA + AAA_LORA)]
    gl = w[:, o + N_DIR * (DECAY_LORA + AAA_LORA):RWKV_IN]
    m = w[:, RWKV_IN:RWKV_IN + MLA_IN]
    kpe = m[:, Q_LORA + KV_LORA:]
    return jnp.concatenate(
        [w[:, :o], _pad_lora_cols(wl), _pad_lora_cols(al), gl, kpe, _rope_partner(kpe),
         jnp.zeros((w.shape[0], LANES), w.dtype), m[:, :Q_LORA], m[:, Q_LORA:Q_LORA + KV_LORA],
         w[:, RWKV_IN + MLA_IN:]], axis=1)


def _pack_mu(mu):
    o = 3 * RWKV_DIM
    wl = mu[None, o:o + N_DIR * DECAY_LORA]
    al = mu[None, o + N_DIR * DECAY_LORA:o + N_DIR * (DECAY_LORA + AAA_LORA)]
    return jnp.concatenate([mu[:o], _pad_lora_cols(wl)[0], _pad_lora_cols(al)[0],
                            mu[o + N_DIR * (DECAY_LORA + AAA_LORA):RWKV_IN]])


def _rope_tables():
    rows = SEQ // GRID_W
    row = jnp.repeat(jnp.arange(rows), GRID_W).astype(F32)
    col = jnp.tile(jnp.arange(GRID_W), rows).astype(F32)
    half = QK_ROPE // 2
    freqs = ROPE_THETA ** (-jnp.arange(0, half, 2, dtype=F32) / half)
    ar, ac = row[:, None] * freqs, col[:, None] * freqs
    cos = jnp.concatenate([jnp.cos(ar), jnp.cos(ar), jnp.cos(ac), jnp.cos(ac)], axis=-1)
    sin = jnp.concatenate([jnp.sin(ar), jnp.sin(ar), jnp.sin(ac), jnp.sin(ac)], axis=-1)
    cos_all = jnp.concatenate([cos, jnp.ones((CTX_LEN, QK_ROPE), F32)], axis=0)
    sin_all = jnp.concatenate([sin, jnp.zeros((CTX_LEN, QK_ROPE), F32)], axis=0)
    return cos_all, sin_all


def kernel(x, c, ctx, c_ctx, w_ada, b_ada, norm_mix_pre, norm_mix_post, norm_ffn_pre, norm_ffn_post, w_in, rwkv_mu, rwkv_w0, rwkv_w2, rwkv_a0, rwkv_a2, rwkv_k_k, rwkv_k_a, rwkv_r_k, rwkv_lnx_w, rwkv_lnx_b, rwkv_g2, w_rwkv_proj, mla_q_norm, mla_kv_norm, mla_w_uq, mla_w_ukv, w_mla_proj, gate_b, w_out, ffn_w_gate, ffn_w_val, ffn_conv_w, ffn_conv_b, ffn_w_down):
    B = x.shape[0]
    D = D_MODEL
    C = RWKV_DIM
    l = 0

    s = jnp.concatenate([jax.nn.silu(c), jax.nn.silu(c_ctx)[None, :]], axis=0)
    mods = _ada(s, w_ada[l], b_ada[l])
    lat = mods[:B].reshape(B, 6, D)
    cm = mods[B].reshape(6, D)
    sh1, sc1, g1, sh2, sc2, g2 = [lat[:, j] for j in range(6)]
    mods1 = jnp.stack([jnp.stack([sh1, sc1], axis=1),
                       jnp.broadcast_to(jnp.stack([cm[0], cm[1]])[None], (B, 2, D))], axis=1)
    mods2 = jnp.stack([sh2, sc2], axis=1)

    h = _norm_mod(x, ctx, norm_mix_pre[l], mods1)
    w_in_p = _pack_w_in(w_in[l]).astype(BF16)
    p = _matmul(h.reshape(B * S_ALL, D), w_in_p, tm=1088, tn=1024, name="w_in").reshape(B, S_ALL, P_COLS)

    pad_rows = lambda w: jnp.pad(w, ((0, 0), (0, LORA_PAD - w.shape[1]), (0, 0))).astype(BF16)
    lane_head = jnp.arange(LANES) // RWKV_HEAD
    ones2 = (lane_head[:, None] == lane_head[None, :]).astype(BF16)
    avg2 = ones2 * (1.0 / RWKV_HEAD)
    r, k, v, kk, lw, a, g, bg = _prep(p, _pack_mu(rwkv_mu[l]), rwkv_w0[l], pad_rows(rwkv_w2[l]), rwkv_a0[l],
                                      pad_rows(rwkv_a2[l]), rwkv_k_k[l], rwkv_k_a[l], rwkv_r_k[l].reshape(C),
                                      rwkv_g2[l].astype(BF16), ones2)
    y_dir = _wkv(r, k, v, kk, lw, a, rwkv_k_a[l])

    cos, sin = _rope_tables()
    uq = mla_w_uq[l].reshape(Q_LORA, MLA_HEADS, QK_NOPE + QK_ROPE)
    wq = jnp.concatenate([uq, _rope_partner(uq[..., QK_NOPE:])], axis=-1).transpose(1, 0, 2).astype(BF16)
    wkv = mla_w_ukv[l].reshape(KV_LORA, MLA_HEADS, QK_NOPE + V_HEAD).transpose(1, 0, 2).astype(BF16)
    q = _qproj(p, mla_q_norm[l], wq, cos[:SEQ], sin[:SEQ])
    k_all, v_all = _kvproj(p, mla_kv_norm[l], wkv, cos, sin)
    o = _attention(q, k_all, v_all)

    merged = _merge(y_dir, g, bg, rwkv_lnx_w[l], rwkv_lnx_b[l], avg2, o, w_rwkv_proj[l].astype(BF16),
                    w_mla_proj[l].astype(BF16), p, gate_b[l])
    x1, h2 = _outproj(merged, w_out[l].astype(BF16), x, norm_mix_post[l], g1, norm_ffn_pre[l], mods2)
    return _ffn(h2, ffn_w_gate[l].astype(BF16), ffn_w_val[l].astype(BF16), ffn_w_down[l].astype(BF16),
                ffn_conv_w[l], ffn_conv_b[l], x1, g2, norm_ffn_post[l])
```

```python
import functools
import math

import jax
import jax.numpy as jnp
from jax import lax
from jax.experimental import pallas as pl
from jax.experimental.pallas import tpu as pltpu

D_MODEL = 2048
BATCH = 2
SEQ = 4096
GRID_W = 64
CTX_LEN = 256
S_ALL = SEQ + CTX_LEN
EPS = 1e-6
RWKV_HEADS = 16
RWKV_HEAD = 64
RWKV_DIM = RWKV_HEADS * RWKV_HEAD
DECAY_LORA = 96
AAA_LORA = 96
GATE_LORA = 256
N_DIR = 2
DECAY_SCALE = math.exp(-0.5)
LNX_EPS = 64e-5
MLA_HEADS = 16
Q_LORA = 512
KV_LORA = 512
QK_NOPE = 128
QK_ROPE = 64
V_HEAD = 128
ROPE_THETA = 10000.0
ATTN_SCALE = (QK_NOPE + QK_ROPE) ** -0.5
LOG2_E = math.log2(math.e)
D_FF = 5632
RWKV_IN = 3 * RWKV_DIM + N_DIR * DECAY_LORA + N_DIR * AAA_LORA + GATE_LORA
MLA_IN = Q_LORA + KV_LORA + QK_ROPE

LANES = 128
SUBLANES = 8
VMEM_LIMIT = 56 * 1024 * 1024

C_R, C_K, C_V = 0, RWKV_DIM, 2 * RWKV_DIM
C_WL = 3 * RWKV_DIM
C_AL = C_WL + N_DIR * DECAY_LORA
C_GL = C_AL + N_DIR * AAA_LORA
RWKV_COLS = C_GL + GATE_LORA
C_KPE = RWKV_COLS
C_CQ = -(-(C_KPE + 2 * QK_ROPE) // Q_LORA) * Q_LORA
C_CKV = C_CQ + Q_LORA
P_COLS = C_CKV + KV_LORA
LORA_WIN = 2 * LANES
C_WL_WIN = C_WL // LANES * LANES
C_AL_WIN = C_AL // LANES * LANES

WKV_CHUNK = 64
N_CHUNKS = S_ALL // WKV_CHUNK
LAT_CHUNKS = SEQ // WKV_CHUNK
HEAD_PAIRS = RWKV_DIM // LANES

F32 = jnp.float32
BF16 = jnp.bfloat16


def _cparams(sem):
    return pltpu.CompilerParams(dimension_semantics=sem, vmem_limit_bytes=VMEM_LIMIT)


def _bdot(a, b):
    return jnp.dot(a.astype(BF16), b.astype(BF16), preferred_element_type=F32)


def _hilo_dot(x, w):
    hi = x.astype(BF16)
    lo = (x - hi.astype(F32)).astype(BF16)
    return jnp.dot(hi, w, preferred_element_type=F32) + jnp.dot(lo, w, preferred_element_type=F32)


def _head_sum(x, ones2):
    return jnp.concatenate([_hilo_dot(x[:, c * LANES:(c + 1) * LANES], ones2) for c in range(HEAD_PAIRS)], axis=-1)


def _ada_kernel(s_ref, w_ref, b_ref, o_ref):
    w = w_ref[...]
    for r in range(s_ref.shape[0]):
        o_ref[r:r + 1, :] = jnp.sum(s_ref[r] * w, axis=0, keepdims=True) + b_ref[...]


def _ada(s, w, b, tn=1024):
    R, K = s.shape
    N = w.shape[1]
    return pl.pallas_call(
        _ada_kernel, grid=(N // tn,),
        in_specs=[pl.BlockSpec((R, K, 1), lambda j: (0, 0, 0)),
                  pl.BlockSpec((K, tn), lambda j: (0, j)),
                  pl.BlockSpec((1, tn), lambda j: (0, j))],
        out_specs=pl.BlockSpec((R, tn), lambda j: (0, j)),
        out_shape=jax.ShapeDtypeStruct((R, N), F32),
        compiler_params=_cparams(("parallel",)), name="ada",
    )(s[:, :, None], w, b[None, :])


def _norm_mod_kernel(x_ref, c_ref, g_ref, m_ref, o_ref, *, lat_tiles):
    def emit(x):
        n = x * lax.rsqrt(jnp.mean(x * x, axis=-1, keepdims=True) + EPS) * g_ref[...]
        m = m_ref[0, 0]
        o_ref[0] = (n * (1.0 + m[1:2]) + m[0:1]).astype(o_ref.dtype)

    is_latent = pl.program_id(1) < lat_tiles

    @pl.when(is_latent)
    def _():
        emit(x_ref[0])

    @pl.when(jnp.logical_not(is_latent))
    def _():
        emit(c_ref[0])


def _norm_mod(x, ctx, gain, mods, tm=256):
    B, _, D = x.shape
    assert ctx.shape[1] == tm
    lat_tiles = SEQ // tm
    return pl.pallas_call(
        functools.partial(_norm_mod_kernel, lat_tiles=lat_tiles), grid=(B, S_ALL // tm),
        in_specs=[pl.BlockSpec((1, tm, D), lambda b, i: (b, jnp.minimum(i, lat_tiles - 1), 0)),
                  pl.BlockSpec((1, tm, D), lambda b, i: (b, 0, 0)),
                  pl.BlockSpec((1, D), lambda b, i: (0, 0)),
                  pl.BlockSpec((1, 1, 2, D), lambda b, i: (b, i // lat_tiles, 0, 0))],
        out_specs=pl.BlockSpec((1, tm, D), lambda b, i: (b, i, 0)),
        out_shape=jax.ShapeDtypeStruct((B, S_ALL, D), BF16),
        compiler_params=_cparams(("parallel", "parallel")), name="norm_mod",
    )(x, ctx, gain[None, :], mods)


def _mm_kernel(a_ref, w_ref, o_ref):
    o_ref[...] = _bdot(a_ref[...], w_ref[...]).astype(o_ref.dtype)


def _matmul(a, w, tm, tn, out_dtype=F32, name="matmul"):
    M, K = a.shape
    N = w.shape[1]
    return pl.pallas_call(
        _mm_kernel, grid=(M // tm, N // tn),
        in_specs=[pl.BlockSpec((tm, K), lambda i, j: (i, 0)),
                  pl.BlockSpec((K, tn), lambda i, j: (0, j))],
        out_specs=pl.BlockSpec((tm, tn), lambda i, j: (i, j)),
        out_shape=jax.ShapeDtypeStruct((M, N), out_dtype),
        compiler_params=_cparams(("parallel", "parallel")), name=name,
    )(a, w)


def _prep_kernel(p_ref, pp_ref, pn_ref, mu_ref, w0_ref, w2_ref, a0_ref, a2_ref, kk_ref, ka_ref, rk_ref, g2_ref,
                 ones_ref, r_o, k_o, v_o, kkn_o, lw_o, a_o, g_o, bg_o):
    i = pl.program_id(1)
    x = p_ref[0]
    tm = x.shape[0]
    lat_tiles = SEQ // tm
    first = jnp.logical_or(i == 0, i == lat_tiles)
    last = jnp.logical_or(i == lat_tiles - 1, i == pl.num_programs(1) - 1)
    x_before = jnp.where(first, 0.0, pp_ref[0][SUBLANES - 1:SUBLANES])
    x_after = jnp.where(last, 0.0, pn_ref[0][0:1])
    rows = lax.broadcasted_iota(jnp.int32, (tm, 1), 0)
    prev = jnp.where(rows == 0, x_before, pltpu.roll(x, 1, 0))
    nxt = jnp.where(rows == tm - 1, x_after, pltpu.roll(x, tm - 1, 0))
    xs = x + mu_ref[...] * (0.5 * (prev + nxt) - x)

    C = RWKV_DIM
    r, k, v = xs[:, C_R:C_R + C], xs[:, C_K:C_K + C], xs[:, C_V:C_V + C]
    w_lora = _bdot(jnp.tanh(xs[:, C_WL_WIN:C_WL_WIN + LORA_WIN]), w2_ref[...])
    a_lora = _bdot(xs[:, C_AL_WIN:C_AL_WIN + LORA_WIN], a2_ref[...])
    a_sum = None
    for d in range(N_DIR):
        w_raw = w0_ref[d:d + 1] + w_lora[:, d * C:(d + 1) * C]
        lw_o[0, d] = -DECAY_SCALE * jax.nn.sigmoid(w_raw)
        a_d = jax.nn.sigmoid(a0_ref[d:d + 1] + a_lora[:, d * C:(d + 1) * C])
        a_o[0, d] = a_d
        a_sum = a_d if a_sum is None else a_sum + a_d
    ones2 = ones_ref[...]
    kkx = k * kk_ref[...]
    kkn_o[0] = kkx / jnp.maximum(jnp.sqrt(_head_sum(kkx * kkx, ones2)), 1e-12)
    g = _bdot(jax.nn.sigmoid(xs[:, C_GL:C_GL + GATE_LORA]), g2_ref[...])
    k_bar = k * (1.0 + (0.5 * a_sum - 1.0) * ka_ref[...])
    bonus = _head_sum(r * k_bar * rk_ref[...], ones2) * v
    r_o[0] = r
    k_o[0] = k
    v_o[0] = v
    g_o[0] = g.astype(g_o.dtype)
    bg_o[0] = (bonus * g).astype(bg_o.dtype)


def _prep(p, mu, w0, w2p, a0, a2p, k_k, k_a, r_k, g2, ones2, tm=256):
    B = p.shape[0]
    C = RWKV_DIM
    hb = tm // SUBLANES
    last = S_ALL // SUBLANES - 1
    row = lambda n: pl.BlockSpec((1, n), lambda b, i: (0, 0))
    tok = pl.BlockSpec((1, tm, C), lambda b, i: (b, i, 0))
    tok_dir = pl.BlockSpec((1, N_DIR, tm, C), lambda b, i: (b, 0, i, 0))
    shp = jax.ShapeDtypeStruct((B, S_ALL, C), F32)
    shp_dir = jax.ShapeDtypeStruct((B, N_DIR, S_ALL, C), F32)
    shp_b = jax.ShapeDtypeStruct((B, S_ALL, C), BF16)
    return pl.pallas_call(
        _prep_kernel, grid=(B, S_ALL // tm),
        in_specs=[pl.BlockSpec((1, tm, RWKV_COLS), lambda b, i: (b, i, 0)),
                  pl.BlockSpec((1, SUBLANES, RWKV_COLS), lambda b, i: (b, jnp.maximum(i * hb - 1, 0), 0)),
                  pl.BlockSpec((1, SUBLANES, RWKV_COLS), lambda b, i: (b, jnp.minimum((i + 1) * hb, last), 0)),
                  row(RWKV_COLS),
                  pl.BlockSpec((N_DIR, C), lambda b, i: (0, 0)),
                  pl.BlockSpec((LORA_WIN, N_DIR * C), lambda b, i: (0, 0)),
                  pl.BlockSpec((N_DIR, C), lambda b, i: (0, 0)),
                  pl.BlockSpec((LORA_WIN, N_DIR * C), lambda b, i: (0, 0)),
                  row(C), row(C), row(C),
                  pl.BlockSpec((GATE_LORA, C), lambda b, i: (0, 0)),
                  pl.BlockSpec((LANES, LANES), lambda b, i: (0, 0))],
        out_specs=[tok, tok, tok, tok, tok_dir, tok_dir, tok, tok],
        out_shape=[shp, shp, shp, shp, shp_dir, shp_dir, shp_b, shp_b],
        compiler_params=_cparams(("parallel", "parallel")), name="rwkv_prep",
    )(p, p, p, mu[None, :], w0, w2p, a0, a2p, k_k[None, :], k_a[None, :], r_k[None, :], g2, ones2)


def _wkv_kernel(r_ref, k_ref, v_ref, kk_ref, lw_ref, a_ref, ka_ref, y_ref, s_ref):
    d = pl.program_id(1)
    C = WKV_CHUNK
    C2 = 2 * C
    P = HEAD_PAIRS

    @pl.when(pl.program_id(2) == 0)
    def _():
        s_ref[...] = jnp.zeros_like(s_ref)

    r = r_ref[0]
    v = v_ref[0]
    kk = kk_ref[0]
    lw = lw_ref[0, 0]
    a = a_ref[0, 0]
    k = k_ref[0] * (1.0 + (a - 1.0) * ka_ref[...])
    sign = 1 - 2 * d

    trow = lax.broadcasted_iota(jnp.int32, (C, C), 0)
    tcol = lax.broadcasted_iota(jnp.int32, (C, C), 1)
    tri = ((trow - tcol) * sign >= 0).astype(BF16)
    cum = _hilo_dot_left(tri, lw)
    tot = jnp.sum(lw, axis=0, keepdims=True)

    e_cum = jnp.exp(cum)
    e_neg = jnp.exp(-cum)
    e_rem = jnp.exp(tot - cum)
    b = kk * a
    at = -kk * jnp.exp(cum - lw)
    rt = r * e_cum
    bt = b * e_neg
    kt = k * e_neg
    bdec = b * e_rem
    kdec = k * e_rem
    dec = jnp.exp(tot)

    lane = lax.broadcasted_iota(jnp.int32, (C, LANES), 1)
    head0 = lane < RWKV_HEAD
    zero = jnp.zeros((), F32)

    def expand(x):
        x = x.astype(BF16)
        zb = jnp.zeros((), BF16)
        return jnp.stack([jnp.concatenate([jnp.where(head0, x[:, j * LANES:(j + 1) * LANES], zb),
                                           jnp.where(head0, zb, x[:, j * LANES:(j + 1) * LANES])], axis=0)
                          for j in range(P)])

    def bmm(x, y):
        return jnp.einsum('hab,hbc->hac', x.astype(BF16), y.astype(BF16), preferred_element_type=F32)

    def bmm_nt(x, y):
        return jnp.einsum('han,hbn->hab', x.astype(BF16), y.astype(BF16), preferred_element_type=F32)

    def bmm_tn(x, y):
        return jnp.einsum('hca,hcb->hab', x.astype(BF16), y.astype(BF16), preferred_element_type=F32)

    AT, RT, BT, KT, BD, KD, V = (expand(t) for t in (at, rt, bt, kt, bdec, kdec, v))

    row = lax.broadcasted_iota(jnp.int32, (C2, C2), 0)
    col = lax.broadcasted_iota(jnp.int32, (C2, C2), 1)
    same = (row // C) == (col // C)
    lag = (row - col) * sign
    incl = jnp.logical_and(same, lag >= 0)[None]
    strict = jnp.logical_and(same, lag > 0)[None]

    def blk(m):
        return (row // m) == (col // m)

    G = bmm_nt(jnp.concatenate([AT, RT], axis=1), jnp.concatenate([BT, KT], axis=1))
    a_ab, a_ak = G[:, :C2, :C2], G[:, :C2, C2:]
    a_rb, a_rk = G[:, C2:, :C2], G[:, C2:, C2:]

    L = jnp.where(strict, a_ab, zero)
    Lb = jnp.where(blk(8)[None], L, zero)
    L2 = bmm(Lb, Lb)
    L4 = bmm(L2, L2)
    S1 = Lb + L2 + bmm(Lb, L2)
    Nm = S1 + L4 + bmm(L4, S1)
    for m in (8, 16, 32):
        Lo = jnp.where(jnp.logical_and(blk(2 * m), jnp.logical_not(blk(m)))[None], L, zero)
        Pm = Lo + bmm(Nm, Lo)
        Nm = Nm + Pm + bmm(Pm, Nm)

    s0 = s_ref[...]
    rhs = bmm_nt(AT, s0) + bmm(jnp.where(strict, a_ak, zero), V)
    u = rhs + bmm(Nm, rhs)
    uv = jnp.concatenate([u.astype(BF16), V], axis=1)
    y2 = bmm_nt(RT, s0) + bmm(jnp.concatenate([jnp.where(incl, a_rb, zero), jnp.where(incl, a_rk, zero)], axis=2), uv)
    y = y2[:, :C] + y2[:, C:]
    y_ref[0, 0] = jnp.concatenate([y[j] for j in range(P)], axis=-1)
    dec3 = jnp.stack([dec[:, j * LANES:(j + 1) * LANES] for j in range(P)])
    s_ref[...] = s0 * dec3 + bmm_tn(uv, jnp.concatenate([BD, KD], axis=1))


def _hilo_dot_left(w, x):
    hi = x.astype(BF16)
    lo = (x - hi.astype(F32)).astype(BF16)
    return jnp.dot(w, hi, preferred_element_type=F32) + jnp.dot(w, lo, preferred_element_type=F32)


def _wkv(r, k, v, kk, lw, a, k_a):
    B, S, Cd = r.shape
    C = WKV_CHUNK

    def cidx(d, i):
        return jnp.where(d == 0, (i + LAT_CHUNKS) % N_CHUNKS, N_CHUNKS - 1 - i)

    shared = pl.BlockSpec((1, C, Cd), lambda b, d, i: (b, cidx(d, i), 0))
    perdir = pl.BlockSpec((1, 1, C, Cd), lambda b, d, i: (b, d, cidx(d, i), 0))
    return pl.pallas_call(
        _wkv_kernel, grid=(B, N_DIR, N_CHUNKS),
        in_specs=[shared, shared, shared, shared, perdir, perdir, pl.BlockSpec((1, Cd), lambda b, d, i: (0, 0))],
        out_specs=perdir,
        out_shape=jax.ShapeDtypeStruct((B, N_DIR, S, Cd), F32),
        scratch_shapes=[pltpu.VMEM((HEAD_PAIRS, LANES, LANES), F32)],
        compiler_params=_cparams(("parallel", "parallel", "arbitrary")), name="wkv7",
    )(r, k, v, kk, lw, a, k_a[None, :])


def _rms(x, g):
    return x * lax.rsqrt(jnp.mean(x * x, axis=-1, keepdims=True) + EPS) * g


def _qproj_kernel(p_ref, g_ref, w_ref, cos_ref, sin_ref, q_ref):
    cq = _rms(p_ref[0], g_ref[...]).astype(BF16)
    cs, sn = cos_ref[...], sin_ref[...]
    for h in range(MLA_HEADS):
        y = jnp.dot(cq, w_ref[h], preferred_element_type=F32)
        rope = y[:, QK_NOPE:QK_NOPE + QK_ROPE] * cs + y[:, QK_NOPE + QK_ROPE:] * sn
        q = jnp.concatenate([y[:, :QK_NOPE], rope], axis=-1) * (ATTN_SCALE * LOG2_E)
        q_ref[0, h] = q.astype(q_ref.dtype)


def _qproj(p, gain, wq, cos, sin, tm=512):
    B = p.shape[0]
    return pl.pallas_call(
        _qproj_kernel, grid=(B, SEQ // tm),
        in_specs=[pl.BlockSpec((1, tm, Q_LORA), lambda b, i: (b, i, C_CQ // Q_LORA)),
                  pl.BlockSpec((1, Q_LORA), lambda b, i: (0, 0)),
                  pl.BlockSpec((MLA_HEADS, Q_LORA, 2 * LANES), lambda b, i: (0, 0, 0)),
                  pl.BlockSpec((tm, QK_ROPE), lambda b, i: (i, 0)),
                  pl.BlockSpec((tm, QK_ROPE), lambda b, i: (i, 0))],
        out_specs=pl.BlockSpec((1, MLA_HEADS, tm, QK_NOPE + QK_ROPE), lambda b, i: (b, 0, i, 0)),
        out_shape=jax.ShapeDtypeStruct((B, MLA_HEADS, SEQ, QK_NOPE + QK_ROPE), BF16),
        compiler_params=_cparams(("parallel", "parallel")), name="mla_q",
    )(p, gain[None, :], wq, cos, sin)


def _kvproj_kernel(p_ref, pe_ref, g_ref, w_ref, cos_ref, sin_ref, k_ref, v_ref):
    ckv = _rms(p_ref[0], g_ref[...]).astype(BF16)
    pe = pe_ref[0]
    kpe = pe[:, :QK_ROPE] * cos_ref[...] + pe[:, QK_ROPE:] * sin_ref[...]
    ones = jnp.ones((ckv.shape[0], V_HEAD), F32)
    for h in range(MLA_HEADS):
        y = jnp.dot(ckv, w_ref[h], preferred_element_type=F32)
        k_ref[0, h] = jnp.concatenate([y[:, :QK_NOPE], kpe], axis=-1).astype(k_ref.dtype)
        v_ref[0, h] = jnp.concatenate([y[:, QK_NOPE:], ones], axis=-1).astype(v_ref.dtype)


def _kvproj(p, gain, wkv, cos, sin, tm=256):
    B = p.shape[0]
    return pl.pallas_call(
        _kvproj_kernel, grid=(B, S_ALL // tm),
        in_specs=[pl.BlockSpec((1, tm, KV_LORA), lambda b, i: (b, i, C_CKV // KV_LORA)),
                  pl.BlockSpec((1, tm, 2 * QK_ROPE), lambda b, i: (b, i, C_KPE // (2 * QK_ROPE))),
                  pl.BlockSpec((1, KV_LORA), lambda b, i: (0, 0)),
                  pl.BlockSpec((MLA_HEADS, KV_LORA, 2 * LANES), lambda b, i: (0, 0, 0)),
                  pl.BlockSpec((tm, QK_ROPE), lambda b, i: (i, 0)),
                  pl.BlockSpec((tm, QK_ROPE), lambda b, i: (i, 0))],
        out_specs=[pl.BlockSpec((1, MLA_HEADS, tm, QK_NOPE + QK_ROPE), lambda b, i: (b, 0, i, 0)),
                   pl.BlockSpec((1, MLA_HEADS, tm, 2 * V_HEAD), lambda b, i: (b, 0, i, 0))],
        out_shape=[jax.ShapeDtypeStruct((B, MLA_HEADS, S_ALL, QK_NOPE + QK_ROPE), BF16),
                   jax.ShapeDtypeStruct((B, MLA_HEADS, S_ALL, 2 * V_HEAD), BF16)],
        compiler_params=_cparams(("parallel", "parallel")), name="mla_kv",
    )(p, p, gain[None, :], wkv, cos, sin)


def _attn_kernel(q_ref, k_ref, v_ref, o_ref, sa, sb, pa, pb, acc, m_ref, al_ref, *, tk):
    q = q_ref[0, 0]
    n_kv = k_ref.shape[2] // tk

    def scores(j):
        kj = k_ref[0, 0, pl.ds(pl.multiple_of(j * tk, tk), tk), :]
        return lax.dot_general(q, kj, (((1,), (1,)), ((), ())), preferred_element_type=F32)

    def accumulate(p_ref, j):
        vj = v_ref[0, 0, pl.ds(pl.multiple_of(j * tk, tk), tk), :]
        pv = jnp.dot(p_ref[...], vj, preferred_element_type=F32)
        al = al_ref[...]
        for c in range(0, 2 * V_HEAD, LANES):
            acc[:, c:c + LANES] = al * acc[:, c:c + LANES] + pv[:, c:c + LANES]

    def softmax(s_ref, p_ref):
        s = s_ref[...]
        m_old = m_ref[...]
        m_new = jnp.maximum(m_old, jnp.max(s, axis=-1, keepdims=True))
        al_ref[...] = jnp.exp2(m_old - m_new)
        m_ref[...] = m_new
        for c in range(0, tk, LANES):
            p_ref[:, c:c + LANES] = jnp.exp2(s[:, c:c + LANES] - m_new).astype(p_ref.dtype)

    s_bufs, p_bufs = (sa, sb), (pa, pb)
    sa[...] = scores(0)
    acc[...] = jnp.zeros_like(acc)
    m_ref[...] = jnp.full_like(m_ref, -jnp.inf)
    for j in range(n_kv):
        if j + 1 < n_kv:
            s_bufs[(j + 1) % 2][...] = scores(j + 1)
        if j >= 1:
            accumulate(p_bufs[(j - 1) % 2], j - 1)
        softmax(s_bufs[j % 2], p_bufs[j % 2])
    accumulate(p_bufs[(n_kv - 1) % 2], n_kv - 1)
    o_ref[0] = (acc[:, :V_HEAD] / acc[:, V_HEAD:]).astype(o_ref.dtype)


def _attention(q, k, v, tq=1024, tk=256):
    B, H, T, Dk = q.shape
    S = k.shape[2]
    return pl.pallas_call(
        functools.partial(_attn_kernel, tk=tk), grid=(B, H, T // tq),
        in_specs=[pl.BlockSpec((1, 1, tq, Dk), lambda b, h, i: (b, h, i, 0)),
                  pl.BlockSpec((1, 1, S, Dk), lambda b, h, i: (b, h, 0, 0)),
                  pl.BlockSpec((1, 1, S, 2 * V_HEAD), lambda b, h, i: (b, h, 0, 0))],
        out_specs=pl.BlockSpec((1, tq, V_HEAD), lambda b, h, i: (b, i, h)),
        out_shape=jax.ShapeDtypeStruct((B, T, H * V_HEAD), BF16),
        scratch_shapes=[pltpu.VMEM((tq, tk), F32), pltpu.VMEM((tq, tk), F32),
                        pltpu.VMEM((tq, tk), BF16), pltpu.VMEM((tq, tk), BF16),
                        pltpu.VMEM((tq, 2 * V_HEAD), F32), pltpu.VMEM((tq, LANES), F32), pltpu.VMEM((tq, LANES), F32)],
        compiler_params=_cparams(("parallel", "parallel", "parallel")), name="mla_attn",
    )(q, k, v)


def _gates_kernel(h_ref, w_ref, b_ref, o_ref):
    o_ref[0] = jax.nn.sigmoid(jnp.dot(h_ref[0], w_ref[...], preferred_element_type=F32) + b_ref[...]).astype(o_ref.dtype)


def _gates(h, w_gate, gate_b, tm=1024, tn=1024):
    B, _, D = h.shape
    N = w_gate.shape[1]
    return pl.pallas_call(
        _gates_kernel, grid=(B, SEQ // tm, N // tn),
        in_specs=[pl.BlockSpec((1, tm, D), lambda b, i, j: (b, i, 0)),
                  pl.BlockSpec((D, tn), lambda b, i, j: (0, j)),
                  pl.BlockSpec((1, tn), lambda b, i, j: (0, j))],
        out_specs=pl.BlockSpec((1, tm, tn), lambda b, i, j: (b, i, j)),
        out_shape=jax.ShapeDtypeStruct((B, SEQ, N), BF16),
        compiler_params=_cparams(("parallel", "parallel", "parallel")), name="gates",
    )(h, w_gate, gate_b[None, :])


MERGE_TN = 512


def _merge_kernel(y0_ref, y1_ref, g_ref, bg_ref, lnw_ref, lnb_ref, avg_ref, o_ref, wr_ref, wm_ref, gate_ref, out_ref):
    avg = avg_ref[...]
    y = y0_ref[0, 0] + y1_ref[0, 0]
    yc = y - _head_sum(y, avg)
    yn = yc * lax.rsqrt(_head_sum(yc * yc, avg) + LNX_EPS) * lnw_ref[...] + lnb_ref[...]
    yr = (yn * g_ref[0].astype(F32) + bg_ref[0].astype(F32)).astype(BF16)
    o = o_ref[0]
    D = D_MODEL
    for n in range(0, D, MERGE_TN):
        gr = gate_ref[0, :, n:n + MERGE_TN].astype(F32)
        gm = gate_ref[0, :, D + n:D + n + MERGE_TN].astype(F32)
        out_ref[0, :, n:n + MERGE_TN] = (
            gr * jnp.dot(yr, wr_ref[:, n:n + MERGE_TN], preferred_element_type=F32)
            + gm * jnp.dot(o, wm_ref[:, n:n + MERGE_TN], preferred_element_type=F32)).astype(out_ref.dtype)


def _merge(y_dir, g, bg, lnx_w, lnx_b, avg2, o, w_rp, w_mp, gates, tm=256):
    B = o.shape[0]
    D = D_MODEL
    C = RWKV_DIM
    tok = pl.BlockSpec((1, tm, C), lambda b, i: (b, i, 0))
    row = lambda n: pl.BlockSpec((1, n), lambda b, i: (0, 0))
    return pl.pallas_call(
        _merge_kernel, grid=(B, SEQ // tm),
        in_specs=[pl.BlockSpec((1, 1, tm, C), lambda b, i: (b, 0, i, 0)),
                  pl.BlockSpec((1, 1, tm, C), lambda b, i: (b, 1, i, 0)),
                  tok, tok, row(C), row(C),
                  pl.BlockSpec((LANES, LANES), lambda b, i: (0, 0)),
                  pl.BlockSpec((1, tm, D), lambda b, i: (b, i, 0)),
                  pl.BlockSpec((C, D), lambda b, i: (0, 0)),
                  pl.BlockSpec((D, D), lambda b, i: (0, 0)),
                  pl.BlockSpec((1, tm, 2 * D), lambda b, i: (b, i, 0))],
        out_specs=pl.BlockSpec((1, tm, D), lambda b, i: (b, i, 0)),
        out_shape=jax.ShapeDtypeStruct((B, SEQ, D), BF16),
        compiler_params=_cparams(("parallel", "parallel")), name="merge",
    )(y_dir, y_dir, g, bg, lnx_w[None, :], lnx_b[None, :], avg2, o, w_rp, w_mp, gates)


def _outproj_kernel(m_ref, w_ref, x_ref, gpost_ref, gate_ref, gpre_ref, mod_ref, x1_ref, h_ref):
    out = jnp.dot(m_ref[0], w_ref[...], preferred_element_type=F32)
    x1 = x_ref[0] + gate_ref[0] * _rms(out, gpost_ref[...])
    x1_ref[0] = x1
    mod = mod_ref[0]
    h_ref[0] = (_rms(x1, gpre_ref[...]) * (1.0 + mod[1:2]) + mod[0:1]).astype(h_ref.dtype)


def _outproj(merged, w_out, x, g_post, gate, g_pre, mod2, tm=256):
    B = x.shape[0]
    D = D_MODEL
    return pl.pallas_call(
        _outproj_kernel, grid=(B, SEQ // tm),
        in_specs=[pl.BlockSpec((1, tm, D), lambda b, i: (b, i, 0)),
                  pl.BlockSpec((D, D), lambda b, i: (0, 0)),
                  pl.BlockSpec((1, tm, D), lambda b, i: (b, i, 0)),
                  pl.BlockSpec((1, D), lambda b, i: (0, 0)),
                  pl.BlockSpec((1, 1, D), lambda b, i: (b, 0, 0)),
                  pl.BlockSpec((1, D), lambda b, i: (0, 0)),
                  pl.BlockSpec((1, 2, D), lambda b, i: (b, 0, 0))],
        out_specs=[pl.BlockSpec((1, tm, D), lambda b, i: (b, i, 0)),
                   pl.BlockSpec((1, tm, D), lambda b, i: (b, i, 0))],
        out_shape=[jax.ShapeDtypeStruct((B, SEQ, D), F32),
                   jax.ShapeDtypeStruct((B, SEQ, D), BF16)],
        compiler_params=_cparams(("parallel", "parallel")), name="outproj",
    )(merged, w_out, x, g_post[None, :], gate[:, None, :], g_pre[None, :], mod2)


HALO = 16


def _ffn_kernel(h_ref, hp_ref, hn_ref, wg_ref, wv_ref, wd_ref, cw_ref, cb_ref, x_ref, gate_ref, gpost_ref,
                o_ref, acc_ref):
    i = pl.program_id(1)
    f = pl.program_id(2)
    tm = h_ref.shape[1]

    @pl.when(f == 0)
    def _():
        acc_ref[...] = jnp.zeros_like(acc_ref)

    h = h_ref[0]
    wg = wg_ref[...]
    g_ext = jnp.dot(jnp.concatenate([hp_ref[0], h, hn_ref[0]], axis=0), wg, preferred_element_type=F32)
    g = g_ext[HALO:HALO + tm]
    g_prev = g_ext[HALO - 1:HALO]
    g_next = g_ext[HALO + tm:HALO + tm + 1]
    g_prev = jnp.where(i == 0, 0.0, g_prev)
    g_next = jnp.where(i == pl.num_programs(1) - 1, 0.0, g_next)
    rows = lax.broadcasted_iota(jnp.int32, g.shape, 0)
    up = jnp.where(rows == 0, g_prev, pltpu.roll(g, 1, 0))
    dn = jnp.where(rows == tm - 1, g_next, pltpu.roll(g, tm - 1, 0))
    cw = cw_ref[...]
    u = cb_ref[...] + up * cw[0:1] + g * cw[1:2] + dn * cw[2:3]
    val = jnp.dot(h, wv_ref[...], preferred_element_type=F32)
    act = jax.nn.gelu(u, approximate=True) * val
    acc_ref[...] += jnp.dot(act.astype(BF16), wd_ref[...], preferred_element_type=F32)

    @pl.when(f == pl.num_programs(2) - 1)
    def _():
        o_ref[0] = x_ref[0] + gate_ref[0] * _rms(acc_ref[...], gpost_ref[...])


def _ffn(h, wg, wv, wd, cw, cb, x1, gate, g_post, tm=512, tf=512):
    B = h.shape[0]
    D = D_MODEL
    hb = tm // HALO
    last = SEQ // HALO - 1
    return pl.pallas_call(
        _ffn_kernel, grid=(B, SEQ // tm, D_FF // tf),
        in_specs=[pl.BlockSpec((1, tm, D), lambda b, i, f: (b, i, 0)),
                  pl.BlockSpec((1, HALO, D), lambda b, i, f: (b, jnp.maximum(i * hb - 1, 0), 0)),
                  pl.BlockSpec((1, HALO, D), lambda b, i, f: (b, jnp.minimum((i + 1) * hb, last), 0)),
                  pl.BlockSpec((D, tf), lambda b, i, f: (0, f)),
                  pl.BlockSpec((D, tf), lambda b, i, f: (0, f)),
                  pl.BlockSpec((tf, D), lambda b, i, f: (f, 0)),
                  pl.BlockSpec((3, tf), lambda b, i, f: (0, f)),
                  pl.BlockSpec((1, tf), lambda b, i, f: (0, f)),
                  pl.BlockSpec((1, tm, D), lambda b, i, f: (b, i, 0)),
                  pl.BlockSpec((1, 1, D), lambda b, i, f: (b, 0, 0)),
                  pl.BlockSpec((1, D), lambda b, i, f: (0, 0))],
        out_specs=pl.BlockSpec((1, tm, D), lambda b, i, f: (b, i, 0)),
        out_shape=jax.ShapeDtypeStruct((B, SEQ, D), F32),
        scratch_shapes=[pltpu.VMEM((tm, D), F32)],
        compiler_params=_cparams(("parallel", "parallel", "arbitrary")), name="convffn",
    )(h, h, h, wg, wv, wd, cw, cb[None, :], x1, gate[:, None, :], g_post[None, :])


def _rope_partner(w):
    q = QK_ROPE // 4
    return jnp.concatenate([-w[..., q:2 * q], w[..., :q], -w[..., 3 * q:], w[..., 2 * q:3 * q]], axis=-1)


def _pack_w_main(w):
    w = w.astype(BF16)
    kpe = w[:, RWKV_IN + Q_LORA + KV_LORA:RWKV_IN + MLA_IN]
    pad = jnp.zeros((w.shape[0], C_CQ - C_KPE - 2 * QK_ROPE), BF16)
    return jnp.concatenate([w[:, :RWKV_IN], kpe, _rope_partner(kpe), pad,
                            w[:, RWKV_IN:RWKV_IN + Q_LORA + KV_LORA]], axis=1)


def _lora_window_weight(w2, first_row):
    n_dir, R, C = w2.shape
    out = jnp.zeros((LORA_WIN, n_dir, C), BF16)
    for d in range(n_dir):
        out = out.at[first_row + d * R:first_row + (d + 1) * R, d].set(w2[d].astype(BF16))
    return out.reshape(LORA_WIN, n_dir * C)


def _rope_tables():
    rows = SEQ // GRID_W
    row = jnp.repeat(jnp.arange(rows), GRID_W).astype(F32)
    col = jnp.tile(jnp.arange(GRID_W), rows).astype(F32)
    half = QK_ROPE // 2
    freqs = ROPE_THETA ** (-jnp.arange(0, half, 2, dtype=F32) / half)
    ar, ac = row[:, None] * freqs, col[:, None] * freqs
    cos = jnp.concatenate([jnp.cos(ar), jnp.cos(ar), jnp.cos(ac), jnp.cos(ac)], axis=-1)
    sin = jnp.concatenate([jnp.sin(ar), jnp.sin(ar), jnp.sin(ac), jnp.sin(ac)], axis=-1)
    cos_all = jnp.concatenate([cos, jnp.ones((CTX_LEN, QK_ROPE), F32)], axis=0)
    sin_all = jnp.concatenate([sin, jnp.zeros((CTX_LEN, QK_ROPE), F32)], axis=0)
    return cos_all, sin_all


def kernel(x, c, ctx, c_ctx, w_ada, b_ada, norm_mix_pre, norm_mix_post, norm_ffn_pre, norm_ffn_post, w_in, rwkv_mu, rwkv_w0, rwkv_w2, rwkv_a0, rwkv_a2, rwkv_k_k, rwkv_k_a, rwkv_r_k, rwkv_lnx_w, rwkv_lnx_b, rwkv_g2, w_rwkv_proj, mla_q_norm, mla_kv_norm, mla_w_uq, mla_w_ukv, w_mla_proj, gate_b, w_out, ffn_w_gate, ffn_w_val, ffn_conv_w, ffn_conv_b, ffn_w_down):
    B = x.shape[0]
    D = D_MODEL
    C = RWKV_DIM
    l = 0

    s = jnp.concatenate([jax.nn.silu(c), jax.nn.silu(c_ctx)[None, :]], axis=0)
    mods = _ada(s, w_ada[l], b_ada[l])
    lat = mods[:B].reshape(B, 6, D)
    cm = mods[B].reshape(6, D)
    sh1, sc1, g1, sh2, sc2, g2 = [lat[:, j] for j in range(6)]
    mods1 = jnp.stack([jnp.stack([sh1, sc1], axis=1),
                       jnp.broadcast_to(jnp.stack([cm[0], cm[1]])[None], (B, 2, D))], axis=1)
    mods2 = jnp.stack([sh2, sc2], axis=1)

    h = _norm_mod(x, ctx, norm_mix_pre[l], mods1)
    p = _matmul(h.reshape(B * S_ALL, D), _pack_w_main(w_in[l]), tm=1088, tn=1024,
                name="w_in").reshape(B, S_ALL, P_COLS)
    gates = _gates(h, w_in[l][:, RWKV_IN + MLA_IN:].astype(BF16), gate_b[l])

    lane_head = jnp.arange(LANES) // RWKV_HEAD
    ones2 = (lane_head[:, None] == lane_head[None, :]).astype(BF16)
    avg2 = ones2 * (1.0 / RWKV_HEAD)
    r, k, v, kk, lw, a, g, bg = _prep(p, rwkv_mu[l], rwkv_w0[l], _lora_window_weight(rwkv_w2[l], C_WL - C_WL_WIN),
                                      rwkv_a0[l], _lora_window_weight(rwkv_a2[l], C_AL - C_AL_WIN), rwkv_k_k[l],
                                      rwkv_k_a[l], rwkv_r_k[l].reshape(C), rwkv_g2[l].astype(BF16), ones2)
    y_dir = _wkv(r, k, v, kk, lw, a, rwkv_k_a[l])

    cos, sin = _rope_tables()
    uq = mla_w_uq[l].reshape(Q_LORA, MLA_HEADS, QK_NOPE + QK_ROPE)
    wq = jnp.concatenate([uq, _rope_partner(uq[..., QK_NOPE:])], axis=-1).transpose(1, 0, 2).astype(BF16)
    wkv = mla_w_ukv[l].reshape(KV_LORA, MLA_HEADS, QK_NOPE + V_HEAD).transpose(1, 0, 2).astype(BF16)
    q = _qproj(p, mla_q_norm[l], wq, cos[:SEQ], sin[:SEQ])
    k_all, v_all = _kvproj(p, mla_kv_norm[l], wkv, cos, sin)
    o = _attention(q, k_all, v_all)

    merged = _merge(y_dir, g, bg, rwkv_lnx_w[l], rwkv_lnx_b[l], avg2, o, w_rwkv_proj[l].astype(BF16),
                    w_mla_proj[l].astype(BF16), gates)
    x1, h2 = _outproj(merged, w_out[l].astype(BF16), x, norm_mix_post[l], g1, norm_ffn_pre[l], mods2)
    return _ffn(h2, ffn_w_gate[l].astype(BF16), ffn_w_val[l].astype(BF16), ffn_w_down[l].astype(BF16),
                ffn_conv_w[l], ffn_conv_b[l], x1, g2, norm_ffn_post[l])
```

```python
import functools
import math

import jax
import jax.numpy as jnp
from jax import lax
from jax.experimental import pallas as pl
from jax.experimental.pallas import tpu as pltpu

D_MODEL = 2048
BATCH = 2
SEQ = 4096
GRID_W = 64
CTX_LEN = 256
S_ALL = SEQ + CTX_LEN
EPS = 1e-6
RWKV_HEADS = 16
RWKV_HEAD = 64
RWKV_DIM = RWKV_HEADS * RWKV_HEAD
DECAY_LORA = 96
AAA_LORA = 96
GATE_LORA = 256
N_DIR = 2
DECAY_SCALE = math.exp(-0.5)
LNX_EPS = 64e-5
MLA_HEADS = 16
Q_LORA = 512
KV_LORA = 512
QK_NOPE = 128
QK_ROPE = 64
V_HEAD = 128
ROPE_THETA = 10000.0
ATTN_SCALE = (QK_NOPE + QK_ROPE) ** -0.5
LOG2_E = math.log2(math.e)
D_FF = 5632
RWKV_IN = 3 * RWKV_DIM + N_DIR * DECAY_LORA + N_DIR * AAA_LORA + GATE_LORA
MLA_IN = Q_LORA + KV_LORA + QK_ROPE

LANES = 128
SUBLANES = 8
VMEM_LIMIT = 56 * 1024 * 1024

C_R, C_K, C_V = 0, RWKV_DIM, 2 * RWKV_DIM
C_WL = 3 * RWKV_DIM
C_AL = C_WL + N_DIR * DECAY_LORA
C_GL = C_AL + N_DIR * AAA_LORA
RWKV_COLS = C_GL + GATE_LORA
C_KPE = RWKV_COLS
C_CQ = -(-(C_KPE + 2 * QK_ROPE) // Q_LORA) * Q_LORA
C_CKV = C_CQ + Q_LORA
P_COLS = C_CKV + KV_LORA
LORA_WIN = 2 * LANES
C_WL_WIN = C_WL // LANES * LANES
C_AL_WIN = C_AL // LANES * LANES

WKV_CHUNK = 64
WKV_SUB = 4
HEAD_PAIRS = RWKV_DIM // LANES

F32 = jnp.float32
BF16 = jnp.bfloat16


def _cparams(sem):
    return pltpu.CompilerParams(dimension_semantics=sem, vmem_limit_bytes=VMEM_LIMIT)


def _bdot(a, b):
    return jnp.dot(a.astype(BF16), b.astype(BF16), preferred_element_type=F32)


def _hilo_dot(x, w):
    hi = x.astype(BF16)
    lo = (x - hi.astype(F32)).astype(BF16)
    return jnp.dot(hi, w, preferred_element_type=F32) + jnp.dot(lo, w, preferred_element_type=F32)


def _head_sum(x, ones2):
    return jnp.concatenate([_hilo_dot(x[:, c * LANES:(c + 1) * LANES], ones2) for c in range(HEAD_PAIRS)], axis=-1)


def _ada_kernel(s_ref, w_ref, b_ref, o_ref):
    w = w_ref[...]
    for r in range(s_ref.shape[0]):
        o_ref[r:r + 1, :] = jnp.sum(s_ref[r] * w, axis=0, keepdims=True) + b_ref[...]


def _ada(s, w, b, tn=1024):
    R, K = s.shape
    N = w.shape[1]
    return pl.pallas_call(
        _ada_kernel, grid=(N // tn,),
        in_specs=[pl.BlockSpec((R, K, 1), lambda j: (0, 0, 0)),
                  pl.BlockSpec((K, tn), lambda j: (0, j)),
                  pl.BlockSpec((1, tn), lambda j: (0, j))],
        out_specs=pl.BlockSpec((R, tn), lambda j: (0, j)),
        out_shape=jax.ShapeDtypeStruct((R, N), F32),
        compiler_params=_cparams(("parallel",)), name="ada",
    )(s[:, :, None], w, b[None, :])


def _norm_mod_kernel(x_ref, c_ref, g_ref, m_ref, o_ref, *, lat_tiles):
    def emit(x):
        n = x * lax.rsqrt(jnp.mean(x * x, axis=-1, keepdims=True) + EPS) * g_ref[...]
        m = m_ref[0, 0]
        o_ref[0] = (n * (1.0 + m[1:2]) + m[0:1]).astype(o_ref.dtype)

    is_latent = pl.program_id(1) < lat_tiles

    @pl.when(is_latent)
    def _():
        emit(x_ref[0])

    @pl.when(jnp.logical_not(is_latent))
    def _():
        emit(c_ref[0])


def _norm_mod(x, ctx, gain, mods, tm=256):
    B, _, D = x.shape
    assert ctx.shape[1] == tm
    lat_tiles = SEQ // tm
    return pl.pallas_call(
        functools.partial(_norm_mod_kernel, lat_tiles=lat_tiles), grid=(B, S_ALL // tm),
        in_specs=[pl.BlockSpec((1, tm, D), lambda b, i: (b, jnp.minimum(i, lat_tiles - 1), 0)),
                  pl.BlockSpec((1, tm, D), lambda b, i: (b, 0, 0)),
                  pl.BlockSpec((1, D), lambda b, i: (0, 0)),
                  pl.BlockSpec((1, 1, 2, D), lambda b, i: (b, i // lat_tiles, 0, 0))],
        out_specs=pl.BlockSpec((1, tm, D), lambda b, i: (b, i, 0)),
        out_shape=jax.ShapeDtypeStruct((B, S_ALL, D), BF16),
        compiler_params=_cparams(("parallel", "parallel")), name="norm_mod",
    )(x, ctx, gain[None, :], mods)


def _mm_kernel(a_ref, w_ref, o_ref):
    o_ref[...] = _bdot(a_ref[...], w_ref[...]).astype(o_ref.dtype)


def _matmul(a, w, tm, tn, out_dtype=F32, name="matmul"):
    M, K = a.shape
    N = w.shape[1]
    return pl.pallas_call(
        _mm_kernel, grid=(M // tm, N // tn),
        in_specs=[pl.BlockSpec((tm, K), lambda i, j: (i, 0)),
                  pl.BlockSpec((K, tn), lambda i, j: (0, j))],
        out_specs=pl.BlockSpec((tm, tn), lambda i, j: (i, j)),
        out_shape=jax.ShapeDtypeStruct((M, N), out_dtype),
        compiler_params=_cparams(("parallel", "parallel")), name=name,
    )(a, w)


def _prep_kernel(p_ref, pp_ref, pn_ref, mu_ref, w0_ref, w2_ref, a0_ref, a2_ref, kk_ref, ka_ref, rk_ref, g2_ref,
                 ones_ref, r_o, k_o, v_o, kkn_o, lw_o, a_o, g_o, bg_o):
    i = pl.program_id(1)
    x = p_ref[0]
    tm = x.shape[0]
    lat_tiles = SEQ // tm
    first = jnp.logical_or(i == 0, i == lat_tiles)
    last = jnp.logical_or(i == lat_tiles - 1, i == pl.num_programs(1) - 1)
    x_before = jnp.where(first, 0.0, pp_ref[0][SUBLANES - 1:SUBLANES])
    x_after = jnp.where(last, 0.0, pn_ref[0][0:1])
    rows = lax.broadcasted_iota(jnp.int32, (tm, 1), 0)
    prev = jnp.where(rows == 0, x_before, pltpu.roll(x, 1, 0))
    nxt = jnp.where(rows == tm - 1, x_after, pltpu.roll(x, tm - 1, 0))
    xs = x + mu_ref[...] * (0.5 * (prev + nxt) - x)

    C = RWKV_DIM
    r, k, v = xs[:, C_R:C_R + C], xs[:, C_K:C_K + C], xs[:, C_V:C_V + C]
    w_lora = _bdot(jnp.tanh(xs[:, C_WL_WIN:C_WL_WIN + LORA_WIN]), w2_ref[...])
    a_lora = _bdot(xs[:, C_AL_WIN:C_AL_WIN + LORA_WIN], a2_ref[...])
    a_sum = None
    for d in range(N_DIR):
        w_raw = w0_ref[d:d + 1] + w_lora[:, d * C:(d + 1) * C]
        lw_o[0, d] = -DECAY_SCALE * jax.nn.sigmoid(w_raw)
        a_d = jax.nn.sigmoid(a0_ref[d:d + 1] + a_lora[:, d * C:(d + 1) * C])
        a_o[0, d] = a_d
        a_sum = a_d if a_sum is None else a_sum + a_d
    ones2 = ones_ref[...]
    kkx = k * kk_ref[...]
    kkn_o[0] = kkx / jnp.maximum(jnp.sqrt(_head_sum(kkx * kkx, ones2)), 1e-12)
    g = _bdot(jax.nn.sigmoid(xs[:, C_GL:C_GL + GATE_LORA]), g2_ref[...])
    k_bar = k * (1.0 + (0.5 * a_sum - 1.0) * ka_ref[...])
    bonus = _head_sum(r * k_bar * rk_ref[...], ones2) * v
    r_o[0] = r
    k_o[0] = k
    v_o[0] = v
    g_o[0] = g.astype(g_o.dtype)
    bg_o[0] = (bonus * g).astype(bg_o.dtype)


def _prep(p, mu, w0, w2p, a0, a2p, k_k, k_a, r_k, g2, ones2, tm=256):
    B = p.shape[0]
    C = RWKV_DIM
    hb = tm // SUBLANES
    last = S_ALL // SUBLANES - 1
    row = lambda n: pl.BlockSpec((1, n), lambda b, i: (0, 0))
    tok = pl.BlockSpec((1, tm, C), lambda b, i: (b, i, 0))
    tok_dir = pl.BlockSpec((1, N_DIR, tm, C), lambda b, i: (b, 0, i, 0))
    shp = jax.ShapeDtypeStruct((B, S_ALL, C), F32)
    shp_dir = jax.ShapeDtypeStruct((B, N_DIR, S_ALL, C), F32)
    shp_b = jax.ShapeDtypeStruct((B, S_ALL, C), BF16)
    return pl.pallas_call(
        _prep_kernel, grid=(B, S_ALL // tm),
        in_specs=[pl.BlockSpec((1, tm, RWKV_COLS), lambda b, i: (b, i, 0)),
                  pl.BlockSpec((1, SUBLANES, RWKV_COLS), lambda b, i: (b, jnp.maximum(i * hb - 1, 0), 0)),
                  pl.BlockSpec((1, SUBLANES, RWKV_COLS), lambda b, i: (b, jnp.minimum((i + 1) * hb, last), 0)),
                  row(RWKV_COLS),
                  pl.BlockSpec((N_DIR, C), lambda b, i: (0, 0)),
                  pl.BlockSpec((LORA_WIN, N_DIR * C), lambda b, i: (0, 0)),
                  pl.BlockSpec((N_DIR, C), lambda b, i: (0, 0)),
                  pl.BlockSpec((LORA_WIN, N_DIR * C), lambda b, i: (0, 0)),
                  row(C), row(C), row(C),
                  pl.BlockSpec((GATE_LORA, C), lambda b, i: (0, 0)),
                  pl.BlockSpec((LANES, LANES), lambda b, i: (0, 0))],
        out_specs=[tok, tok, tok, tok, tok_dir, tok_dir, tok, tok],
        out_shape=[shp, shp, shp, shp, shp_dir, shp_dir, shp_b, shp_b],
        compiler_params=_cparams(("parallel", "parallel")), name="rwkv_prep",
    )(p, p, p, mu[None, :], w0, w2p, a0, a2p, k_k[None, :], k_a[None, :], r_k[None, :], g2, ones2)


def _wkv_kernel(r_ref, k_ref, v_ref, kk_ref, lw_ref, a_ref, ka_ref, y_ref, s_ref):
    d = pl.program_id(1)
    C = WKV_CHUNK
    C2 = 2 * C
    P = HEAD_PAIRS
    sign = 1 - 2 * d
    fwd = d == 0

    @pl.when(pl.program_id(2) == 0)
    def _():
        s_ref[...] = jnp.zeros_like(s_ref)

    trow = lax.broadcasted_iota(jnp.int32, (C, C), 0)
    tcol = lax.broadcasted_iota(jnp.int32, (C, C), 1)
    tri = ((trow - tcol) * sign >= 0).astype(BF16)
    lane = lax.broadcasted_iota(jnp.int32, (C, LANES), 1)
    head0 = lane < RWKV_HEAD
    zero = jnp.zeros((), F32)

    def expand(x):
        x = x.astype(BF16)
        zb = jnp.zeros((), BF16)
        return jnp.stack([jnp.concatenate([jnp.where(head0, x[:, j * LANES:(j + 1) * LANES], zb),
                                           jnp.where(head0, zb, x[:, j * LANES:(j + 1) * LANES])], axis=0)
                          for j in range(P)])

    def bmm(x, y):
        return jnp.einsum('hab,hbc->hac', x.astype(BF16), y.astype(BF16), preferred_element_type=F32)

    def bmm_nt(x, y):
        return jnp.einsum('han,hbn->hab', x.astype(BF16), y.astype(BF16), preferred_element_type=F32)

    def bmm_tn(x, y):
        return jnp.einsum('hca,hcb->hab', x.astype(BF16), y.astype(BF16), preferred_element_type=F32)

    row = lax.broadcasted_iota(jnp.int32, (C2, C2), 0)
    col = lax.broadcasted_iota(jnp.int32, (C2, C2), 1)
    same = (row // C) == (col // C)
    lag = (row - col) * sign
    incl = jnp.logical_and(same, lag >= 0)[None]
    strict = jnp.logical_and(same, lag > 0)[None]

    def blk(m):
        return (row // m) == (col // m)

    def half_rows(t, m, second):
        return jnp.concatenate([t[:, g + second * m:g + (second + 1) * m] for g in range(0, C2, 2 * m)], axis=1)

    def state_free(rows):
        r = r_ref[0, rows, :]
        v = v_ref[0, rows, :]
        kk = kk_ref[0, rows, :]
        lw = lw_ref[0, 0, rows, :]
        a = a_ref[0, 0, rows, :]
        k = k_ref[0, rows, :] * (1.0 + (a - 1.0) * ka_ref[...])

        cum = _hilo_dot_left(tri, lw)
        tot = jnp.sum(lw, axis=0, keepdims=True)
        e_cum = jnp.exp(cum)
        e_neg = jnp.exp(-cum)
        e_rem = jnp.exp(tot - cum)
        b = kk * a
        at = -kk * jnp.exp(cum - lw)
        dec = jnp.exp(tot)
        AT, RT, BT, KT, BD, KD, V = (expand(t) for t in (at, r * e_cum, b * e_neg, k * e_neg, b * e_rem,
                                                         k * e_rem, v))
        AR = jnp.concatenate([AT, RT], axis=1)
        G = bmm_nt(AR, jnp.concatenate([BT, KT], axis=1))
        a_ab, a_ak = G[:, :C2, :C2], G[:, :C2, C2:]
        a_rb, a_rk = G[:, C2:, :C2], G[:, C2:, C2:]

        L = jnp.where(strict, a_ab, zero)
        Lb = jnp.where(blk(8)[None], L, zero)
        L2 = bmm(Lb, Lb)
        L4 = bmm(L2, L2)
        S1 = Lb + L2 + bmm(Lb, L2)
        Nm = S1 + L4 + bmm(L4, S1)
        for m in (8, 16, 32):
            Lo = jnp.where(jnp.logical_and(blk(2 * m), jnp.logical_not(blk(m)))[None], L, zero)
            n_first, n_second = half_rows(Nm, m, 0), half_rows(Nm, m, 1)
            n_late = jnp.where(fwd, n_second, n_first)
            p_late = jnp.where(fwd, half_rows(Lo, m, 1), half_rows(Lo, m, 0)) + bmm(n_late, Lo)
            n_late = n_late + p_late + bmm(p_late, Nm)
            n_first, n_second = jnp.where(fwd, n_first, n_late), jnp.where(fwd, n_late, n_second)
            Nm = jnp.concatenate([piece for g in range(C2 // (2 * m))
                                  for piece in (n_first[:, g * m:(g + 1) * m], n_second[:, g * m:(g + 1) * m])],
                                 axis=1)
        rhs_v = bmm(jnp.where(strict, a_ak, zero), V)
        a_y = jnp.concatenate([jnp.where(incl, a_rb, zero), jnp.where(incl, a_rk, zero)], axis=2).astype(BF16)
        dec3 = jnp.stack([dec[:, j * LANES:(j + 1) * LANES] for j in range(P)])
        return AR, rhs_v, Nm.astype(BF16), V, a_y, jnp.concatenate([BD, KD], axis=1), dec3

    def state_step(part, s0):
        AR, rhs_v, Nm, V, a_y, BKD, dec3 = part
        ars0 = bmm_nt(AR, s0)
        rhs = ars0[:, :C2] + rhs_v
        u = rhs + bmm(Nm, rhs)
        uv = jnp.concatenate([u.astype(BF16), V], axis=1)
        y2 = ars0[:, C2:] + bmm(a_y, uv)
        y = y2[:, :C] + y2[:, C:]
        return jnp.concatenate([y[j] for j in range(P)], axis=-1), s0 * dec3 + bmm_tn(uv, BKD)

    chunk_rows = [pl.ds(pl.multiple_of(jnp.where(fwd, j, WKV_SUB - 1 - j) * C, C), C) for j in range(WKV_SUB)]
    parts = [state_free(rows) for rows in chunk_rows]
    s = s_ref[...]
    for rows, part in zip(chunk_rows, parts):
        y, s = state_step(part, s)
        y_ref[0, 0, rows, :] = y
    s_ref[...] = s


def _hilo_dot_left(w, x):
    hi = x.astype(BF16)
    lo = (x - hi.astype(F32)).astype(BF16)
    return jnp.dot(w, hi, preferred_element_type=F32) + jnp.dot(w, lo, preferred_element_type=F32)


def _wkv(r, k, v, kk, lw, a, k_a):
    B, S, Cd = r.shape
    R = WKV_SUB * WKV_CHUNK
    n_blocks = S // R
    lat_blocks = SEQ // R

    def bidx(d, i):
        return jnp.where(d == 0, (i + lat_blocks) % n_blocks, n_blocks - 1 - i)

    shared = pl.BlockSpec((1, R, Cd), lambda b, d, i: (b, bidx(d, i), 0))
    perdir = pl.BlockSpec((1, 1, R, Cd), lambda b, d, i: (b, d, bidx(d, i), 0))
    return pl.pallas_call(
        _wkv_kernel, grid=(B, N_DIR, n_blocks),
        in_specs=[shared, shared, shared, shared, perdir, perdir, pl.BlockSpec((1, Cd), lambda b, d, i: (0, 0))],
        out_specs=perdir,
        out_shape=jax.ShapeDtypeStruct((B, N_DIR, S, Cd), F32),
        scratch_shapes=[pltpu.VMEM((HEAD_PAIRS, LANES, LANES), F32)],
        compiler_params=_cparams(("parallel", "parallel", "arbitrary")), name="wkv7",
    )(r, k, v, kk, lw, a, k_a[None, :])


def _rms(x, g):
    return x * lax.rsqrt(jnp.mean(x * x, axis=-1, keepdims=True) + EPS) * g


def _qproj_kernel(p_ref, g_ref, w_ref, cos_ref, sin_ref, q_ref):
    cq = _rms(p_ref[0], g_ref[...]).astype(BF16)
    cs, sn = cos_ref[...], sin_ref[...]
    for h in range(MLA_HEADS):
        y = jnp.dot(cq, w_ref[h], preferred_element_type=F32)
        rope = y[:, QK_NOPE:QK_NOPE + QK_ROPE] * cs + y[:, QK_NOPE + QK_ROPE:] * sn
        q = jnp.concatenate([y[:, :QK_NOPE], rope], axis=-1) * (ATTN_SCALE * LOG2_E)
        q_ref[0, h] = q.astype(q_ref.dtype)


def _qproj(p, gain, wq, cos, sin, tm=512):
    B = p.shape[0]
    return pl.pallas_call(
        _qproj_kernel, grid=(B, SEQ // tm),
        in_specs=[pl.BlockSpec((1, tm, Q_LORA), lambda b, i: (b, i, C_CQ // Q_LORA)),
                  pl.BlockSpec((1, Q_LORA), lambda b, i: (0, 0)),
                  pl.BlockSpec((MLA_HEADS, Q_LORA, 2 * LANES), lambda b, i: (0, 0, 0)),
                  pl.BlockSpec((tm, QK_ROPE), lambda b, i: (i, 0)),
                  pl.BlockSpec((tm, QK_ROPE), lambda b, i: (i, 0))],
        out_specs=pl.BlockSpec((1, MLA_HEADS, tm, QK_NOPE + QK_ROPE), lambda b, i: (b, 0, i, 0)),
        out_shape=jax.ShapeDtypeStruct((B, MLA_HEADS, SEQ, QK_NOPE + QK_ROPE), BF16),
        compiler_params=_cparams(("parallel", "parallel")), name="mla_q",
    )(p, gain[None, :], wq, cos, sin)


def _kvproj_kernel(p_ref, pe_ref, g_ref, w_ref, cos_ref, sin_ref, k_ref, v_ref):
    ckv = _rms(p_ref[0], g_ref[...]).astype(BF16)
    pe = pe_ref[0]
    kpe = pe[:, :QK_ROPE] * cos_ref[...] + pe[:, QK_ROPE:] * sin_ref[...]
    ones = jnp.ones((ckv.shape[0], V_HEAD), F32)
    for h in range(MLA_HEADS):
        y = jnp.dot(ckv, w_ref[h], preferred_element_type=F32)
        k_ref[0, h] = jnp.concatenate([y[:, :QK_NOPE], kpe], axis=-1).astype(k_ref.dtype)
        v_ref[0, h] = jnp.concatenate([y[:, QK_NOPE:], ones], axis=-1).astype(v_ref.dtype)


def _kvproj(p, gain, wkv, cos, sin, tm=256):
    B = p.shape[0]
    return pl.pallas_call(
        _kvproj_kernel, grid=(B, S_ALL // tm),
        in_specs=[pl.BlockSpec((1, tm, KV_LORA), lambda b, i: (b, i, C_CKV // KV_LORA)),
                  pl.BlockSpec((1, tm, 2 * QK_ROPE), lambda b, i: (b, i, C_KPE // (2 * QK_ROPE))),
                  pl.BlockSpec((1, KV_LORA), lambda b, i: (0, 0)),
                  pl.BlockSpec((MLA_HEADS, KV_LORA, 2 * LANES), lambda b, i: (0, 0, 0)),
                  pl.BlockSpec((tm, QK_ROPE), lambda b, i: (i, 0)),
                  pl.BlockSpec((tm, QK_ROPE), lambda b, i: (i, 0))],
        out_specs=[pl.BlockSpec((1, MLA_HEADS, tm, QK_NOPE + QK_ROPE), lambda b, i: (b, 0, i, 0)),
                   pl.BlockSpec((1, MLA_HEADS, tm, 2 * V_HEAD), lambda b, i: (b, 0, i, 0))],
        out_shape=[jax.ShapeDtypeStruct((B, MLA_HEADS, S_ALL, QK_NOPE + QK_ROPE), BF16),
                   jax.ShapeDtypeStruct((B, MLA_HEADS, S_ALL, 2 * V_HEAD), BF16)],
        compiler_params=_cparams(("parallel", "parallel")), name="mla_kv",
    )(p, p, gain[None, :], wkv, cos, sin)


def _attn_kernel(q_ref, k_ref, v_ref, o_ref, sa, sb, pa, pb, acc, m_ref, al_ref, *, tk):
    q = q_ref[0, 0]
    n_kv = k_ref.shape[2] // tk

    def scores(j):
        kj = k_ref[0, 0, pl.ds(pl.multiple_of(j * tk, tk), tk), :]
        return lax.dot_general(q, kj, (((1,), (1,)), ((), ())), preferred_element_type=F32)

    def accumulate(p_ref, j):
        vj = v_ref[0, 0, pl.ds(pl.multiple_of(j * tk, tk), tk), :]
        pv = jnp.dot(p_ref[...], vj, preferred_element_type=F32)
        al = al_ref[...]
        for c in range(0, 2 * V_HEAD, LANES):
            acc[:, c:c + LANES] = al * acc[:, c:c + LANES] + pv[:, c:c + LANES]

    def softmax(s_ref, p_ref):
        s = s_ref[...]
        m_old = m_ref[...]
        m_new = jnp.maximum(m_old, jnp.max(s, axis=-1, keepdims=True))
        al_ref[...] = jnp.exp2(m_old - m_new)
        m_ref[...] = m_new
        for c in range(0, tk, LANES):
            p_ref[:, c:c + LANES] = jnp.exp2(s[:, c:c + LANES] - m_new).astype(p_ref.dtype)

    s_bufs, p_bufs = (sa, sb), (pa, pb)
    sa[...] = scores(0)
    acc[...] = jnp.zeros_like(acc)
    m_ref[...] = jnp.full_like(m_ref, -jnp.inf)
    for j in range(n_kv):
        if j + 1 < n_kv:
            s_bufs[(j + 1) % 2][...] = scores(j + 1)
        if j >= 1:
            accumulate(p_bufs[(j - 1) % 2], j - 1)
        softmax(s_bufs[j % 2], p_bufs[j % 2])
    accumulate(p_bufs[(n_kv - 1) % 2], n_kv - 1)
    o_ref[0] = (acc[:, :V_HEAD] / acc[:, V_HEAD:]).astype(o_ref.dtype)


def _attention(q, k, v, tq=1024, tk=256):
    B, H, T, Dk = q.shape
    S = k.shape[2]
    return pl.pallas_call(
        functools.partial(_attn_kernel, tk=tk), grid=(B, H, T // tq),
        in_specs=[pl.BlockSpec((1, 1, tq, Dk), lambda b, h, i: (b, h, i, 0)),
                  pl.BlockSpec((1, 1, S, Dk), lambda b, h, i: (b, h, 0, 0)),
                  pl.BlockSpec((1, 1, S, 2 * V_HEAD), lambda b, h, i: (b, h, 0, 0))],
        out_specs=pl.BlockSpec((1, tq, V_HEAD), lambda b, h, i: (b, i, h)),
        out_shape=jax.ShapeDtypeStruct((B, T, H * V_HEAD), BF16),
        scratch_shapes=[pltpu.VMEM((tq, tk), F32), pltpu.VMEM((tq, tk), F32),
                        pltpu.VMEM((tq, tk), BF16), pltpu.VMEM((tq, tk), BF16),
                        pltpu.VMEM((tq, 2 * V_HEAD), F32), pltpu.VMEM((tq, LANES), F32), pltpu.VMEM((tq, LANES), F32)],
        compiler_params=_cparams(("parallel", "parallel", "parallel")), name="mla_attn",
    )(q, k, v)


def _gates_kernel(h_ref, w_ref, b_ref, o_ref):
    o_ref[0] = jax.nn.sigmoid(jnp.dot(h_ref[0], w_ref[...], preferred_element_type=F32) + b_ref[...]).astype(o_ref.dtype)


def _gates(h, w_gate, gate_b, tm=1024, tn=1024):
    B, _, D = h.shape
    N = w_gate.shape[1]
    return pl.pallas_call(
        _gates_kernel, grid=(B, SEQ // tm, N // tn),
        in_specs=[pl.BlockSpec((1, tm, D), lambda b, i, j: (b, i, 0)),
                  pl.BlockSpec((D, tn), lambda b, i, j: (0, j)),
                  pl.BlockSpec((1, tn), lambda b, i, j: (0, j))],
        out_specs=pl.BlockSpec((1, tm, tn), lambda b, i, j: (b, i, j)),
        out_shape=jax.ShapeDtypeStruct((B, SEQ, N), BF16),
        compiler_params=_cparams(("parallel", "parallel", "parallel")), name="gates",
    )(h, w_gate, gate_b[None, :])


MERGE_TN = 512


def _merge_kernel(y0_ref, y1_ref, g_ref, bg_ref, lnw_ref, lnb_ref, avg_ref, o_ref, wr_ref, wm_ref, gate_ref, out_ref):
    avg = avg_ref[...]
    y = y0_ref[0, 0] + y1_ref[0, 0]
    yc = y - _head_sum(y, avg)
    yn = yc * lax.rsqrt(_head_sum(yc * yc, avg) + LNX_EPS) * lnw_ref[...] + lnb_ref[...]
    yr = (yn * g_ref[0].astype(F32) + bg_ref[0].astype(F32)).astype(BF16)
    o = o_ref[0]
    D = D_MODEL
    for n in range(0, D, MERGE_TN):
        gr = gate_ref[0, :, n:n + MERGE_TN].astype(F32)
        gm = gate_ref[0, :, D + n:D + n + MERGE_TN].astype(F32)
        out_ref[0, :, n:n + MERGE_TN] = (
            gr * jnp.dot(yr, wr_ref[:, n:n + MERGE_TN], preferred_element_type=F32)
            + gm * jnp.dot(o, wm_ref[:, n:n + MERGE_TN], preferred_element_type=F32)).astype(out_ref.dtype)


def _merge(y_dir, g, bg, lnx_w, lnx_b, avg2, o, w_rp, w_mp, gates, tm=256):
    B = o.shape[0]
    D = D_MODEL
    C = RWKV_DIM
    tok = pl.BlockSpec((1, tm, C), lambda b, i: (b, i, 0))
    row = lambda n: pl.BlockSpec((1, n), lambda b, i: (0, 0))
    return pl.pallas_call(
        _merge_kernel, grid=(B, SEQ // tm),
        in_specs=[pl.BlockSpec((1, 1, tm, C), lambda b, i: (b, 0, i, 0)),
                  pl.BlockSpec((1, 1, tm, C), lambda b, i: (b, 1, i, 0)),
                  tok, tok, row(C), row(C),
                  pl.BlockSpec((LANES, LANES), lambda b, i: (0, 0)),
                  pl.BlockSpec((1, tm, D), lambda b, i: (b, i, 0)),
                  pl.BlockSpec((C, D), lambda b, i: (0, 0)),
                  pl.BlockSpec((D, D), lambda b, i: (0, 0)),
                  pl.BlockSpec((1, tm, 2 * D), lambda b, i: (b, i, 0))],
        out_specs=pl.BlockSpec((1, tm, D), lambda b, i: (b, i, 0)),
        out_shape=jax.ShapeDtypeStruct((B, SEQ, D), BF16),
        compiler_params=_cparams(("parallel", "parallel")), name="merge",
    )(y_dir, y_dir, g, bg, lnx_w[None, :], lnx_b[None, :], avg2, o, w_rp, w_mp, gates)


def _outproj_kernel(m_ref, w_ref, x_ref, gpost_ref, gate_ref, gpre_ref, mod_ref, x1_ref, h_ref):
    out = jnp.dot(m_ref[0], w_ref[...], preferred_element_type=F32)
    x1 = x_ref[0] + gate_ref[0] * _rms(out, gpost_ref[...])
    x1_ref[0] = x1
    mod = mod_ref[0]
    h_ref[0] = (_rms(x1, gpre_ref[...]) * (1.0 + mod[1:2]) + mod[0:1]).astype(h_ref.dtype)


def _outproj(merged, w_out, x, g_post, gate, g_pre, mod2, tm=512):
    B = x.shape[0]
    D = D_MODEL
    return pl.pallas_call(
        _outproj_kernel, grid=(B, SEQ // tm),
        in_specs=[pl.BlockSpec((1, tm, D), lambda b, i: (b, i, 0)),
                  pl.BlockSpec((D, D), lambda b, i: (0, 0)),
                  pl.BlockSpec((1, tm, D), lambda b, i: (b, i, 0)),
                  pl.BlockSpec((1, D), lambda b, i: (0, 0)),
                  pl.BlockSpec((1, 1, D), lambda b, i: (b, 0, 0)),
                  pl.BlockSpec((1, D), lambda b, i: (0, 0)),
                  pl.BlockSpec((1, 2, D), lambda b, i: (b, 0, 0))],
        out_specs=[pl.BlockSpec((1, tm, D), lambda b, i: (b, i, 0)),
                   pl.BlockSpec((1, tm, D), lambda b, i: (b, i, 0))],
        out_shape=[jax.ShapeDtypeStruct((B, SEQ, D), F32),
                   jax.ShapeDtypeStruct((B, SEQ, D), BF16)],
        compiler_params=_cparams(("parallel", "parallel")), name="outproj",
    )(merged, w_out, x, g_post[None, :], gate[:, None, :], g_pre[None, :], mod2)


HALO = 16


def _ffn_kernel(h_ref, hp_ref, hn_ref, wg_ref, wv_ref, wd_ref, cw_ref, cb_ref, x_ref, gate_ref, gpost_ref,
                o_ref, acc_ref):
    i = pl.program_id(1)
    f = pl.program_id(2)
    tm = h_ref.shape[1]

    @pl.when(f == 0)
    def _():
        acc_ref[...] = jnp.zeros_like(acc_ref)

    h = h_ref[0]
    wg = wg_ref[...]
    g_ext = jnp.dot(jnp.concatenate([hp_ref[0], h, hn_ref[0]], axis=0), wg, preferred_element_type=F32)
    g = g_ext[HALO:HALO + tm]
    g_prev = g_ext[HALO - 1:HALO]
    g_next = g_ext[HALO + tm:HALO + tm + 1]
    g_prev = jnp.where(i == 0, 0.0, g_prev)
    g_next = jnp.where(i == pl.num_programs(1) - 1, 0.0, g_next)
    rows = lax.broadcasted_iota(jnp.int32, g.shape, 0)
    up = jnp.where(rows == 0, g_prev, pltpu.roll(g, 1, 0))
    dn = jnp.where(rows == tm - 1, g_next, pltpu.roll(g, tm - 1, 0))
    cw = cw_ref[...]
    u = cb_ref[...] + up * cw[0:1] + g * cw[1:2] + dn * cw[2:3]
    val = jnp.dot(h, wv_ref[...], preferred_element_type=F32)
    act = jax.nn.gelu(u, approximate=True) * val
    acc_ref[...] += jnp.dot(act.astype(BF16), wd_ref[...], preferred_element_type=F32)

    @pl.when(f == pl.num_programs(2) - 1)
    def _():
        o_ref[0] = x_ref[0] + gate_ref[0] * _rms(acc_ref[...], gpost_ref[...])


def _ffn(h, wg, wv, wd, cw, cb, x1, gate, g_post, tm=512, tf=512):
    B = h.shape[0]
    D = D_MODEL
    hb = tm // HALO
    last = SEQ // HALO - 1
    return pl.pallas_call(
        _ffn_kernel, grid=(B, SEQ // tm, D_FF // tf),
        in_specs=[pl.BlockSpec((1, tm, D), lambda b, i, f: (b, i, 0)),
                  pl.BlockSpec((1, HALO, D), lambda b, i, f: (b, jnp.maximum(i * hb - 1, 0), 0)),
                  pl.BlockSpec((1, HALO, D), lambda b, i, f: (b, jnp.minimum((i + 1) * hb, last), 0)),
                  pl.BlockSpec((D, tf), lambda b, i, f: (0, f)),
                  pl.BlockSpec((D, tf), lambda b, i, f: (0, f)),
                  pl.BlockSpec((tf, D), lambda b, i, f: (f, 0)),
                  pl.BlockSpec((3, tf), lambda b, i, f: (0, f)),
                  pl.BlockSpec((1, tf), lambda b, i, f: (0, f)),
                  pl.BlockSpec((1, tm, D), lambda b, i, f: (b, i, 0)),
                  pl.BlockSpec((1, 1, D), lambda b, i, f: (b, 0, 0)),
                  pl.BlockSpec((1, D), lambda b, i, f: (0, 0))],
        out_specs=pl.BlockSpec((1, tm, D), lambda b, i, f: (b, i, 0)),
        out_shape=jax.ShapeDtypeStruct((B, SEQ, D), F32),
        scratch_shapes=[pltpu.VMEM((tm, D), F32)],
        compiler_params=_cparams(("parallel", "parallel", "arbitrary")), name="convffn",
    )(h, h, h, wg, wv, wd, cw, cb[None, :], x1, gate[:, None, :], g_post[None, :])


def _rope_partner(w):
    q = QK_ROPE // 4
    return jnp.concatenate([-w[..., q:2 * q], w[..., :q], -w[..., 3 * q:], w[..., 2 * q:3 * q]], axis=-1)


def _pack_w_main(w):
    w = w.astype(BF16)
    kpe = w[:, RWKV_IN + Q_LORA + KV_LORA:RWKV_IN + MLA_IN]
    pad = jnp.zeros((w.shape[0], C_CQ - C_KPE - 2 * QK_ROPE), BF16)
    return jnp.concatenate([w[:, :RWKV_IN], kpe, _rope_partner(kpe), pad,
                            w[:, RWKV_IN:RWKV_IN + Q_LORA + KV_LORA]], axis=1)


def _lora_window_weight(w2, first_row):
    n_dir, R, C = w2.shape
    out = jnp.zeros((LORA_WIN, n_dir, C), BF16)
    for d in range(n_dir):
        out = out.at[first_row + d * R:first_row + (d + 1) * R, d].set(w2[d].astype(BF16))
    return out.reshape(LORA_WIN, n_dir * C)


def _rope_tables():
    rows = SEQ // GRID_W
    row = jnp.repeat(jnp.arange(rows), GRID_W).astype(F32)
    col = jnp.tile(jnp.arange(GRID_W), rows).astype(F32)
    half = QK_ROPE // 2
    freqs = ROPE_THETA ** (-jnp.arange(0, half, 2, dtype=F32) / half)
    ar, ac = row[:, None] * freqs, col[:, None] * freqs
    cos = jnp.concatenate([jnp.cos(ar), jnp.cos(ar), jnp.cos(ac), jnp.cos(ac)], axis=-1)
    sin = jnp.concatenate([jnp.sin(ar), jnp.sin(ar), jnp.sin(ac), jnp.sin(ac)], axis=-1)
    cos_all = jnp.concatenate([cos, jnp.ones((CTX_LEN, QK_ROPE), F32)], axis=0)
    sin_all = jnp.concatenate([sin, jnp.zeros((CTX_LEN, QK_ROPE), F32)], axis=0)
    return cos_all, sin_all


def kernel(x, c, ctx, c_ctx, w_ada, b_ada, norm_mix_pre, norm_mix_post, norm_ffn_pre, norm_ffn_post, w_in, rwkv_mu, rwkv_w0, rwkv_w2, rwkv_a0, rwkv_a2, rwkv_k_k, rwkv_k_a, rwkv_r_k, rwkv_lnx_w, rwkv_lnx_b, rwkv_g2, w_rwkv_proj, mla_q_norm, mla_kv_norm, mla_w_uq, mla_w_ukv, w_mla_proj, gate_b, w_out, ffn_w_gate, ffn_w_val, ffn_conv_w, ffn_conv_b, ffn_w_down):
    B = x.shape[0]
    D = D_MODEL
    C = RWKV_DIM
    l = 0

    s = jnp.concatenate([jax.nn.silu(c), jax.nn.silu(c_ctx)[None, :]], axis=0)
    mods = _ada(s, w_ada[l], b_ada[l])
    lat = mods[:B].reshape(B, 6, D)
    cm = mods[B].reshape(6, D)
    sh1, sc1, g1, sh2, sc2, g2 = [lat[:, j] for j in range(6)]
    mods1 = jnp.stack([jnp.stack([sh1, sc1], axis=1),
                       jnp.broadcast_to(jnp.stack([cm[0], cm[1]])[None], (B, 2, D))], axis=1)
    mods2 = jnp.stack([sh2, sc2], axis=1)

    h = _norm_mod(x, ctx, norm_mix_pre[l], mods1)
    p = _matmul(h.reshape(B * S_ALL, D), _pack_w_main(w_in[l]), tm=1088, tn=1024,
                name="w_in").reshape(B, S_ALL, P_COLS)
    gates = _gates(h, w_in[l][:, RWKV_IN + MLA_IN:].astype(BF16), gate_b[l])

    lane_head = jnp.arange(LANES) // RWKV_HEAD
    ones2 = (lane_head[:, None] == lane_head[None, :]).astype(BF16)
    avg2 = ones2 * (1.0 / RWKV_HEAD)
    r, k, v, kk, lw, a, g, bg = _prep(p, rwkv_mu[l], rwkv_w0[l], _lora_window_weight(rwkv_w2[l], C_WL - C_WL_WIN),
                                      rwkv_a0[l], _lora_window_weight(rwkv_a2[l], C_AL - C_AL_WIN), rwkv_k_k[l],
                                      rwkv_k_a[l], rwkv_r_k[l].reshape(C), rwkv_g2[l].astype(BF16), ones2)
    y_dir = _wkv(r, k, v, kk, lw, a, rwkv_k_a[l])

    cos, sin = _rope_tables()
    uq = mla_w_uq[l].reshape(Q_LORA, MLA_HEADS, QK_NOPE + QK_ROPE)
    wq = jnp.concatenate([uq, _rope_partner(uq[..., QK_NOPE:])], axis=-1).transpose(1, 0, 2).astype(BF16)
    wkv = mla_w_ukv[l].reshape(KV_LORA, MLA_HEADS, QK_NOPE + V_HEAD).transpose(1, 0, 2).astype(BF16)
    q = _qproj(p, mla_q_norm[l], wq, cos[:SEQ], sin[:SEQ])
    k_all, v_all = _kvproj(p, mla_kv_norm[l], wkv, cos, sin)
    o = _attention(q, k_all, v_all)

    merged = _merge(y_dir, g, bg, rwkv_lnx_w[l], rwkv_lnx_b[l], avg2, o, w_rwkv_proj[l].astype(BF16),
                    w_mla_proj[l].astype(BF16), gates)
    x1, h2 = _outproj(merged, w_out[l].astype(BF16), x, norm_mix_post[l], g1, norm_ffn_pre[l], mods2)
    return _ffn(h2, ffn_w_gate[l].astype(BF16), ffn_w_val[l].astype(BF16), ffn_w_down[l].astype(BF16),
                ffn_conv_w[l], ffn_conv_b[l], x1, g2, norm_ffn_post[l])
```

```python
import functools
import math

import jax
import jax.numpy as jnp
from jax import lax
from jax.experimental import pallas as pl
from jax.experimental.pallas import tpu as pltpu

D_MODEL = 2048
BATCH = 2
SEQ = 4096
GRID_W = 64
CTX_LEN = 256
S_ALL = SEQ + CTX_LEN
EPS = 1e-6
RWKV_HEADS = 16
RWKV_HEAD = 64
RWKV_DIM = RWKV_HEADS * RWKV_HEAD
DECAY_LORA = 96
AAA_LORA = 96
GATE_LORA = 256
N_DIR = 2
DECAY_SCALE = math.exp(-0.5)
LNX_EPS = 64e-5
MLA_HEADS = 16
Q_LORA = 512
KV_LORA = 512
QK_NOPE = 128
QK_ROPE = 64
V_HEAD = 128
ROPE_THETA = 10000.0
ATTN_SCALE = (QK_NOPE + QK_ROPE) ** -0.5
LOG2_E = math.log2(math.e)
D_FF = 5632
RWKV_IN = 3 * RWKV_DIM + N_DIR * DECAY_LORA + N_DIR * AAA_LORA + GATE_LORA
MLA_IN = Q_LORA + KV_LORA + QK_ROPE

LANES = 128
SUBLANES = 8
VMEM_LIMIT = 56 * 1024 * 1024

C_R, C_K, C_V = 0, RWKV_DIM, 2 * RWKV_DIM
C_WL = 3 * RWKV_DIM
C_AL = C_WL + N_DIR * DECAY_LORA
C_GL = C_AL + N_DIR * AAA_LORA
RWKV_COLS = C_GL + GATE_LORA
C_KPE = RWKV_COLS
C_CQ = -(-(C_KPE + 2 * QK_ROPE) // Q_LORA) * Q_LORA
C_CKV = C_CQ + Q_LORA
P_COLS = C_CKV + KV_LORA
LORA_WIN = 2 * LANES
C_WL_WIN = C_WL // LANES * LANES
C_AL_WIN = C_AL // LANES * LANES

WKV_CHUNK = 64
WKV_SUB = 4
HEAD_PAIRS = RWKV_DIM // LANES

F32 = jnp.float32
BF16 = jnp.bfloat16


def _cparams(sem):
    return pltpu.CompilerParams(dimension_semantics=sem, vmem_limit_bytes=VMEM_LIMIT)


def _bdot(a, b):
    return jnp.dot(a.astype(BF16), b.astype(BF16), preferred_element_type=F32)


def _hilo_dot(x, w):
    hi = x.astype(BF16)
    lo = (x - hi.astype(F32)).astype(BF16)
    return jnp.dot(hi, w, preferred_element_type=F32) + jnp.dot(lo, w, preferred_element_type=F32)


def _head_sum(x, ones2):
    return jnp.concatenate([_hilo_dot(x[:, c * LANES:(c + 1) * LANES], ones2) for c in range(HEAD_PAIRS)], axis=-1)


def _ada_kernel(s_ref, w_ref, b_ref, o_ref):
    w = w_ref[...]
    for r in range(s_ref.shape[0]):
        o_ref[r:r + 1, :] = jnp.sum(s_ref[r] * w, axis=0, keepdims=True) + b_ref[...]


def _ada(s, w, b, tn=1024):
    R, K = s.shape
    N = w.shape[1]
    return pl.pallas_call(
        _ada_kernel, grid=(N // tn,),
        in_specs=[pl.BlockSpec((R, K, 1), lambda j: (0, 0, 0)),
                  pl.BlockSpec((K, tn), lambda j: (0, j)),
                  pl.BlockSpec((1, tn), lambda j: (0, j))],
        out_specs=pl.BlockSpec((R, tn), lambda j: (0, j)),
        out_shape=jax.ShapeDtypeStruct((R, N), F32),
        compiler_params=_cparams(("parallel",)), name="ada",
    )(s[:, :, None], w, b[None, :])


def _norm_mod_kernel(x_ref, c_ref, g_ref, m_ref, o_ref, *, lat_tiles):
    def emit(x):
        n = x * lax.rsqrt(jnp.mean(x * x, axis=-1, keepdims=True) + EPS) * g_ref[...]
        m = m_ref[0, 0]
        o_ref[0] = (n * (1.0 + m[1:2]) + m[0:1]).astype(o_ref.dtype)

    is_latent = pl.program_id(1) < lat_tiles

    @pl.when(is_latent)
    def _():
        emit(x_ref[0])

    @pl.when(jnp.logical_not(is_latent))
    def _():
        emit(c_ref[0])


def _norm_mod(x, ctx, gain, mods, tm=256):
    B, _, D = x.shape
    assert ctx.shape[1] == tm
    lat_tiles = SEQ // tm
    return pl.pallas_call(
        functools.partial(_norm_mod_kernel, lat_tiles=lat_tiles), grid=(B, S_ALL // tm),
        in_specs=[pl.BlockSpec((1, tm, D), lambda b, i: (b, jnp.minimum(i, lat_tiles - 1), 0)),
                  pl.BlockSpec((1, tm, D), lambda b, i: (b, 0, 0)),
                  pl.BlockSpec((1, D), lambda b, i: (0, 0)),
                  pl.BlockSpec((1, 1, 2, D), lambda b, i: (b, i // lat_tiles, 0, 0))],
        out_specs=pl.BlockSpec((1, tm, D), lambda b, i: (b, i, 0)),
        out_shape=jax.ShapeDtypeStruct((B, S_ALL, D), BF16),
        compiler_params=_cparams(("parallel", "parallel")), name="norm_mod",
    )(x, ctx, gain[None, :], mods)


def _mm_kernel(a_ref, w_ref, o_ref):
    o_ref[...] = _bdot(a_ref[...], w_ref[...]).astype(o_ref.dtype)


def _matmul(a, w, tm, tn, out_dtype=F32, name="matmul"):
    M, K = a.shape
    N = w.shape[1]
    return pl.pallas_call(
        _mm_kernel, grid=(M // tm, N // tn),
        in_specs=[pl.BlockSpec((tm, K), lambda i, j: (i, 0)),
                  pl.BlockSpec((K, tn), lambda i, j: (0, j))],
        out_specs=pl.BlockSpec((tm, tn), lambda i, j: (i, j)),
        out_shape=jax.ShapeDtypeStruct((M, N), out_dtype),
        compiler_params=_cparams(("parallel", "parallel")), name=name,
    )(a, w)


def _prep_kernel(p_ref, pp_ref, pn_ref, mu_ref, w0_ref, w2_ref, a0_ref, a2_ref, kk_ref, ka_ref, rk_ref, g2_ref,
                 ones_ref, r_o, k_o, v_o, kkn_o, lw_o, a_o, g_o, bg_o):
    i = pl.program_id(1)
    x = p_ref[0]
    tm = x.shape[0]
    lat_tiles = SEQ // tm
    first = jnp.logical_or(i == 0, i == lat_tiles)
    last = jnp.logical_or(i == lat_tiles - 1, i == pl.num_programs(1) - 1)
    x_before = jnp.where(first, 0.0, pp_ref[0][SUBLANES - 1:SUBLANES])
    x_after = jnp.where(last, 0.0, pn_ref[0][0:1])
    rows = lax.broadcasted_iota(jnp.int32, (tm, 1), 0)
    prev = jnp.where(rows == 0, x_before, pltpu.roll(x, 1, 0))
    nxt = jnp.where(rows == tm - 1, x_after, pltpu.roll(x, tm - 1, 0))
    xs = x + mu_ref[...] * (0.5 * (prev + nxt) - x)

    C = RWKV_DIM
    r, k, v = xs[:, C_R:C_R + C], xs[:, C_K:C_K + C], xs[:, C_V:C_V + C]
    w_lora = _bdot(jnp.tanh(xs[:, C_WL_WIN:C_WL_WIN + LORA_WIN]), w2_ref[...])
    a_lora = _bdot(xs[:, C_AL_WIN:C_AL_WIN + LORA_WIN], a2_ref[...])
    a_sum = None
    for d in range(N_DIR):
        w_raw = w0_ref[d:d + 1] + w_lora[:, d * C:(d + 1) * C]
        lw_o[0, d] = -DECAY_SCALE * jax.nn.sigmoid(w_raw)
        a_d = jax.nn.sigmoid(a0_ref[d:d + 1] + a_lora[:, d * C:(d + 1) * C])
        a_o[0, d] = a_d
        a_sum = a_d if a_sum is None else a_sum + a_d
    ones2 = ones_ref[...]
    kkx = k * kk_ref[...]
    kkn_o[0] = kkx / jnp.maximum(jnp.sqrt(_head_sum(kkx * kkx, ones2)), 1e-12)
    g = _bdot(jax.nn.sigmoid(xs[:, C_GL:C_GL + GATE_LORA]), g2_ref[...])
    k_bar = k * (1.0 + (0.5 * a_sum - 1.0) * ka_ref[...])
    bonus = _head_sum(r * k_bar * rk_ref[...], ones2) * v
    r_o[0] = r
    k_o[0] = k
    v_o[0] = v
    g_o[0] = g.astype(g_o.dtype)
    bg_o[0] = (bonus * g).astype(bg_o.dtype)


def _prep(p, mu, w0, w2p, a0, a2p, k_k, k_a, r_k, g2, ones2, tm=256):
    B = p.shape[0]
    C = RWKV_DIM
    hb = tm // SUBLANES
    last = S_ALL // SUBLANES - 1
    row = lambda n: pl.BlockSpec((1, n), lambda b, i: (0, 0))
    tok = pl.BlockSpec((1, tm, C), lambda b, i: (b, i, 0))
    tok_dir = pl.BlockSpec((1, N_DIR, tm, C), lambda b, i: (b, 0, i, 0))
    shp = jax.ShapeDtypeStruct((B, S_ALL, C), F32)
    shp_dir = jax.ShapeDtypeStruct((B, N_DIR, S_ALL, C), F32)
    shp_b = jax.ShapeDtypeStruct((B, S_ALL, C), BF16)
    return pl.pallas_call(
        _prep_kernel, grid=(B, S_ALL // tm),
        in_specs=[pl.BlockSpec((1, tm, RWKV_COLS), lambda b, i: (b, i, 0)),
                  pl.BlockSpec((1, SUBLANES, RWKV_COLS), lambda b, i: (b, jnp.maximum(i * hb - 1, 0), 0)),
                  pl.BlockSpec((1, SUBLANES, RWKV_COLS), lambda b, i: (b, jnp.minimum((i + 1) * hb, last), 0)),
                  row(RWKV_COLS),
                  pl.BlockSpec((N_DIR, C), lambda b, i: (0, 0)),
                  pl.BlockSpec((LORA_WIN, N_DIR * C), lambda b, i: (0, 0)),
                  pl.BlockSpec((N_DIR, C), lambda b, i: (0, 0)),
                  pl.BlockSpec((LORA_WIN, N_DIR * C), lambda b, i: (0, 0)),
                  row(C), row(C), row(C),
                  pl.BlockSpec((GATE_LORA, C), lambda b, i: (0, 0)),
                  pl.BlockSpec((LANES, LANES), lambda b, i: (0, 0))],
        out_specs=[tok, tok, tok, tok, tok_dir, tok_dir, tok, tok],
        out_shape=[shp, shp, shp, shp, shp_dir, shp_dir, shp_b, shp_b],
        compiler_params=_cparams(("parallel", "parallel")), name="rwkv_prep",
    )(p, p, p, mu[None, :], w0, w2p, a0, a2p, k_k[None, :], k_a[None, :], r_k[None, :], g2, ones2)


def _wkv_kernel(r_ref, k_ref, v_ref, kk_ref, lw_ref, a_ref, ka_ref, y_ref, s_ref):
    d = pl.program_id(1)
    C = WKV_CHUNK
    C2 = 2 * C
    P = HEAD_PAIRS
    sign = 1 - 2 * d
    fwd = d == 0

    @pl.when(pl.program_id(2) == 0)
    def _():
        s_ref[...] = jnp.zeros_like(s_ref)

    trow = lax.broadcasted_iota(jnp.int32, (C, C), 0)
    tcol = lax.broadcasted_iota(jnp.int32, (C, C), 1)
    tri = ((trow - tcol) * sign >= 0).astype(BF16)
    lane = lax.broadcasted_iota(jnp.int32, (C, LANES), 1)
    head0 = lane < RWKV_HEAD
    zero = jnp.zeros((), F32)

    def expand(x):
        x = x.astype(BF16)
        zb = jnp.zeros((), BF16)
        return jnp.stack([jnp.concatenate([jnp.where(head0, x[:, j * LANES:(j + 1) * LANES], zb),
                                           jnp.where(head0, zb, x[:, j * LANES:(j + 1) * LANES])], axis=0)
                          for j in range(P)])

    def bmm(x, y):
        return jnp.einsum('hab,hbc->hac', x.astype(BF16), y.astype(BF16), preferred_element_type=F32)

    def bmm_nt(x, y):
        return jnp.einsum('han,hbn->hab', x.astype(BF16), y.astype(BF16), preferred_element_type=F32)

    def bmm_tn(x, y):
        return jnp.einsum('hca,hcb->hab', x.astype(BF16), y.astype(BF16), preferred_element_type=F32)

    row = lax.broadcasted_iota(jnp.int32, (C2, C2), 0)
    col = lax.broadcasted_iota(jnp.int32, (C2, C2), 1)
    same = (row // C) == (col // C)
    lag = (row - col) * sign
    incl = jnp.logical_and(same, lag >= 0)[None]
    strict = jnp.logical_and(same, lag > 0)[None]

    def blk(m):
        return (row // m) == (col // m)

    def half_rows(t, m, second):
        return jnp.concatenate([t[:, g + second * m:g + (second + 1) * m] for g in range(0, C2, 2 * m)], axis=1)

    def state_free(rows):
        r = r_ref[0, rows, :]
        v = v_ref[0, rows, :]
        kk = kk_ref[0, rows, :]
        lw = lw_ref[0, 0, rows, :]
        a = a_ref[0, 0, rows, :]
        k = k_ref[0, rows, :] * (1.0 + (a - 1.0) * ka_ref[...])

        cum = _hilo_dot_left(tri, lw)
        tot = jnp.sum(lw, axis=0, keepdims=True)
        e_cum = jnp.exp(cum)
        e_neg = jnp.exp(-cum)
        e_rem = jnp.exp(tot - cum)
        b = kk * a
        at = -kk * jnp.exp(cum - lw)
        dec = jnp.exp(tot)
        AT, RT, BT, KT, BD, KD, V = (expand(t) for t in (at, r * e_cum, b * e_neg, k * e_neg, b * e_rem,
                                                         k * e_rem, v))
        AR = jnp.concatenate([AT, RT], axis=1)
        G = bmm_nt(AR, jnp.concatenate([BT, KT], axis=1))
        a_ab, a_ak = G[:, :C2, :C2], G[:, :C2, C2:]
        a_rb, a_rk = G[:, C2:, :C2], G[:, C2:, C2:]

        L = jnp.where(strict, a_ab, zero)
        Lb = jnp.where(blk(8)[None], L, zero)
        L2 = bmm(Lb, Lb)
        L4 = bmm(L2, L2)
        S1 = Lb + L2 + bmm(Lb, L2)
        Nm = S1 + L4 + bmm(L4, S1)
        for m in (8, 16, 32):
            Lo = jnp.where(jnp.logical_and(blk(2 * m), jnp.logical_not(blk(m)))[None], L, zero)
            n_first, n_second = half_rows(Nm, m, 0), half_rows(Nm, m, 1)
            n_late = jnp.where(fwd, n_second, n_first)
            p_late = jnp.where(fwd, half_rows(Lo, m, 1), half_rows(Lo, m, 0)) + bmm(n_late, Lo)
            n_late = n_late + p_late + bmm(p_late, Nm)
            n_first, n_second = jnp.where(fwd, n_first, n_late), jnp.where(fwd, n_late, n_second)
            Nm = jnp.concatenate([piece for g in range(C2 // (2 * m))
                                  for piece in (n_first[:, g * m:(g + 1) * m], n_second[:, g * m:(g + 1) * m])],
                                 axis=1)
        rhs_v = bmm(jnp.where(strict, a_ak, zero), V)
        a_y = jnp.concatenate([jnp.where(incl, a_rb, zero), jnp.where(incl, a_rk, zero)], axis=2).astype(BF16)
        dec3 = jnp.stack([dec[:, j * LANES:(j + 1) * LANES] for j in range(P)])
        return AR, rhs_v, Nm.astype(BF16), V, a_y, jnp.concatenate([BD, KD], axis=1), dec3

    def state_step(part, s0):
        AR, rhs_v, Nm, V, a_y, BKD, dec3 = part
        ars0 = bmm_nt(AR, s0)
        rhs = ars0[:, :C2] + rhs_v
        u = rhs + bmm(Nm, rhs)
        uv = jnp.concatenate([u.astype(BF16), V], axis=1)
        y2 = ars0[:, C2:] + bmm(a_y, uv)
        y = y2[:, :C] + y2[:, C:]
        return jnp.concatenate([y[j] for j in range(P)], axis=-1), s0 * dec3 + bmm_tn(uv, BKD)

    chunk_rows = [pl.ds(pl.multiple_of(jnp.where(fwd, j, WKV_SUB - 1 - j) * C, C), C) for j in range(WKV_SUB)]
    parts = [state_free(rows) for rows in chunk_rows]
    s = s_ref[...]
    for rows, part in zip(chunk_rows, parts):
        y, s = state_step(part, s)
        y_ref[0, 0, rows, :] = y
    s_ref[...] = s


def _hilo_dot_left(w, x):
    hi = x.astype(BF16)
    lo = (x - hi.astype(F32)).astype(BF16)
    return jnp.dot(w, hi, preferred_element_type=F32) + jnp.dot(w, lo, preferred_element_type=F32)


def _wkv(r, k, v, kk, lw, a, k_a):
    B, S, Cd = r.shape
    R = WKV_SUB * WKV_CHUNK
    n_blocks = S // R
    lat_blocks = SEQ // R

    def bidx(d, i):
        return jnp.where(d == 0, (i + lat_blocks) % n_blocks, n_blocks - 1 - i)

    shared = pl.BlockSpec((1, R, Cd), lambda b, d, i: (b, bidx(d, i), 0))
    perdir = pl.BlockSpec((1, 1, R, Cd), lambda b, d, i: (b, d, bidx(d, i), 0))
    return pl.pallas_call(
        _wkv_kernel, grid=(B, N_DIR, n_blocks),
        in_specs=[shared, shared, shared, shared, perdir, perdir, pl.BlockSpec((1, Cd), lambda b, d, i: (0, 0))],
        out_specs=perdir,
        out_shape=jax.ShapeDtypeStruct((B, N_DIR, S, Cd), F32),
        scratch_shapes=[pltpu.VMEM((HEAD_PAIRS, LANES, LANES), F32)],
        compiler_params=_cparams(("parallel", "parallel", "arbitrary")), name="wkv7",
    )(r, k, v, kk, lw, a, k_a[None, :])


def _rms(x, g):
    return x * lax.rsqrt(jnp.mean(x * x, axis=-1, keepdims=True) + EPS) * g


def _qproj_kernel(p_ref, g_ref, w_ref, cos_ref, sin_ref, q_ref):
    cq = _rms(p_ref[0], g_ref[...]).astype(BF16)
    cs, sn = cos_ref[...], sin_ref[...]
    for h in range(MLA_HEADS):
        y = jnp.dot(cq, w_ref[h], preferred_element_type=F32)
        rope = y[:, QK_NOPE:QK_NOPE + QK_ROPE] * cs + y[:, QK_NOPE + QK_ROPE:] * sn
        q = jnp.concatenate([y[:, :QK_NOPE], rope], axis=-1) * (ATTN_SCALE * LOG2_E)
        q_ref[0, h] = q.astype(q_ref.dtype)


def _qproj(p, gain, wq, cos, sin, tm=512):
    B = p.shape[0]
    return pl.pallas_call(
        _qproj_kernel, grid=(B, SEQ // tm),
        in_specs=[pl.BlockSpec((1, tm, Q_LORA), lambda b, i: (b, i, C_CQ // Q_LORA)),
                  pl.BlockSpec((1, Q_LORA), lambda b, i: (0, 0)),
                  pl.BlockSpec((MLA_HEADS, Q_LORA, 2 * LANES), lambda b, i: (0, 0, 0)),
                  pl.BlockSpec((tm, QK_ROPE), lambda b, i: (i, 0)),
                  pl.BlockSpec((tm, QK_ROPE), lambda b, i: (i, 0))],
        out_specs=pl.BlockSpec((1, MLA_HEADS, tm, QK_NOPE + QK_ROPE), lambda b, i: (b, 0, i, 0)),
        out_shape=jax.ShapeDtypeStruct((B, MLA_HEADS, SEQ, QK_NOPE + QK_ROPE), BF16),
        compiler_params=_cparams(("parallel", "parallel")), name="mla_q",
    )(p, gain[None, :], wq, cos, sin)


def _kvproj_kernel(p_ref, pe_ref, g_ref, w_ref, cos_ref, sin_ref, k_ref, v_ref):
    ckv = _rms(p_ref[0], g_ref[...]).astype(BF16)
    pe = pe_ref[0]
    kpe = pe[:, :QK_ROPE] * cos_ref[...] + pe[:, QK_ROPE:] * sin_ref[...]
    ones = jnp.ones((ckv.shape[0], V_HEAD), F32)
    for h in range(MLA_HEADS):
        y = jnp.dot(ckv, w_ref[h], preferred_element_type=F32)
        k_ref[0, h] = jnp.concatenate([y[:, :QK_NOPE], kpe], axis=-1).astype(k_ref.dtype)
        v_ref[0, h] = jnp.concatenate([y[:, QK_NOPE:], ones], axis=-1).astype(v_ref.dtype)


def _kvproj(p, gain, wkv, cos, sin, tm=256):
    B = p.shape[0]
    return pl.pallas_call(
        _kvproj_kernel, grid=(B, S_ALL // tm),
        in_specs=[pl.BlockSpec((1, tm, KV_LORA), lambda b, i: (b, i, C_CKV // KV_LORA)),
                  pl.BlockSpec((1, tm, 2 * QK_ROPE), lambda b, i: (b, i, C_KPE // (2 * QK_ROPE))),
                  pl.BlockSpec((1, KV_LORA), lambda b, i: (0, 0)),
                  pl.BlockSpec((MLA_HEADS, KV_LORA, 2 * LANES), lambda b, i: (0, 0, 0)),
                  pl.BlockSpec((tm, QK_ROPE), lambda b, i: (i, 0)),
                  pl.BlockSpec((tm, QK_ROPE), lambda b, i: (i, 0))],
        out_specs=[pl.BlockSpec((1, MLA_HEADS, tm, QK_NOPE + QK_ROPE), lambda b, i: (b, 0, i, 0)),
                   pl.BlockSpec((1, MLA_HEADS, tm, 2 * V_HEAD), lambda b, i: (b, 0, i, 0))],
        out_shape=[jax.ShapeDtypeStruct((B, MLA_HEADS, S_ALL, QK_NOPE + QK_ROPE), BF16),
                   jax.ShapeDtypeStruct((B, MLA_HEADS, S_ALL, 2 * V_HEAD), BF16)],
        compiler_params=_cparams(("parallel", "parallel")), name="mla_kv",
    )(p, p, gain[None, :], wkv, cos, sin)


def _attn_kernel(q_ref, k_ref, v_ref, o_ref, sa, sb, pa, pb, acc, m_ref, al_ref, *, tk):
    q = q_ref[0, 0]
    n_keys = k_ref.shape[2]
    blocks = [(off, min(tk, n_keys - off)) for off in range(0, n_keys, tk)]
    n_kv = len(blocks)

    def put_scores(s_ref, j):
        off, w = blocks[j]
        kj = k_ref[0, 0, off:off + w, :]
        s_ref[:, :w] = lax.dot_general(q, kj, (((1,), (1,)), ((), ())), preferred_element_type=F32)

    def accumulate(p_ref, j):
        off, w = blocks[j]
        pv = jnp.dot(p_ref[:, :w], v_ref[0, 0, off:off + w, :], preferred_element_type=F32)
        al = al_ref[...]
        for c in range(0, 2 * V_HEAD, LANES):
            acc[:, c:c + LANES] = al * acc[:, c:c + LANES] + pv[:, c:c + LANES]

    def softmax(s_ref, p_ref, j):
        w = blocks[j][1]
        s = s_ref[:, :w]
        m_old = m_ref[...]
        m_new = jnp.maximum(m_old, jnp.max(s, axis=-1, keepdims=True))
        al_ref[...] = jnp.exp2(m_old - m_new)
        m_ref[...] = m_new
        for c in range(0, w, LANES):
            p_ref[:, c:c + LANES] = jnp.exp2(s[:, c:c + LANES] - m_new).astype(p_ref.dtype)

    s_bufs, p_bufs = (sa, sb), (pa, pb)
    put_scores(sa, 0)
    acc[...] = jnp.zeros_like(acc)
    m_ref[...] = jnp.full_like(m_ref, -jnp.inf)
    for j in range(n_kv):
        if j + 1 < n_kv:
            put_scores(s_bufs[(j + 1) % 2], j + 1)
        if j >= 1:
            accumulate(p_bufs[(j - 1) % 2], j - 1)
        softmax(s_bufs[j % 2], p_bufs[j % 2], j)
    accumulate(p_bufs[(n_kv - 1) % 2], n_kv - 1)
    o_ref[0] = (acc[:, :V_HEAD] / acc[:, V_HEAD:]).astype(o_ref.dtype)


def _attention(q, k, v, tq=1024, tk=1024):
    B, H, T, Dk = q.shape
    S = k.shape[2]
    return pl.pallas_call(
        functools.partial(_attn_kernel, tk=tk), grid=(B, H, T // tq),
        in_specs=[pl.BlockSpec((1, 1, tq, Dk), lambda b, h, i: (b, h, i, 0)),
                  pl.BlockSpec((1, 1, S, Dk), lambda b, h, i: (b, h, 0, 0)),
                  pl.BlockSpec((1, 1, S, 2 * V_HEAD), lambda b, h, i: (b, h, 0, 0))],
        out_specs=pl.BlockSpec((1, tq, V_HEAD), lambda b, h, i: (b, i, h)),
        out_shape=jax.ShapeDtypeStruct((B, T, H * V_HEAD), BF16),
        scratch_shapes=[pltpu.VMEM((tq, tk), F32), pltpu.VMEM((tq, tk), F32),
                        pltpu.VMEM((tq, tk), BF16), pltpu.VMEM((tq, tk), BF16),
                        pltpu.VMEM((tq, 2 * V_HEAD), F32), pltpu.VMEM((tq, LANES), F32), pltpu.VMEM((tq, LANES), F32)],
        compiler_params=_cparams(("parallel", "parallel", "parallel")), name="mla_attn",
    )(q, k, v)


def _gates_kernel(h_ref, w_ref, b_ref, o_ref):
    o_ref[0] = jax.nn.sigmoid(jnp.dot(h_ref[0], w_ref[...], preferred_element_type=F32) + b_ref[...]).astype(o_ref.dtype)


def _gates(h, w_gate, gate_b, tm=1024, tn=1024):
    B, _, D = h.shape
    N = w_gate.shape[1]
    return pl.pallas_call(
        _gates_kernel, grid=(B, SEQ // tm, N // tn),
        in_specs=[pl.BlockSpec((1, tm, D), lambda b, i, j: (b, i, 0)),
                  pl.BlockSpec((D, tn), lambda b, i, j: (0, j)),
                  pl.BlockSpec((1, tn), lambda b, i, j: (0, j))],
        out_specs=pl.BlockSpec((1, tm, tn), lambda b, i, j: (b, i, j)),
        out_shape=jax.ShapeDtypeStruct((B, SEQ, N), BF16),
        compiler_params=_cparams(("parallel", "parallel", "parallel")), name="gates",
    )(h, w_gate, gate_b[None, :])


MERGE_TN = 512


def _merge_kernel(y0_ref, y1_ref, g_ref, bg_ref, lnw_ref, lnb_ref, avg_ref, o_ref, wr_ref, wm_ref, gate_ref, out_ref):
    avg = avg_ref[...]
    y = y0_ref[0, 0] + y1_ref[0, 0]
    yc = y - _head_sum(y, avg)
    yn = yc * lax.rsqrt(_head_sum(yc * yc, avg) + LNX_EPS) * lnw_ref[...] + lnb_ref[...]
    yr = (yn * g_ref[0].astype(F32) + bg_ref[0].astype(F32)).astype(BF16)
    o = o_ref[0]
    D = D_MODEL
    for n in range(0, D, MERGE_TN):
        gr = gate_ref[0, :, n:n + MERGE_TN].astype(F32)
        gm = gate_ref[0, :, D + n:D + n + MERGE_TN].astype(F32)
        out_ref[0, :, n:n + MERGE_TN] = (
            gr * jnp.dot(yr, wr_ref[:, n:n + MERGE_TN], preferred_element_type=F32)
            + gm * jnp.dot(o, wm_ref[:, n:n + MERGE_TN], preferred_element_type=F32)).astype(out_ref.dtype)


def _merge(y_dir, g, bg, lnx_w, lnx_b, avg2, o, w_rp, w_mp, gates, tm=256):
    B = o.shape[0]
    D = D_MODEL
    C = RWKV_DIM
    tok = pl.BlockSpec((1, tm, C), lambda b, i: (b, i, 0))
    row = lambda n: pl.BlockSpec((1, n), lambda b, i: (0, 0))
    return pl.pallas_call(
        _merge_kernel, grid=(B, SEQ // tm),
        in_specs=[pl.BlockSpec((1, 1, tm, C), lambda b, i: (b, 0, i, 0)),
                  pl.BlockSpec((1, 1, tm, C), lambda b, i: (b, 1, i, 0)),
                  tok, tok, row(C), row(C),
                  pl.BlockSpec((LANES, LANES), lambda b, i: (0, 0)),
                  pl.BlockSpec((1, tm, D), lambda b, i: (b, i, 0)),
                  pl.BlockSpec((C, D), lambda b, i: (0, 0)),
                  pl.BlockSpec((D, D), lambda b, i: (0, 0)),
                  pl.BlockSpec((1, tm, 2 * D), lambda b, i: (b, i, 0))],
        out_specs=pl.BlockSpec((1, tm, D), lambda b, i: (b, i, 0)),
        out_shape=jax.ShapeDtypeStruct((B, SEQ, D), BF16),
        compiler_params=_cparams(("parallel", "parallel")), name="merge",
    )(y_dir, y_dir, g, bg, lnx_w[None, :], lnx_b[None, :], avg2, o, w_rp, w_mp, gates)


def _outproj_kernel(m_ref, w_ref, x_ref, gpost_ref, gate_ref, gpre_ref, mod_ref, x1_ref, h_ref):
    out = jnp.dot(m_ref[0], w_ref[...], preferred_element_type=F32)
    x1 = x_ref[0] + gate_ref[0] * _rms(out, gpost_ref[...])
    x1_ref[0] = x1
    mod = mod_ref[0]
    h_ref[0] = (_rms(x1, gpre_ref[...]) * (1.0 + mod[1:2]) + mod[0:1]).astype(h_ref.dtype)


def _outproj(merged, w_out, x, g_post, gate, g_pre, mod2, tm=512):
    B = x.shape[0]
    D = D_MODEL
    return pl.pallas_call(
        _outproj_kernel, grid=(B, SEQ // tm),
        in_specs=[pl.BlockSpec((1, tm, D), lambda b, i: (b, i, 0)),
                  pl.BlockSpec((D, D), lambda b, i: (0, 0)),
                  pl.BlockSpec((1, tm, D), lambda b, i: (b, i, 0)),
                  pl.BlockSpec((1, D), lambda b, i: (0, 0)),
                  pl.BlockSpec((1, 1, D), lambda b, i: (b, 0, 0)),
                  pl.BlockSpec((1, D), lambda b, i: (0, 0)),
                  pl.BlockSpec((1, 2, D), lambda b, i: (b, 0, 0))],
        out_specs=[pl.BlockSpec((1, tm, D), lambda b, i: (b, i, 0)),
                   pl.BlockSpec((1, tm, D), lambda b, i: (b, i, 0))],
        out_shape=[jax.ShapeDtypeStruct((B, SEQ, D), F32),
                   jax.ShapeDtypeStruct((B, SEQ, D), BF16)],
        compiler_params=_cparams(("parallel", "parallel")), name="outproj",
    )(merged, w_out, x, g_post[None, :], gate[:, None, :], g_pre[None, :], mod2)


HALO = 16


def _ffn_kernel(h_ref, hp_ref, hn_ref, wg_ref, wv_ref, wd_ref, cw_ref, cb_ref, x_ref, gate_ref, gpost_ref,
                o_ref, acc_ref):
    i = pl.program_id(1)
    f = pl.program_id(2)
    tm = h_ref.shape[1]

    @pl.when(f == 0)
    def _():
        acc_ref[...] = jnp.zeros_like(acc_ref)

    h = h_ref[0]
    wg = wg_ref[...]
    g_ext = jnp.dot(jnp.concatenate([hp_ref[0], h, hn_ref[0]], axis=0), wg, preferred_element_type=F32)
    g = g_ext[HALO:HALO + tm]
    g_prev = g_ext[HALO - 1:HALO]
    g_next = g_ext[HALO + tm:HALO + tm + 1]
    g_prev = jnp.where(i == 0, 0.0, g_prev)
    g_next = jnp.where(i == pl.num_programs(1) - 1, 0.0, g_next)
    rows = lax.broadcasted_iota(jnp.int32, g.shape, 0)
    up = jnp.where(rows == 0, g_prev, pltpu.roll(g, 1, 0))
    dn = jnp.where(rows == tm - 1, g_next, pltpu.roll(g, tm - 1, 0))
    cw = cw_ref[...]
    u = cb_ref[...] + up * cw[0:1] + g * cw[1:2] + dn * cw[2:3]
    val = jnp.dot(h, wv_ref[...], preferred_element_type=F32)
    act = jax.nn.gelu(u, approximate=True) * val
    acc_ref[...] += jnp.dot(act.astype(BF16), wd_ref[...], preferred_element_type=F32)

    @pl.when(f == pl.num_programs(2) - 1)
    def _():
        o_ref[0] = x_ref[0] + gate_ref[0] * _rms(acc_ref[...], gpost_ref[...])


def _ffn(h, wg, wv, wd, cw, cb, x1, gate, g_post, tm=512, tf=512):
    B = h.shape[0]
    D = D_MODEL
    hb = tm // HALO
    last = SEQ // HALO - 1
    return pl.pallas_call(
        _ffn_kernel, grid=(B, SEQ // tm, D_FF // tf),
        in_specs=[pl.BlockSpec((1, tm, D), lambda b, i, f: (b, i, 0)),
                  pl.BlockSpec((1, HALO, D), lambda b, i, f: (b, jnp.maximum(i * hb - 1, 0), 0)),
                  pl.BlockSpec((1, HALO, D), lambda b, i, f: (b, jnp.minimum((i + 1) * hb, last), 0)),
                  pl.BlockSpec((D, tf), lambda b, i, f: (0, f)),
                  pl.BlockSpec((D, tf), lambda b, i, f: (0, f)),
                  pl.BlockSpec((tf, D), lambda b, i, f: (f, 0)),
                  pl.BlockSpec((3, tf), lambda b, i, f: (0, f)),
                  pl.BlockSpec((1, tf), lambda b, i, f: (0, f)),
                  pl.BlockSpec((1, tm, D), lambda b, i, f: (b, i, 0)),
                  pl.BlockSpec((1, 1, D), lambda b, i, f: (b, 0, 0)),
                  pl.BlockSpec((1, D), lambda b, i, f: (0, 0))],
        out_specs=pl.BlockSpec((1, tm, D), lambda b, i, f: (b, i, 0)),
        out_shape=jax.ShapeDtypeStruct((B, SEQ, D), F32),
        scratch_shapes=[pltpu.VMEM((tm, D), F32)],
        compiler_params=_cparams(("parallel", "parallel", "arbitrary")), name="convffn",
    )(h, h, h, wg, wv, wd, cw, cb[None, :], x1, gate[:, None, :], g_post[None, :])


def _rope_partner(w):
    q = QK_ROPE // 4
    return jnp.concatenate([-w[..., q:2 * q], w[..., :q], -w[..., 3 * q:], w[..., 2 * q:3 * q]], axis=-1)


def _pack_w_in_kernel(w_ref, kp_ref, main_ref, gate_ref):
    w = w_ref[...]
    latents = RWKV_IN + Q_LORA + KV_LORA
    main_ref[:, :RWKV_IN] = w[:, :RWKV_IN].astype(BF16)
    main_ref[:, C_KPE:C_KPE + 2 * QK_ROPE] = kp_ref[...].astype(BF16)
    main_ref[:, C_KPE + 2 * QK_ROPE:C_CQ] = jnp.zeros((w.shape[0], C_CQ - C_KPE - 2 * QK_ROPE), BF16)
    main_ref[:, C_CQ:] = w[:, RWKV_IN:latents].astype(BF16)
    gate_ref[...] = w[:, RWKV_IN + MLA_IN:].astype(BF16)


def _pack_w_in(w, tk=256):
    K, n_in = w.shape
    kpe = w[:, RWKV_IN + Q_LORA + KV_LORA:RWKV_IN + MLA_IN]
    kp = jnp.concatenate([kpe, _rope_partner(kpe)], axis=1)
    n_gate = n_in - RWKV_IN - MLA_IN
    return pl.pallas_call(
        _pack_w_in_kernel, grid=(K // tk,),
        in_specs=[pl.BlockSpec((tk, n_in), lambda i: (i, 0)),
                  pl.BlockSpec((tk, 2 * QK_ROPE), lambda i: (i, 0))],
        out_specs=[pl.BlockSpec((tk, P_COLS), lambda i: (i, 0)),
                   pl.BlockSpec((tk, n_gate), lambda i: (i, 0))],
        out_shape=[jax.ShapeDtypeStruct((K, P_COLS), BF16), jax.ShapeDtypeStruct((K, n_gate), BF16)],
        compiler_params=_cparams(("parallel",)), name="pack_w_in",
    )(w, kp)


def _lora_window_weight(w2, first_row):
    n_dir, R, C = w2.shape
    out = jnp.zeros((LORA_WIN, n_dir, C), BF16)
    for d in range(n_dir):
        out = out.at[first_row + d * R:first_row + (d + 1) * R, d].set(w2[d].astype(BF16))
    return out.reshape(LORA_WIN, n_dir * C)


def _rope_tables():
    rows = SEQ // GRID_W
    row = jnp.repeat(jnp.arange(rows), GRID_W).astype(F32)
    col = jnp.tile(jnp.arange(GRID_W), rows).astype(F32)
    half = QK_ROPE // 2
    freqs = ROPE_THETA ** (-jnp.arange(0, half, 2, dtype=F32) / half)
    ar, ac = row[:, None] * freqs, col[:, None] * freqs
    cos = jnp.concatenate([jnp.cos(ar), jnp.cos(ar), jnp.cos(ac), jnp.cos(ac)], axis=-1)
    sin = jnp.concatenate([jnp.sin(ar), jnp.sin(ar), jnp.sin(ac), jnp.sin(ac)], axis=-1)
    cos_all = jnp.concatenate([cos, jnp.ones((CTX_LEN, QK_ROPE), F32)], axis=0)
    sin_all = jnp.concatenate([sin, jnp.zeros((CTX_LEN, QK_ROPE), F32)], axis=0)
    return cos_all, sin_all


def kernel(x, c, ctx, c_ctx, w_ada, b_ada, norm_mix_pre, norm_mix_post, norm_ffn_pre, norm_ffn_post, w_in, rwkv_mu, rwkv_w0, rwkv_w2, rwkv_a0, rwkv_a2, rwkv_k_k, rwkv_k_a, rwkv_r_k, rwkv_lnx_w, rwkv_lnx_b, rwkv_g2, w_rwkv_proj, mla_q_norm, mla_kv_norm, mla_w_uq, mla_w_ukv, w_mla_proj, gate_b, w_out, ffn_w_gate, ffn_w_val, ffn_conv_w, ffn_conv_b, ffn_w_down):
    B = x.shape[0]
    D = D_MODEL
    C = RWKV_DIM
    l = 0

    s = jnp.concatenate([jax.nn.silu(c), jax.nn.silu(c_ctx)[None, :]], axis=0)
    mods = _ada(s, w_ada[l], b_ada[l])
    lat = mods[:B].reshape(B, 6, D)
    cm = mods[B].reshape(6, D)
    sh1, sc1, g1, sh2, sc2, g2 = [lat[:, j] for j in range(6)]
    mods1 = jnp.stack([jnp.stack([sh1, sc1], axis=1),
                       jnp.broadcast_to(jnp.stack([cm[0], cm[1]])[None], (B, 2, D))], axis=1)
    mods2 = jnp.stack([sh2, sc2], axis=1)

    h = _norm_mod(x, ctx, norm_mix_pre[l], mods1)
    w_main, w_gate = _pack_w_in(w_in[l])
    p = _matmul(h.reshape(B * S_ALL, D), w_main, tm=1088, tn=1024, name="w_in").reshape(B, S_ALL, P_COLS)
    gates = _gates(h, w_gate, gate_b[l])

    lane_head = jnp.arange(LANES) // RWKV_HEAD
    ones2 = (lane_head[:, None] == lane_head[None, :]).astype(BF16)
    avg2 = ones2 * (1.0 / RWKV_HEAD)
    r, k, v, kk, lw, a, g, bg = _prep(p, rwkv_mu[l], rwkv_w0[l], _lora_window_weight(rwkv_w2[l], C_WL - C_WL_WIN),
                                      rwkv_a0[l], _lora_window_weight(rwkv_a2[l], C_AL - C_AL_WIN), rwkv_k_k[l],
                                      rwkv_k_a[l], rwkv_r_k[l].reshape(C), rwkv_g2[l].astype(BF16), ones2)
    y_dir = _wkv(r, k, v, kk, lw, a, rwkv_k_a[l])

    cos, sin = _rope_tables()
    uq = mla_w_uq[l].reshape(Q_LORA, MLA_HEADS, QK_NOPE + QK_ROPE)
    wq = jnp.concatenate([uq, _rope_partner(uq[..., QK_NOPE:])], axis=-1).transpose(1, 0, 2).astype(BF16)
    wkv = mla_w_ukv[l].reshape(KV_LORA, MLA_HEADS, QK_NOPE + V_HEAD).transpose(1, 0, 2).astype(BF16)
    q = _qproj(p, mla_q_norm[l], wq, cos[:SEQ], sin[:SEQ])
    k_all, v_all = _kvproj(p, mla_kv_norm[l], wkv, cos, sin)
    o = _attention(q, k_all, v_all)

    merged = _merge(y_dir, g, bg, rwkv_lnx_w[l], rwkv_lnx_b[l], avg2, o, w_rwkv_proj[l].astype(BF16),
                    w_mla_proj[l].astype(BF16), gates)
    x1, h2 = _outproj(merged, w_out[l].astype(BF16), x, norm_mix_post[l], g1, norm_ffn_pre[l], mods2)
    return _ffn(h2, ffn_w_gate[l].astype(BF16), ffn_w_val[l].astype(BF16), ffn_w_down[l].astype(BF16),
                ffn_conv_w[l], ffn_conv_b[l], x1, g2, norm_ffn_post[l])
```

```python
import functools
import math

import jax
import jax.numpy as jnp
from jax import lax
from jax.experimental import pallas as pl
from jax.experimental.pallas import tpu as pltpu

D_MODEL = 2048
BATCH = 2
SEQ = 4096
GRID_W = 64
CTX_LEN = 256
S_ALL = SEQ + CTX_LEN
EPS = 1e-6
RWKV_HEADS = 16
RWKV_HEAD = 64
RWKV_DIM = RWKV_HEADS * RWKV_HEAD
DECAY_LORA = 96
AAA_LORA = 96
GATE_LORA = 256
N_DIR = 2
DECAY_SCALE = math.exp(-0.5)
LNX_EPS = 64e-5
MLA_HEADS = 16
Q_LORA = 512
KV_LORA = 512
QK_NOPE = 128
QK_ROPE = 64
V_HEAD = 128
ROPE_THETA = 10000.0
ATTN_SCALE = (QK_NOPE + QK_ROPE) ** -0.5
LOG2_E = math.log2(math.e)
D_FF = 5632
RWKV_IN = 3 * RWKV_DIM + N_DIR * DECAY_LORA + N_DIR * AAA_LORA + GATE_LORA
MLA_IN = Q_LORA + KV_LORA + QK_ROPE

LANES = 128
SUBLANES = 8
VMEM_LIMIT = 56 * 1024 * 1024

C_R, C_K, C_V = 0, RWKV_DIM, 2 * RWKV_DIM
C_WL = 3 * RWKV_DIM
C_AL = C_WL + N_DIR * DECAY_LORA
C_GL = C_AL + N_DIR * AAA_LORA
RWKV_COLS = C_GL + GATE_LORA
C_KPE = RWKV_COLS
C_CQ = -(-(C_KPE + 2 * QK_ROPE) // Q_LORA) * Q_LORA
C_CKV = C_CQ + Q_LORA
P_COLS = C_CKV + KV_LORA
LORA_WIN = 2 * LANES
C_WL_WIN = C_WL // LANES * LANES
C_AL_WIN = C_AL // LANES * LANES

WKV_CHUNK = 64
WKV_SUB = 4
HEAD_PAIRS = RWKV_DIM // LANES

F32 = jnp.float32
BF16 = jnp.bfloat16


def _cparams(sem):
    return pltpu.CompilerParams(dimension_semantics=sem, vmem_limit_bytes=VMEM_LIMIT)


def _bdot(a, b):
    return jnp.dot(a.astype(BF16), b.astype(BF16), preferred_element_type=F32)


def _hilo_dot(x, w):
    hi = x.astype(BF16)
    lo = (x - hi.astype(F32)).astype(BF16)
    return jnp.dot(hi, w, preferred_element_type=F32) + jnp.dot(lo, w, preferred_element_type=F32)


def _head_sum(x, ones2):
    return jnp.concatenate([_hilo_dot(x[:, c * LANES:(c + 1) * LANES], ones2) for c in range(HEAD_PAIRS)], axis=-1)


def _ada_kernel(s_ref, w_ref, b_ref, o_ref):
    w = w_ref[...]
    for r in range(s_ref.shape[0]):
        o_ref[r:r + 1, :] = jnp.sum(s_ref[r] * w, axis=0, keepdims=True) + b_ref[...]


def _ada(s, w, b, tn=1024):
    R, K = s.shape
    N = w.shape[1]
    return pl.pallas_call(
        _ada_kernel, grid=(N // tn,),
        in_specs=[pl.BlockSpec((R, K, 1), lambda j: (0, 0, 0)),
                  pl.BlockSpec((K, tn), lambda j: (0, j)),
                  pl.BlockSpec((1, tn), lambda j: (0, j))],
        out_specs=pl.BlockSpec((R, tn), lambda j: (0, j)),
        out_shape=jax.ShapeDtypeStruct((R, N), F32),
        compiler_params=_cparams(("parallel",)), name="ada",
    )(s[:, :, None], w, b[None, :])


def _norm_mod_kernel(x_ref, c_ref, g_ref, m_ref, o_ref, *, lat_tiles):
    def emit(x):
        n = x * lax.rsqrt(jnp.mean(x * x, axis=-1, keepdims=True) + EPS) * g_ref[...]
        m = m_ref[0, 0]
        o_ref[0] = (n * (1.0 + m[1:2]) + m[0:1]).astype(o_ref.dtype)

    is_latent = pl.program_id(1) < lat_tiles

    @pl.when(is_latent)
    def _():
        emit(x_ref[0])

    @pl.when(jnp.logical_not(is_latent))
    def _():
        emit(c_ref[0])


def _norm_mod(x, ctx, gain, mods, tm=256):
    B, _, D = x.shape
    assert ctx.shape[1] == tm
    lat_tiles = SEQ // tm
    return pl.pallas_call(
        functools.partial(_norm_mod_kernel, lat_tiles=lat_tiles), grid=(B, S_ALL // tm),
        in_specs=[pl.BlockSpec((1, tm, D), lambda b, i: (b, jnp.minimum(i, lat_tiles - 1), 0)),
                  pl.BlockSpec((1, tm, D), lambda b, i: (b, 0, 0)),
                  pl.BlockSpec((1, D), lambda b, i: (0, 0)),
                  pl.BlockSpec((1, 1, 2, D), lambda b, i: (b, i // lat_tiles, 0, 0))],
        out_specs=pl.BlockSpec((1, tm, D), lambda b, i: (b, i, 0)),
        out_shape=jax.ShapeDtypeStruct((B, S_ALL, D), BF16),
        compiler_params=_cparams(("parallel", "parallel")), name="norm_mod",
    )(x, ctx, gain[None, :], mods)


def _dot_nt(a, wt):
    return lax.dot_general(a, wt, (((1,), (1,)), ((), ())), preferred_element_type=F32)


def _mm_kernel(a_ref, wt_ref, o_ref):
    o_ref[...] = _dot_nt(a_ref[...], wt_ref[...]).astype(o_ref.dtype)


def _matmul_nt(a, wt, tm, tn, out_dtype=F32, name="matmul"):
    M, K = a.shape
    N = wt.shape[0]
    return pl.pallas_call(
        _mm_kernel, grid=(M // tm, N // tn),
        in_specs=[pl.BlockSpec((tm, K), lambda i, j: (i, 0)),
                  pl.BlockSpec((tn, K), lambda i, j: (j, 0))],
        out_specs=pl.BlockSpec((tm, tn), lambda i, j: (i, j)),
        out_shape=jax.ShapeDtypeStruct((M, N), out_dtype),
        compiler_params=_cparams(("parallel", "parallel")), name=name,
    )(a, wt)


def _prep_kernel(p_ref, pp_ref, pn_ref, mu_ref, w0_ref, w2_ref, a0_ref, a2_ref, kk_ref, ka_ref, rk_ref, g2_ref,
                 ones_ref, r_o, k_o, v_o, kkn_o, lw_o, a_o, g_o, bg_o):
    i = pl.program_id(1)
    x = p_ref[0]
    tm = x.shape[0]
    lat_tiles = SEQ // tm
    first = jnp.logical_or(i == 0, i == lat_tiles)
    last = jnp.logical_or(i == lat_tiles - 1, i == pl.num_programs(1) - 1)
    x_before = jnp.where(first, 0.0, pp_ref[0][SUBLANES - 1:SUBLANES])
    x_after = jnp.where(last, 0.0, pn_ref[0][0:1])
    rows = lax.broadcasted_iota(jnp.int32, (tm, 1), 0)
    prev = jnp.where(rows == 0, x_before, pltpu.roll(x, 1, 0))
    nxt = jnp.where(rows == tm - 1, x_after, pltpu.roll(x, tm - 1, 0))
    xs = x + mu_ref[...] * (0.5 * (prev + nxt) - x)

    C = RWKV_DIM
    r, k, v = xs[:, C_R:C_R + C], xs[:, C_K:C_K + C], xs[:, C_V:C_V + C]
    w_lora = _bdot(jnp.tanh(xs[:, C_WL_WIN:C_WL_WIN + LORA_WIN]), w2_ref[...])
    a_lora = _bdot(xs[:, C_AL_WIN:C_AL_WIN + LORA_WIN], a2_ref[...])
    a_sum = None
    for d in range(N_DIR):
        w_raw = w0_ref[d:d + 1] + w_lora[:, d * C:(d + 1) * C]
        lw_o[0, d] = -DECAY_SCALE * jax.nn.sigmoid(w_raw)
        a_d = jax.nn.sigmoid(a0_ref[d:d + 1] + a_lora[:, d * C:(d + 1) * C])
        a_o[0, d] = a_d
        a_sum = a_d if a_sum is None else a_sum + a_d
    ones2 = ones_ref[...]
    kkx = k * kk_ref[...]
    kkn_o[0] = kkx / jnp.maximum(jnp.sqrt(_head_sum(kkx * kkx, ones2)), 1e-12)
    g = _bdot(jax.nn.sigmoid(xs[:, C_GL:C_GL + GATE_LORA]), g2_ref[...])
    k_bar = k * (1.0 + (0.5 * a_sum - 1.0) * ka_ref[...])
    bonus = _head_sum(r * k_bar * rk_ref[...], ones2) * v
    r_o[0] = r
    k_o[0] = k
    v_o[0] = v
    g_o[0] = g.astype(g_o.dtype)
    bg_o[0] = (bonus * g).astype(bg_o.dtype)


def _prep(p, mu, w0, w2p, a0, a2p, k_k, k_a, r_k, g2, ones2, tm=256):
    B = p.shape[0]
    C = RWKV_DIM
    hb = tm // SUBLANES
    last = S_ALL // SUBLANES - 1
    row = lambda n: pl.BlockSpec((1, n), lambda b, i: (0, 0))
    tok = pl.BlockSpec((1, tm, C), lambda b, i: (b, i, 0))
    tok_dir = pl.BlockSpec((1, N_DIR, tm, C), lambda b, i: (b, 0, i, 0))
    shp = jax.ShapeDtypeStruct((B, S_ALL, C), F32)
    shp_dir = jax.ShapeDtypeStruct((B, N_DIR, S_ALL, C), F32)
    shp_b = jax.ShapeDtypeStruct((B, S_ALL, C), BF16)
    return pl.pallas_call(
        _prep_kernel, grid=(B, S_ALL // tm),
        in_specs=[pl.BlockSpec((1, tm, RWKV_COLS), lambda b, i: (b, i, 0)),
                  pl.BlockSpec((1, SUBLANES, RWKV_COLS), lambda b, i: (b, jnp.maximum(i * hb - 1, 0), 0)),
                  pl.BlockSpec((1, SUBLANES, RWKV_COLS), lambda b, i: (b, jnp.minimum((i + 1) * hb, last), 0)),
                  row(RWKV_COLS),
                  pl.BlockSpec((N_DIR, C), lambda b, i: (0, 0)),
                  pl.BlockSpec((LORA_WIN, N_DIR * C), lambda b, i: (0, 0)),
                  pl.BlockSpec((N_DIR, C), lambda b, i: (0, 0)),
                  pl.BlockSpec((LORA_WIN, N_DIR * C), lambda b, i: (0, 0)),
                  row(C), row(C), row(C),
                  pl.BlockSpec((GATE_LORA, C), lambda b, i: (0, 0)),
                  pl.BlockSpec((LANES, LANES), lambda b, i: (0, 0))],
        out_specs=[tok, tok, tok, tok, tok_dir, tok_dir, tok, tok],
        out_shape=[shp, shp, shp, shp, shp_dir, shp_dir, shp_b, shp_b],
        compiler_params=_cparams(("parallel", "parallel")), name="rwkv_prep",
    )(p, p, p, mu[None, :], w0, w2p, a0, a2p, k_k[None, :], k_a[None, :], r_k[None, :], g2, ones2)


def _wkv_kernel(r_ref, k_ref, v_ref, kk_ref, lw_ref, a_ref, ka_ref, y_ref, s_ref):
    d = pl.program_id(1)
    C = WKV_CHUNK
    C2 = 2 * C
    P = HEAD_PAIRS
    sign = 1 - 2 * d
    fwd = d == 0

    @pl.when(pl.program_id(2) == 0)
    def _():
        s_ref[...] = jnp.zeros_like(s_ref)

    trow = lax.broadcasted_iota(jnp.int32, (C, C), 0)
    tcol = lax.broadcasted_iota(jnp.int32, (C, C), 1)
    tri = ((trow - tcol) * sign >= 0).astype(BF16)
    lane = lax.broadcasted_iota(jnp.int32, (C, LANES), 1)
    head0 = lane < RWKV_HEAD
    zero = jnp.zeros((), F32)

    def expand(x):
        x = x.astype(BF16)
        zb = jnp.zeros((), BF16)
        return jnp.stack([jnp.concatenate([jnp.where(head0, x[:, j * LANES:(j + 1) * LANES], zb),
                                           jnp.where(head0, zb, x[:, j * LANES:(j + 1) * LANES])], axis=0)
                          for j in range(P)])

    def bmm(x, y):
        return jnp.einsum('hab,hbc->hac', x.astype(BF16), y.astype(BF16), preferred_element_type=F32)

    def bmm_nt(x, y):
        return jnp.einsum('han,hbn->hab', x.astype(BF16), y.astype(BF16), preferred_element_type=F32)

    def bmm_tn(x, y):
        return jnp.einsum('hca,hcb->hab', x.astype(BF16), y.astype(BF16), preferred_element_type=F32)

    row = lax.broadcasted_iota(jnp.int32, (C2, C2), 0)
    col = lax.broadcasted_iota(jnp.int32, (C2, C2), 1)
    same = (row // C) == (col // C)
    lag = (row - col) * sign
    incl = jnp.logical_and(same, lag >= 0)[None]
    strict = jnp.logical_and(same, lag > 0)[None]

    def blk(m):
        return (row // m) == (col // m)

    def half_rows(t, m, second):
        return jnp.concatenate([t[:, g + second * m:g + (second + 1) * m] for g in range(0, C2, 2 * m)], axis=1)

    def state_free(rows):
        r = r_ref[0, rows, :]
        v = v_ref[0, rows, :]
        kk = kk_ref[0, rows, :]
        lw = lw_ref[0, 0, rows, :]
        a = a_ref[0, 0, rows, :]
        k = k_ref[0, rows, :] * (1.0 + (a - 1.0) * ka_ref[...])

        cum = _hilo_dot_left(tri, lw)
        tot = jnp.sum(lw, axis=0, keepdims=True)
        e_cum = jnp.exp(cum)
        e_neg = jnp.exp(-cum)
        e_rem = jnp.exp(tot - cum)
        b = kk * a
        at = -kk * jnp.exp(cum - lw)
        dec = jnp.exp(tot)
        AT, RT, BT, KT, BD, KD, V = (expand(t) for t in (at, r * e_cum, b * e_neg, k * e_neg, b * e_rem,
                                                         k * e_rem, v))
        AR = jnp.concatenate([AT, RT], axis=1)
        G = bmm_nt(AR, jnp.concatenate([BT, KT], axis=1))
        a_ab, a_ak = G[:, :C2, :C2], G[:, :C2, C2:]
        a_rb, a_rk = G[:, C2:, :C2], G[:, C2:, C2:]

        L = jnp.where(strict, a_ab, zero)
        Lb = jnp.where(blk(8)[None], L, zero)
        L2 = bmm(Lb, Lb)
        L4 = bmm(L2, L2)
        S1 = Lb + L2 + bmm(Lb, L2)
        Nm = S1 + L4 + bmm(L4, S1)
        for m in (8, 16, 32):
            Lo = jnp.where(jnp.logical_and(blk(2 * m), jnp.logical_not(blk(m)))[None], L, zero)
            n_first, n_second = half_rows(Nm, m, 0), half_rows(Nm, m, 1)
            n_late = jnp.where(fwd, n_second, n_first)
            p_late = jnp.where(fwd, half_rows(Lo, m, 1), half_rows(Lo, m, 0)) + bmm(n_late, Lo)
            n_late = n_late + p_late + bmm(p_late, Nm)
            n_first, n_second = jnp.where(fwd, n_first, n_late), jnp.where(fwd, n_late, n_second)
            Nm = jnp.concatenate([piece for g in range(C2 // (2 * m))
                                  for piece in (n_first[:, g * m:(g + 1) * m], n_second[:, g * m:(g + 1) * m])],
                                 axis=1)
        rhs_v = bmm(jnp.where(strict, a_ak, zero), V)
        a_y = jnp.concatenate([jnp.where(incl, a_rb, zero), jnp.where(incl, a_rk, zero)], axis=2).astype(BF16)
        dec3 = jnp.stack([dec[:, j * LANES:(j + 1) * LANES] for j in range(P)])
        return AR, rhs_v, Nm.astype(BF16), V, a_y, jnp.concatenate([BD, KD], axis=1), dec3

    def state_step(part, s0):
        AR, rhs_v, Nm, V, a_y, BKD, dec3 = part
        ars0 = bmm_nt(AR, s0)
        rhs = ars0[:, :C2] + rhs_v
        u = rhs + bmm(Nm, rhs)
        uv = jnp.concatenate([u.astype(BF16), V], axis=1)
        y2 = ars0[:, C2:] + bmm(a_y, uv)
        y = y2[:, :C] + y2[:, C:]
        return jnp.concatenate([y[j] for j in range(P)], axis=-1), s0 * dec3 + bmm_tn(uv, BKD)

    chunk_rows = [pl.ds(pl.multiple_of(jnp.where(fwd, j, WKV_SUB - 1 - j) * C, C), C) for j in range(WKV_SUB)]
    parts = [state_free(rows) for rows in chunk_rows]
    s = s_ref[...]
    for rows, part in zip(chunk_rows, parts):
        y, s = state_step(part, s)
        y_ref[0, 0, rows, :] = y
    s_ref[...] = s


def _hilo_dot_left(w, x):
    hi = x.astype(BF16)
    lo = (x - hi.astype(F32)).astype(BF16)
    return jnp.dot(w, hi, preferred_element_type=F32) + jnp.dot(w, lo, preferred_element_type=F32)


def _wkv(r, k, v, kk, lw, a, k_a):
    B, S, Cd = r.shape
    R = WKV_SUB * WKV_CHUNK
    n_blocks = S // R
    lat_blocks = SEQ // R

    def bidx(d, i):
        return jnp.where(d == 0, (i + lat_blocks) % n_blocks, n_blocks - 1 - i)

    shared = pl.BlockSpec((1, R, Cd), lambda b, d, i: (b, bidx(d, i), 0))
    perdir = pl.BlockSpec((1, 1, R, Cd), lambda b, d, i: (b, d, bidx(d, i), 0))
    return pl.pallas_call(
        _wkv_kernel, grid=(B, N_DIR, n_blocks),
        in_specs=[shared, shared, shared, shared, perdir, perdir, pl.BlockSpec((1, Cd), lambda b, d, i: (0, 0))],
        out_specs=perdir,
        out_shape=jax.ShapeDtypeStruct((B, N_DIR, S, Cd), F32),
        scratch_shapes=[pltpu.VMEM((HEAD_PAIRS, LANES, LANES), F32)],
        compiler_params=_cparams(("parallel", "parallel", "arbitrary")), name="wkv7",
    )(r, k, v, kk, lw, a, k_a[None, :])


def _rms(x, g):
    return x * lax.rsqrt(jnp.mean(x * x, axis=-1, keepdims=True) + EPS) * g


def _qproj_kernel(p_ref, g_ref, w_ref, cos_ref, sin_ref, q_ref):
    cq = _rms(p_ref[0], g_ref[...]).astype(BF16)
    cs, sn = cos_ref[...], sin_ref[...]
    for h in range(MLA_HEADS):
        y = jnp.dot(cq, w_ref[h], preferred_element_type=F32)
        rope = y[:, QK_NOPE:QK_NOPE + QK_ROPE] * cs + y[:, QK_NOPE + QK_ROPE:] * sn
        q = jnp.concatenate([y[:, :QK_NOPE], rope], axis=-1) * (ATTN_SCALE * LOG2_E)
        q_ref[0, h] = q.astype(q_ref.dtype)


def _qproj(p, gain, wq, cos, sin, tm=512):
    B = p.shape[0]
    return pl.pallas_call(
        _qproj_kernel, grid=(B, SEQ // tm),
        in_specs=[pl.BlockSpec((1, tm, Q_LORA), lambda b, i: (b, i, C_CQ // Q_LORA)),
                  pl.BlockSpec((1, Q_LORA), lambda b, i: (0, 0)),
                  pl.BlockSpec((MLA_HEADS, Q_LORA, 2 * LANES), lambda b, i: (0, 0, 0)),
                  pl.BlockSpec((tm, QK_ROPE), lambda b, i: (i, 0)),
                  pl.BlockSpec((tm, QK_ROPE), lambda b, i: (i, 0))],
        out_specs=pl.BlockSpec((1, MLA_HEADS, tm, QK_NOPE + QK_ROPE), lambda b, i: (b, 0, i, 0)),
        out_shape=jax.ShapeDtypeStruct((B, MLA_HEADS, SEQ, QK_NOPE + QK_ROPE), BF16),
        compiler_params=_cparams(("parallel", "parallel")), name="mla_q",
    )(p, gain[None, :], wq, cos, sin)


def _kvproj_kernel(p_ref, pe_ref, g_ref, w_ref, cos_ref, sin_ref, k_ref, v_ref):
    ckv = _rms(p_ref[0], g_ref[...]).astype(BF16)
    pe = pe_ref[0]
    kpe = pe[:, :QK_ROPE] * cos_ref[...] + pe[:, QK_ROPE:] * sin_ref[...]
    ones = jnp.ones((ckv.shape[0], V_HEAD), F32)
    for h in range(MLA_HEADS):
        y = jnp.dot(ckv, w_ref[h], preferred_element_type=F32)
        k_ref[0, h] = jnp.concatenate([y[:, :QK_NOPE], kpe], axis=-1).astype(k_ref.dtype)
        v_ref[0, h] = jnp.concatenate([y[:, QK_NOPE:], ones], axis=-1).astype(v_ref.dtype)


def _kvproj(p, gain, wkv, cos, sin, tm=256):
    B = p.shape[0]
    return pl.pallas_call(
        _kvproj_kernel, grid=(B, S_ALL // tm),
        in_specs=[pl.BlockSpec((1, tm, KV_LORA), lambda b, i: (b, i, C_CKV // KV_LORA)),
                  pl.BlockSpec((1, tm, 2 * QK_ROPE), lambda b, i: (b, i, C_KPE // (2 * QK_ROPE))),
                  pl.BlockSpec((1, KV_LORA), lambda b, i: (0, 0)),
                  pl.BlockSpec((MLA_HEADS, KV_LORA, 2 * LANES), lambda b, i: (0, 0, 0)),
                  pl.BlockSpec((tm, QK_ROPE), lambda b, i: (i, 0)),
                  pl.BlockSpec((tm, QK_ROPE), lambda b, i: (i, 0))],
        out_specs=[pl.BlockSpec((1, MLA_HEADS, tm, QK_NOPE + QK_ROPE), lambda b, i: (b, 0, i, 0)),
                   pl.BlockSpec((1, MLA_HEADS, tm, 2 * V_HEAD), lambda b, i: (b, 0, i, 0))],
        out_shape=[jax.ShapeDtypeStruct((B, MLA_HEADS, S_ALL, QK_NOPE + QK_ROPE), BF16),
                   jax.ShapeDtypeStruct((B, MLA_HEADS, S_ALL, 2 * V_HEAD), BF16)],
        compiler_params=_cparams(("parallel", "parallel")), name="mla_kv",
    )(p, p, gain[None, :], wkv, cos, sin)


def _attn_kernel(q_ref, k_ref, v_ref, o_ref, sa, sb, pa, pb, acc, m_ref, al_ref, *, tk):
    q = q_ref[0, 0]
    n_keys = k_ref.shape[2]
    blocks = [(off, min(tk, n_keys - off)) for off in range(0, n_keys, tk)]
    n_kv = len(blocks)

    def put_scores(s_ref, j):
        off, w = blocks[j]
        kj = k_ref[0, 0, off:off + w, :]
        s_ref[:, :w] = lax.dot_general(q, kj, (((1,), (1,)), ((), ())), preferred_element_type=F32)

    def accumulate(p_ref, j):
        off, w = blocks[j]
        pv = jnp.dot(p_ref[:, :w], v_ref[0, 0, off:off + w, :], preferred_element_type=F32)
        al = al_ref[...]
        for c in range(0, 2 * V_HEAD, LANES):
            acc[:, c:c + LANES] = al * acc[:, c:c + LANES] + pv[:, c:c + LANES]

    def softmax(s_ref, p_ref, j):
        w = blocks[j][1]
        s = s_ref[:, :w]
        m_old = m_ref[...]
        m_new = jnp.maximum(m_old, jnp.max(s, axis=-1, keepdims=True))
        al_ref[...] = jnp.exp2(m_old - m_new)
        m_ref[...] = m_new
        for c in range(0, w, LANES):
            p_ref[:, c:c + LANES] = jnp.exp2(s[:, c:c + LANES] - m_new).astype(p_ref.dtype)

    s_bufs, p_bufs = (sa, sb), (pa, pb)
    put_scores(sa, 0)
    acc[...] = jnp.zeros_like(acc)
    m_ref[...] = jnp.full_like(m_ref, -jnp.inf)
    for j in range(n_kv):
        if j + 1 < n_kv:
            put_scores(s_bufs[(j + 1) % 2], j + 1)
        if j >= 1:
            accumulate(p_bufs[(j - 1) % 2], j - 1)
        softmax(s_bufs[j % 2], p_bufs[j % 2], j)
    accumulate(p_bufs[(n_kv - 1) % 2], n_kv - 1)
    o_ref[0] = (acc[:, :V_HEAD] / acc[:, V_HEAD:]).astype(o_ref.dtype)


def _attention(q, k, v, tq=1024, tk=1024):
    B, H, T, Dk = q.shape
    S = k.shape[2]
    return pl.pallas_call(
        functools.partial(_attn_kernel, tk=tk), grid=(B, H, T // tq),
        in_specs=[pl.BlockSpec((1, 1, tq, Dk), lambda b, h, i: (b, h, i, 0)),
                  pl.BlockSpec((1, 1, S, Dk), lambda b, h, i: (b, h, 0, 0)),
                  pl.BlockSpec((1, 1, S, 2 * V_HEAD), lambda b, h, i: (b, h, 0, 0))],
        out_specs=pl.BlockSpec((1, tq, V_HEAD), lambda b, h, i: (b, i, h)),
        out_shape=jax.ShapeDtypeStruct((B, T, H * V_HEAD), BF16),
        scratch_shapes=[pltpu.VMEM((tq, tk), F32), pltpu.VMEM((tq, tk), F32),
                        pltpu.VMEM((tq, tk), BF16), pltpu.VMEM((tq, tk), BF16),
                        pltpu.VMEM((tq, 2 * V_HEAD), F32), pltpu.VMEM((tq, LANES), F32), pltpu.VMEM((tq, LANES), F32)],
        compiler_params=_cparams(("parallel", "parallel", "parallel")), name="mla_attn",
    )(q, k, v)


def _gates_kernel(h_ref, wt_ref, b_ref, o_ref):
    o_ref[0] = jax.nn.sigmoid(_dot_nt(h_ref[0], wt_ref[...]) + b_ref[...]).astype(o_ref.dtype)


def _gates(h, wt_gate, gate_b, tm=1024, tn=1024):
    B, _, D = h.shape
    N = wt_gate.shape[0]
    return pl.pallas_call(
        _gates_kernel, grid=(B, SEQ // tm, N // tn),
        in_specs=[pl.BlockSpec((1, tm, D), lambda b, i, j: (b, i, 0)),
                  pl.BlockSpec((tn, D), lambda b, i, j: (j, 0)),
                  pl.BlockSpec((1, tn), lambda b, i, j: (0, j))],
        out_specs=pl.BlockSpec((1, tm, tn), lambda b, i, j: (b, i, j)),
        out_shape=jax.ShapeDtypeStruct((B, SEQ, N), BF16),
        compiler_params=_cparams(("parallel", "parallel", "parallel")), name="gates",
    )(h, wt_gate, gate_b[None, :])


MERGE_TN = 512


def _merge_kernel(y0_ref, y1_ref, g_ref, bg_ref, lnw_ref, lnb_ref, avg_ref, o_ref, wr_ref, wm_ref, gate_ref, out_ref):
    avg = avg_ref[...]
    y = y0_ref[0, 0] + y1_ref[0, 0]
    yc = y - _head_sum(y, avg)
    yn = yc * lax.rsqrt(_head_sum(yc * yc, avg) + LNX_EPS) * lnw_ref[...] + lnb_ref[...]
    yr = (yn * g_ref[0].astype(F32) + bg_ref[0].astype(F32)).astype(BF16)
    o = o_ref[0]
    D = D_MODEL
    for n in range(0, D, MERGE_TN):
        gr = gate_ref[0, :, n:n + MERGE_TN].astype(F32)
        gm = gate_ref[0, :, D + n:D + n + MERGE_TN].astype(F32)
        out_ref[0, :, n:n + MERGE_TN] = (
            gr * jnp.dot(yr, wr_ref[:, n:n + MERGE_TN], preferred_element_type=F32)
            + gm * jnp.dot(o, wm_ref[:, n:n + MERGE_TN], preferred_element_type=F32)).astype(out_ref.dtype)


def _merge(y_dir, g, bg, lnx_w, lnx_b, avg2, o, w_rp, w_mp, gates, tm=256):
    B = o.shape[0]
    D = D_MODEL
    C = RWKV_DIM
    tok = pl.BlockSpec((1, tm, C), lambda b, i: (b, i, 0))
    row = lambda n: pl.BlockSpec((1, n), lambda b, i: (0, 0))
    return pl.pallas_call(
        _merge_kernel, grid=(B, SEQ // tm),
        in_specs=[pl.BlockSpec((1, 1, tm, C), lambda b, i: (b, 0, i, 0)),
                  pl.BlockSpec((1, 1, tm, C), lambda b, i: (b, 1, i, 0)),
                  tok, tok, row(C), row(C),
                  pl.BlockSpec((LANES, LANES), lambda b, i: (0, 0)),
                  pl.BlockSpec((1, tm, D), lambda b, i: (b, i, 0)),
                  pl.BlockSpec((C, D), lambda b, i: (0, 0)),
                  pl.BlockSpec((D, D), lambda b, i: (0, 0)),
                  pl.BlockSpec((1, tm, 2 * D), lambda b, i: (b, i, 0))],
        out_specs=pl.BlockSpec((1, tm, D), lambda b, i: (b, i, 0)),
        out_shape=jax.ShapeDtypeStruct((B, SEQ, D), BF16),
        compiler_params=_cparams(("parallel", "parallel")), name="merge",
    )(y_dir, y_dir, g, bg, lnx_w[None, :], lnx_b[None, :], avg2, o, w_rp, w_mp, gates)


def _outproj_kernel(m_ref, w_ref, x_ref, gpost_ref, gate_ref, gpre_ref, mod_ref, x1_ref, h_ref):
    out = jnp.dot(m_ref[0], w_ref[...], preferred_element_type=F32)
    x1 = x_ref[0] + gate_ref[0] * _rms(out, gpost_ref[...])
    x1_ref[0] = x1
    mod = mod_ref[0]
    h_ref[0] = (_rms(x1, gpre_ref[...]) * (1.0 + mod[1:2]) + mod[0:1]).astype(h_ref.dtype)


def _outproj(merged, w_out, x, g_post, gate, g_pre, mod2, tm=512):
    B = x.shape[0]
    D = D_MODEL
    return pl.pallas_call(
        _outproj_kernel, grid=(B, SEQ // tm),
        in_specs=[pl.BlockSpec((1, tm, D), lambda b, i: (b, i, 0)),
                  pl.BlockSpec((D, D), lambda b, i: (0, 0)),
                  pl.BlockSpec((1, tm, D), lambda b, i: (b, i, 0)),
                  pl.BlockSpec((1, D), lambda b, i: (0, 0)),
                  pl.BlockSpec((1, 1, D), lambda b, i: (b, 0, 0)),
                  pl.BlockSpec((1, D), lambda b, i: (0, 0)),
                  pl.BlockSpec((1, 2, D), lambda b, i: (b, 0, 0))],
        out_specs=[pl.BlockSpec((1, tm, D), lambda b, i: (b, i, 0)),
                   pl.BlockSpec((1, tm, D), lambda b, i: (b, i, 0))],
        out_shape=[jax.ShapeDtypeStruct((B, SEQ, D), F32),
                   jax.ShapeDtypeStruct((B, SEQ, D), BF16)],
        compiler_params=_cparams(("parallel", "parallel")), name="outproj",
    )(merged, w_out, x, g_post[None, :], gate[:, None, :], g_pre[None, :], mod2)


HALO = 16


def _ffn_kernel(h_ref, hp_ref, hn_ref, wg_ref, wv_ref, wd_ref, cw_ref, cb_ref, x_ref, gate_ref, gpost_ref,
                o_ref, acc_ref):
    i = pl.program_id(1)
    f = pl.program_id(2)
    tm = h_ref.shape[1]

    @pl.when(f == 0)
    def _():
        acc_ref[...] = jnp.zeros_like(acc_ref)

    h = h_ref[0]
    wg = wg_ref[...]
    g_ext = jnp.dot(jnp.concatenate([hp_ref[0], h, hn_ref[0]], axis=0), wg, preferred_element_type=F32)
    g = g_ext[HALO:HALO + tm]
    g_prev = g_ext[HALO - 1:HALO]
    g_next = g_ext[HALO + tm:HALO + tm + 1]
    g_prev = jnp.where(i == 0, 0.0, g_prev)
    g_next = jnp.where(i == pl.num_programs(1) - 1, 0.0, g_next)
    rows = lax.broadcasted_iota(jnp.int32, g.shape, 0)
    up = jnp.where(rows == 0, g_prev, pltpu.roll(g, 1, 0))
    dn = jnp.where(rows == tm - 1, g_next, pltpu.roll(g, tm - 1, 0))
    cw = cw_ref[...]
    u = cb_ref[...] + up * cw[0:1] + g * cw[1:2] + dn * cw[2:3]
    val = jnp.dot(h, wv_ref[...], preferred_element_type=F32)
    act = jax.nn.gelu(u, approximate=True) * val
    acc_ref[...] += jnp.dot(act.astype(BF16), wd_ref[...], preferred_element_type=F32)

    @pl.when(f == pl.num_programs(2) - 1)
    def _():
        o_ref[0] = x_ref[0] + gate_ref[0] * _rms(acc_ref[...], gpost_ref[...])


def _ffn(h, wg, wv, wd, cw, cb, x1, gate, g_post, tm=512, tf=512):
    B = h.shape[0]
    D = D_MODEL
    hb = tm // HALO
    last = SEQ // HALO - 1
    return pl.pallas_call(
        _ffn_kernel, grid=(B, SEQ // tm, D_FF // tf),
        in_specs=[pl.BlockSpec((1, tm, D), lambda b, i, f: (b, i, 0)),
                  pl.BlockSpec((1, HALO, D), lambda b, i, f: (b, jnp.maximum(i * hb - 1, 0), 0)),
                  pl.BlockSpec((1, HALO, D), lambda b, i, f: (b, jnp.minimum((i + 1) * hb, last), 0)),
                  pl.BlockSpec((D, tf), lambda b, i, f: (0, f)),
                  pl.BlockSpec((D, tf), lambda b, i, f: (0, f)),
                  pl.BlockSpec((tf, D), lambda b, i, f: (f, 0)),
                  pl.BlockSpec((3, tf), lambda b, i, f: (0, f)),
                  pl.BlockSpec((1, tf), lambda b, i, f: (0, f)),
                  pl.BlockSpec((1, tm, D), lambda b, i, f: (b, i, 0)),
                  pl.BlockSpec((1, 1, D), lambda b, i, f: (b, 0, 0)),
                  pl.BlockSpec((1, D), lambda b, i, f: (0, 0))],
        out_specs=pl.BlockSpec((1, tm, D), lambda b, i, f: (b, i, 0)),
        out_shape=jax.ShapeDtypeStruct((B, SEQ, D), F32),
        scratch_shapes=[pltpu.VMEM((tm, D), F32)],
        compiler_params=_cparams(("parallel", "parallel", "arbitrary")), name="convffn",
    )(h, h, h, wg, wv, wd, cw, cb[None, :], x1, gate[:, None, :], g_post[None, :])


def _rope_partner(w):
    q = QK_ROPE // 4
    return jnp.concatenate([-w[..., q:2 * q], w[..., :q], -w[..., 3 * q:], w[..., 2 * q:3 * q]], axis=-1)


def _pack_w_in_kernel(wt_ref, perm_ref, main_ref, gate_ref):
    latents = RWKV_IN + Q_LORA + KV_LORA
    main_ref[:RWKV_IN] = wt_ref[:RWKV_IN].astype(BF16)
    kpe = wt_ref[latents:latents + QK_ROPE].astype(BF16)
    partner = jnp.dot(perm_ref[...], kpe, preferred_element_type=F32).astype(BF16)
    main_ref[C_KPE:C_KPE + QK_ROPE] = kpe
    main_ref[C_KPE + QK_ROPE:C_KPE + 2 * QK_ROPE] = partner
    main_ref[C_KPE + 2 * QK_ROPE:C_CQ] = jnp.zeros((C_CQ - C_KPE - 2 * QK_ROPE, kpe.shape[1]), BF16)
    main_ref[C_CQ:] = wt_ref[RWKV_IN:latents].astype(BF16)
    gate_ref[...] = wt_ref[RWKV_IN + MLA_IN:].astype(BF16)


def _pack_w_in(wt, tk=256):
    n_in, K = wt.shape
    perm_t = _rope_partner(jnp.eye(QK_ROPE, dtype=F32)).T.astype(BF16)
    n_gate = n_in - RWKV_IN - MLA_IN
    return pl.pallas_call(
        _pack_w_in_kernel, grid=(K // tk,),
        in_specs=[pl.BlockSpec((n_in, tk), lambda i: (0, i)),
                  pl.BlockSpec((QK_ROPE, QK_ROPE), lambda i: (0, 0))],
        out_specs=[pl.BlockSpec((P_COLS, tk), lambda i: (0, i)),
                   pl.BlockSpec((n_gate, tk), lambda i: (0, i))],
        out_shape=[jax.ShapeDtypeStruct((P_COLS, K), BF16), jax.ShapeDtypeStruct((n_gate, K), BF16)],
        compiler_params=_cparams(("parallel",)), name="pack_w_in",
    )(wt, perm_t)


def _lora_window_weight(w2, first_row):
    n_dir, R, C = w2.shape
    out = jnp.zeros((LORA_WIN, n_dir, C), BF16)
    for d in range(n_dir):
        out = out.at[first_row + d * R:first_row + (d + 1) * R, d].set(w2[d].astype(BF16))
    return out.reshape(LORA_WIN, n_dir * C)


def _rope_tables():
    rows = SEQ // GRID_W
    half = QK_ROPE // 2
    freqs = ROPE_THETA ** (-jnp.arange(0, half, 2, dtype=F32) / half)
    ar = jnp.arange(rows, dtype=F32)[:, None] * freqs
    ac = jnp.arange(GRID_W, dtype=F32)[:, None] * freqs
    by_row = lambda t: jnp.repeat(t, GRID_W, axis=0)
    by_col = lambda t: jnp.tile(t, (rows, 1))
    cos = jnp.concatenate([by_row(jnp.cos(ar))] * 2 + [by_col(jnp.cos(ac))] * 2, axis=-1)
    sin = jnp.concatenate([by_row(jnp.sin(ar))] * 2 + [by_col(jnp.sin(ac))] * 2, axis=-1)
    cos_all = jnp.concatenate([cos, jnp.ones((CTX_LEN, QK_ROPE), F32)], axis=0)
    sin_all = jnp.concatenate([sin, jnp.zeros((CTX_LEN, QK_ROPE), F32)], axis=0)
    return cos_all, sin_all


def kernel(x, c, ctx, c_ctx, w_ada, b_ada, norm_mix_pre, norm_mix_post, norm_ffn_pre, norm_ffn_post, w_in, rwkv_mu, rwkv_w0, rwkv_w2, rwkv_a0, rwkv_a2, rwkv_k_k, rwkv_k_a, rwkv_r_k, rwkv_lnx_w, rwkv_lnx_b, rwkv_g2, w_rwkv_proj, mla_q_norm, mla_kv_norm, mla_w_uq, mla_w_ukv, w_mla_proj, gate_b, w_out, ffn_w_gate, ffn_w_val, ffn_conv_w, ffn_conv_b, ffn_w_down):
    B = x.shape[0]
    D = D_MODEL
    C = RWKV_DIM
    l = 0

    s = jnp.concatenate([jax.nn.silu(c), jax.nn.silu(c_ctx)[None, :]], axis=0)
    mods = _ada(s, w_ada[l], b_ada[l])
    lat = mods[:B].reshape(B, 6, D)
    cm = mods[B].reshape(6, D)
    sh1, sc1, g1, sh2, sc2, g2 = [lat[:, j] for j in range(6)]
    mods1 = jnp.stack([jnp.stack([sh1, sc1], axis=1),
                       jnp.broadcast_to(jnp.stack([cm[0], cm[1]])[None], (B, 2, D))], axis=1)
    mods2 = jnp.stack([sh2, sc2], axis=1)

    h = _norm_mod(x, ctx, norm_mix_pre[l], mods1)
    wt_main, wt_gate = _pack_w_in(w_in[l].T)
    p = _matmul_nt(h.reshape(B * S_ALL, D), wt_main, tm=1088, tn=1024, name="w_in").reshape(B, S_ALL, P_COLS)
    gates = _gates(h, wt_gate, gate_b[l])

    lane_head = jnp.arange(LANES) // RWKV_HEAD
    ones2 = (lane_head[:, None] == lane_head[None, :]).astype(BF16)
    avg2 = ones2 * (1.0 / RWKV_HEAD)
    r, k, v, kk, lw, a, g, bg = _prep(p, rwkv_mu[l], rwkv_w0[l], _lora_window_weight(rwkv_w2[l], C_WL - C_WL_WIN),
                                      rwkv_a0[l], _lora_window_weight(rwkv_a2[l], C_AL - C_AL_WIN), rwkv_k_k[l],
                                      rwkv_k_a[l], rwkv_r_k[l].reshape(C), rwkv_g2[l].astype(BF16), ones2)
    y_dir = _wkv(r, k, v, kk, lw, a, rwkv_k_a[l])

    cos, sin = _rope_tables()
    uq = mla_w_uq[l].reshape(Q_LORA, MLA_HEADS, QK_NOPE + QK_ROPE)
    wq = jnp.concatenate([uq, _rope_partner(uq[..., QK_NOPE:])], axis=-1).transpose(1, 0, 2).astype(BF16)
    wkv = mla_w_ukv[l].reshape(KV_LORA, MLA_HEADS, QK_NOPE + V_HEAD).transpose(1, 0, 2).astype(BF16)
    q = _qproj(p, mla_q_norm[l], wq, cos[:SEQ], sin[:SEQ])
    k_all, v_all = _kvproj(p, mla_kv_norm[l], wkv, cos, sin)
    o = _attention(q, k_all, v_all)

    merged = _merge(y_dir, g, bg, rwkv_lnx_w[l], rwkv_lnx_b[l], avg2, o, w_rwkv_proj[l].astype(BF16),
                    w_mla_proj[l].astype(BF16), gates)
    x1, h2 = _outproj(merged, w_out[l].astype(BF16), x, norm_mix_post[l], g1, norm_ffn_pre[l], mods2)
    return _ffn(h2, ffn_w_gate[l].astype(BF16), ffn_w_val[l].astype(BF16), ffn_w_down[l].astype(BF16),
                ffn_conv_w[l], ffn_conv_b[l], x1, g2, norm_ffn_post[l])
```

```python
import functools
import math

import jax
import jax.numpy as jnp
from jax import lax
from jax.experimental import pallas as pl
from jax.experimental.pallas import tpu as pltpu

D_MODEL = 2048
BATCH = 2
SEQ = 4096
GRID_W = 64
CTX_LEN = 256
S_ALL = SEQ + CTX_LEN
EPS = 1e-6
RWKV_HEADS = 16
RWKV_HEAD = 64
RWKV_DIM = RWKV_HEADS * RWKV_HEAD
DECAY_LORA = 96
AAA_LORA = 96
GATE_LORA = 256
N_DIR = 2
DECAY_SCALE = math.exp(-0.5)
LNX_EPS = 64e-5
MLA_HEADS = 16
Q_LORA = 512
KV_LORA = 512
QK_NOPE = 128
QK_ROPE = 64
V_HEAD = 128
ROPE_THETA = 10000.0
ATTN_SCALE = (QK_NOPE + QK_ROPE) ** -0.5
LOG2_E = math.log2(math.e)
D_FF = 5632
RWKV_IN = 3 * RWKV_DIM + N_DIR * DECAY_LORA + N_DIR * AAA_LORA + GATE_LORA
MLA_IN = Q_LORA + KV_LORA + QK_ROPE

LANES = 128
SUBLANES = 8
VMEM_LIMIT = 56 * 1024 * 1024

C_R, C_K, C_V = 0, RWKV_DIM, 2 * RWKV_DIM
C_WL = 3 * RWKV_DIM
C_AL = C_WL + N_DIR * DECAY_LORA
C_GL = C_AL + N_DIR * AAA_LORA
RWKV_COLS = C_GL + GATE_LORA
C_KPE = RWKV_COLS
C_CQ = -(-(C_KPE + 2 * QK_ROPE) // Q_LORA) * Q_LORA
C_CKV = C_CQ + Q_LORA
P_COLS = C_CKV + KV_LORA
LORA_WIN = 2 * LANES
C_WL_WIN = C_WL // LANES * LANES
C_AL_WIN = C_AL // LANES * LANES

WKV_CHUNK = 64
WKV_SUB = 4
HEAD_PAIRS = RWKV_DIM // LANES

F32 = jnp.float32
BF16 = jnp.bfloat16


def _cparams(sem):
    return pltpu.CompilerParams(dimension_semantics=sem, vmem_limit_bytes=VMEM_LIMIT)


def _bdot(a, b):
    return jnp.dot(a.astype(BF16), b.astype(BF16), preferred_element_type=F32)


def _hilo_dot(x, w):
    hi = x.astype(BF16)
    lo = (x - hi.astype(F32)).astype(BF16)
    return jnp.dot(hi, w, preferred_element_type=F32) + jnp.dot(lo, w, preferred_element_type=F32)


def _head_sum(x, ones2):
    return jnp.concatenate([_hilo_dot(x[:, c * LANES:(c + 1) * LANES], ones2) for c in range(HEAD_PAIRS)], axis=-1)


def _ada_kernel(s_ref, w_ref, b_ref, o_ref):
    w = w_ref[...]
    for r in range(s_ref.shape[0]):
        o_ref[r:r + 1, :] = jnp.sum(s_ref[r] * w, axis=0, keepdims=True) + b_ref[...]


def _ada(s, w, b, tn=1024):
    R, K = s.shape
    N = w.shape[1]
    return pl.pallas_call(
        _ada_kernel, grid=(N // tn,),
        in_specs=[pl.BlockSpec((R, K, 1), lambda j: (0, 0, 0)),
                  pl.BlockSpec((K, tn), lambda j: (0, j)),
                  pl.BlockSpec((1, tn), lambda j: (0, j))],
        out_specs=pl.BlockSpec((R, tn), lambda j: (0, j)),
        out_shape=jax.ShapeDtypeStruct((R, N), F32),
        compiler_params=_cparams(("parallel",)), name="ada",
    )(s[:, :, None], w, b[None, :])


def _norm_mod_kernel(x_ref, c_ref, g_ref, m_ref, o_ref, *, lat_tiles):
    def emit(x):
        n = x * lax.rsqrt(jnp.mean(x * x, axis=-1, keepdims=True) + EPS) * g_ref[...]
        m = m_ref[0, 0]
        o_ref[0] = (n * (1.0 + m[1:2]) + m[0:1]).astype(o_ref.dtype)

    is_latent = pl.program_id(1) < lat_tiles

    @pl.when(is_latent)
    def _():
        emit(x_ref[0])

    @pl.when(jnp.logical_not(is_latent))
    def _():
        emit(c_ref[0])


def _norm_mod(x, ctx, gain, mods, tm=256):
    B, _, D = x.shape
    assert ctx.shape[1] == tm
    lat_tiles = SEQ // tm
    return pl.pallas_call(
        functools.partial(_norm_mod_kernel, lat_tiles=lat_tiles), grid=(B, S_ALL // tm),
        in_specs=[pl.BlockSpec((1, tm, D), lambda b, i: (b, jnp.minimum(i, lat_tiles - 1), 0)),
                  pl.BlockSpec((1, tm, D), lambda b, i: (b, 0, 0)),
                  pl.BlockSpec((1, D), lambda b, i: (0, 0)),
                  pl.BlockSpec((1, 1, 2, D), lambda b, i: (b, i // lat_tiles, 0, 0))],
        out_specs=pl.BlockSpec((1, tm, D), lambda b, i: (b, i, 0)),
        out_shape=jax.ShapeDtypeStruct((B, S_ALL, D), BF16),
        compiler_params=_cparams(("parallel", "parallel")), name="norm_mod",
    )(x, ctx, gain[None, :], mods)


def _dot_nt(a, wt):
    return lax.dot_general(a, wt, (((1,), (1,)), ((), ())), preferred_element_type=F32)


def _mm_kernel(a_ref, wt_ref, o_ref):
    o_ref[...] = _dot_nt(a_ref[...], wt_ref[...]).astype(o_ref.dtype)


def _matmul_nt(a, wt, tm, tn, out_dtype=F32, name="matmul"):
    M, K = a.shape
    N = wt.shape[0]
    return pl.pallas_call(
        _mm_kernel, grid=(M // tm, N // tn),
        in_specs=[pl.BlockSpec((tm, K), lambda i, j: (i, 0)),
                  pl.BlockSpec((tn, K), lambda i, j: (j, 0))],
        out_specs=pl.BlockSpec((tm, tn), lambda i, j: (i, j)),
        out_shape=jax.ShapeDtypeStruct((M, N), out_dtype),
        compiler_params=_cparams(("parallel", "parallel")), name=name,
    )(a, wt)


def _prep_kernel(p_ref, pp_ref, pn_ref, mu_ref, w0_ref, w2_ref, a0_ref, a2_ref, kk_ref, ka_ref, rk_ref, g2_ref,
                 ones_ref, r_o, k_o, v_o, kkn_o, lw_o, a_o, g_o, bg_o):
    i = pl.program_id(1)
    x = p_ref[0]
    tm = x.shape[0]
    lat_tiles = SEQ // tm
    first = jnp.logical_or(i == 0, i == lat_tiles)
    last = jnp.logical_or(i == lat_tiles - 1, i == pl.num_programs(1) - 1)
    x_before = jnp.where(first, 0.0, pp_ref[0][SUBLANES - 1:SUBLANES])
    x_after = jnp.where(last, 0.0, pn_ref[0][0:1])
    rows = lax.broadcasted_iota(jnp.int32, (tm, 1), 0)
    prev = jnp.where(rows == 0, x_before, pltpu.roll(x, 1, 0))
    nxt = jnp.where(rows == tm - 1, x_after, pltpu.roll(x, tm - 1, 0))
    xs = x + mu_ref[...] * (0.5 * (prev + nxt) - x)

    C = RWKV_DIM
    r, k, v = xs[:, C_R:C_R + C], xs[:, C_K:C_K + C], xs[:, C_V:C_V + C]
    w_lora = _bdot(jnp.tanh(xs[:, C_WL_WIN:C_WL_WIN + LORA_WIN]), w2_ref[...])
    a_lora = _bdot(xs[:, C_AL_WIN:C_AL_WIN + LORA_WIN], a2_ref[...])
    a_sum = None
    for d in range(N_DIR):
        w_raw = w0_ref[d:d + 1] + w_lora[:, d * C:(d + 1) * C]
        lw_o[0, d] = -DECAY_SCALE * jax.nn.sigmoid(w_raw)
        a_d = jax.nn.sigmoid(a0_ref[d:d + 1] + a_lora[:, d * C:(d + 1) * C])
        a_o[0, d] = a_d
        a_sum = a_d if a_sum is None else a_sum + a_d
    ones2 = ones_ref[...]
    kkx = k * kk_ref[...]
    kkn_o[0] = kkx / jnp.maximum(jnp.sqrt(_head_sum(kkx * kkx, ones2)), 1e-12)
    g = _bdot(jax.nn.sigmoid(xs[:, C_GL:C_GL + GATE_LORA]), g2_ref[...])
    k_bar = k * (1.0 + (0.5 * a_sum - 1.0) * ka_ref[...])
    bonus = _head_sum(r * k_bar * rk_ref[...], ones2) * v
    r_o[0] = r
    k_o[0] = k
    v_o[0] = v
    g_o[0] = g.astype(g_o.dtype)
    bg_o[0] = (bonus * g).astype(bg_o.dtype)


def _prep(p, mu, w0, w2p, a0, a2p, k_k, k_a, r_k, g2, ones2, tm=256):
    B = p.shape[0]
    C = RWKV_DIM
    hb = tm // SUBLANES
    last = S_ALL // SUBLANES - 1
    row = lambda n: pl.BlockSpec((1, n), lambda b, i: (0, 0))
    tok = pl.BlockSpec((1, tm, C), lambda b, i: (b, i, 0))
    tok_dir = pl.BlockSpec((1, N_DIR, tm, C), lambda b, i: (b, 0, i, 0))
    shp = jax.ShapeDtypeStruct((B, S_ALL, C), F32)
    shp_dir = jax.ShapeDtypeStruct((B, N_DIR, S_ALL, C), F32)
    shp_b = jax.ShapeDtypeStruct((B, S_ALL, C), BF16)
    return pl.pallas_call(
        _prep_kernel, grid=(B, S_ALL // tm),
        in_specs=[pl.BlockSpec((1, tm, RWKV_COLS), lambda b, i: (b, i, 0)),
                  pl.BlockSpec((1, SUBLANES, RWKV_COLS), lambda b, i: (b, jnp.maximum(i * hb - 1, 0), 0)),
                  pl.BlockSpec((1, SUBLANES, RWKV_COLS), lambda b, i: (b, jnp.minimum((i + 1) * hb, last), 0)),
                  row(RWKV_COLS),
                  pl.BlockSpec((N_DIR, C), lambda b, i: (0, 0)),
                  pl.BlockSpec((LORA_WIN, N_DIR * C), lambda b, i: (0, 0)),
                  pl.BlockSpec((N_DIR, C), lambda b, i: (0, 0)),
                  pl.BlockSpec((LORA_WIN, N_DIR * C), lambda b, i: (0, 0)),
                  row(C), row(C), row(C),
                  pl.BlockSpec((GATE_LORA, C), lambda b, i: (0, 0)),
                  pl.BlockSpec((LANES, LANES), lambda b, i: (0, 0))],
        out_specs=[tok, tok, tok, tok, tok_dir, tok_dir, tok, tok],
        out_shape=[shp, shp, shp, shp, shp_dir, shp_dir, shp_b, shp_b],
        compiler_params=_cparams(("parallel", "parallel")), name="rwkv_prep",
    )(p, p, p, mu[None, :], w0, w2p, a0, a2p, k_k[None, :], k_a[None, :], r_k[None, :], g2, ones2)


def _wkv_kernel(r_ref, k_ref, v_ref, kk_ref, lw_ref, a_ref, ka_ref, y_ref, s_ref):
    d = pl.program_id(1)
    C = WKV_CHUNK
    C2 = 2 * C
    P = HEAD_PAIRS
    sign = 1 - 2 * d
    fwd = d == 0

    @pl.when(pl.program_id(2) == 0)
    def _():
        s_ref[...] = jnp.zeros_like(s_ref)

    n_rows = WKV_SUB * C
    trow = lax.broadcasted_iota(jnp.int32, (n_rows, n_rows), 0)
    tcol = lax.broadcasted_iota(jnp.int32, (n_rows, n_rows), 1)
    same_chunk = (trow // C) == (tcol // C)
    tri = jnp.logical_and(same_chunk, (trow - tcol) * sign >= 0).astype(BF16)
    lane = lax.broadcasted_iota(jnp.int32, (C, LANES), 1)
    head0 = lane < RWKV_HEAD
    zero = jnp.zeros((), F32)

    def expand(x):
        x = x.astype(BF16)
        zb = jnp.zeros((), BF16)
        tiles = [x[c * C:(c + 1) * C, j * LANES:(j + 1) * LANES] for c in range(WKV_SUB) for j in range(P)]
        return jnp.stack([jnp.concatenate([jnp.where(head0, t, zb), jnp.where(head0, zb, t)], axis=0)
                          for t in tiles])

    def bmm(x, y):
        return jnp.einsum('hab,hbc->hac', x.astype(BF16), y.astype(BF16), preferred_element_type=F32)

    def bmm_nt(x, y):
        return jnp.einsum('han,hbn->hab', x.astype(BF16), y.astype(BF16), preferred_element_type=F32)

    def bmm_tn(x, y):
        return jnp.einsum('hca,hcb->hab', x.astype(BF16), y.astype(BF16), preferred_element_type=F32)

    row = lax.broadcasted_iota(jnp.int32, (C2, C2), 0)
    col = lax.broadcasted_iota(jnp.int32, (C2, C2), 1)
    same = (row // C) == (col // C)
    lag = (row - col) * sign
    incl = jnp.logical_and(same, lag >= 0)[None]
    strict = jnp.logical_and(same, lag > 0)[None]

    def blk(m):
        return (row // m) == (col // m)

    def half_rows(t, m, second):
        return jnp.concatenate([t[:, g + second * m:g + (second + 1) * m] for g in range(0, C2, 2 * m)], axis=1)

    def state_free(chunk_rows):
        def load(ref, *lead):
            return jnp.concatenate([ref[(*lead, rows, slice(None))] for rows in chunk_rows], axis=0)

        r = load(r_ref, 0)
        v = load(v_ref, 0)
        kk = load(kk_ref, 0)
        lw = load(lw_ref, 0, 0)
        a = load(a_ref, 0, 0)
        k = load(k_ref, 0) * (1.0 + (a - 1.0) * ka_ref[...])

        cum = _hilo_dot_left(tri, lw)
        tot = jnp.concatenate(
            [jnp.broadcast_to(jnp.where(fwd, cum[(c + 1) * C - 1:(c + 1) * C], cum[c * C:c * C + 1]), (C, cum.shape[1]))
             for c in range(WKV_SUB)], axis=0)
        e_cum = jnp.exp(cum)
        e_neg = jnp.exp(-cum)
        e_rem = jnp.exp(tot - cum)
        b = kk * a
        at = -kk * jnp.exp(cum - lw)
        dec = jnp.exp(tot)
        AT, RT, BT, KT, BD, KD, V = (expand(t) for t in (at, r * e_cum, b * e_neg, k * e_neg, b * e_rem,
                                                         k * e_rem, v))
        AR = jnp.concatenate([AT, RT], axis=1)
        G = bmm_nt(AR, jnp.concatenate([BT, KT], axis=1))
        a_ab, a_ak = G[:, :C2, :C2], G[:, :C2, C2:]
        a_rb, a_rk = G[:, C2:, :C2], G[:, C2:, C2:]

        L = jnp.where(strict, a_ab, zero)
        Lb = jnp.where(blk(8)[None], L, zero)
        L2 = bmm(Lb, Lb)
        L4 = bmm(L2, L2)
        S1 = Lb + L2 + bmm(Lb, L2)
        Nm = S1 + L4 + bmm(L4, S1)
        for m in (8, 16, 32):
            Lo = jnp.where(jnp.logical_and(blk(2 * m), jnp.logical_not(blk(m)))[None], L, zero)
            n_first, n_second = half_rows(Nm, m, 0), half_rows(Nm, m, 1)
            n_late = jnp.where(fwd, n_second, n_first)
            p_late = jnp.where(fwd, half_rows(Lo, m, 1), half_rows(Lo, m, 0)) + bmm(n_late, Lo)
            n_late = n_late + p_late + bmm(p_late, Nm)
            n_first, n_second = jnp.where(fwd, n_first, n_late), jnp.where(fwd, n_late, n_second)
            Nm = jnp.concatenate([piece for g in range(C2 // (2 * m))
                                  for piece in (n_first[:, g * m:(g + 1) * m], n_second[:, g * m:(g + 1) * m])],
                                 axis=1)
        rhs_v = bmm(jnp.where(strict, a_ak, zero), V)
        a_y = jnp.concatenate([jnp.where(incl, a_rb, zero), jnp.where(incl, a_rk, zero)], axis=2).astype(BF16)
        dec3 = jnp.stack([dec[c * C:c * C + 1, j * LANES:(j + 1) * LANES]
                          for c in range(WKV_SUB) for j in range(P)])
        whole = (AR, rhs_v, Nm.astype(BF16), V, a_y, jnp.concatenate([BD, KD], axis=1), dec3)
        return [tuple(t[c * P:(c + 1) * P] for t in whole) for c in range(WKV_SUB)]

    def state_step(part, s0):
        AR, rhs_v, Nm, V, a_y, BKD, dec3 = part
        ars0 = bmm_nt(AR, s0)
        rhs = ars0[:, :C2] + rhs_v
        u = rhs + bmm(Nm, rhs)
        uv = jnp.concatenate([u.astype(BF16), V], axis=1)
        y2 = ars0[:, C2:] + bmm(a_y, uv)
        y = y2[:, :C] + y2[:, C:]
        return jnp.concatenate([y[j] for j in range(P)], axis=-1), s0 * dec3 + bmm_tn(uv, BKD)

    chunk_rows = [pl.ds(pl.multiple_of(jnp.where(fwd, j, WKV_SUB - 1 - j) * C, C), C) for j in range(WKV_SUB)]
    parts = state_free(chunk_rows)
    s = s_ref[...]
    for rows, part in zip(chunk_rows, parts):
        y, s = state_step(part, s)
        y_ref[0, 0, rows, :] = y
    s_ref[...] = s


def _hilo_dot_left(w, x):
    hi = x.astype(BF16)
    lo = (x - hi.astype(F32)).astype(BF16)
    return jnp.dot(w, hi, preferred_element_type=F32) + jnp.dot(w, lo, preferred_element_type=F32)


def _wkv(r, k, v, kk, lw, a, k_a):
    B, S, Cd = r.shape
    R = WKV_SUB * WKV_CHUNK
    n_blocks = S // R
    lat_blocks = SEQ // R

    def bidx(d, i):
        return jnp.where(d == 0, (i + lat_blocks) % n_blocks, n_blocks - 1 - i)

    shared = pl.BlockSpec((1, R, Cd), lambda b, d, i: (b, bidx(d, i), 0))
    perdir = pl.BlockSpec((1, 1, R, Cd), lambda b, d, i: (b, d, bidx(d, i), 0))
    return pl.pallas_call(
        _wkv_kernel, grid=(B, N_DIR, n_blocks),
        in_specs=[shared, shared, shared, shared, perdir, perdir, pl.BlockSpec((1, Cd), lambda b, d, i: (0, 0))],
        out_specs=perdir,
        out_shape=jax.ShapeDtypeStruct((B, N_DIR, S, Cd), F32),
        scratch_shapes=[pltpu.VMEM((HEAD_PAIRS, LANES, LANES), F32)],
        compiler_params=_cparams(("parallel", "parallel", "arbitrary")), name="wkv7",
    )(r, k, v, kk, lw, a, k_a[None, :])


def _rms(x, g):
    return x * lax.rsqrt(jnp.mean(x * x, axis=-1, keepdims=True) + EPS) * g


def _qproj_kernel(p_ref, g_ref, w_ref, cos_ref, sin_ref, q_ref):
    cq = _rms(p_ref[0], g_ref[...]).astype(BF16)
    cs, sn = cos_ref[...], sin_ref[...]
    for h in range(MLA_HEADS):
        y = jnp.dot(cq, w_ref[h], preferred_element_type=F32)
        rope = y[:, QK_NOPE:QK_NOPE + QK_ROPE] * cs + y[:, QK_NOPE + QK_ROPE:] * sn
        q = jnp.concatenate([y[:, :QK_NOPE], rope], axis=-1) * (ATTN_SCALE * LOG2_E)
        q_ref[0, h] = q.astype(q_ref.dtype)


def _qproj(p, gain, wq, cos, sin, tm=512):
    B = p.shape[0]
    return pl.pallas_call(
        _qproj_kernel, grid=(B, SEQ // tm),
        in_specs=[pl.BlockSpec((1, tm, Q_LORA), lambda b, i: (b, i, C_CQ // Q_LORA)),
                  pl.BlockSpec((1, Q_LORA), lambda b, i: (0, 0)),
                  pl.BlockSpec((MLA_HEADS, Q_LORA, 2 * LANES), lambda b, i: (0, 0, 0)),
                  pl.BlockSpec((tm, QK_ROPE), lambda b, i: (i, 0)),
                  pl.BlockSpec((tm, QK_ROPE), lambda b, i: (i, 0))],
        out_specs=pl.BlockSpec((1, MLA_HEADS, tm, QK_NOPE + QK_ROPE), lambda b, i: (b, 0, i, 0)),
        out_shape=jax.ShapeDtypeStruct((B, MLA_HEADS, SEQ, QK_NOPE + QK_ROPE), BF16),
        compiler_params=_cparams(("parallel", "parallel")), name="mla_q",
    )(p, gain[None, :], wq, cos, sin)


def _kvproj_kernel(p_ref, pe_ref, g_ref, w_ref, cos_ref, sin_ref, k_ref, v_ref):
    ckv = _rms(p_ref[0], g_ref[...]).astype(BF16)
    pe = pe_ref[0]
    kpe = pe[:, :QK_ROPE] * cos_ref[...] + pe[:, QK_ROPE:] * sin_ref[...]
    ones = jnp.ones((ckv.shape[0], V_HEAD), F32)
    for h in range(MLA_HEADS):
        y = jnp.dot(ckv, w_ref[h], preferred_element_type=F32)
        k_ref[0, h] = jnp.concatenate([y[:, :QK_NOPE], kpe], axis=-1).astype(k_ref.dtype)
        v_ref[0, h] = jnp.concatenate([y[:, QK_NOPE:], ones], axis=-1).astype(v_ref.dtype)


def _kvproj(p, gain, wkv, cos, sin, tm=256):
    B = p.shape[0]
    return pl.pallas_call(
        _kvproj_kernel, grid=(B, S_ALL // tm),
        in_specs=[pl.BlockSpec((1, tm, KV_LORA), lambda b, i: (b, i, C_CKV // KV_LORA)),
                  pl.BlockSpec((1, tm, 2 * QK_ROPE), lambda b, i: (b, i, C_KPE // (2 * QK_ROPE))),
                  pl.BlockSpec((1, KV_LORA), lambda b, i: (0, 0)),
                  pl.BlockSpec((MLA_HEADS, KV_LORA, 2 * LANES), lambda b, i: (0, 0, 0)),
                  pl.BlockSpec((tm, QK_ROPE), lambda b, i: (i, 0)),
                  pl.BlockSpec((tm, QK_ROPE), lambda b, i: (i, 0))],
        out_specs=[pl.BlockSpec((1, MLA_HEADS, tm, QK_NOPE + QK_ROPE), lambda b, i: (b, 0, i, 0)),
                   pl.BlockSpec((1, MLA_HEADS, tm, 2 * V_HEAD), lambda b, i: (b, 0, i, 0))],
        out_shape=[jax.ShapeDtypeStruct((B, MLA_HEADS, S_ALL, QK_NOPE + QK_ROPE), BF16),
                   jax.ShapeDtypeStruct((B, MLA_HEADS, S_ALL, 2 * V_HEAD), BF16)],
        compiler_params=_cparams(("parallel", "parallel")), name="mla_kv",
    )(p, p, gain[None, :], wkv, cos, sin)


def _attn_kernel(q_ref, k_ref, v_ref, o_ref, sa, sb, pa, pb, acc, m_ref, al_ref, *, tk):
    q = q_ref[0, 0]
    n_keys = k_ref.shape[2]
    blocks = [(off, min(tk, n_keys - off)) for off in range(0, n_keys, tk)]
    n_kv = len(blocks)

    def put_scores(s_ref, j):
        off, w = blocks[j]
        kj = k_ref[0, 0, off:off + w, :]
        s_ref[:, :w] = lax.dot_general(q, kj, (((1,), (1,)), ((), ())), preferred_element_type=F32)

    def accumulate(p_ref, j):
        off, w = blocks[j]
        pv = jnp.dot(p_ref[:, :w], v_ref[0, 0, off:off + w, :], preferred_element_type=F32)
        al = al_ref[...]
        for c in range(0, 2 * V_HEAD, LANES):
            acc[:, c:c + LANES] = al * acc[:, c:c + LANES] + pv[:, c:c + LANES]

    def softmax(s_ref, p_ref, j):
        w = blocks[j][1]
        s = s_ref[:, :w]
        m_old = m_ref[...]
        m_new = jnp.maximum(m_old, jnp.max(s, axis=-1, keepdims=True))
        al_ref[...] = jnp.exp2(m_old - m_new)
        m_ref[...] = m_new
        for c in range(0, w, LANES):
            p_ref[:, c:c + LANES] = jnp.exp2(s[:, c:c + LANES] - m_new).astype(p_ref.dtype)

    s_bufs, p_bufs = (sa, sb), (pa, pb)
    put_scores(sa, 0)
    acc[...] = jnp.zeros_like(acc)
    m_ref[...] = jnp.full_like(m_ref, -jnp.inf)
    for j in range(n_kv):
        if j + 1 < n_kv:
            put_scores(s_bufs[(j + 1) % 2], j + 1)
        if j >= 1:
            accumulate(p_bufs[(j - 1) % 2], j - 1)
        softmax(s_bufs[j % 2], p_bufs[j % 2], j)
    accumulate(p_bufs[(n_kv - 1) % 2], n_kv - 1)
    o_ref[0] = (acc[:, :V_HEAD] / acc[:, V_HEAD:]).astype(o_ref.dtype)


def _attention(q, k, v, tq=1024, tk=1024):
    B, H, T, Dk = q.shape
    S = k.shape[2]
    return pl.pallas_call(
        functools.partial(_attn_kernel, tk=tk), grid=(B, H, T // tq),
        in_specs=[pl.BlockSpec((1, 1, tq, Dk), lambda b, h, i: (b, h, i, 0)),
                  pl.BlockSpec((1, 1, S, Dk), lambda b, h, i: (b, h, 0, 0)),
                  pl.BlockSpec((1, 1, S, 2 * V_HEAD), lambda b, h, i: (b, h, 0, 0))],
        out_specs=pl.BlockSpec((1, tq, V_HEAD), lambda b, h, i: (b, i, h)),
        out_shape=jax.ShapeDtypeStruct((B, T, H * V_HEAD), BF16),
        scratch_shapes=[pltpu.VMEM((tq, tk), F32), pltpu.VMEM((tq, tk), F32),
                        pltpu.VMEM((tq, tk), BF16), pltpu.VMEM((tq, tk), BF16),
                        pltpu.VMEM((tq, 2 * V_HEAD), F32), pltpu.VMEM((tq, LANES), F32), pltpu.VMEM((tq, LANES), F32)],
        compiler_params=_cparams(("parallel", "parallel", "parallel")), name="mla_attn",
    )(q, k, v)


def _gates_kernel(h_ref, wt_ref, b_ref, o_ref):
    o_ref[0] = jax.nn.sigmoid(_dot_nt(h_ref[0], wt_ref[...]) + b_ref[...]).astype(o_ref.dtype)


def _gates(h, wt_gate, gate_b, tm=1024, tn=1024):
    B, _, D = h.shape
    N = wt_gate.shape[0]
    return pl.pallas_call(
        _gates_kernel, grid=(B, SEQ // tm, N // tn),
        in_specs=[pl.BlockSpec((1, tm, D), lambda b, i, j: (b, i, 0)),
                  pl.BlockSpec((tn, D), lambda b, i, j: (j, 0)),
                  pl.BlockSpec((1, tn), lambda b, i, j: (0, j))],
        out_specs=pl.BlockSpec((1, tm, tn), lambda b, i, j: (b, i, j)),
        out_shape=jax.ShapeDtypeStruct((B, SEQ, N), BF16),
        compiler_params=_cparams(("parallel", "parallel", "parallel")), name="gates",
    )(h, wt_gate, gate_b[None, :])


MERGE_TN = 512


def _merge_kernel(y0_ref, y1_ref, g_ref, bg_ref, lnw_ref, lnb_ref, avg_ref, o_ref, wr_ref, wm_ref, gate_ref, out_ref):
    avg = avg_ref[...]
    y = y0_ref[0, 0] + y1_ref[0, 0]
    yc = y - _head_sum(y, avg)
    yn = yc * lax.rsqrt(_head_sum(yc * yc, avg) + LNX_EPS) * lnw_ref[...] + lnb_ref[...]
    yr = (yn * g_ref[0].astype(F32) + bg_ref[0].astype(F32)).astype(BF16)
    o = o_ref[0]
    D = D_MODEL
    for n in range(0, D, MERGE_TN):
        gr = gate_ref[0, :, n:n + MERGE_TN].astype(F32)
        gm = gate_ref[0, :, D + n:D + n + MERGE_TN].astype(F32)
        out_ref[0, :, n:n + MERGE_TN] = (
            gr * jnp.dot(yr, wr_ref[:, n:n + MERGE_TN], preferred_element_type=F32)
            + gm * jnp.dot(o, wm_ref[:, n:n + MERGE_TN], preferred_element_type=F32)).astype(out_ref.dtype)


def _merge(y_dir, g, bg, lnx_w, lnx_b, avg2, o, w_rp, w_mp, gates, tm=256):
    B = o.shape[0]
    D = D_MODEL
    C = RWKV_DIM
    tok = pl.BlockSpec((1, tm, C), lambda b, i: (b, i, 0))
    row = lambda n: pl.BlockSpec((1, n), lambda b, i: (0, 0))
    return pl.pallas_call(
        _merge_kernel, grid=(B, SEQ // tm),
        in_specs=[pl.BlockSpec((1, 1, tm, C), lambda b, i: (b, 0, i, 0)),
                  pl.BlockSpec((1, 1, tm, C), lambda b, i: (b, 1, i, 0)),
                  tok, tok, row(C), row(C),
                  pl.BlockSpec((LANES, LANES), lambda b, i: (0, 0)),
                  pl.BlockSpec((1, tm, D), lambda b, i: (b, i, 0)),
                  pl.BlockSpec((C, D), lambda b, i: (0, 0)),
                  pl.BlockSpec((D, D), lambda b, i: (0, 0)),
                  pl.BlockSpec((1, tm, 2 * D), lambda b, i: (b, i, 0))],
        out_specs=pl.BlockSpec((1, tm, D), lambda b, i: (b, i, 0)),
        out_shape=jax.ShapeDtypeStruct((B, SEQ, D), BF16),
        compiler_params=_cparams(("parallel", "parallel")), name="merge",
    )(y_dir, y_dir, g, bg, lnx_w[None, :], lnx_b[None, :], avg2, o, w_rp, w_mp, gates)


def _outproj_kernel(m_ref, w_ref, x_ref, gpost_ref, gate_ref, gpre_ref, mod_ref, x1_ref, h_ref):
    out = jnp.dot(m_ref[0], w_ref[...], preferred_element_type=F32)
    x1 = x_ref[0] + gate_ref[0] * _rms(out, gpost_ref[...])
    x1_ref[0] = x1
    mod = mod_ref[0]
    h_ref[0] = (_rms(x1, gpre_ref[...]) * (1.0 + mod[1:2]) + mod[0:1]).astype(h_ref.dtype)


def _outproj(merged, w_out, x, g_post, gate, g_pre, mod2, tm=512):
    B = x.shape[0]
    D = D_MODEL
    return pl.pallas_call(
        _outproj_kernel, grid=(B, SEQ // tm),
        in_specs=[pl.BlockSpec((1, tm, D), lambda b, i: (b, i, 0)),
                  pl.BlockSpec((D, D), lambda b, i: (0, 0)),
                  pl.BlockSpec((1, tm, D), lambda b, i: (b, i, 0)),
                  pl.BlockSpec((1, D), lambda b, i: (0, 0)),
                  pl.BlockSpec((1, 1, D), lambda b, i: (b, 0, 0)),
                  pl.BlockSpec((1, D), lambda b, i: (0, 0)),
                  pl.BlockSpec((1, 2, D), lambda b, i: (b, 0, 0))],
        out_specs=[pl.BlockSpec((1, tm, D), lambda b, i: (b, i, 0)),
                   pl.BlockSpec((1, tm, D), lambda b, i: (b, i, 0))],
        out_shape=[jax.ShapeDtypeStruct((B, SEQ, D), F32),
                   jax.ShapeDtypeStruct((B, SEQ, D), BF16)],
        compiler_params=_cparams(("parallel", "parallel")), name="outproj",
    )(merged, w_out, x, g_post[None, :], gate[:, None, :], g_pre[None, :], mod2)


HALO = 16


def _ffn_kernel(h_ref, hp_ref, hn_ref, wg_ref, wv_ref, wd_ref, cw_ref, cb_ref, x_ref, gate_ref, gpost_ref,
                o_ref, acc_ref):
    i = pl.program_id(1)
    f = pl.program_id(2)
    tm = h_ref.shape[1]

    @pl.when(f == 0)
    def _():
        acc_ref[...] = jnp.zeros_like(acc_ref)

    h = h_ref[0]
    wg = wg_ref[...]
    g_ext = jnp.dot(jnp.concatenate([hp_ref[0], h, hn_ref[0]], axis=0), wg, preferred_element_type=F32)
    g = g_ext[HALO:HALO + tm]
    g_prev = g_ext[HALO - 1:HALO]
    g_next = g_ext[HALO + tm:HALO + tm + 1]
    g_prev = jnp.where(i == 0, 0.0, g_prev)
    g_next = jnp.where(i == pl.num_programs(1) - 1, 0.0, g_next)
    rows = lax.broadcasted_iota(jnp.int32, g.shape, 0)
    up = jnp.where(rows == 0, g_prev, pltpu.roll(g, 1, 0))
    dn = jnp.where(rows == tm - 1, g_next, pltpu.roll(g, tm - 1, 0))
    cw = cw_ref[...]
    u = cb_ref[...] + up * cw[0:1] + g * cw[1:2] + dn * cw[2:3]
    val = jnp.dot(h, wv_ref[...], preferred_element_type=F32)
    act = jax.nn.gelu(u, approximate=True) * val
    acc_ref[...] += jnp.dot(act.astype(BF16), wd_ref[...], preferred_element_type=F32)

    @pl.when(f == pl.num_programs(2) - 1)
    def _():
        o_ref[0] = x_ref[0] + gate_ref[0] * _rms(acc_ref[...], gpost_ref[...])


def _ffn(h, wg, wv, wd, cw, cb, x1, gate, g_post, tm=512, tf=512):
    B = h.shape[0]
    D = D_MODEL
    hb = tm // HALO
    last = SEQ // HALO - 1
    return pl.pallas_call(
        _ffn_kernel, grid=(B, SEQ // tm, D_FF // tf),
        in_specs=[pl.BlockSpec((1, tm, D), lambda b, i, f: (b, i, 0)),
                  pl.BlockSpec((1, HALO, D), lambda b, i, f: (b, jnp.maximum(i * hb - 1, 0), 0)),
                  pl.BlockSpec((1, HALO, D), lambda b, i, f: (b, jnp.minimum((i + 1) * hb, last), 0)),
                  pl.BlockSpec((D, tf), lambda b, i, f: (0, f)),
                  pl.BlockSpec((D, tf), lambda b, i, f: (0, f)),
                  pl.BlockSpec((tf, D), lambda b, i, f: (f, 0)),
                  pl.BlockSpec((3, tf), lambda b, i, f: (0, f)),
                  pl.BlockSpec((1, tf), lambda b, i, f: (0, f)),
                  pl.BlockSpec((1, tm, D), lambda b, i, f: (b, i, 0)),
                  pl.BlockSpec((1, 1, D), lambda b, i, f: (b, 0, 0)),
                  pl.BlockSpec((1, D), lambda b, i, f: (0, 0))],
        out_specs=pl.BlockSpec((1, tm, D), lambda b, i, f: (b, i, 0)),
        out_shape=jax.ShapeDtypeStruct((B, SEQ, D), F32),
        scratch_shapes=[pltpu.VMEM((tm, D), F32)],
        compiler_params=_cparams(("parallel", "parallel", "arbitrary")), name="convffn",
    )(h, h, h, wg, wv, wd, cw, cb[None, :], x1, gate[:, None, :], g_post[None, :])


def _rope_partner(w):
    q = QK_ROPE // 4
    return jnp.concatenate([-w[..., q:2 * q], w[..., :q], -w[..., 3 * q:], w[..., 2 * q:3 * q]], axis=-1)


def _pack_w_in_kernel(wt_ref, perm_ref, main_ref, gate_ref):
    latents = RWKV_IN + Q_LORA + KV_LORA
    main_ref[:RWKV_IN] = wt_ref[:RWKV_IN].astype(BF16)
    kpe = wt_ref[latents:latents + QK_ROPE].astype(BF16)
    partner = jnp.dot(perm_ref[...], kpe, preferred_element_type=F32).astype(BF16)
    main_ref[C_KPE:C_KPE + QK_ROPE] = kpe
    main_ref[C_KPE + QK_ROPE:C_KPE + 2 * QK_ROPE] = partner
    main_ref[C_KPE + 2 * QK_ROPE:C_CQ] = jnp.zeros((C_CQ - C_KPE - 2 * QK_ROPE, kpe.shape[1]), BF16)
    main_ref[C_CQ:] = wt_ref[RWKV_IN:latents].astype(BF16)
    gate_ref[...] = wt_ref[RWKV_IN + MLA_IN:].astype(BF16)


def _pack_w_in(wt, tk=256):
    n_in, K = wt.shape
    perm_t = _rope_partner(jnp.eye(QK_ROPE, dtype=F32)).T.astype(BF16)
    n_gate = n_in - RWKV_IN - MLA_IN
    return pl.pallas_call(
        _pack_w_in_kernel, grid=(K // tk,),
        in_specs=[pl.BlockSpec((n_in, tk), lambda i: (0, i)),
                  pl.BlockSpec((QK_ROPE, QK_ROPE), lambda i: (0, 0))],
        out_specs=[pl.BlockSpec((P_COLS, tk), lambda i: (0, i)),
                   pl.BlockSpec((n_gate, tk), lambda i: (0, i))],
        out_shape=[jax.ShapeDtypeStruct((P_COLS, K), BF16), jax.ShapeDtypeStruct((n_gate, K), BF16)],
        compiler_params=_cparams(("parallel",)), name="pack_w_in",
    )(wt, perm_t)


def _lora_window_weight(w2, first_row):
    n_dir, R, C = w2.shape
    out = jnp.zeros((LORA_WIN, n_dir, C), BF16)
    for d in range(n_dir):
        out = out.at[first_row + d * R:first_row + (d + 1) * R, d].set(w2[d].astype(BF16))
    return out.reshape(LORA_WIN, n_dir * C)


def _rope_tables():
    rows = SEQ // GRID_W
    half = QK_ROPE // 2
    freqs = ROPE_THETA ** (-jnp.arange(0, half, 2, dtype=F32) / half)
    ar = jnp.arange(rows, dtype=F32)[:, None] * freqs
    ac = jnp.arange(GRID_W, dtype=F32)[:, None] * freqs
    by_row = lambda t: jnp.repeat(t, GRID_W, axis=0)
    by_col = lambda t: jnp.tile(t, (rows, 1))
    cos = jnp.concatenate([by_row(jnp.cos(ar))] * 2 + [by_col(jnp.cos(ac))] * 2, axis=-1)
    sin = jnp.concatenate([by_row(jnp.sin(ar))] * 2 + [by_col(jnp.sin(ac))] * 2, axis=-1)
    cos_all = jnp.concatenate([cos, jnp.ones((CTX_LEN, QK_ROPE), F32)], axis=0)
    sin_all = jnp.concatenate([sin, jnp.zeros((CTX_LEN, QK_ROPE), F32)], axis=0)
    return cos_all, sin_all


def kernel(x, c, ctx, c_ctx, w_ada, b_ada, norm_mix_pre, norm_mix_post, norm_ffn_pre, norm_ffn_post, w_in, rwkv_mu, rwkv_w0, rwkv_w2, rwkv_a0, rwkv_a2, rwkv_k_k, rwkv_k_a, rwkv_r_k, rwkv_lnx_w, rwkv_lnx_b, rwkv_g2, w_rwkv_proj, mla_q_norm, mla_kv_norm, mla_w_uq, mla_w_ukv, w_mla_proj, gate_b, w_out, ffn_w_gate, ffn_w_val, ffn_conv_w, ffn_conv_b, ffn_w_down):
    B = x.shape[0]
    D = D_MODEL
    C = RWKV_DIM
    l = 0

    s = jnp.concatenate([jax.nn.silu(c), jax.nn.silu(c_ctx)[None, :]], axis=0)
    mods = _ada(s, w_ada[l], b_ada[l])
    lat = mods[:B].reshape(B, 6, D)
    cm = mods[B].reshape(6, D)
    sh1, sc1, g1, sh2, sc2, g2 = [lat[:, j] for j in range(6)]
    mods1 = jnp.stack([jnp.stack([sh1, sc1], axis=1),
                       jnp.broadcast_to(jnp.stack([cm[0], cm[1]])[None], (B, 2, D))], axis=1)
    mods2 = jnp.stack([sh2, sc2], axis=1)

    h = _norm_mod(x, ctx, norm_mix_pre[l], mods1)
    wt_main, wt_gate = _pack_w_in(w_in[l].T)
    p = _matmul_nt(h.reshape(B * S_ALL, D), wt_main, tm=1088, tn=1024, name="w_in").reshape(B, S_ALL, P_COLS)
    gates = _gates(h, wt_gate, gate_b[l])

    lane_head = jnp.arange(LANES) // RWKV_HEAD
    ones2 = (lane_head[:, None] == lane_head[None, :]).astype(BF16)
    avg2 = ones2 * (1.0 / RWKV_HEAD)
    r, k, v, kk, lw, a, g, bg = _prep(p, rwkv_mu[l], rwkv_w0[l], _lora_window_weight(rwkv_w2[l], C_WL - C_WL_WIN),
                                      rwkv_a0[l], _lora_window_weight(rwkv_a2[l], C_AL - C_AL_WIN), rwkv_k_k[l],
                                      rwkv_k_a[l], rwkv_r_k[l].reshape(C), rwkv_g2[l].astype(BF16), ones2)
    y_dir = _wkv(r, k, v, kk, lw, a, rwkv_k_a[l])

    cos, sin = _rope_tables()
    uq = mla_w_uq[l].reshape(Q_LORA, MLA_HEADS, QK_NOPE + QK_ROPE)
    wq = jnp.concatenate([uq, _rope_partner(uq[..., QK_NOPE:])], axis=-1).transpose(1, 0, 2).astype(BF16)
    wkv = mla_w_ukv[l].reshape(KV_LORA, MLA_HEADS, QK_NOPE + V_HEAD).transpose(1, 0, 2).astype(BF16)
    q = _qproj(p, mla_q_norm[l], wq, cos[:SEQ], sin[:SEQ])
    k_all, v_all = _kvproj(p, mla_kv_norm[l], wkv, cos, sin)
    o = _attention(q, k_all, v_all)

    merged = _merge(y_dir, g, bg, rwkv_lnx_w[l], rwkv_lnx_b[l], avg2, o, w_rwkv_proj[l].astype(BF16),
                    w_mla_proj[l].astype(BF16), gates)
    x1, h2 = _outproj(merged, w_out[l].astype(BF16), x, norm_mix_post[l], g1, norm_ffn_pre[l], mods2)
    return _ffn(h2, ffn_w_gate[l].astype(BF16), ffn_w_val[l].astype(BF16), ffn_w_down[l].astype(BF16),
                ffn_conv_w[l], ffn_conv_b[l], x1, g2, norm_ffn_post[l])
```

```python
import functools
import math

import jax
import jax.numpy as jnp
from jax import lax
from jax.experimental import pallas as pl
from jax.experimental.pallas import tpu as pltpu

D_MODEL = 2048
BATCH = 2
SEQ = 4096
GRID_W = 64
CTX_LEN = 256
S_ALL = SEQ + CTX_LEN
EPS = 1e-6
RWKV_HEADS = 16
RWKV_HEAD = 64
RWKV_DIM = RWKV_HEADS * RWKV_HEAD
DECAY_LORA = 96
AAA_LORA = 96
GATE_LORA = 256
N_DIR = 2
DECAY_SCALE = math.exp(-0.5)
LNX_EPS = 64e-5
MLA_HEADS = 16
Q_LORA = 512
KV_LORA = 512
QK_NOPE = 128
QK_ROPE = 64
V_HEAD = 128
ROPE_THETA = 10000.0
ATTN_SCALE = (QK_NOPE + QK_ROPE) ** -0.5
LOG2_E = math.log2(math.e)
D_FF = 5632
RWKV_IN = 3 * RWKV_DIM + N_DIR * DECAY_LORA + N_DIR * AAA_LORA + GATE_LORA
MLA_IN = Q_LORA + KV_LORA + QK_ROPE

LANES = 128
SUBLANES = 8
VMEM_LIMIT = 56 * 1024 * 1024

C_R, C_K, C_V = 0, RWKV_DIM, 2 * RWKV_DIM
C_WL = 3 * RWKV_DIM
C_AL = C_WL + N_DIR * DECAY_LORA
C_GL = C_AL + N_DIR * AAA_LORA
RWKV_COLS = C_GL + GATE_LORA
C_KPE = RWKV_COLS
C_CQ = -(-(C_KPE + 2 * QK_ROPE) // Q_LORA) * Q_LORA
C_CKV = C_CQ + Q_LORA
P_COLS = C_CKV + KV_LORA
LORA_WIN = 2 * LANES
C_WL_WIN = C_WL // LANES * LANES
C_AL_WIN = C_AL // LANES * LANES

WKV_CHUNK = 64
WKV_SUB = 4
HEAD_PAIRS = RWKV_DIM // LANES

F32 = jnp.float32
BF16 = jnp.bfloat16


def _cparams(sem):
    return pltpu.CompilerParams(dimension_semantics=sem, vmem_limit_bytes=VMEM_LIMIT)


def _bdot(a, b):
    return jnp.dot(a.astype(BF16), b.astype(BF16), preferred_element_type=F32)


def _hilo_dot(x, w):
    hi = x.astype(BF16)
    lo = (x - hi.astype(F32)).astype(BF16)
    return jnp.dot(hi, w, preferred_element_type=F32) + jnp.dot(lo, w, preferred_element_type=F32)


def _head_sum(x, ones2):
    return jnp.concatenate([_hilo_dot(x[:, c * LANES:(c + 1) * LANES], ones2) for c in range(HEAD_PAIRS)], axis=-1)


def _ada_kernel(s_ref, w_ref, b_ref, o_ref):
    w = w_ref[...]
    for r in range(s_ref.shape[0]):
        o_ref[r:r + 1, :] = jnp.sum(s_ref[r] * w, axis=0, keepdims=True) + b_ref[...]


def _ada(s, w, b, tn=1024):
    R, K = s.shape
    N = w.shape[1]
    return pl.pallas_call(
        _ada_kernel, grid=(N // tn,),
        in_specs=[pl.BlockSpec((R, K, 1), lambda j: (0, 0, 0)),
                  pl.BlockSpec((K, tn), lambda j: (0, j)),
                  pl.BlockSpec((1, tn), lambda j: (0, j))],
        out_specs=pl.BlockSpec((R, tn), lambda j: (0, j)),
        out_shape=jax.ShapeDtypeStruct((R, N), F32),
        compiler_params=_cparams(("parallel",)), name="ada",
    )(s[:, :, None], w, b[None, :])


def _norm_mod_kernel(x_ref, c_ref, g_ref, m_ref, o_ref, *, lat_tiles):
    def emit(x):
        n = x * lax.rsqrt(jnp.mean(x * x, axis=-1, keepdims=True) + EPS) * g_ref[...]
        m = m_ref[0, 0]
        o_ref[0] = (n * (1.0 + m[1:2]) + m[0:1]).astype(o_ref.dtype)

    is_latent = pl.program_id(1) < lat_tiles

    @pl.when(is_latent)
    def _():
        emit(x_ref[0])

    @pl.when(jnp.logical_not(is_latent))
    def _():
        emit(c_ref[0])


def _norm_mod(x, ctx, gain, mods, tm=256):
    B, _, D = x.shape
    assert ctx.shape[1] == tm
    lat_tiles = SEQ // tm
    return pl.pallas_call(
        functools.partial(_norm_mod_kernel, lat_tiles=lat_tiles), grid=(B, S_ALL // tm),
        in_specs=[pl.BlockSpec((1, tm, D), lambda b, i: (b, jnp.minimum(i, lat_tiles - 1), 0)),
                  pl.BlockSpec((1, tm, D), lambda b, i: (b, 0, 0)),
                  pl.BlockSpec((1, D), lambda b, i: (0, 0)),
                  pl.BlockSpec((1, 1, 2, D), lambda b, i: (b, i // lat_tiles, 0, 0))],
        out_specs=pl.BlockSpec((1, tm, D), lambda b, i: (b, i, 0)),
        out_shape=jax.ShapeDtypeStruct((B, S_ALL, D), BF16),
        compiler_params=_cparams(("parallel", "parallel")), name="norm_mod",
    )(x, ctx, gain[None, :], mods)


def _dot_nt(a, wt):
    return lax.dot_general(a, wt, (((1,), (1,)), ((), ())), preferred_element_type=F32)


def _mm_kernel(a_ref, wt_ref, o_ref):
    o_ref[...] = _dot_nt(a_ref[...], wt_ref[...]).astype(o_ref.dtype)


def _matmul_nt(a, wt, tm, tn, out_dtype=F32, name="matmul"):
    M, K = a.shape
    N = wt.shape[0]
    return pl.pallas_call(
        _mm_kernel, grid=(M // tm, N // tn),
        in_specs=[pl.BlockSpec((tm, K), lambda i, j: (i, 0)),
                  pl.BlockSpec((tn, K), lambda i, j: (j, 0))],
        out_specs=pl.BlockSpec((tm, tn), lambda i, j: (i, j)),
        out_shape=jax.ShapeDtypeStruct((M, N), out_dtype),
        compiler_params=_cparams(("parallel", "parallel")), name=name,
    )(a, wt)


def _prep_kernel(p_ref, pp_ref, pn_ref, mu_ref, w0_ref, w2_ref, a0_ref, a2_ref, kk_ref, ka_ref, rk_ref, g2_ref,
                 ones_ref, r_o, k_o, v_o, kkn_o, lw_o, a_o, g_o, bg_o):
    i = pl.program_id(1)
    x = p_ref[0]
    tm = x.shape[0]
    lat_tiles = SEQ // tm
    first = jnp.logical_or(i == 0, i == lat_tiles)
    last = jnp.logical_or(i == lat_tiles - 1, i == pl.num_programs(1) - 1)
    x_before = jnp.where(first, 0.0, pp_ref[0][SUBLANES - 1:SUBLANES])
    x_after = jnp.where(last, 0.0, pn_ref[0][0:1])
    rows = lax.broadcasted_iota(jnp.int32, (tm, 1), 0)
    prev = jnp.where(rows == 0, x_before, pltpu.roll(x, 1, 0))
    nxt = jnp.where(rows == tm - 1, x_after, pltpu.roll(x, tm - 1, 0))
    xs = x + mu_ref[...] * (0.5 * (prev + nxt) - x)

    C = RWKV_DIM
    r, k, v = xs[:, C_R:C_R + C], xs[:, C_K:C_K + C], xs[:, C_V:C_V + C]
    w_lora = _bdot(jnp.tanh(xs[:, C_WL_WIN:C_WL_WIN + LORA_WIN]), w2_ref[...])
    a_lora = _bdot(xs[:, C_AL_WIN:C_AL_WIN + LORA_WIN], a2_ref[...])
    a_sum = None
    for d in range(N_DIR):
        w_raw = w0_ref[d:d + 1] + w_lora[:, d * C:(d + 1) * C]
        lw_o[0, d] = -DECAY_SCALE * jax.nn.sigmoid(w_raw)
        a_d = jax.nn.sigmoid(a0_ref[d:d + 1] + a_lora[:, d * C:(d + 1) * C])
        a_o[0, d] = a_d
        a_sum = a_d if a_sum is None else a_sum + a_d
    ones2 = ones_ref[...]
    kkx = k * kk_ref[...]
    kkn_o[0] = kkx / jnp.maximum(jnp.sqrt(_head_sum(kkx * kkx, ones2)), 1e-12)
    g = _bdot(jax.nn.sigmoid(xs[:, C_GL:C_GL + GATE_LORA]), g2_ref[...])
    k_bar = k * (1.0 + (0.5 * a_sum - 1.0) * ka_ref[...])
    bonus = _head_sum(r * k_bar * rk_ref[...], ones2) * v
    r_o[0] = r
    k_o[0] = k
    v_o[0] = v
    g_o[0] = g.astype(g_o.dtype)
    bg_o[0] = (bonus * g).astype(bg_o.dtype)


def _prep(p, mu, w0, w2p, a0, a2p, k_k, k_a, r_k, g2, ones2, tm=256):
    B = p.shape[0]
    C = RWKV_DIM
    hb = tm // SUBLANES
    last = S_ALL // SUBLANES - 1
    row = lambda n: pl.BlockSpec((1, n), lambda b, i: (0, 0))
    tok = pl.BlockSpec((1, tm, C), lambda b, i: (b, i, 0))
    tok_dir = pl.BlockSpec((1, N_DIR, tm, C), lambda b, i: (b, 0, i, 0))
    shp = jax.ShapeDtypeStruct((B, S_ALL, C), F32)
    shp_dir = jax.ShapeDtypeStruct((B, N_DIR, S_ALL, C), F32)
    shp_b = jax.ShapeDtypeStruct((B, S_ALL, C), BF16)
    return pl.pallas_call(
        _prep_kernel, grid=(B, S_ALL // tm),
        in_specs=[pl.BlockSpec((1, tm, RWKV_COLS), lambda b, i: (b, i, 0)),
                  pl.BlockSpec((1, SUBLANES, RWKV_COLS), lambda b, i: (b, jnp.maximum(i * hb - 1, 0), 0)),
                  pl.BlockSpec((1, SUBLANES, RWKV_COLS), lambda b, i: (b, jnp.minimum((i + 1) * hb, last), 0)),
                  row(RWKV_COLS),
                  pl.BlockSpec((N_DIR, C), lambda b, i: (0, 0)),
                  pl.BlockSpec((LORA_WIN, N_DIR * C), lambda b, i: (0, 0)),
                  pl.BlockSpec((N_DIR, C), lambda b, i: (0, 0)),
                  pl.BlockSpec((LORA_WIN, N_DIR * C), lambda b, i: (0, 0)),
                  row(C), row(C), row(C),
                  pl.BlockSpec((GATE_LORA, C), lambda b, i: (0, 0)),
                  pl.BlockSpec((LANES, LANES), lambda b, i: (0, 0))],
        out_specs=[tok, tok, tok, tok, tok_dir, tok_dir, tok, tok],
        out_shape=[shp, shp, shp, shp, shp_dir, shp_dir, shp_b, shp_b],
        compiler_params=_cparams(("parallel", "parallel")), name="rwkv_prep",
    )(p, p, p, mu[None, :], w0, w2p, a0, a2p, k_k[None, :], k_a[None, :], r_k[None, :], g2, ones2)


def _wkv_kernel(r_ref, k_ref, v_ref, kk_ref, lw_ref, a_ref, ka_ref, y_ref, s_ref):
    d = pl.program_id(1)
    C = WKV_CHUNK
    C2 = 2 * C
    P = HEAD_PAIRS
    sign = 1 - 2 * d
    fwd = d == 0

    @pl.when(pl.program_id(2) == 0)
    def _():
        s_ref[...] = jnp.zeros_like(s_ref)

    def chunk_tri(n):
        trow = lax.broadcasted_iota(jnp.int32, (n * C, n * C), 0)
        tcol = lax.broadcasted_iota(jnp.int32, (n * C, n * C), 1)
        return jnp.logical_and((trow // C) == (tcol // C), (trow - tcol) * sign >= 0).astype(BF16)

    lane = lax.broadcasted_iota(jnp.int32, (C, LANES), 1)
    head0 = lane < RWKV_HEAD
    zero = jnp.zeros((), F32)

    def expand(x):
        x = x.astype(BF16)
        zb = jnp.zeros((), BF16)
        tiles = [x[c:c + C, j * LANES:(j + 1) * LANES] for c in range(0, x.shape[0], C) for j in range(P)]
        return jnp.stack([jnp.concatenate([jnp.where(head0, t, zb), jnp.where(head0, zb, t)], axis=0)
                          for t in tiles])

    def bmm(x, y):
        return jnp.einsum('hab,hbc->hac', x.astype(BF16), y.astype(BF16), preferred_element_type=F32)

    def bmm_nt(x, y):
        return jnp.einsum('han,hbn->hab', x.astype(BF16), y.astype(BF16), preferred_element_type=F32)

    def bmm_tn(x, y):
        return jnp.einsum('hca,hcb->hab', x.astype(BF16), y.astype(BF16), preferred_element_type=F32)

    row = lax.broadcasted_iota(jnp.int32, (C2, C2), 0)
    col = lax.broadcasted_iota(jnp.int32, (C2, C2), 1)
    same = (row // C) == (col // C)
    lag = (row - col) * sign
    incl = jnp.logical_and(same, lag >= 0)[None]
    strict = jnp.logical_and(same, lag > 0)[None]

    def blk(m):
        return (row // m) == (col // m)

    def half_rows(t, m, second):
        return jnp.concatenate([t[:, g + second * m:g + (second + 1) * m] for g in range(0, C2, 2 * m)], axis=1)

    def state_free(chunk_rows):
        def load(ref, *lead):
            return jnp.concatenate([ref[(*lead, rows, slice(None))] for rows in chunk_rows], axis=0)

        r = load(r_ref, 0)
        v = load(v_ref, 0)
        kk = load(kk_ref, 0)
        lw = load(lw_ref, 0, 0)
        a = load(a_ref, 0, 0)
        k = load(k_ref, 0) * (1.0 + (a - 1.0) * ka_ref[...])
        n = len(chunk_rows)

        cum = _hilo_dot_left(chunk_tri(n), lw)
        tot = jnp.concatenate(
            [jnp.broadcast_to(jnp.where(fwd, cum[(c + 1) * C - 1:(c + 1) * C], cum[c * C:c * C + 1]), (C, cum.shape[1]))
             for c in range(n)], axis=0)
        e_cum = jnp.exp(cum)
        e_neg = jnp.exp(-cum)
        e_rem = jnp.exp(tot - cum)
        b = kk * a
        at = -kk * jnp.exp(cum - lw)
        dec = jnp.exp(tot)
        AT, RT, BT, KT, BD, KD, V = (expand(t) for t in (at, r * e_cum, b * e_neg, k * e_neg, b * e_rem,
                                                         k * e_rem, v))
        AR = jnp.concatenate([AT, RT], axis=1)
        G = bmm_nt(AR, jnp.concatenate([BT, KT], axis=1))
        a_ab, a_ak = G[:, :C2, :C2], G[:, :C2, C2:]
        a_rb, a_rk = G[:, C2:, :C2], G[:, C2:, C2:]

        L = jnp.where(strict, a_ab, zero)
        Lb = jnp.where(blk(8)[None], L, zero)
        L2 = bmm(Lb, Lb)
        L4 = bmm(L2, L2)
        S1 = Lb + L2 + bmm(Lb, L2)
        Nm = S1 + L4 + bmm(L4, S1)
        for m in (8, 16, 32):
            Lo = jnp.where(jnp.logical_and(blk(2 * m), jnp.logical_not(blk(m)))[None], L, zero)
            n_first, n_second = half_rows(Nm, m, 0), half_rows(Nm, m, 1)
            n_late = jnp.where(fwd, n_second, n_first)
            p_late = jnp.where(fwd, half_rows(Lo, m, 1), half_rows(Lo, m, 0)) + bmm(n_late, Lo)
            n_late = n_late + p_late + bmm(p_late, Nm)
            n_first, n_second = jnp.where(fwd, n_first, n_late), jnp.where(fwd, n_late, n_second)
            Nm = jnp.concatenate([piece for g in range(C2 // (2 * m))
                                  for piece in (n_first[:, g * m:(g + 1) * m], n_second[:, g * m:(g + 1) * m])],
                                 axis=1)
        rhs_v = bmm(jnp.where(strict, a_ak, zero), V)
        a_y = jnp.concatenate([jnp.where(incl, a_rb, zero), jnp.where(incl, a_rk, zero)], axis=2).astype(BF16)
        dec3 = jnp.stack([dec[c * C:c * C + 1, j * LANES:(j + 1) * LANES]
                          for c in range(n) for j in range(P)])
        whole = (AR, rhs_v, Nm.astype(BF16), V, a_y, jnp.concatenate([BD, KD], axis=1), dec3)
        return [tuple(t[c * P:(c + 1) * P] for t in whole) for c in range(n)]

    def state_step(part, s0):
        AR, rhs_v, Nm, V, a_y, BKD, dec3 = part
        ars0 = bmm_nt(AR, s0)
        rhs = ars0[:, :C2] + rhs_v
        u = rhs + bmm(Nm, rhs)
        uv = jnp.concatenate([u.astype(BF16), V], axis=1)
        y2 = ars0[:, C2:] + bmm(a_y, uv)
        y = y2[:, :C] + y2[:, C:]
        return jnp.concatenate([y[j] for j in range(P)], axis=-1), s0 * dec3 + bmm_tn(uv, BKD)

    chunk_rows = [pl.ds(pl.multiple_of(jnp.where(fwd, j, WKV_SUB - 1 - j) * C, C), C) for j in range(WKV_SUB)]
    half = WKV_SUB // 2
    s = s_ref[...]
    for group in (chunk_rows[:half], chunk_rows[half:]):
        for rows, part in zip(group, state_free(group)):
            y, s = state_step(part, s)
            y_ref[0, 0, rows, :] = y
    s_ref[...] = s


def _hilo_dot_left(w, x):
    hi = x.astype(BF16)
    lo = (x - hi.astype(F32)).astype(BF16)
    return jnp.dot(w, hi, preferred_element_type=F32) + jnp.dot(w, lo, preferred_element_type=F32)


def _wkv(r, k, v, kk, lw, a, k_a):
    B, S, Cd = r.shape
    R = WKV_SUB * WKV_CHUNK
    n_blocks = S // R
    lat_blocks = SEQ // R

    def bidx(d, i):
        return jnp.where(d == 0, (i + lat_blocks) % n_blocks, n_blocks - 1 - i)

    shared = pl.BlockSpec((1, R, Cd), lambda b, d, i: (b, bidx(d, i), 0))
    perdir = pl.BlockSpec((1, 1, R, Cd), lambda b, d, i: (b, d, bidx(d, i), 0))
    return pl.pallas_call(
        _wkv_kernel, grid=(B, N_DIR, n_blocks),
        in_specs=[shared, shared, shared, shared, perdir, perdir, pl.BlockSpec((1, Cd), lambda b, d, i: (0, 0))],
        out_specs=perdir,
        out_shape=jax.ShapeDtypeStruct((B, N_DIR, S, Cd), F32),
        scratch_shapes=[pltpu.VMEM((HEAD_PAIRS, LANES, LANES), F32)],
        compiler_params=_cparams(("parallel", "parallel", "arbitrary")), name="wkv7",
    )(r, k, v, kk, lw, a, k_a[None, :])


def _rms(x, g):
    return x * lax.rsqrt(jnp.mean(x * x, axis=-1, keepdims=True) + EPS) * g


def _qproj_kernel(p_ref, g_ref, w_ref, cos_ref, sin_ref, q_ref):
    cq = _rms(p_ref[0], g_ref[...]).astype(BF16)
    cs, sn = cos_ref[...], sin_ref[...]
    for h in range(MLA_HEADS):
        y = jnp.dot(cq, w_ref[h], preferred_element_type=F32)
        rope = y[:, QK_NOPE:QK_NOPE + QK_ROPE] * cs + y[:, QK_NOPE + QK_ROPE:] * sn
        q = jnp.concatenate([y[:, :QK_NOPE], rope], axis=-1) * (ATTN_SCALE * LOG2_E)
        q_ref[0, h] = q.astype(q_ref.dtype)


def _qproj(p, gain, wq, cos, sin, tm=512):
    B = p.shape[0]
    return pl.pallas_call(
        _qproj_kernel, grid=(B, SEQ // tm),
        in_specs=[pl.BlockSpec((1, tm, Q_LORA), lambda b, i: (b, i, C_CQ // Q_LORA)),
                  pl.BlockSpec((1, Q_LORA), lambda b, i: (0, 0)),
                  pl.BlockSpec((MLA_HEADS, Q_LORA, 2 * LANES), lambda b, i: (0, 0, 0)),
                  pl.BlockSpec((tm, QK_ROPE), lambda b, i: (i, 0)),
                  pl.BlockSpec((tm, QK_ROPE), lambda b, i: (i, 0))],
        out_specs=pl.BlockSpec((1, MLA_HEADS, tm, QK_NOPE + QK_ROPE), lambda b, i: (b, 0, i, 0)),
        out_shape=jax.ShapeDtypeStruct((B, MLA_HEADS, SEQ, QK_NOPE + QK_ROPE), BF16),
        compiler_params=_cparams(("parallel", "parallel")), name="mla_q",
    )(p, gain[None, :], wq, cos, sin)


def _kvproj_kernel(p_ref, pe_ref, g_ref, w_ref, cos_ref, sin_ref, k_ref, v_ref):
    ckv = _rms(p_ref[0], g_ref[...]).astype(BF16)
    pe = pe_ref[0]
    kpe = pe[:, :QK_ROPE] * cos_ref[...] + pe[:, QK_ROPE:] * sin_ref[...]
    ones = jnp.ones((ckv.shape[0], V_HEAD), F32)
    for h in range(MLA_HEADS):
        y = jnp.dot(ckv, w_ref[h], preferred_element_type=F32)
        k_ref[0, h] = jnp.concatenate([y[:, :QK_NOPE], kpe], axis=-1).astype(k_ref.dtype)
        v_ref[0, h] = jnp.concatenate([y[:, QK_NOPE:], ones], axis=-1).astype(v_ref.dtype)


def _kvproj(p, gain, wkv, cos, sin, tm=256):
    B = p.shape[0]
    return pl.pallas_call(
        _kvproj_kernel, grid=(B, S_ALL // tm),
        in_specs=[pl.BlockSpec((1, tm, KV_LORA), lambda b, i: (b, i, C_CKV // KV_LORA)),
                  pl.BlockSpec((1, tm, 2 * QK_ROPE), lambda b, i: (b, i, C_KPE // (2 * QK_ROPE))),
                  pl.BlockSpec((1, KV_LORA), lambda b, i: (0, 0)),
                  pl.BlockSpec((MLA_HEADS, KV_LORA, 2 * LANES), lambda b, i: (0, 0, 0)),
                  pl.BlockSpec((tm, QK_ROPE), lambda b, i: (i, 0)),
                  pl.BlockSpec((tm, QK_ROPE), lambda b, i: (i, 0))],
        out_specs=[pl.BlockSpec((1, MLA_HEADS, tm, QK_NOPE + QK_ROPE), lambda b, i: (b, 0, i, 0)),
                   pl.BlockSpec((1, MLA_HEADS, tm, 2 * V_HEAD), lambda b, i: (b, 0, i, 0))],
        out_shape=[jax.ShapeDtypeStruct((B, MLA_HEADS, S_ALL, QK_NOPE + QK_ROPE), BF16),
                   jax.ShapeDtypeStruct((B, MLA_HEADS, S_ALL, 2 * V_HEAD), BF16)],
        compiler_params=_cparams(("parallel", "parallel")), name="mla_kv",
    )(p, p, gain[None, :], wkv, cos, sin)


def _attn_kernel(*refs, tk, n_cast):
    q_ref, k_ref, v_ref = refs[:3]
    cast_in = refs[3:3 + n_cast]
    o_ref = refs[3 + n_cast]
    cast_out = refs[4 + n_cast:4 + 2 * n_cast]
    sa, sb, pa, pb, acc, m_ref, al_ref = refs[4 + 2 * n_cast:]
    for w_ref, wb_ref in zip(cast_in, cast_out):
        wb_ref[...] = w_ref[...].astype(wb_ref.dtype)

    q = q_ref[0, 0]
    n_keys = k_ref.shape[2]
    blocks = [(off, min(tk, n_keys - off)) for off in range(0, n_keys, tk)]
    n_kv = len(blocks)

    def put_scores(s_ref, j):
        off, w = blocks[j]
        kj = k_ref[0, 0, off:off + w, :]
        s_ref[:, :w] = lax.dot_general(q, kj, (((1,), (1,)), ((), ())), preferred_element_type=F32)

    def accumulate(p_ref, j):
        off, w = blocks[j]
        pv = jnp.dot(p_ref[:, :w], v_ref[0, 0, off:off + w, :], preferred_element_type=F32)
        al = al_ref[...]
        for c in range(0, 2 * V_HEAD, LANES):
            acc[:, c:c + LANES] = al * acc[:, c:c + LANES] + pv[:, c:c + LANES]

    def softmax(s_ref, p_ref, j):
        w = blocks[j][1]
        s = s_ref[:, :w]
        m_old = m_ref[...]
        m_new = jnp.maximum(m_old, jnp.max(s, axis=-1, keepdims=True))
        al_ref[...] = jnp.exp2(m_old - m_new)
        m_ref[...] = m_new
        for c in range(0, w, LANES):
            p_ref[:, c:c + LANES] = jnp.exp2(s[:, c:c + LANES] - m_new).astype(p_ref.dtype)

    s_bufs, p_bufs = (sa, sb), (pa, pb)
    put_scores(sa, 0)
    acc[...] = jnp.zeros_like(acc)
    m_ref[...] = jnp.full_like(m_ref, -jnp.inf)
    for j in range(n_kv):
        if j + 1 < n_kv:
            put_scores(s_bufs[(j + 1) % 2], j + 1)
        if j >= 1:
            accumulate(p_bufs[(j - 1) % 2], j - 1)
        softmax(s_bufs[j % 2], p_bufs[j % 2], j)
    accumulate(p_bufs[(n_kv - 1) % 2], n_kv - 1)
    o_ref[0] = (acc[:, :V_HEAD] / acc[:, V_HEAD:]).astype(o_ref.dtype)


BF16_SUBLANES = 16


def _attention(q, k, v, to_cast, tq=1024, tk=1024):
    B, H, T, Dk = q.shape
    S = k.shape[2]
    nt = T // tq
    n_steps = B * H * nt

    def cast_spec(w):
        rows, cols = w.shape
        rb = next(r for r in range(BF16_SUBLANES, rows + 1, BF16_SUBLANES)
                  if rows % r == 0 and rows // r <= n_steps)
        n_blk = rows // rb
        return pl.BlockSpec((rb, cols), lambda b, h, i: (((b * H + h) * nt + i) * n_blk // n_steps, 0))

    cast_specs = [cast_spec(w) for w in to_cast]
    outs = pl.pallas_call(
        functools.partial(_attn_kernel, tk=tk, n_cast=len(to_cast)), grid=(B, H, nt),
        in_specs=[pl.BlockSpec((1, 1, tq, Dk), lambda b, h, i: (b, h, i, 0)),
                  pl.BlockSpec((1, 1, S, Dk), lambda b, h, i: (b, h, 0, 0)),
                  pl.BlockSpec((1, 1, S, 2 * V_HEAD), lambda b, h, i: (b, h, 0, 0))] + cast_specs,
        out_specs=[pl.BlockSpec((1, tq, V_HEAD), lambda b, h, i: (b, i, h))] + cast_specs,
        out_shape=[jax.ShapeDtypeStruct((B, T, H * V_HEAD), BF16)]
                  + [jax.ShapeDtypeStruct(w.shape, BF16) for w in to_cast],
        scratch_shapes=[pltpu.VMEM((tq, tk), F32), pltpu.VMEM((tq, tk), F32),
                        pltpu.VMEM((tq, tk), BF16), pltpu.VMEM((tq, tk), BF16),
                        pltpu.VMEM((tq, 2 * V_HEAD), F32), pltpu.VMEM((tq, LANES), F32), pltpu.VMEM((tq, LANES), F32)],
        compiler_params=_cparams(("arbitrary", "arbitrary", "arbitrary")), name="mla_attn",
    )(q, k, v, *to_cast)
    return outs[0], outs[1:]


def _gates_kernel(h_ref, wt_ref, b_ref, o_ref):
    o_ref[0] = jax.nn.sigmoid(_dot_nt(h_ref[0], wt_ref[...]) + b_ref[...]).astype(o_ref.dtype)


def _gates(h, wt_gate, gate_b, tm=1024, tn=1024):
    B, _, D = h.shape
    N = wt_gate.shape[0]
    return pl.pallas_call(
        _gates_kernel, grid=(B, SEQ // tm, N // tn),
        in_specs=[pl.BlockSpec((1, tm, D), lambda b, i, j: (b, i, 0)),
                  pl.BlockSpec((tn, D), lambda b, i, j: (j, 0)),
                  pl.BlockSpec((1, tn), lambda b, i, j: (0, j))],
        out_specs=pl.BlockSpec((1, tm, tn), lambda b, i, j: (b, i, j)),
        out_shape=jax.ShapeDtypeStruct((B, SEQ, N), BF16),
        compiler_params=_cparams(("parallel", "parallel", "parallel")), name="gates",
    )(h, wt_gate, gate_b[None, :])


MERGE_TN = 512


def _merge_kernel(y0_ref, y1_ref, g_ref, bg_ref, lnw_ref, lnb_ref, avg_ref, o_ref, wr_ref, wm_ref, gate_ref, out_ref):
    avg = avg_ref[...]
    y = y0_ref[0, 0] + y1_ref[0, 0]
    yc = y - _head_sum(y, avg)
    yn = yc * lax.rsqrt(_head_sum(yc * yc, avg) + LNX_EPS) * lnw_ref[...] + lnb_ref[...]
    yr = (yn * g_ref[0].astype(F32) + bg_ref[0].astype(F32)).astype(BF16)
    o = o_ref[0]
    D = D_MODEL
    for n in range(0, D, MERGE_TN):
        gr = gate_ref[0, :, n:n + MERGE_TN].astype(F32)
        gm = gate_ref[0, :, D + n:D + n + MERGE_TN].astype(F32)
        out_ref[0, :, n:n + MERGE_TN] = (
            gr * jnp.dot(yr, wr_ref[:, n:n + MERGE_TN], preferred_element_type=F32)
            + gm * jnp.dot(o, wm_ref[:, n:n + MERGE_TN], preferred_element_type=F32)).astype(out_ref.dtype)


def _merge(y_dir, g, bg, lnx_w, lnx_b, avg2, o, w_rp, w_mp, gates, tm=512):
    B = o.shape[0]
    D = D_MODEL
    C = RWKV_DIM
    tok = pl.BlockSpec((1, tm, C), lambda b, i: (b, i, 0))
    row = lambda n: pl.BlockSpec((1, n), lambda b, i: (0, 0))
    return pl.pallas_call(
        _merge_kernel, grid=(B, SEQ // tm),
        in_specs=[pl.BlockSpec((1, 1, tm, C), lambda b, i: (b, 0, i, 0)),
                  pl.BlockSpec((1, 1, tm, C), lambda b, i: (b, 1, i, 0)),
                  tok, tok, row(C), row(C),
                  pl.BlockSpec((LANES, LANES), lambda b, i: (0, 0)),
                  pl.BlockSpec((1, tm, D), lambda b, i: (b, i, 0)),
                  pl.BlockSpec((C, D), lambda b, i: (0, 0), pipeline_mode=pl.Buffered(1)),
                  pl.BlockSpec((D, D), lambda b, i: (0, 0), pipeline_mode=pl.Buffered(1)),
                  pl.BlockSpec((1, tm, 2 * D), lambda b, i: (b, i, 0))],
        out_specs=pl.BlockSpec((1, tm, D), lambda b, i: (b, i, 0)),
        out_shape=jax.ShapeDtypeStruct((B, SEQ, D), BF16),
        compiler_params=_cparams(("parallel", "parallel")), name="merge",
    )(y_dir, y_dir, g, bg, lnx_w[None, :], lnx_b[None, :], avg2, o, w_rp, w_mp, gates)


def _outproj_kernel(m_ref, w_ref, x_ref, gpost_ref, gate_ref, gpre_ref, mod_ref, x1_ref, h_ref):
    out = jnp.dot(m_ref[0], w_ref[...], preferred_element_type=F32)
    x1 = x_ref[0] + gate_ref[0] * _rms(out, gpost_ref[...])
    x1_ref[0] = x1
    mod = mod_ref[0]
    h_ref[0] = (_rms(x1, gpre_ref[...]) * (1.0 + mod[1:2]) + mod[0:1]).astype(h_ref.dtype)


def _outproj(merged, w_out, x, g_post, gate, g_pre, mod2, tm=512):
    B = x.shape[0]
    D = D_MODEL
    return pl.pallas_call(
        _outproj_kernel, grid=(B, SEQ // tm),
        in_specs=[pl.BlockSpec((1, tm, D), lambda b, i: (b, i, 0)),
                  pl.BlockSpec((D, D), lambda b, i: (0, 0)),
                  pl.BlockSpec((1, tm, D), lambda b, i: (b, i, 0)),
                  pl.BlockSpec((1, D), lambda b, i: (0, 0)),
                  pl.BlockSpec((1, 1, D), lambda b, i: (b, 0, 0)),
                  pl.BlockSpec((1, D), lambda b, i: (0, 0)),
                  pl.BlockSpec((1, 2, D), lambda b, i: (b, 0, 0))],
        out_specs=[pl.BlockSpec((1, tm, D), lambda b, i: (b, i, 0)),
                   pl.BlockSpec((1, tm, D), lambda b, i: (b, i, 0))],
        out_shape=[jax.ShapeDtypeStruct((B, SEQ, D), F32),
                   jax.ShapeDtypeStruct((B, SEQ, D), BF16)],
        compiler_params=_cparams(("parallel", "parallel")), name="outproj",
    )(merged, w_out, x, g_post[None, :], gate[:, None, :], g_pre[None, :], mod2)


HALO = 16


def _ffn_kernel(h_ref, hp_ref, hn_ref, wg_ref, wv_ref, wd_ref, cw_ref, cb_ref, x_ref, gate_ref, gpost_ref,
                o_ref, acc_ref):
    i = pl.program_id(1)
    f = pl.program_id(2)
    tm = h_ref.shape[1]

    @pl.when(f == 0)
    def _():
        acc_ref[...] = jnp.zeros_like(acc_ref)

    h = h_ref[0]
    wg = wg_ref[...]
    g_ext = jnp.dot(jnp.concatenate([hp_ref[0], h, hn_ref[0]], axis=0), wg, preferred_element_type=F32)
    g = g_ext[HALO:HALO + tm]
    g_prev = g_ext[HALO - 1:HALO]
    g_next = g_ext[HALO + tm:HALO + tm + 1]
    g_prev = jnp.where(i == 0, 0.0, g_prev)
    g_next = jnp.where(i == pl.num_programs(1) - 1, 0.0, g_next)
    rows = lax.broadcasted_iota(jnp.int32, g.shape, 0)
    up = jnp.where(rows == 0, g_prev, pltpu.roll(g, 1, 0))
    dn = jnp.where(rows == tm - 1, g_next, pltpu.roll(g, tm - 1, 0))
    cw = cw_ref[...]
    u = cb_ref[...] + up * cw[0:1] + g * cw[1:2] + dn * cw[2:3]
    val = jnp.dot(h, wv_ref[...], preferred_element_type=F32)
    act = jax.nn.gelu(u, approximate=True) * val
    acc_ref[...] += jnp.dot(act.astype(BF16), wd_ref[...], preferred_element_type=F32)

    @pl.when(f == pl.num_programs(2) - 1)
    def _():
        o_ref[0] = x_ref[0] + gate_ref[0] * _rms(acc_ref[...], gpost_ref[...])


def _ffn(h, wg, wv, wd, cw, cb, x1, gate, g_post, tm=512, tf=512):
    B = h.shape[0]
    D = D_MODEL
    hb = tm // HALO
    last = SEQ // HALO - 1
    return pl.pallas_call(
        _ffn_kernel, grid=(B, SEQ // tm, D_FF // tf),
        in_specs=[pl.BlockSpec((1, tm, D), lambda b, i, f: (b, i, 0)),
                  pl.BlockSpec((1, HALO, D), lambda b, i, f: (b, jnp.maximum(i * hb - 1, 0), 0)),
                  pl.BlockSpec((1, HALO, D), lambda b, i, f: (b, jnp.minimum((i + 1) * hb, last), 0)),
                  pl.BlockSpec((D, tf), lambda b, i, f: (0, f)),
                  pl.BlockSpec((D, tf), lambda b, i, f: (0, f)),
                  pl.BlockSpec((tf, D), lambda b, i, f: (f, 0)),
                  pl.BlockSpec((3, tf), lambda b, i, f: (0, f)),
                  pl.BlockSpec((1, tf), lambda b, i, f: (0, f)),
                  pl.BlockSpec((1, tm, D), lambda b, i, f: (b, i, 0)),
                  pl.BlockSpec((1, 1, D), lambda b, i, f: (b, 0, 0)),
                  pl.BlockSpec((1, D), lambda b, i, f: (0, 0))],
        out_specs=pl.BlockSpec((1, tm, D), lambda b, i, f: (b, i, 0)),
        out_shape=jax.ShapeDtypeStruct((B, SEQ, D), F32),
        scratch_shapes=[pltpu.VMEM((tm, D), F32)],
        compiler_params=_cparams(("parallel", "parallel", "arbitrary")), name="convffn",
    )(h, h, h, wg, wv, wd, cw, cb[None, :], x1, gate[:, None, :], g_post[None, :])


def _rope_partner(w):
    q = QK_ROPE // 4
    return jnp.concatenate([-w[..., q:2 * q], w[..., :q], -w[..., 3 * q:], w[..., 2 * q:3 * q]], axis=-1)


def _pack_w_in_kernel(wt_ref, perm_ref, main_ref, gate_ref):
    latents = RWKV_IN + Q_LORA + KV_LORA
    main_ref[:RWKV_IN] = wt_ref[:RWKV_IN].astype(BF16)
    kpe = wt_ref[latents:latents + QK_ROPE].astype(BF16)
    partner = jnp.dot(perm_ref[...], kpe, preferred_element_type=F32).astype(BF16)
    main_ref[C_KPE:C_KPE + QK_ROPE] = kpe
    main_ref[C_KPE + QK_ROPE:C_KPE + 2 * QK_ROPE] = partner
    main_ref[C_KPE + 2 * QK_ROPE:C_CQ] = jnp.zeros((C_CQ - C_KPE - 2 * QK_ROPE, kpe.shape[1]), BF16)
    main_ref[C_CQ:] = wt_ref[RWKV_IN:latents].astype(BF16)
    gate_ref[...] = wt_ref[RWKV_IN + MLA_IN:].astype(BF16)


def _pack_w_in(wt, tk=256):
    n_in, K = wt.shape
    perm_t = _rope_partner(jnp.eye(QK_ROPE, dtype=F32)).T.astype(BF16)
    n_gate = n_in - RWKV_IN - MLA_IN
    return pl.pallas_call(
        _pack_w_in_kernel, grid=(K // tk,),
        in_specs=[pl.BlockSpec((n_in, tk), lambda i: (0, i)),
                  pl.BlockSpec((QK_ROPE, QK_ROPE), lambda i: (0, 0))],
        out_specs=[pl.BlockSpec((P_COLS, tk), lambda i: (0, i)),
                   pl.BlockSpec((n_gate, tk), lambda i: (0, i))],
        out_shape=[jax.ShapeDtypeStruct((P_COLS, K), BF16), jax.ShapeDtypeStruct((n_gate, K), BF16)],
        compiler_params=_cparams(("parallel",)), name="pack_w_in",
    )(wt, perm_t)


def _lora_window_weight(w2, first_row):
    n_dir, R, C = w2.shape
    out = jnp.zeros((LORA_WIN, n_dir, C), BF16)
    for d in range(n_dir):
        out = out.at[first_row + d * R:first_row + (d + 1) * R, d].set(w2[d].astype(BF16))
    return out.reshape(LORA_WIN, n_dir * C)


def _rope_tables():
    rows = SEQ // GRID_W
    half = QK_ROPE // 2
    freqs = ROPE_THETA ** (-jnp.arange(0, half, 2, dtype=F32) / half)
    ar = jnp.arange(rows, dtype=F32)[:, None] * freqs
    ac = jnp.arange(GRID_W, dtype=F32)[:, None] * freqs
    by_row = lambda t: jnp.repeat(t, GRID_W, axis=0)
    by_col = lambda t: jnp.tile(t, (rows, 1))
    cos = jnp.concatenate([by_row(jnp.cos(ar))] * 2 + [by_col(jnp.cos(ac))] * 2, axis=-1)
    sin = jnp.concatenate([by_row(jnp.sin(ar))] * 2 + [by_col(jnp.sin(ac))] * 2, axis=-1)
    cos_all = jnp.concatenate([cos, jnp.ones((CTX_LEN, QK_ROPE), F32)], axis=0)
    sin_all = jnp.concatenate([sin, jnp.zeros((CTX_LEN, QK_ROPE), F32)], axis=0)
    return cos_all, sin_all


def kernel(x, c, ctx, c_ctx, w_ada, b_ada, norm_mix_pre, norm_mix_post, norm_ffn_pre, norm_ffn_post, w_in, rwkv_mu, rwkv_w0, rwkv_w2, rwkv_a0, rwkv_a2, rwkv_k_k, rwkv_k_a, rwkv_r_k, rwkv_lnx_w, rwkv_lnx_b, rwkv_g2, w_rwkv_proj, mla_q_norm, mla_kv_norm, mla_w_uq, mla_w_ukv, w_mla_proj, gate_b, w_out, ffn_w_gate, ffn_w_val, ffn_conv_w, ffn_conv_b, ffn_w_down):
    B = x.shape[0]
    D = D_MODEL
    C = RWKV_DIM
    l = 0

    s = jnp.concatenate([jax.nn.silu(c), jax.nn.silu(c_ctx)[None, :]], axis=0)
    mods = _ada(s, w_ada[l], b_ada[l])
    lat = mods[:B].reshape(B, 6, D)
    cm = mods[B].reshape(6, D)
    sh1, sc1, g1, sh2, sc2, g2 = [lat[:, j] for j in range(6)]
    mods1 = jnp.stack([jnp.stack([sh1, sc1], axis=1),
                       jnp.broadcast_to(jnp.stack([cm[0], cm[1]])[None], (B, 2, D))], axis=1)
    mods2 = jnp.stack([sh2, sc2], axis=1)

    h = _norm_mod(x, ctx, norm_mix_pre[l], mods1)
    wt_main, wt_gate = _pack_w_in(w_in[l].T)
    p = _matmul_nt(h.reshape(B * S_ALL, D), wt_main, tm=1088, tn=1024, name="w_in").reshape(B, S_ALL, P_COLS)
    gates = _gates(h, wt_gate, gate_b[l])

    lane_head = jnp.arange(LANES) // RWKV_HEAD
    ones2 = (lane_head[:, None] == lane_head[None, :]).astype(BF16)
    avg2 = ones2 * (1.0 / RWKV_HEAD)
    r, k, v, kk, lw, a, g, bg = _prep(p, rwkv_mu[l], rwkv_w0[l], _lora_window_weight(rwkv_w2[l], C_WL - C_WL_WIN),
                                      rwkv_a0[l], _lora_window_weight(rwkv_a2[l], C_AL - C_AL_WIN), rwkv_k_k[l],
                                      rwkv_k_a[l], rwkv_r_k[l].reshape(C), rwkv_g2[l].astype(BF16), ones2)
    y_dir = _wkv(r, k, v, kk, lw, a, rwkv_k_a[l])

    cos, sin = _rope_tables()
    uq = mla_w_uq[l].reshape(Q_LORA, MLA_HEADS, QK_NOPE + QK_ROPE)
    wq = jnp.concatenate([uq, _rope_partner(uq[..., QK_NOPE:])], axis=-1).transpose(1, 0, 2).astype(BF16)
    wkv = mla_w_ukv[l].reshape(KV_LORA, MLA_HEADS, QK_NOPE + V_HEAD).transpose(1, 0, 2).astype(BF16)
    q = _qproj(p, mla_q_norm[l], wq, cos[:SEQ], sin[:SEQ])
    k_all, v_all = _kvproj(p, mla_kv_norm[l], wkv, cos, sin)
    o, (wg_b, wv_b, wd_b, wrp_b, wmp_b, wo_b) = _attention(
        q, k_all, v_all, (ffn_w_gate[l], ffn_w_val[l], ffn_w_down[l], w_rwkv_proj[l], w_mla_proj[l], w_out[l]))

    merged = _merge(y_dir, g, bg, rwkv_lnx_w[l], rwkv_lnx_b[l], avg2, o, wrp_b, wmp_b, gates)
    x1, h2 = _outproj(merged, wo_b, x, norm_mix_post[l], g1, norm_ffn_pre[l], mods2)
    return _ffn(h2, wg_b, wv_b, wd_b, ffn_conv_w[l], ffn_conv_b[l], x1, g2, norm_ffn_post[l])
```

```python
import functools
import math

import jax
import jax.numpy as jnp
from jax import lax
from jax.experimental import pallas as pl
from jax.experimental.pallas import tpu as pltpu

D_MODEL = 2048
BATCH = 2
SEQ = 4096
GRID_W = 64
CTX_LEN = 256
S_ALL = SEQ + CTX_LEN
EPS = 1e-6
RWKV_HEADS = 16
RWKV_HEAD = 64
RWKV_DIM = RWKV_HEADS * RWKV_HEAD
DECAY_LORA = 96
AAA_LORA = 96
GATE_LORA = 256
N_DIR = 2
DECAY_SCALE = math.exp(-0.5)
LNX_EPS = 64e-5
MLA_HEADS = 16
Q_LORA = 512
KV_LORA = 512
QK_NOPE = 128
QK_ROPE = 64
V_HEAD = 128
ROPE_THETA = 10000.0
ATTN_SCALE = (QK_NOPE + QK_ROPE) ** -0.5
LOG2_E = math.log2(math.e)
D_FF = 5632
RWKV_IN = 3 * RWKV_DIM + N_DIR * DECAY_LORA + N_DIR * AAA_LORA + GATE_LORA
MLA_IN = Q_LORA + KV_LORA + QK_ROPE

LANES = 128
SUBLANES = 8
VMEM_LIMIT = 56 * 1024 * 1024

C_R, C_K, C_V = 0, RWKV_DIM, 2 * RWKV_DIM
C_WL = 3 * RWKV_DIM
C_AL = C_WL + N_DIR * DECAY_LORA
C_GL = C_AL + N_DIR * AAA_LORA
RWKV_COLS = C_GL + GATE_LORA
C_KPE = RWKV_COLS
C_CQ = -(-(C_KPE + 2 * QK_ROPE) // Q_LORA) * Q_LORA
C_CKV = C_CQ + Q_LORA
P_COLS = C_CKV + KV_LORA
LORA_WIN = 2 * LANES
C_WL_WIN = C_WL // LANES * LANES
C_AL_WIN = C_AL // LANES * LANES

WKV_CHUNK = 64
WKV_SUB = 4
HEAD_PAIRS = RWKV_DIM // LANES

F32 = jnp.float32
BF16 = jnp.bfloat16


def _cparams(sem):
    return pltpu.CompilerParams(dimension_semantics=sem, vmem_limit_bytes=VMEM_LIMIT)


def _bdot(a, b):
    return jnp.dot(a.astype(BF16), b.astype(BF16), preferred_element_type=F32)


def _hilo_dot(x, w):
    hi = x.astype(BF16)
    lo = (x - hi.astype(F32)).astype(BF16)
    return jnp.dot(hi, w, preferred_element_type=F32) + jnp.dot(lo, w, preferred_element_type=F32)


def _head_sum(x, ones2):
    return jnp.concatenate([_hilo_dot(x[:, c * LANES:(c + 1) * LANES], ones2) for c in range(HEAD_PAIRS)], axis=-1)


def _ada_kernel(s_ref, w_ref, b_ref, o_ref):
    w = w_ref[...]
    for r in range(s_ref.shape[0]):
        o_ref[r:r + 1, :] = jnp.sum(s_ref[r] * w, axis=0, keepdims=True) + b_ref[...]


def _ada(s, w, b, tn=2048):
    R, K = s.shape
    N = w.shape[1]
    return pl.pallas_call(
        _ada_kernel, grid=(N // tn,),
        in_specs=[pl.BlockSpec((R, K, 1), lambda j: (0, 0, 0)),
                  pl.BlockSpec((K, tn), lambda j: (0, j)),
                  pl.BlockSpec((1, tn), lambda j: (0, j))],
        out_specs=pl.BlockSpec((R, tn), lambda j: (0, j)),
        out_shape=jax.ShapeDtypeStruct((R, N), F32),
        compiler_params=_cparams(("parallel",)), name="ada",
    )(s[:, :, None], w, b[None, :])


def _norm_mod_kernel(x_ref, c_ref, g_ref, m_ref, o_ref, *, lat_tiles):
    def emit(x):
        n = x * lax.rsqrt(jnp.mean(x * x, axis=-1, keepdims=True) + EPS) * g_ref[...]
        m = m_ref[0, 0]
        o_ref[0] = (n * (1.0 + m[1:2]) + m[0:1]).astype(o_ref.dtype)

    is_latent = pl.program_id(1) < lat_tiles

    @pl.when(is_latent)
    def _():
        emit(x_ref[0])

    @pl.when(jnp.logical_not(is_latent))
    def _():
        emit(c_ref[0])


def _norm_mod(x, ctx, gain, mods, tm=256):
    B, _, D = x.shape
    assert ctx.shape[1] == tm
    lat_tiles = SEQ // tm
    return pl.pallas_call(
        functools.partial(_norm_mod_kernel, lat_tiles=lat_tiles), grid=(B, S_ALL // tm),
        in_specs=[pl.BlockSpec((1, tm, D), lambda b, i: (b, jnp.minimum(i, lat_tiles - 1), 0)),
                  pl.BlockSpec((1, tm, D), lambda b, i: (b, 0, 0)),
                  pl.BlockSpec((1, D), lambda b, i: (0, 0)),
                  pl.BlockSpec((1, 1, 2, D), lambda b, i: (b, i // lat_tiles, 0, 0))],
        out_specs=pl.BlockSpec((1, tm, D), lambda b, i: (b, i, 0)),
        out_shape=jax.ShapeDtypeStruct((B, S_ALL, D), BF16),
        compiler_params=_cparams(("parallel", "parallel")), name="norm_mod",
    )(x, ctx, gain[None, :], mods)


def _dot_nt(a, wt):
    return lax.dot_general(a, wt, (((1,), (1,)), ((), ())), preferred_element_type=F32)


def _mm_kernel(a_ref, wt_ref, o_ref):
    o_ref[...] = _dot_nt(a_ref[...], wt_ref[...]).astype(o_ref.dtype)


def _matmul_nt(a, wt, tm, tn, out_dtype=F32, name="matmul"):
    M, K = a.shape
    N = wt.shape[0]
    return pl.pallas_call(
        _mm_kernel, grid=(M // tm, N // tn),
        in_specs=[pl.BlockSpec((tm, K), lambda i, j: (i, 0)),
                  pl.BlockSpec((tn, K), lambda i, j: (j, 0))],
        out_specs=pl.BlockSpec((tm, tn), lambda i, j: (i, j)),
        out_shape=jax.ShapeDtypeStruct((M, N), out_dtype),
        compiler_params=_cparams(("parallel", "parallel")), name=name,
    )(a, wt)


def _prep_kernel(p_ref, pp_ref, pn_ref, mu_ref, w0_ref, w2_ref, a0_ref, a2_ref, kk_ref, ka_ref, rk_ref, g2_ref,
                 ones_ref, r_o, k_o, v_o, kkn_o, lw_o, a_o, g_o, bg_o):
    i = pl.program_id(1)
    x = p_ref[0]
    tm = x.shape[0]
    lat_tiles = SEQ // tm
    first = jnp.logical_or(i == 0, i == lat_tiles)
    last = jnp.logical_or(i == lat_tiles - 1, i == pl.num_programs(1) - 1)
    x_before = jnp.where(first, 0.0, pp_ref[0][SUBLANES - 1:SUBLANES])
    x_after = jnp.where(last, 0.0, pn_ref[0][0:1])
    rows = lax.broadcasted_iota(jnp.int32, (tm, 1), 0)
    prev = jnp.where(rows == 0, x_before, pltpu.roll(x, 1, 0))
    nxt = jnp.where(rows == tm - 1, x_after, pltpu.roll(x, tm - 1, 0))
    xs = x + mu_ref[...] * (0.5 * (prev + nxt) - x)

    C = RWKV_DIM
    r, k, v = xs[:, C_R:C_R + C], xs[:, C_K:C_K + C], xs[:, C_V:C_V + C]
    w_lora = _bdot(jnp.tanh(xs[:, C_WL_WIN:C_WL_WIN + LORA_WIN]), w2_ref[...])
    a_lora = _bdot(xs[:, C_AL_WIN:C_AL_WIN + LORA_WIN], a2_ref[...])
    a_sum = None
    for d in range(N_DIR):
        w_raw = w0_ref[d:d + 1] + w_lora[:, d * C:(d + 1) * C]
        lw_o[0, d] = -DECAY_SCALE * jax.nn.sigmoid(w_raw)
        a_d = jax.nn.sigmoid(a0_ref[d:d + 1] + a_lora[:, d * C:(d + 1) * C])
        a_o[0, d] = a_d
        a_sum = a_d if a_sum is None else a_sum + a_d
    ones2 = ones_ref[...]
    kkx = k * kk_ref[...]
    kkn_o[0] = kkx / jnp.maximum(jnp.sqrt(_head_sum(kkx * kkx, ones2)), 1e-12)
    g = _bdot(jax.nn.sigmoid(xs[:, C_GL:C_GL + GATE_LORA]), g2_ref[...])
    k_bar = k * (1.0 + (0.5 * a_sum - 1.0) * ka_ref[...])
    bonus = _head_sum(r * k_bar * rk_ref[...], ones2) * v
    r_o[0] = r
    k_o[0] = k
    v_o[0] = v
    g_o[0] = g.astype(g_o.dtype)
    bg_o[0] = (bonus * g).astype(bg_o.dtype)


def _prep(p, mu, w0, w2p, a0, a2p, k_k, k_a, r_k, g2, ones2, tm=256):
    B = p.shape[0]
    C = RWKV_DIM
    hb = tm // SUBLANES
    last = S_ALL // SUBLANES - 1
    row = lambda n: pl.BlockSpec((1, n), lambda b, i: (0, 0))
    tok = pl.BlockSpec((1, tm, C), lambda b, i: (b, i, 0))
    tok_dir = pl.BlockSpec((1, N_DIR, tm, C), lambda b, i: (b, 0, i, 0))
    shp = jax.ShapeDtypeStruct((B, S_ALL, C), F32)
    shp_dir = jax.ShapeDtypeStruct((B, N_DIR, S_ALL, C), F32)
    shp_b = jax.ShapeDtypeStruct((B, S_ALL, C), BF16)
    return pl.pallas_call(
        _prep_kernel, grid=(B, S_ALL // tm),
        in_specs=[pl.BlockSpec((1, tm, RWKV_COLS), lambda b, i: (b, i, 0)),
                  pl.BlockSpec((1, SUBLANES, RWKV_COLS), lambda b, i: (b, jnp.maximum(i * hb - 1, 0), 0)),
                  pl.BlockSpec((1, SUBLANES, RWKV_COLS), lambda b, i: (b, jnp.minimum((i + 1) * hb, last), 0)),
                  row(RWKV_COLS),
                  pl.BlockSpec((N_DIR, C), lambda b, i: (0, 0)),
                  pl.BlockSpec((LORA_WIN, N_DIR * C), lambda b, i: (0, 0)),
                  pl.BlockSpec((N_DIR, C), lambda b, i: (0, 0)),
                  pl.BlockSpec((LORA_WIN, N_DIR * C), lambda b, i: (0, 0)),
                  row(C), row(C), row(C),
                  pl.BlockSpec((GATE_LORA, C), lambda b, i: (0, 0)),
                  pl.BlockSpec((LANES, LANES), lambda b, i: (0, 0))],
        out_specs=[tok, tok, tok, tok, tok_dir, tok_dir, tok, tok],
        out_shape=[shp, shp, shp, shp, shp_dir, shp_dir, shp_b, shp_b],
        compiler_params=_cparams(("parallel", "parallel")), name="rwkv_prep",
    )(p, p, p, mu[None, :], w0, w2p, a0, a2p, k_k[None, :], k_a[None, :], r_k[None, :], g2, ones2)


def _wkv_kernel(r_ref, k_ref, v_ref, kk_ref, lw_ref, a_ref, ka_ref, y_ref, s_ref):
    d = pl.program_id(1)
    C = WKV_CHUNK
    C2 = 2 * C
    P = HEAD_PAIRS
    sign = 1 - 2 * d
    fwd = d == 0

    @pl.when(pl.program_id(2) == 0)
    def _():
        s_ref[...] = jnp.zeros_like(s_ref)

    def chunk_tri(n):
        trow = lax.broadcasted_iota(jnp.int32, (n * C, n * C), 0)
        tcol = lax.broadcasted_iota(jnp.int32, (n * C, n * C), 1)
        return jnp.logical_and((trow // C) == (tcol // C), (trow - tcol) * sign >= 0).astype(BF16)

    lane = lax.broadcasted_iota(jnp.int32, (C, LANES), 1)
    head0 = lane < RWKV_HEAD
    zero = jnp.zeros((), F32)

    def expand(x):
        x = x.astype(BF16)
        zb = jnp.zeros((), BF16)
        tiles = [x[c:c + C, j * LANES:(j + 1) * LANES] for c in range(0, x.shape[0], C) for j in range(P)]
        return jnp.stack([jnp.concatenate([jnp.where(head0, t, zb), jnp.where(head0, zb, t)], axis=0)
                          for t in tiles])

    def bmm(x, y):
        return jnp.einsum('hab,hbc->hac', x.astype(BF16), y.astype(BF16), preferred_element_type=F32)

    def bmm_nt(x, y):
        return jnp.einsum('han,hbn->hab', x.astype(BF16), y.astype(BF16), preferred_element_type=F32)

    def bmm_tn(x, y):
        return jnp.einsum('hca,hcb->hab', x.astype(BF16), y.astype(BF16), preferred_element_type=F32)

    row = lax.broadcasted_iota(jnp.int32, (C2, C2), 0)
    col = lax.broadcasted_iota(jnp.int32, (C2, C2), 1)
    same = (row // C) == (col // C)
    lag = (row - col) * sign
    incl = jnp.logical_and(same, lag >= 0)[None]
    strict = jnp.logical_and(same, lag > 0)[None]

    def blk(m):
        return (row // m) == (col // m)

    def half_rows(t, m, second):
        return jnp.concatenate([t[:, g + second * m:g + (second + 1) * m] for g in range(0, C2, 2 * m)], axis=1)

    def state_free(chunk_rows):
        def load(ref, *lead):
            return jnp.concatenate([ref[(*lead, rows, slice(None))] for rows in chunk_rows], axis=0)

        r = load(r_ref, 0)
        v = load(v_ref, 0)
        kk = load(kk_ref, 0)
        lw = load(lw_ref, 0, 0)
        a = load(a_ref, 0, 0)
        k = load(k_ref, 0) * (1.0 + (a - 1.0) * ka_ref[...])
        n = len(chunk_rows)

        cum = _hilo_dot_left(chunk_tri(n), lw)
        tot = jnp.concatenate(
            [jnp.broadcast_to(jnp.where(fwd, cum[(c + 1) * C - 1:(c + 1) * C], cum[c * C:c * C + 1]), (C, cum.shape[1]))
             for c in range(n)], axis=0)
        e_cum = jnp.exp(cum)
        e_neg = jnp.exp(-cum)
        e_rem = jnp.exp(tot - cum)
        b = kk * a
        at = -kk * jnp.exp(cum - lw)
        dec = jnp.exp(tot)
        AT, RT, BT, KT, BD, KD, V = (expand(t) for t in (at, r * e_cum, b * e_neg, k * e_neg, b * e_rem,
                                                         k * e_rem, v))
        AR = jnp.concatenate([AT, RT], axis=1)
        G = bmm_nt(AR, jnp.concatenate([BT, KT], axis=1))
        a_ab, a_ak = G[:, :C2, :C2], G[:, :C2, C2:]
        a_rb, a_rk = G[:, C2:, :C2], G[:, C2:, C2:]

        L = jnp.where(strict, a_ab, zero)
        Lb = jnp.where(blk(8)[None], L, zero)
        L2 = bmm(Lb, Lb)
        L4 = bmm(L2, L2)
        S1 = Lb + L2 + bmm(Lb, L2)
        Nm = S1 + L4 + bmm(L4, S1)
        for m in (8, 16, 32):
            Lo = jnp.where(jnp.logical_and(blk(2 * m), jnp.logical_not(blk(m)))[None], L, zero)
            n_first, n_second = half_rows(Nm, m, 0), half_rows(Nm, m, 1)
            n_late = jnp.where(fwd, n_second, n_first)
            p_late = jnp.where(fwd, half_rows(Lo, m, 1), half_rows(Lo, m, 0)) + bmm(n_late, Lo)
            n_late = n_late + p_late + bmm(p_late, Nm)
            n_first, n_second = jnp.where(fwd, n_first, n_late), jnp.where(fwd, n_late, n_second)
            Nm = jnp.concatenate([piece for g in range(C2 // (2 * m))
                                  for piece in (n_first[:, g * m:(g + 1) * m], n_second[:, g * m:(g + 1) * m])],
                                 axis=1)
        rhs_v = bmm(jnp.where(strict, a_ak, zero), V)
        a_y = jnp.concatenate([jnp.where(incl, a_rb, zero), jnp.where(incl, a_rk, zero)], axis=2).astype(BF16)
        dec3 = jnp.stack([dec[c * C:c * C + 1, j * LANES:(j + 1) * LANES]
                          for c in range(n) for j in range(P)])
        whole = (AR, rhs_v, Nm.astype(BF16), V, a_y, jnp.concatenate([BD, KD], axis=1), dec3)
        return [tuple(t[c * P:(c + 1) * P] for t in whole) for c in range(n)]

    def state_step(part, s0):
        AR, rhs_v, Nm, V, a_y, BKD, dec3 = part
        ars0 = bmm_nt(AR, s0)
        rhs = ars0[:, :C2] + rhs_v
        u = rhs + bmm(Nm, rhs)
        uv = jnp.concatenate([u.astype(BF16), V], axis=1)
        y2 = ars0[:, C2:] + bmm(a_y, uv)
        y = y2[:, :C] + y2[:, C:]
        return jnp.concatenate([y[j] for j in range(P)], axis=-1), s0 * dec3 + bmm_tn(uv, BKD)

    chunk_rows = [pl.ds(pl.multiple_of(jnp.where(fwd, j, WKV_SUB - 1 - j) * C, C), C) for j in range(WKV_SUB)]
    half = WKV_SUB // 2
    s = s_ref[...]
    for group in (chunk_rows[:half], chunk_rows[half:]):
        for rows, part in zip(group, state_free(group)):
            y, s = state_step(part, s)
            y_ref[0, 0, rows, :] = y
    s_ref[...] = s


def _hilo_dot_left(w, x):
    hi = x.astype(BF16)
    lo = (x - hi.astype(F32)).astype(BF16)
    return jnp.dot(w, hi, preferred_element_type=F32) + jnp.dot(w, lo, preferred_element_type=F32)


def _wkv(r, k, v, kk, lw, a, k_a):
    B, S, Cd = r.shape
    R = WKV_SUB * WKV_CHUNK
    n_blocks = S // R
    lat_blocks = SEQ // R

    def bidx(d, i):
        return jnp.where(d == 0, (i + lat_blocks) % n_blocks, n_blocks - 1 - i)

    shared = pl.BlockSpec((1, R, Cd), lambda b, d, i: (b, bidx(d, i), 0))
    perdir = pl.BlockSpec((1, 1, R, Cd), lambda b, d, i: (b, d, bidx(d, i), 0))
    return pl.pallas_call(
        _wkv_kernel, grid=(B, N_DIR, n_blocks),
        in_specs=[shared, shared, shared, shared, perdir, perdir, pl.BlockSpec((1, Cd), lambda b, d, i: (0, 0))],
        out_specs=perdir,
        out_shape=jax.ShapeDtypeStruct((B, N_DIR, S, Cd), F32),
        scratch_shapes=[pltpu.VMEM((HEAD_PAIRS, LANES, LANES), F32)],
        compiler_params=_cparams(("parallel", "parallel", "arbitrary")), name="wkv7",
    )(r, k, v, kk, lw, a, k_a[None, :])


def _rms(x, g):
    return x * lax.rsqrt(jnp.mean(x * x, axis=-1, keepdims=True) + EPS) * g


def _qproj_kernel(p_ref, g_ref, w_ref, cos_ref, sin_ref, q_ref):
    cq = _rms(p_ref[0], g_ref[...]).astype(BF16)
    cs, sn = cos_ref[...], sin_ref[...]
    for h in range(MLA_HEADS):
        y = jnp.dot(cq, w_ref[h], preferred_element_type=F32)
        rope = y[:, QK_NOPE:QK_NOPE + QK_ROPE] * cs + y[:, QK_NOPE + QK_ROPE:] * sn
        q = jnp.concatenate([y[:, :QK_NOPE], rope], axis=-1) * (ATTN_SCALE * LOG2_E)
        q_ref[0, h] = q.astype(q_ref.dtype)


def _qproj(p, gain, wq, cos, sin, tm=512):
    B = p.shape[0]
    return pl.pallas_call(
        _qproj_kernel, grid=(B, SEQ // tm),
        in_specs=[pl.BlockSpec((1, tm, Q_LORA), lambda b, i: (b, i, C_CQ // Q_LORA)),
                  pl.BlockSpec((1, Q_LORA), lambda b, i: (0, 0)),
                  pl.BlockSpec((MLA_HEADS, Q_LORA, 2 * LANES), lambda b, i: (0, 0, 0)),
                  pl.BlockSpec((tm, QK_ROPE), lambda b, i: (i, 0)),
                  pl.BlockSpec((tm, QK_ROPE), lambda b, i: (i, 0))],
        out_specs=pl.BlockSpec((1, MLA_HEADS, tm, QK_NOPE + QK_ROPE), lambda b, i: (b, 0, i, 0)),
        out_shape=jax.ShapeDtypeStruct((B, MLA_HEADS, SEQ, QK_NOPE + QK_ROPE), BF16),
        compiler_params=_cparams(("parallel", "parallel")), name="mla_q",
    )(p, gain[None, :], wq, cos, sin)


def _kvproj_kernel(p_ref, pe_ref, g_ref, w_ref, cos_ref, sin_ref, k_ref, v_ref):
    ckv = _rms(p_ref[0], g_ref[...]).astype(BF16)
    pe = pe_ref[0]
    kpe = pe[:, :QK_ROPE] * cos_ref[...] + pe[:, QK_ROPE:] * sin_ref[...]
    ones = jnp.ones((ckv.shape[0], V_HEAD), F32)
    for h in range(MLA_HEADS):
        y = jnp.dot(ckv, w_ref[h], preferred_element_type=F32)
        k_ref[0, h] = jnp.concatenate([y[:, :QK_NOPE], kpe], axis=-1).astype(k_ref.dtype)
        v_ref[0, h] = jnp.concatenate([y[:, QK_NOPE:], ones], axis=-1).astype(v_ref.dtype)


def _kvproj(p, gain, wkv, cos, sin, tm=256):
    B = p.shape[0]
    return pl.pallas_call(
        _kvproj_kernel, grid=(B, S_ALL // tm),
        in_specs=[pl.BlockSpec((1, tm, KV_LORA), lambda b, i: (b, i, C_CKV // KV_LORA)),
                  pl.BlockSpec((1, tm, 2 * QK_ROPE), lambda b, i: (b, i, C_KPE // (2 * QK_ROPE))),
                  pl.BlockSpec((1, KV_LORA), lambda b, i: (0, 0)),
                  pl.BlockSpec((MLA_HEADS, KV_LORA, 2 * LANES), lambda b, i: (0, 0, 0)),
                  pl.BlockSpec((tm, QK_ROPE), lambda b, i: (i, 0)),
                  pl.BlockSpec((tm, QK_ROPE), lambda b, i: (i, 0))],
        out_specs=[pl.BlockSpec((1, MLA_HEADS, tm, QK_NOPE + QK_ROPE), lambda b, i: (b, 0, i, 0)),
                   pl.BlockSpec((1, MLA_HEADS, tm, 2 * V_HEAD), lambda b, i: (b, 0, i, 0))],
        out_shape=[jax.ShapeDtypeStruct((B, MLA_HEADS, S_ALL, QK_NOPE + QK_ROPE), BF16),
                   jax.ShapeDtypeStruct((B, MLA_HEADS, S_ALL, 2 * V_HEAD), BF16)],
        compiler_params=_cparams(("parallel", "parallel")), name="mla_kv",
    )(p, p, gain[None, :], wkv, cos, sin)


def _attn_kernel(*refs, tk, n_cast):
    q_ref, k_ref, v_ref = refs[:3]
    cast_in = refs[3:3 + n_cast]
    o_ref = refs[3 + n_cast]
    cast_out = refs[4 + n_cast:4 + 2 * n_cast]
    sa, sb, pa, pb, acc, m_ref, al_ref = refs[4 + 2 * n_cast:]
    for w_ref, wb_ref in zip(cast_in, cast_out):
        wb_ref[...] = w_ref[...].astype(wb_ref.dtype)

    q = q_ref[0, 0]
    n_keys = k_ref.shape[2]
    blocks = [(off, min(tk, n_keys - off)) for off in range(0, n_keys, tk)]
    n_kv = len(blocks)

    def put_scores(s_ref, j):
        off, w = blocks[j]
        kj = k_ref[0, 0, off:off + w, :]
        s_ref[:, :w] = lax.dot_general(q, kj, (((1,), (1,)), ((), ())), preferred_element_type=F32)

    def accumulate(p_ref, j):
        off, w = blocks[j]
        pv = jnp.dot(p_ref[:, :w], v_ref[0, 0, off:off + w, :], preferred_element_type=F32)
        al = al_ref[...]
        for c in range(0, 2 * V_HEAD, LANES):
            acc[:, c:c + LANES] = al * acc[:, c:c + LANES] + pv[:, c:c + LANES]

    def softmax(s_ref, p_ref, j):
        w = blocks[j][1]
        s = s_ref[:, :w]
        m_old = m_ref[...]
        m_new = jnp.maximum(m_old, jnp.max(s, axis=-1, keepdims=True))
        al_ref[...] = jnp.exp2(m_old - m_new)
        m_ref[...] = m_new
        for c in range(0, w, LANES):
            p_ref[:, c:c + LANES] = jnp.exp2(s[:, c:c + LANES] - m_new).astype(p_ref.dtype)

    s_bufs, p_bufs = (sa, sb), (pa, pb)
    put_scores(sa, 0)
    acc[...] = jnp.zeros_like(acc)
    m_ref[...] = jnp.full_like(m_ref, -jnp.inf)
    for j in range(n_kv):
        if j + 1 < n_kv:
            put_scores(s_bufs[(j + 1) % 2], j + 1)
        if j >= 1:
            accumulate(p_bufs[(j - 1) % 2], j - 1)
        softmax(s_bufs[j % 2], p_bufs[j % 2], j)
    accumulate(p_bufs[(n_kv - 1) % 2], n_kv - 1)
    o_ref[0] = (acc[:, :V_HEAD] / acc[:, V_HEAD:]).astype(o_ref.dtype)


BF16_SUBLANES = 16


def _attention(q, k, v, to_cast, tq=1024, tk=1024):
    B, H, T, Dk = q.shape
    S = k.shape[2]
    nt = T // tq
    n_steps = B * H * nt

    def cast_spec(w):
        rows, cols = w.shape
        rb = next(r for r in range(BF16_SUBLANES, rows + 1, BF16_SUBLANES)
                  if rows % r == 0 and rows // r <= n_steps)
        n_blk = rows // rb
        return pl.BlockSpec((rb, cols), lambda b, h, i: (((b * H + h) * nt + i) * n_blk // n_steps, 0))

    cast_specs = [cast_spec(w) for w in to_cast]
    outs = pl.pallas_call(
        functools.partial(_attn_kernel, tk=tk, n_cast=len(to_cast)), grid=(B, H, nt),
        in_specs=[pl.BlockSpec((1, 1, tq, Dk), lambda b, h, i: (b, h, i, 0)),
                  pl.BlockSpec((1, 1, S, Dk), lambda b, h, i: (b, h, 0, 0)),
                  pl.BlockSpec((1, 1, S, 2 * V_HEAD), lambda b, h, i: (b, h, 0, 0))] + cast_specs,
        out_specs=[pl.BlockSpec((1, tq, V_HEAD), lambda b, h, i: (b, i, h))] + cast_specs,
        out_shape=[jax.ShapeDtypeStruct((B, T, H * V_HEAD), BF16)]
                  + [jax.ShapeDtypeStruct(w.shape, BF16) for w in to_cast],
        scratch_shapes=[pltpu.VMEM((tq, tk), F32), pltpu.VMEM((tq, tk), F32),
                        pltpu.VMEM((tq, tk), BF16), pltpu.VMEM((tq, tk), BF16),
                        pltpu.VMEM((tq, 2 * V_HEAD), F32), pltpu.VMEM((tq, LANES), F32), pltpu.VMEM((tq, LANES), F32)],
        compiler_params=_cparams(("arbitrary", "arbitrary", "arbitrary")), name="mla_attn",
    )(q, k, v, *to_cast)
    return outs[0], outs[1:]


def _gates_kernel(h_ref, wt_ref, b_ref, o_ref):
    o_ref[0] = jax.nn.sigmoid(_dot_nt(h_ref[0], wt_ref[...]) + b_ref[...]).astype(o_ref.dtype)


def _gates(h, wt_gate, gate_b, tm=2048, tn=1024):
    B, _, D = h.shape
    N = wt_gate.shape[0]
    return pl.pallas_call(
        _gates_kernel, grid=(B, SEQ // tm, N // tn),
        in_specs=[pl.BlockSpec((1, tm, D), lambda b, i, j: (b, i, 0)),
                  pl.BlockSpec((tn, D), lambda b, i, j: (j, 0)),
                  pl.BlockSpec((1, tn), lambda b, i, j: (0, j))],
        out_specs=pl.BlockSpec((1, tm, tn), lambda b, i, j: (b, i, j)),
        out_shape=jax.ShapeDtypeStruct((B, SEQ, N), BF16),
        compiler_params=_cparams(("parallel", "parallel", "parallel")), name="gates",
    )(h, wt_gate, gate_b[None, :])


MERGE_TN = 512


def _merge_kernel(y0_ref, y1_ref, g_ref, bg_ref, lnw_ref, lnb_ref, avg_ref, o_ref, wr_ref, wm_ref, gate_ref, out_ref):
    avg = avg_ref[...]
    y = y0_ref[0, 0] + y1_ref[0, 0]
    yc = y - _head_sum(y, avg)
    yn = yc * lax.rsqrt(_head_sum(yc * yc, avg) + LNX_EPS) * lnw_ref[...] + lnb_ref[...]
    yr = (yn * g_ref[0].astype(F32) + bg_ref[0].astype(F32)).astype(BF16)
    o = o_ref[0]
    D = D_MODEL
    for n in range(0, D, MERGE_TN):
        gr = gate_ref[0, :, n:n + MERGE_TN].astype(F32)
        gm = gate_ref[0, :, D + n:D + n + MERGE_TN].astype(F32)
        out_ref[0, :, n:n + MERGE_TN] = (
            gr * jnp.dot(yr, wr_ref[:, n:n + MERGE_TN], preferred_element_type=F32)
            + gm * jnp.dot(o, wm_ref[:, n:n + MERGE_TN], preferred_element_type=F32)).astype(out_ref.dtype)


def _merge(y_dir, g, bg, lnx_w, lnx_b, avg2, o, w_rp, w_mp, gates, tm=512):
    B = o.shape[0]
    D = D_MODEL
    C = RWKV_DIM
    tok = pl.BlockSpec((1, tm, C), lambda b, i: (b, i, 0))
    row = lambda n: pl.BlockSpec((1, n), lambda b, i: (0, 0))
    return pl.pallas_call(
        _merge_kernel, grid=(B, SEQ // tm),
        in_specs=[pl.BlockSpec((1, 1, tm, C), lambda b, i: (b, 0, i, 0)),
                  pl.BlockSpec((1, 1, tm, C), lambda b, i: (b, 1, i, 0)),
                  tok, tok, row(C), row(C),
                  pl.BlockSpec((LANES, LANES), lambda b, i: (0, 0)),
                  pl.BlockSpec((1, tm, D), lambda b, i: (b, i, 0)),
                  pl.BlockSpec((C, D), lambda b, i: (0, 0), pipeline_mode=pl.Buffered(1)),
                  pl.BlockSpec((D, D), lambda b, i: (0, 0), pipeline_mode=pl.Buffered(1)),
                  pl.BlockSpec((1, tm, 2 * D), lambda b, i: (b, i, 0))],
        out_specs=pl.BlockSpec((1, tm, D), lambda b, i: (b, i, 0)),
        out_shape=jax.ShapeDtypeStruct((B, SEQ, D), BF16),
        compiler_params=_cparams(("parallel", "parallel")), name="merge",
    )(y_dir, y_dir, g, bg, lnx_w[None, :], lnx_b[None, :], avg2, o, w_rp, w_mp, gates)


def _outproj_kernel(m_ref, w_ref, x_ref, gpost_ref, gate_ref, gpre_ref, mod_ref, x1_ref, h_ref):
    mod = mod_ref[0]
    tm = m_ref.shape[1]
    for rows in (slice(0, tm // 2), slice(tm // 2, tm)):
        out = jnp.dot(m_ref[0, rows], w_ref[...], preferred_element_type=F32)
        x1 = x_ref[0, rows] + gate_ref[0] * _rms(out, gpost_ref[...])
        x1_ref[0, rows] = x1
        h_ref[0, rows] = (_rms(x1, gpre_ref[...]) * (1.0 + mod[1:2]) + mod[0:1]).astype(h_ref.dtype)


def _outproj(merged, w_out, x, g_post, gate, g_pre, mod2, tm=512):
    B = x.shape[0]
    D = D_MODEL
    return pl.pallas_call(
        _outproj_kernel, grid=(B, SEQ // tm),
        in_specs=[pl.BlockSpec((1, tm, D), lambda b, i: (b, i, 0)),
                  pl.BlockSpec((D, D), lambda b, i: (0, 0)),
                  pl.BlockSpec((1, tm, D), lambda b, i: (b, i, 0)),
                  pl.BlockSpec((1, D), lambda b, i: (0, 0)),
                  pl.BlockSpec((1, 1, D), lambda b, i: (b, 0, 0)),
                  pl.BlockSpec((1, D), lambda b, i: (0, 0)),
                  pl.BlockSpec((1, 2, D), lambda b, i: (b, 0, 0))],
        out_specs=[pl.BlockSpec((1, tm, D), lambda b, i: (b, i, 0)),
                   pl.BlockSpec((1, tm, D), lambda b, i: (b, i, 0))],
        out_shape=[jax.ShapeDtypeStruct((B, SEQ, D), F32),
                   jax.ShapeDtypeStruct((B, SEQ, D), BF16)],
        compiler_params=_cparams(("parallel", "parallel")), name="outproj",
    )(merged, w_out, x, g_post[None, :], gate[:, None, :], g_pre[None, :], mod2)


HALO = 16


def _ffn_kernel(h_ref, hp_ref, hn_ref, wg_ref, wv_ref, wd_ref, cw_ref, cb_ref, x_ref, gate_ref, gpost_ref,
                o_ref, acc_ref):
    i = pl.program_id(1)
    f = pl.program_id(2)
    tm = h_ref.shape[1]

    @pl.when(f == 0)
    def _():
        acc_ref[...] = jnp.zeros_like(acc_ref)

    h = h_ref[0]
    wg = wg_ref[...]
    g_ext = jnp.dot(jnp.concatenate([hp_ref[0], h, hn_ref[0]], axis=0), wg, preferred_element_type=F32)
    g = g_ext[HALO:HALO + tm]
    g_prev = g_ext[HALO - 1:HALO]
    g_next = g_ext[HALO + tm:HALO + tm + 1]
    g_prev = jnp.where(i == 0, 0.0, g_prev)
    g_next = jnp.where(i == pl.num_programs(1) - 1, 0.0, g_next)
    rows = lax.broadcasted_iota(jnp.int32, g.shape, 0)
    up = jnp.where(rows == 0, g_prev, pltpu.roll(g, 1, 0))
    dn = jnp.where(rows == tm - 1, g_next, pltpu.roll(g, tm - 1, 0))
    cw = cw_ref[...]
    u = cb_ref[...] + up * cw[0:1] + g * cw[1:2] + dn * cw[2:3]
    val = jnp.dot(h, wv_ref[...], preferred_element_type=F32)
    act = jax.nn.gelu(u, approximate=True) * val
    acc_ref[...] += jnp.dot(act.astype(BF16), wd_ref[...], preferred_element_type=F32)

    @pl.when(f == pl.num_programs(2) - 1)
    def _():
        o_ref[0] = x_ref[0] + gate_ref[0] * _rms(acc_ref[...], gpost_ref[...])


def _ffn(h, wg, wv, wd, cw, cb, x1, gate, g_post, tm=512, tf=512):
    B = h.shape[0]
    D = D_MODEL
    hb = tm // HALO
    last = SEQ // HALO - 1
    return pl.pallas_call(
        _ffn_kernel, grid=(B, SEQ // tm, D_FF // tf),
        in_specs=[pl.BlockSpec((1, tm, D), lambda b, i, f: (b, i, 0)),
                  pl.BlockSpec((1, HALO, D), lambda b, i, f: (b, jnp.maximum(i * hb - 1, 0), 0)),
                  pl.BlockSpec((1, HALO, D), lambda b, i, f: (b, jnp.minimum((i + 1) * hb, last), 0)),
                  pl.BlockSpec((D, tf), lambda b, i, f: (0, f)),
                  pl.BlockSpec((D, tf), lambda b, i, f: (0, f)),
                  pl.BlockSpec((tf, D), lambda b, i, f: (f, 0)),
                  pl.BlockSpec((3, tf), lambda b, i, f: (0, f)),
                  pl.BlockSpec((1, tf), lambda b, i, f: (0, f)),
                  pl.BlockSpec((1, tm, D), lambda b, i, f: (b, i, 0)),
                  pl.BlockSpec((1, 1, D), lambda b, i, f: (b, 0, 0)),
                  pl.BlockSpec((1, D), lambda b, i, f: (0, 0))],
        out_specs=pl.BlockSpec((1, tm, D), lambda b, i, f: (b, i, 0)),
        out_shape=jax.ShapeDtypeStruct((B, SEQ, D), F32),
        scratch_shapes=[pltpu.VMEM((tm, D), F32)],
        compiler_params=_cparams(("parallel", "parallel", "arbitrary")), name="convffn",
    )(h, h, h, wg, wv, wd, cw, cb[None, :], x1, gate[:, None, :], g_post[None, :])


def _rope_partner(w):
    q = QK_ROPE // 4
    return jnp.concatenate([-w[..., q:2 * q], w[..., :q], -w[..., 3 * q:], w[..., 2 * q:3 * q]], axis=-1)


def _pack_w_in_kernel(wt_ref, perm_ref, main_ref, gate_ref):
    latents = RWKV_IN + Q_LORA + KV_LORA
    main_ref[:RWKV_IN] = wt_ref[:RWKV_IN].astype(BF16)
    kpe = wt_ref[latents:latents + QK_ROPE].astype(BF16)
    partner = jnp.dot(perm_ref[...], kpe, preferred_element_type=F32).astype(BF16)
    main_ref[C_KPE:C_KPE + QK_ROPE] = kpe
    main_ref[C_KPE + QK_ROPE:C_KPE + 2 * QK_ROPE] = partner
    main_ref[C_KPE + 2 * QK_ROPE:C_CQ] = jnp.zeros((C_CQ - C_KPE - 2 * QK_ROPE, kpe.shape[1]), BF16)
    main_ref[C_CQ:] = wt_ref[RWKV_IN:latents].astype(BF16)
    gate_ref[...] = wt_ref[RWKV_IN + MLA_IN:].astype(BF16)


def _pack_w_in(wt, tk=256):
    n_in, K = wt.shape
    perm_t = _rope_partner(jnp.eye(QK_ROPE, dtype=F32)).T.astype(BF16)
    n_gate = n_in - RWKV_IN - MLA_IN
    return pl.pallas_call(
        _pack_w_in_kernel, grid=(K // tk,),
        in_specs=[pl.BlockSpec((n_in, tk), lambda i: (0, i)),
                  pl.BlockSpec((QK_ROPE, QK_ROPE), lambda i: (0, 0))],
        out_specs=[pl.BlockSpec((P_COLS, tk), lambda i: (0, i)),
                   pl.BlockSpec((n_gate, tk), lambda i: (0, i))],
        out_shape=[jax.ShapeDtypeStruct((P_COLS, K), BF16), jax.ShapeDtypeStruct((n_gate, K), BF16)],
        compiler_params=_cparams(("parallel",)), name="pack_w_in",
    )(wt, perm_t)


def _lora_window_weight(w2, first_row):
    n_dir, R, C = w2.shape
    out = jnp.zeros((LORA_WIN, n_dir, C), BF16)
    for d in range(n_dir):
        out = out.at[first_row + d * R:first_row + (d + 1) * R, d].set(w2[d].astype(BF16))
    return out.reshape(LORA_WIN, n_dir * C)


def _rope_tables():
    rows = SEQ // GRID_W
    half = QK_ROPE // 2
    freqs = ROPE_THETA ** (-jnp.arange(0, half, 2, dtype=F32) / half)
    ar = jnp.arange(rows, dtype=F32)[:, None] * freqs
    ac = jnp.arange(GRID_W, dtype=F32)[:, None] * freqs
    by_row = lambda t: jnp.repeat(t, GRID_W, axis=0)
    by_col = lambda t: jnp.tile(t, (rows, 1))
    cos = jnp.concatenate([by_row(jnp.cos(ar))] * 2 + [by_col(jnp.cos(ac))] * 2, axis=-1)
    sin = jnp.concatenate([by_row(jnp.sin(ar))] * 2 + [by_col(jnp.sin(ac))] * 2, axis=-1)
    cos_all = jnp.concatenate([cos, jnp.ones((CTX_LEN, QK_ROPE), F32)], axis=0)
    sin_all = jnp.concatenate([sin, jnp.zeros((CTX_LEN, QK_ROPE), F32)], axis=0)
    return cos_all, sin_all


def kernel(x, c, ctx, c_ctx, w_ada, b_ada, norm_mix_pre, norm_mix_post, norm_ffn_pre, norm_ffn_post, w_in, rwkv_mu, rwkv_w0, rwkv_w2, rwkv_a0, rwkv_a2, rwkv_k_k, rwkv_k_a, rwkv_r_k, rwkv_lnx_w, rwkv_lnx_b, rwkv_g2, w_rwkv_proj, mla_q_norm, mla_kv_norm, mla_w_uq, mla_w_ukv, w_mla_proj, gate_b, w_out, ffn_w_gate, ffn_w_val, ffn_conv_w, ffn_conv_b, ffn_w_down):
    B = x.shape[0]
    D = D_MODEL
    C = RWKV_DIM
    l = 0

    s = jnp.concatenate([jax.nn.silu(c), jax.nn.silu(c_ctx)[None, :]], axis=0)
    mods = _ada(s, w_ada[l], b_ada[l])
    lat = mods[:B].reshape(B, 6, D)
    cm = mods[B].reshape(6, D)
    sh1, sc1, g1, sh2, sc2, g2 = [lat[:, j] for j in range(6)]
    mods1 = jnp.stack([jnp.stack([sh1, sc1], axis=1),
                       jnp.broadcast_to(jnp.stack([cm[0], cm[1]])[None], (B, 2, D))], axis=1)
    mods2 = jnp.stack([sh2, sc2], axis=1)

    h = _norm_mod(x, ctx, norm_mix_pre[l], mods1)
    wt_main, wt_gate = _pack_w_in(w_in[l].T)
    p = _matmul_nt(h.reshape(B * S_ALL, D), wt_main, tm=2176, tn=1024, name="w_in").reshape(B, S_ALL, P_COLS)
    gates = _gates(h, wt_gate, gate_b[l])

    lane_head = jnp.arange(LANES) // RWKV_HEAD
    ones2 = (lane_head[:, None] == lane_head[None, :]).astype(BF16)
    avg2 = ones2 * (1.0 / RWKV_HEAD)
    r, k, v, kk, lw, a, g, bg = _prep(p, rwkv_mu[l], rwkv_w0[l], _lora_window_weight(rwkv_w2[l], C_WL - C_WL_WIN),
                                      rwkv_a0[l], _lora_window_weight(rwkv_a2[l], C_AL - C_AL_WIN), rwkv_k_k[l],
                                      rwkv_k_a[l], rwkv_r_k[l].reshape(C), rwkv_g2[l].astype(BF16), ones2)
    y_dir = _wkv(r, k, v, kk, lw, a, rwkv_k_a[l])

    cos, sin = _rope_tables()
    uq = mla_w_uq[l].reshape(Q_LORA, MLA_HEADS, QK_NOPE + QK_ROPE)
    wq = jnp.concatenate([uq, _rope_partner(uq[..., QK_NOPE:])], axis=-1).transpose(1, 0, 2).astype(BF16)
    wkv = mla_w_ukv[l].reshape(KV_LORA, MLA_HEADS, QK_NOPE + V_HEAD).transpose(1, 0, 2).astype(BF16)
    q = _qproj(p, mla_q_norm[l], wq, cos[:SEQ], sin[:SEQ])
    k_all, v_all = _kvproj(p, mla_kv_norm[l], wkv, cos, sin)
    o, (wg_b, wv_b, wd_b, wrp_b, wmp_b, wo_b) = _attention(
        q, k_all, v_all, (ffn_w_gate[l], ffn_w_val[l], ffn_w_down[l], w_rwkv_proj[l], w_mla_proj[l], w_out[l]))

    merged = _merge(y_dir, g, bg, rwkv_lnx_w[l], rwkv_lnx_b[l], avg2, o, wrp_b, wmp_b, gates)
    x1, h2 = _outproj(merged, wo_b, x, norm_mix_post[l], g1, norm_ffn_pre[l], mods2)
    return _ffn(h2, wg_b, wv_b, wd_b, ffn_conv_w[l], ffn_conv_b[l], x1, g2, norm_ffn_post[l])
```

```python
import functools
import math

import jax
import jax.numpy as jnp
from jax import lax
from jax.experimental import pallas as pl
from jax.experimental.pallas import tpu as pltpu

D_MODEL = 2048
BATCH = 2
SEQ = 4096
GRID_W = 64
CTX_LEN = 256
S_ALL = SEQ + CTX_LEN
EPS = 1e-6
RWKV_HEADS = 16
RWKV_HEAD = 64
RWKV_DIM = RWKV_HEADS * RWKV_HEAD
DECAY_LORA = 96
AAA_LORA = 96
GATE_LORA = 256
N_DIR = 2
DECAY_SCALE = math.exp(-0.5)
LNX_EPS = 64e-5
MLA_HEADS = 16
Q_LORA = 512
KV_LORA = 512
QK_NOPE = 128
QK_ROPE = 64
V_HEAD = 128
ROPE_THETA = 10000.0
ATTN_SCALE = (QK_NOPE + QK_ROPE) ** -0.5
LOG2_E = math.log2(math.e)
D_FF = 5632
RWKV_IN = 3 * RWKV_DIM + N_DIR * DECAY_LORA + N_DIR * AAA_LORA + GATE_LORA
MLA_IN = Q_LORA + KV_LORA + QK_ROPE

LANES = 128
SUBLANES = 8
VMEM_LIMIT = 56 * 1024 * 1024

C_R, C_K, C_V = 0, RWKV_DIM, 2 * RWKV_DIM
C_WL = 3 * RWKV_DIM
C_AL = C_WL + N_DIR * DECAY_LORA
C_GL = C_AL + N_DIR * AAA_LORA
RWKV_COLS = C_GL + GATE_LORA
C_KPE = RWKV_COLS
C_CQ = -(-(C_KPE + 2 * QK_ROPE) // Q_LORA) * Q_LORA
C_CKV = C_CQ + Q_LORA
P_COLS = C_CKV + KV_LORA
LORA_WIN = 2 * LANES
C_WL_WIN = C_WL // LANES * LANES
C_AL_WIN = C_AL // LANES * LANES

WKV_CHUNK = 64
WKV_SUB = 4
HEAD_PAIRS = RWKV_DIM // LANES

F32 = jnp.float32
BF16 = jnp.bfloat16


def _cparams(sem):
    return pltpu.CompilerParams(dimension_semantics=sem, vmem_limit_bytes=VMEM_LIMIT)


def _bdot(a, b):
    return jnp.dot(a.astype(BF16), b.astype(BF16), preferred_element_type=F32)


def _hilo_dot(x, w):
    hi = x.astype(BF16)
    lo = (x - hi.astype(F32)).astype(BF16)
    return jnp.dot(hi, w, preferred_element_type=F32) + jnp.dot(lo, w, preferred_element_type=F32)


def _head_sum(x, ones2):
    return jnp.concatenate([_hilo_dot(x[:, c * LANES:(c + 1) * LANES], ones2) for c in range(HEAD_PAIRS)], axis=-1)


def _ada_kernel(s_ref, w_ref, b_ref, o_ref):
    w = w_ref[...]
    for r in range(s_ref.shape[0]):
        o_ref[r:r + 1, :] = jnp.sum(s_ref[r] * w, axis=0, keepdims=True) + b_ref[...]


def _ada(s, w, b, tn=1024):
    R, K = s.shape
    N = w.shape[1]
    return pl.pallas_call(
        _ada_kernel, grid=(N // tn,),
        in_specs=[pl.BlockSpec((R, K, 1), lambda j: (0, 0, 0)),
                  pl.BlockSpec((K, tn), lambda j: (0, j)),
                  pl.BlockSpec((1, tn), lambda j: (0, j))],
        out_specs=pl.BlockSpec((R, tn), lambda j: (0, j)),
        out_shape=jax.ShapeDtypeStruct((R, N), F32),
        compiler_params=_cparams(("parallel",)), name="ada",
    )(s[:, :, None], w, b[None, :])


def _norm_mod_kernel(x_ref, c_ref, g_ref, m_ref, o_ref, *, lat_tiles):
    def emit(x):
        n = x * lax.rsqrt(jnp.mean(x * x, axis=-1, keepdims=True) + EPS) * g_ref[...]
        m = m_ref[0, 0]
        o_ref[0] = (n * (1.0 + m[1:2]) + m[0:1]).astype(o_ref.dtype)

    is_latent = pl.program_id(1) < lat_tiles

    @pl.when(is_latent)
    def _():
        emit(x_ref[0])

    @pl.when(jnp.logical_not(is_latent))
    def _():
        emit(c_ref[0])


def _norm_mod(x, ctx, gain, mods, tm=256):
    B, _, D = x.shape
    assert ctx.shape[1] == tm
    lat_tiles = SEQ // tm
    return pl.pallas_call(
        functools.partial(_norm_mod_kernel, lat_tiles=lat_tiles), grid=(B, S_ALL // tm),
        in_specs=[pl.BlockSpec((1, tm, D), lambda b, i: (b, jnp.minimum(i, lat_tiles - 1), 0)),
                  pl.BlockSpec((1, tm, D), lambda b, i: (b, 0, 0)),
                  pl.BlockSpec((1, D), lambda b, i: (0, 0)),
                  pl.BlockSpec((1, 1, 2, D), lambda b, i: (b, i // lat_tiles, 0, 0))],
        out_specs=pl.BlockSpec((1, tm, D), lambda b, i: (b, i, 0)),
        out_shape=jax.ShapeDtypeStruct((B, S_ALL, D), BF16),
        compiler_params=_cparams(("parallel", "parallel")), name="norm_mod",
    )(x, ctx, gain[None, :], mods)


def _dot_nt(a, wt):
    return lax.dot_general(a, wt, (((1,), (1,)), ((), ())), preferred_element_type=F32)


def _mm_kernel(a_ref, wt_ref, o_ref):
    o_ref[...] = _dot_nt(a_ref[...], wt_ref[...]).astype(o_ref.dtype)


def _matmul_nt(a, wt, tm, tn, out_dtype=F32, name="matmul"):
    M, K = a.shape
    N = wt.shape[0]
    return pl.pallas_call(
        _mm_kernel, grid=(M // tm, N // tn),
        in_specs=[pl.BlockSpec((tm, K), lambda i, j: (i, 0)),
                  pl.BlockSpec((tn, K), lambda i, j: (j, 0))],
        out_specs=pl.BlockSpec((tm, tn), lambda i, j: (i, j)),
        out_shape=jax.ShapeDtypeStruct((M, N), out_dtype),
        compiler_params=_cparams(("parallel", "parallel")), name=name,
    )(a, wt)


def _prep_kernel(p_ref, pp_ref, pn_ref, mu_ref, w0_ref, w2_ref, a0_ref, a2_ref, kk_ref, ka_ref, rk_ref, g2_ref,
                 ones_ref, r_o, k_o, v_o, kkn_o, lw_o, a_o, g_o, bg_o):
    i = pl.program_id(1)
    x = p_ref[0]
    tm = x.shape[0]
    lat_tiles = SEQ // tm
    first = jnp.logical_or(i == 0, i == lat_tiles)
    last = jnp.logical_or(i == lat_tiles - 1, i == pl.num_programs(1) - 1)
    x_before = jnp.where(first, 0.0, pp_ref[0][SUBLANES - 1:SUBLANES])
    x_after = jnp.where(last, 0.0, pn_ref[0][0:1])
    rows = lax.broadcasted_iota(jnp.int32, (tm, 1), 0)
    prev = jnp.where(rows == 0, x_before, pltpu.roll(x, 1, 0))
    nxt = jnp.where(rows == tm - 1, x_after, pltpu.roll(x, tm - 1, 0))
    xs = x + mu_ref[...] * (0.5 * (prev + nxt) - x)

    C = RWKV_DIM
    r, k, v = xs[:, C_R:C_R + C], xs[:, C_K:C_K + C], xs[:, C_V:C_V + C]
    w_lora = _bdot(jnp.tanh(xs[:, C_WL_WIN:C_WL_WIN + LORA_WIN]), w2_ref[...])
    a_lora = _bdot(xs[:, C_AL_WIN:C_AL_WIN + LORA_WIN], a2_ref[...])
    a_sum = None
    for d in range(N_DIR):
        w_raw = w0_ref[d:d + 1] + w_lora[:, d * C:(d + 1) * C]
        lw_o[0, d] = -DECAY_SCALE * jax.nn.sigmoid(w_raw)
        a_d = jax.nn.sigmoid(a0_ref[d:d + 1] + a_lora[:, d * C:(d + 1) * C])
        a_o[0, d] = a_d
        a_sum = a_d if a_sum is None else a_sum + a_d
    ones2 = ones_ref[...]
    kkx = k * kk_ref[...]
    kkn_o[0] = kkx / jnp.maximum(jnp.sqrt(_head_sum(kkx * kkx, ones2)), 1e-12)
    g = _bdot(jax.nn.sigmoid(xs[:, C_GL:C_GL + GATE_LORA]), g2_ref[...])
    k_bar = k * (1.0 + (0.5 * a_sum - 1.0) * ka_ref[...])
    bonus = _head_sum(r * k_bar * rk_ref[...], ones2) * v
    r_o[0] = r
    k_o[0] = k
    v_o[0] = v
    g_o[0] = g.astype(g_o.dtype)
    bg_o[0] = (bonus * g).astype(bg_o.dtype)


def _prep(p, mu, w0, w2p, a0, a2p, k_k, k_a, r_k, g2, ones2, tm=256):
    B = p.shape[0]
    C = RWKV_DIM
    hb = tm // SUBLANES
    last = S_ALL // SUBLANES - 1
    row = lambda n: pl.BlockSpec((1, n), lambda b, i: (0, 0))
    tok = pl.BlockSpec((1, tm, C), lambda b, i: (b, i, 0))
    tok_dir = pl.BlockSpec((1, N_DIR, tm, C), lambda b, i: (b, 0, i, 0))
    shp = jax.ShapeDtypeStruct((B, S_ALL, C), F32)
    shp_dir = jax.ShapeDtypeStruct((B, N_DIR, S_ALL, C), F32)
    shp_b = jax.ShapeDtypeStruct((B, S_ALL, C), BF16)
    return pl.pallas_call(
        _prep_kernel, grid=(B, S_ALL // tm),
        in_specs=[pl.BlockSpec((1, tm, RWKV_COLS), lambda b, i: (b, i, 0)),
                  pl.BlockSpec((1, SUBLANES, RWKV_COLS), lambda b, i: (b, jnp.maximum(i * hb - 1, 0), 0)),
                  pl.BlockSpec((1, SUBLANES, RWKV_COLS), lambda b, i: (b, jnp.minimum((i + 1) * hb, last), 0)),
                  row(RWKV_COLS),
                  pl.BlockSpec((N_DIR, C), lambda b, i: (0, 0)),
                  pl.BlockSpec((LORA_WIN, N_DIR * C), lambda b, i: (0, 0)),
                  pl.BlockSpec((N_DIR, C), lambda b, i: (0, 0)),
                  pl.BlockSpec((LORA_WIN, N_DIR * C), lambda b, i: (0, 0)),
                  row(C), row(C), row(C),
                  pl.BlockSpec((GATE_LORA, C), lambda b, i: (0, 0)),
                  pl.BlockSpec((LANES, LANES), lambda b, i: (0, 0))],
        out_specs=[tok, tok, tok, tok, tok_dir, tok_dir, tok, tok],
        out_shape=[shp, shp, shp, shp, shp_dir, shp_dir, shp_b, shp_b],
        compiler_params=_cparams(("parallel", "parallel")), name="rwkv_prep",
    )(p, p, p, mu[None, :], w0, w2p, a0, a2p, k_k[None, :], k_a[None, :], r_k[None, :], g2, ones2)


def _wkv_kernel(r_ref, k_ref, v_ref, kk_ref, lw_ref, a_ref, ka_ref, y_ref, s_ref):
    d = pl.program_id(1)
    C = WKV_CHUNK
    C2 = 2 * C
    P = HEAD_PAIRS
    sign = 1 - 2 * d
    fwd = d == 0

    @pl.when(pl.program_id(2) == 0)
    def _():
        s_ref[...] = jnp.zeros_like(s_ref)

    def chunk_tri(n):
        trow = lax.broadcasted_iota(jnp.int32, (n * C, n * C), 0)
        tcol = lax.broadcasted_iota(jnp.int32, (n * C, n * C), 1)
        return jnp.logical_and((trow // C) == (tcol // C), (trow - tcol) * sign >= 0).astype(BF16)

    lane = lax.broadcasted_iota(jnp.int32, (C, LANES), 1)
    head0 = lane < RWKV_HEAD
    zero = jnp.zeros((), F32)

    def expand(x):
        x = x.astype(BF16)
        zb = jnp.zeros((), BF16)
        tiles = [x[c:c + C, j * LANES:(j + 1) * LANES] for c in range(0, x.shape[0], C) for j in range(P)]
        return jnp.stack([jnp.concatenate([jnp.where(head0, t, zb), jnp.where(head0, zb, t)], axis=0)
                          for t in tiles])

    def bmm(x, y):
        return jnp.einsum('hab,hbc->hac', x.astype(BF16), y.astype(BF16), preferred_element_type=F32)

    def bmm_nt(x, y):
        return jnp.einsum('han,hbn->hab', x.astype(BF16), y.astype(BF16), preferred_element_type=F32)

    def bmm_tn(x, y):
        return jnp.einsum('hca,hcb->hab', x.astype(BF16), y.astype(BF16), preferred_element_type=F32)

    row = lax.broadcasted_iota(jnp.int32, (C2, C2), 0)
    col = lax.broadcasted_iota(jnp.int32, (C2, C2), 1)
    same = (row // C) == (col // C)
    lag = (row - col) * sign
    incl = jnp.logical_and(same, lag >= 0)[None]
    strict = jnp.logical_and(same, lag > 0)[None]

    def blk(m):
        return (row // m) == (col // m)

    def half_rows(t, m, second):
        return jnp.concatenate([t[:, g + second * m:g + (second + 1) * m] for g in range(0, C2, 2 * m)], axis=1)

    def state_free(chunk_rows):
        def load(ref, *lead):
            return jnp.concatenate([ref[(*lead, rows, slice(None))] for rows in chunk_rows], axis=0)

        r = load(r_ref, 0)
        v = load(v_ref, 0)
        kk = load(kk_ref, 0)
        lw = load(lw_ref, 0, 0)
        a = load(a_ref, 0, 0)
        k = load(k_ref, 0) * (1.0 + (a - 1.0) * ka_ref[...])
        n = len(chunk_rows)

        cum = _hilo_dot_left(chunk_tri(n), lw)
        tot = jnp.concatenate(
            [jnp.broadcast_to(jnp.where(fwd, cum[(c + 1) * C - 1:(c + 1) * C], cum[c * C:c * C + 1]), (C, cum.shape[1]))
             for c in range(n)], axis=0)
        e_cum = jnp.exp(cum)
        e_neg = jnp.exp(-cum)
        e_rem = jnp.exp(tot - cum)
        b = kk * a
        at = -kk * jnp.exp(cum - lw)
        dec = jnp.exp(tot)
        AT, RT, BT, KT, BD, KD, V = (expand(t) for t in (at, r * e_cum, b * e_neg, k * e_neg, b * e_rem,
                                                         k * e_rem, v))
        AR = jnp.concatenate([AT, RT], axis=1)
        G = bmm_nt(AR, jnp.concatenate([BT, KT], axis=1))
        a_ab, a_ak = G[:, :C2, :C2], G[:, :C2, C2:]
        a_rb, a_rk = G[:, C2:, :C2], G[:, C2:, C2:]

        L = jnp.where(strict, a_ab, zero)
        Lb = jnp.where(blk(8)[None], L, zero)
        L2 = bmm(Lb, Lb)
        L4 = bmm(L2, L2)
        S1 = Lb + L2 + bmm(Lb, L2)
        Nm = S1 + L4 + bmm(L4, S1)
        for m in (8, 16, 32):
            Lo = jnp.where(jnp.logical_and(blk(2 * m), jnp.logical_not(blk(m)))[None], L, zero)
            n_first, n_second = half_rows(Nm, m, 0), half_rows(Nm, m, 1)
            n_late = jnp.where(fwd, n_second, n_first)
            p_late = jnp.where(fwd, half_rows(Lo, m, 1), half_rows(Lo, m, 0)) + bmm(n_late, Lo)
            n_late = n_late + p_late + bmm(p_late, Nm)
            n_first, n_second = jnp.where(fwd, n_first, n_late), jnp.where(fwd, n_late, n_second)
            Nm = jnp.concatenate([piece for g in range(C2 // (2 * m))
                                  for piece in (n_first[:, g * m:(g + 1) * m], n_second[:, g * m:(g + 1) * m])],
                                 axis=1)
        rhs_v = bmm(jnp.where(strict, a_ak, zero), V)
        a_y = jnp.concatenate([jnp.where(incl, a_rb, zero), jnp.where(incl, a_rk, zero)], axis=2).astype(BF16)
        dec3 = jnp.stack([dec[c * C:c * C + 1, j * LANES:(j + 1) * LANES]
                          for c in range(n) for j in range(P)])
        whole = (AR, rhs_v, Nm.astype(BF16), V, a_y, jnp.concatenate([BD, KD], axis=1), dec3)
        return [tuple(t[c * P:(c + 1) * P] for t in whole) for c in range(n)]

    def state_step(part, s0):
        AR, rhs_v, Nm, V, a_y, BKD, dec3 = part
        ars0 = bmm_nt(AR, s0)
        rhs = ars0[:, :C2] + rhs_v
        u = rhs + bmm(Nm, rhs)
        uv = jnp.concatenate([u.astype(BF16), V], axis=1)
        y2 = ars0[:, C2:] + bmm(a_y, uv)
        y = y2[:, :C] + y2[:, C:]
        return jnp.concatenate([y[j] for j in range(P)], axis=-1), s0 * dec3 + bmm_tn(uv, BKD)

    chunk_rows = [pl.ds(pl.multiple_of(jnp.where(fwd, j, WKV_SUB - 1 - j) * C, C), C) for j in range(WKV_SUB)]
    half = WKV_SUB // 2
    s = s_ref[...]
    for group in (chunk_rows[:half], chunk_rows[half:]):
        for rows, part in zip(group, state_free(group)):
            y, s = state_step(part, s)
            y_ref[0, 0, rows, :] = y
    s_ref[...] = s


def _hilo_dot_left(w, x):
    hi = x.astype(BF16)
    lo = (x - hi.astype(F32)).astype(BF16)
    return jnp.dot(w, hi, preferred_element_type=F32) + jnp.dot(w, lo, preferred_element_type=F32)


def _wkv(r, k, v, kk, lw, a, k_a):
    B, S, Cd = r.shape
    R = WKV_SUB * WKV_CHUNK
    n_blocks = S // R
    lat_blocks = SEQ // R

    def bidx(d, i):
        return jnp.where(d == 0, (i + lat_blocks) % n_blocks, n_blocks - 1 - i)

    shared = pl.BlockSpec((1, R, Cd), lambda b, d, i: (b, bidx(d, i), 0))
    perdir = pl.BlockSpec((1, 1, R, Cd), lambda b, d, i: (b, d, bidx(d, i), 0))
    return pl.pallas_call(
        _wkv_kernel, grid=(B, N_DIR, n_blocks),
        in_specs=[shared, shared, shared, shared, perdir, perdir, pl.BlockSpec((1, Cd), lambda b, d, i: (0, 0))],
        out_specs=perdir,
        out_shape=jax.ShapeDtypeStruct((B, N_DIR, S, Cd), F32),
        scratch_shapes=[pltpu.VMEM((HEAD_PAIRS, LANES, LANES), F32)],
        compiler_params=_cparams(("parallel", "parallel", "arbitrary")), name="wkv7",
    )(r, k, v, kk, lw, a, k_a[None, :])


def _rms(x, g):
    return x * lax.rsqrt(jnp.mean(x * x, axis=-1, keepdims=True) + EPS) * g


def _qproj_kernel(p_ref, g_ref, w_ref, cos_ref, sin_ref, q_ref):
    cq = _rms(p_ref[0], g_ref[...]).astype(BF16)
    cs, sn = cos_ref[...], sin_ref[...]
    for h in range(MLA_HEADS):
        y = jnp.dot(cq, w_ref[h], preferred_element_type=F32)
        rope = y[:, QK_NOPE:QK_NOPE + QK_ROPE] * cs + y[:, QK_NOPE + QK_ROPE:] * sn
        q = jnp.concatenate([y[:, :QK_NOPE], rope], axis=-1) * (ATTN_SCALE * LOG2_E)
        q_ref[0, h] = q.astype(q_ref.dtype)


def _qproj(p, gain, wq, cos, sin, tm=1024):
    B = p.shape[0]
    return pl.pallas_call(
        _qproj_kernel, grid=(B, SEQ // tm),
        in_specs=[pl.BlockSpec((1, tm, Q_LORA), lambda b, i: (b, i, C_CQ // Q_LORA)),
                  pl.BlockSpec((1, Q_LORA), lambda b, i: (0, 0)),
                  pl.BlockSpec((MLA_HEADS, Q_LORA, 2 * LANES), lambda b, i: (0, 0, 0)),
                  pl.BlockSpec((tm, QK_ROPE), lambda b, i: (i, 0)),
                  pl.BlockSpec((tm, QK_ROPE), lambda b, i: (i, 0))],
        out_specs=pl.BlockSpec((1, MLA_HEADS, tm, QK_NOPE + QK_ROPE), lambda b, i: (b, 0, i, 0)),
        out_shape=jax.ShapeDtypeStruct((B, MLA_HEADS, SEQ, QK_NOPE + QK_ROPE), BF16),
        compiler_params=_cparams(("parallel", "parallel")), name="mla_q",
    )(p, gain[None, :], wq, cos, sin)


def _kvproj_kernel(p_ref, pe_ref, g_ref, w_ref, cos_ref, sin_ref, k_ref, v_ref):
    ckv = _rms(p_ref[0], g_ref[...]).astype(BF16)
    pe = pe_ref[0]
    kpe = pe[:, :QK_ROPE] * cos_ref[...] + pe[:, QK_ROPE:] * sin_ref[...]
    ones = jnp.ones((ckv.shape[0], V_HEAD), F32)
    for h in range(MLA_HEADS):
        y = jnp.dot(ckv, w_ref[h], preferred_element_type=F32)
        k_ref[0, h] = jnp.concatenate([y[:, :QK_NOPE], kpe], axis=-1).astype(k_ref.dtype)
        v_ref[0, h] = jnp.concatenate([y[:, QK_NOPE:], ones], axis=-1).astype(v_ref.dtype)


def _kvproj(p, gain, wkv, cos, sin, tm=256):
    B = p.shape[0]
    return pl.pallas_call(
        _kvproj_kernel, grid=(B, S_ALL // tm),
        in_specs=[pl.BlockSpec((1, tm, KV_LORA), lambda b, i: (b, i, C_CKV // KV_LORA)),
                  pl.BlockSpec((1, tm, 2 * QK_ROPE), lambda b, i: (b, i, C_KPE // (2 * QK_ROPE))),
                  pl.BlockSpec((1, KV_LORA), lambda b, i: (0, 0)),
                  pl.BlockSpec((MLA_HEADS, KV_LORA, 2 * LANES), lambda b, i: (0, 0, 0)),
                  pl.BlockSpec((tm, QK_ROPE), lambda b, i: (i, 0)),
                  pl.BlockSpec((tm, QK_ROPE), lambda b, i: (i, 0))],
        out_specs=[pl.BlockSpec((1, MLA_HEADS, tm, QK_NOPE + QK_ROPE), lambda b, i: (b, 0, i, 0)),
                   pl.BlockSpec((1, MLA_HEADS, tm, 2 * V_HEAD), lambda b, i: (b, 0, i, 0))],
        out_shape=[jax.ShapeDtypeStruct((B, MLA_HEADS, S_ALL, QK_NOPE + QK_ROPE), BF16),
                   jax.ShapeDtypeStruct((B, MLA_HEADS, S_ALL, 2 * V_HEAD), BF16)],
        compiler_params=_cparams(("parallel", "parallel")), name="mla_kv",
    )(p, p, gain[None, :], wkv, cos, sin)


def _attn_kernel(*refs, tk, n_cast):
    q_ref, k_ref, v_ref = refs[:3]
    cast_in = refs[3:3 + n_cast]
    o_ref = refs[3 + n_cast]
    cast_out = refs[4 + n_cast:4 + 2 * n_cast]
    sa, sb, pa, pb, acc, m_ref, al_ref = refs[4 + 2 * n_cast:]
    for w_ref, wb_ref in zip(cast_in, cast_out):
        wb_ref[...] = w_ref[...].astype(wb_ref.dtype)

    q = q_ref[0, 0]
    n_keys = k_ref.shape[2]
    blocks = [(off, min(tk, n_keys - off)) for off in range(0, n_keys, tk)]
    n_kv = len(blocks)

    def put_scores(s_ref, j):
        off, w = blocks[j]
        kj = k_ref[0, 0, off:off + w, :]
        s_ref[:, :w] = lax.dot_general(q, kj, (((1,), (1,)), ((), ())), preferred_element_type=F32)

    def accumulate(p_ref, j):
        off, w = blocks[j]
        pv = jnp.dot(p_ref[:, :w], v_ref[0, 0, off:off + w, :], preferred_element_type=F32)
        al = al_ref[...]
        for c in range(0, 2 * V_HEAD, LANES):
            acc[:, c:c + LANES] = al * acc[:, c:c + LANES] + pv[:, c:c + LANES]

    def softmax(s_ref, p_ref, j):
        w = blocks[j][1]
        s = s_ref[:, :w]
        m_old = m_ref[...]
        m_new = jnp.maximum(m_old, jnp.max(s, axis=-1, keepdims=True))
        al_ref[...] = jnp.exp2(m_old - m_new)
        m_ref[...] = m_new
        for c in range(0, w, LANES):
            p_ref[:, c:c + LANES] = jnp.exp2(s[:, c:c + LANES] - m_new).astype(p_ref.dtype)

    s_bufs, p_bufs = (sa, sb), (pa, pb)
    put_scores(sa, 0)
    acc[...] = jnp.zeros_like(acc)
    m_ref[...] = jnp.full_like(m_ref, -jnp.inf)
    for j in range(n_kv):
        if j + 1 < n_kv:
            put_scores(s_bufs[(j + 1) % 2], j + 1)
        if j >= 1:
            accumulate(p_bufs[(j - 1) % 2], j - 1)
        softmax(s_bufs[j % 2], p_bufs[j % 2], j)
    accumulate(p_bufs[(n_kv - 1) % 2], n_kv - 1)
    o_ref[0] = (acc[:, :V_HEAD] / acc[:, V_HEAD:]).astype(o_ref.dtype)


BF16_SUBLANES = 16


def _attention(q, k, v, to_cast, tq=1024, tk=1024):
    B, H, T, Dk = q.shape
    S = k.shape[2]
    nt = T // tq
    n_steps = B * H * nt

    def cast_spec(w):
        rows, cols = w.shape
        rb = next(r for r in range(BF16_SUBLANES, rows + 1, BF16_SUBLANES)
                  if rows % r == 0 and rows // r <= n_steps)
        n_blk = rows // rb
        return pl.BlockSpec((rb, cols), lambda b, h, i: (((b * H + h) * nt + i) * n_blk // n_steps, 0))

    cast_specs = [cast_spec(w) for w in to_cast]
    outs = pl.pallas_call(
        functools.partial(_attn_kernel, tk=tk, n_cast=len(to_cast)), grid=(B, H, nt),
        in_specs=[pl.BlockSpec((1, 1, tq, Dk), lambda b, h, i: (b, h, i, 0)),
                  pl.BlockSpec((1, 1, S, Dk), lambda b, h, i: (b, h, 0, 0)),
                  pl.BlockSpec((1, 1, S, 2 * V_HEAD), lambda b, h, i: (b, h, 0, 0))] + cast_specs,
        out_specs=[pl.BlockSpec((1, tq, V_HEAD), lambda b, h, i: (b, i, h))] + cast_specs,
        out_shape=[jax.ShapeDtypeStruct((B, T, H * V_HEAD), BF16)]
                  + [jax.ShapeDtypeStruct(w.shape, BF16) for w in to_cast],
        scratch_shapes=[pltpu.VMEM((tq, tk), F32), pltpu.VMEM((tq, tk), F32),
                        pltpu.VMEM((tq, tk), BF16), pltpu.VMEM((tq, tk), BF16),
                        pltpu.VMEM((tq, 2 * V_HEAD), F32), pltpu.VMEM((tq, LANES), F32), pltpu.VMEM((tq, LANES), F32)],
        compiler_params=_cparams(("arbitrary", "arbitrary", "arbitrary")), name="mla_attn",
    )(q, k, v, *to_cast)
    return outs[0], outs[1:]


def _gates_kernel(h_ref, wt_ref, b_ref, o_ref):
    o_ref[0] = jax.nn.sigmoid(_dot_nt(h_ref[0], wt_ref[...]) + b_ref[...]).astype(o_ref.dtype)


def _gates(h, wt_gate, gate_b, tm=1024, tn=1024):
    B, _, D = h.shape
    N = wt_gate.shape[0]
    return pl.pallas_call(
        _gates_kernel, grid=(B, SEQ // tm, N // tn),
        in_specs=[pl.BlockSpec((1, tm, D), lambda b, i, j: (b, i, 0)),
                  pl.BlockSpec((tn, D), lambda b, i, j: (j, 0)),
                  pl.BlockSpec((1, tn), lambda b, i, j: (0, j))],
        out_specs=pl.BlockSpec((1, tm, tn), lambda b, i, j: (b, i, j)),
        out_shape=jax.ShapeDtypeStruct((B, SEQ, N), BF16),
        compiler_params=_cparams(("parallel", "parallel", "parallel")), name="gates",
    )(h, wt_gate, gate_b[None, :])


MERGE_TN = 512


def _merge_kernel(y0_ref, y1_ref, g_ref, bg_ref, lnw_ref, lnb_ref, avg_ref, o_ref, wr_ref, wm_ref, gate_ref, out_ref):
    avg = avg_ref[...]
    y = y0_ref[0, 0] + y1_ref[0, 0]
    yc = y - _head_sum(y, avg)
    yn = yc * lax.rsqrt(_head_sum(yc * yc, avg) + LNX_EPS) * lnw_ref[...] + lnb_ref[...]
    yr = (yn * g_ref[0].astype(F32) + bg_ref[0].astype(F32)).astype(BF16)
    o = o_ref[0]
    D = D_MODEL
    for n in range(0, D, MERGE_TN):
        gr = gate_ref[0, :, n:n + MERGE_TN].astype(F32)
        gm = gate_ref[0, :, D + n:D + n + MERGE_TN].astype(F32)
        out_ref[0, :, n:n + MERGE_TN] = (
            gr * jnp.dot(yr, wr_ref[:, n:n + MERGE_TN], preferred_element_type=F32)
            + gm * jnp.dot(o, wm_ref[:, n:n + MERGE_TN], preferred_element_type=F32)).astype(out_ref.dtype)


def _merge(y_dir, g, bg, lnx_w, lnx_b, avg2, o, w_rp, w_mp, gates, tm=512):
    B = o.shape[0]
    D = D_MODEL
    C = RWKV_DIM
    tok = pl.BlockSpec((1, tm, C), lambda b, i: (b, i, 0))
    row = lambda n: pl.BlockSpec((1, n), lambda b, i: (0, 0))
    return pl.pallas_call(
        _merge_kernel, grid=(B, SEQ // tm),
        in_specs=[pl.BlockSpec((1, 1, tm, C), lambda b, i: (b, 0, i, 0)),
                  pl.BlockSpec((1, 1, tm, C), lambda b, i: (b, 1, i, 0)),
                  tok, tok, row(C), row(C),
                  pl.BlockSpec((LANES, LANES), lambda b, i: (0, 0)),
                  pl.BlockSpec((1, tm, D), lambda b, i: (b, i, 0)),
                  pl.BlockSpec((C, D), lambda b, i: (0, 0), pipeline_mode=pl.Buffered(1)),
                  pl.BlockSpec((D, D), lambda b, i: (0, 0), pipeline_mode=pl.Buffered(1)),
                  pl.BlockSpec((1, tm, 2 * D), lambda b, i: (b, i, 0))],
        out_specs=pl.BlockSpec((1, tm, D), lambda b, i: (b, i, 0)),
        out_shape=jax.ShapeDtypeStruct((B, SEQ, D), BF16),
        compiler_params=_cparams(("parallel", "parallel")), name="merge",
    )(y_dir, y_dir, g, bg, lnx_w[None, :], lnx_b[None, :], avg2, o, w_rp, w_mp, gates)


def _outproj_kernel(m_ref, w_ref, x_ref, gpost_ref, gate_ref, gpre_ref, mod_ref, x1_ref, h_ref):
    mod = mod_ref[0]
    tm = m_ref.shape[1]
    for rows in (slice(0, tm // 2), slice(tm // 2, tm)):
        out = jnp.dot(m_ref[0, rows], w_ref[...], preferred_element_type=F32)
        x1 = x_ref[0, rows] + gate_ref[0] * _rms(out, gpost_ref[...])
        x1_ref[0, rows] = x1
        h_ref[0, rows] = (_rms(x1, gpre_ref[...]) * (1.0 + mod[1:2]) + mod[0:1]).astype(h_ref.dtype)


def _outproj(merged, w_out, x, g_post, gate, g_pre, mod2, tm=512):
    B = x.shape[0]
    D = D_MODEL
    return pl.pallas_call(
        _outproj_kernel, grid=(B, SEQ // tm),
        in_specs=[pl.BlockSpec((1, tm, D), lambda b, i: (b, i, 0)),
                  pl.BlockSpec((D, D), lambda b, i: (0, 0)),
                  pl.BlockSpec((1, tm, D), lambda b, i: (b, i, 0)),
                  pl.BlockSpec((1, D), lambda b, i: (0, 0)),
                  pl.BlockSpec((1, 1, D), lambda b, i: (b, 0, 0)),
                  pl.BlockSpec((1, D), lambda b, i: (0, 0)),
                  pl.BlockSpec((1, 2, D), lambda b, i: (b, 0, 0))],
        out_specs=[pl.BlockSpec((1, tm, D), lambda b, i: (b, i, 0)),
                   pl.BlockSpec((1, tm, D), lambda b, i: (b, i, 0))],
        out_shape=[jax.ShapeDtypeStruct((B, SEQ, D), F32),
                   jax.ShapeDtypeStruct((B, SEQ, D), BF16)],
        compiler_params=_cparams(("parallel", "parallel")), name="outproj",
    )(merged, w_out, x, g_post[None, :], gate[:, None, :], g_pre[None, :], mod2)


HALO = 16


def _ffn_kernel(h_ref, hp_ref, hn_ref, wg_ref, wv_ref, wd_ref, cw_ref, cb_ref, x_ref, gate_ref, gpost_ref,
                o_ref, acc_ref):
    i = pl.program_id(1)
    f = pl.program_id(2)
    tm = h_ref.shape[1]

    @pl.when(f == 0)
    def _():
        acc_ref[...] = jnp.zeros_like(acc_ref)

    h = h_ref[0]
    wg = wg_ref[...]
    g_ext = jnp.dot(jnp.concatenate([hp_ref[0], h, hn_ref[0]], axis=0), wg, preferred_element_type=F32)
    g = g_ext[HALO:HALO + tm]
    g_prev = g_ext[HALO - 1:HALO]
    g_next = g_ext[HALO + tm:HALO + tm + 1]
    g_prev = jnp.where(i == 0, 0.0, g_prev)
    g_next = jnp.where(i == pl.num_programs(1) - 1, 0.0, g_next)
    rows = lax.broadcasted_iota(jnp.int32, g.shape, 0)
    up = jnp.where(rows == 0, g_prev, pltpu.roll(g, 1, 0))
    dn = jnp.where(rows == tm - 1, g_next, pltpu.roll(g, tm - 1, 0))
    cw = cw_ref[...]
    u = cb_ref[...] + up * cw[0:1] + g * cw[1:2] + dn * cw[2:3]
    val = jnp.dot(h, wv_ref[...], preferred_element_type=F32)
    act = jax.nn.gelu(u, approximate=True) * val
    acc_ref[...] += jnp.dot(act.astype(BF16), wd_ref[...], preferred_element_type=F32)

    @pl.when(f == pl.num_programs(2) - 1)
    def _():
        o_ref[0] = x_ref[0] + gate_ref[0] * _rms(acc_ref[...], gpost_ref[...])


def _ffn(h, wg, wv, wd, cw, cb, x1, gate, g_post, tm=512, tf=512):
    B = h.shape[0]
    D = D_MODEL
    hb = tm // HALO
    last = SEQ // HALO - 1
    return pl.pallas_call(
        _ffn_kernel, grid=(B, SEQ // tm, D_FF // tf),
        in_specs=[pl.BlockSpec((1, tm, D), lambda b, i, f: (b, i, 0)),
                  pl.BlockSpec((1, HALO, D), lambda b, i, f: (b, jnp.maximum(i * hb - 1, 0), 0)),
                  pl.BlockSpec((1, HALO, D), lambda b, i, f: (b, jnp.minimum((i + 1) * hb, last), 0)),
                  pl.BlockSpec((D, tf), lambda b, i, f: (0, f)),
                  pl.BlockSpec((D, tf), lambda b, i, f: (0, f)),
                  pl.BlockSpec((tf, D), lambda b, i, f: (f, 0)),
                  pl.BlockSpec((3, tf), lambda b, i, f: (0, f)),
                  pl.BlockSpec((1, tf), lambda b, i, f: (0, f)),
                  pl.BlockSpec((1, tm, D), lambda b, i, f: (b, i, 0)),
                  pl.BlockSpec((1, 1, D), lambda b, i, f: (b, 0, 0)),
                  pl.BlockSpec((1, D), lambda b, i, f: (0, 0))],
        out_specs=pl.BlockSpec((1, tm, D), lambda b, i, f: (b, i, 0)),
        out_shape=jax.ShapeDtypeStruct((B, SEQ, D), F32),
        scratch_shapes=[pltpu.VMEM((tm, D), F32)],
        compiler_params=_cparams(("parallel", "parallel", "arbitrary")), name="convffn",
    )(h, h, h, wg, wv, wd, cw, cb[None, :], x1, gate[:, None, :], g_post[None, :])


def _rope_partner(w):
    q = QK_ROPE // 4
    return jnp.concatenate([-w[..., q:2 * q], w[..., :q], -w[..., 3 * q:], w[..., 2 * q:3 * q]], axis=-1)


def _pack_w_in_kernel(wt_ref, perm_ref, main_ref, gate_ref):
    latents = RWKV_IN + Q_LORA + KV_LORA
    main_ref[:RWKV_IN] = wt_ref[:RWKV_IN].astype(BF16)
    kpe = wt_ref[latents:latents + QK_ROPE].astype(BF16)
    partner = jnp.dot(perm_ref[...], kpe, preferred_element_type=F32).astype(BF16)
    main_ref[C_KPE:C_KPE + QK_ROPE] = kpe
    main_ref[C_KPE + QK_ROPE:C_KPE + 2 * QK_ROPE] = partner
    main_ref[C_KPE + 2 * QK_ROPE:C_CQ] = jnp.zeros((C_CQ - C_KPE - 2 * QK_ROPE, kpe.shape[1]), BF16)
    main_ref[C_CQ:] = wt_ref[RWKV_IN:latents].astype(BF16)
    gate_ref[...] = wt_ref[RWKV_IN + MLA_IN:].astype(BF16)


def _pack_w_in(wt, tk=256):
    n_in, K = wt.shape
    perm_t = _rope_partner(jnp.eye(QK_ROPE, dtype=F32)).T.astype(BF16)
    n_gate = n_in - RWKV_IN - MLA_IN
    return pl.pallas_call(
        _pack_w_in_kernel, grid=(K // tk,),
        in_specs=[pl.BlockSpec((n_in, tk), lambda i: (0, i)),
                  pl.BlockSpec((QK_ROPE, QK_ROPE), lambda i: (0, 0))],
        out_specs=[pl.BlockSpec((P_COLS, tk), lambda i: (0, i)),
                   pl.BlockSpec((n_gate, tk), lambda i: (0, i))],
        out_shape=[jax.ShapeDtypeStruct((P_COLS, K), BF16), jax.ShapeDtypeStruct((n_gate, K), BF16)],
        compiler_params=_cparams(("parallel",)), name="pack_w_in",
    )(wt, perm_t)


def _lora_window_weight(w2, first_row):
    n_dir, R, C = w2.shape
    out = jnp.zeros((LORA_WIN, n_dir, C), BF16)
    for d in range(n_dir):
        out = out.at[first_row + d * R:first_row + (d + 1) * R, d].set(w2[d].astype(BF16))
    return out.reshape(LORA_WIN, n_dir * C)


def _rope_tables():
    rows = SEQ // GRID_W
    half = QK_ROPE // 2
    freqs = ROPE_THETA ** (-jnp.arange(0, half, 2, dtype=F32) / half)
    ar = jnp.arange(rows, dtype=F32)[:, None] * freqs
    ac = jnp.arange(GRID_W, dtype=F32)[:, None] * freqs
    by_row = lambda t: jnp.repeat(t, GRID_W, axis=0)
    by_col = lambda t: jnp.tile(t, (rows, 1))
    cos = jnp.concatenate([by_row(jnp.cos(ar))] * 2 + [by_col(jnp.cos(ac))] * 2, axis=-1)
    sin = jnp.concatenate([by_row(jnp.sin(ar))] * 2 + [by_col(jnp.sin(ac))] * 2, axis=-1)
    cos_all = jnp.concatenate([cos, jnp.ones((CTX_LEN, QK_ROPE), F32)], axis=0)
    sin_all = jnp.concatenate([sin, jnp.zeros((CTX_LEN, QK_ROPE), F32)], axis=0)
    return cos_all, sin_all


def kernel(x, c, ctx, c_ctx, w_ada, b_ada, norm_mix_pre, norm_mix_post, norm_ffn_pre, norm_ffn_post, w_in, rwkv_mu, rwkv_w0, rwkv_w2, rwkv_a0, rwkv_a2, rwkv_k_k, rwkv_k_a, rwkv_r_k, rwkv_lnx_w, rwkv_lnx_b, rwkv_g2, w_rwkv_proj, mla_q_norm, mla_kv_norm, mla_w_uq, mla_w_ukv, w_mla_proj, gate_b, w_out, ffn_w_gate, ffn_w_val, ffn_conv_w, ffn_conv_b, ffn_w_down):
    B = x.shape[0]
    D = D_MODEL
    C = RWKV_DIM
    l = 0

    s = jnp.concatenate([jax.nn.silu(c), jax.nn.silu(c_ctx)[None, :]], axis=0)
    mods = _ada(s, w_ada[l], b_ada[l])
    lat = mods[:B].reshape(B, 6, D)
    cm = mods[B].reshape(6, D)
    sh1, sc1, g1, sh2, sc2, g2 = [lat[:, j] for j in range(6)]
    mods1 = jnp.stack([jnp.stack([sh1, sc1], axis=1),
                       jnp.broadcast_to(jnp.stack([cm[0], cm[1]])[None], (B, 2, D))], axis=1)
    mods2 = jnp.stack([sh2, sc2], axis=1)

    h = _norm_mod(x, ctx, norm_mix_pre[l], mods1)
    wt_main, wt_gate = _pack_w_in(w_in[l].T)
    p = _matmul_nt(h.reshape(B * S_ALL, D), wt_main, tm=2176, tn=1024, name="w_in").reshape(B, S_ALL, P_COLS)
    gates = _gates(h, wt_gate, gate_b[l])

    lane_head = jnp.arange(LANES) // RWKV_HEAD
    ones2 = (lane_head[:, None] == lane_head[None, :]).astype(BF16)
    avg2 = ones2 * (1.0 / RWKV_HEAD)
    r, k, v, kk, lw, a, g, bg = _prep(p, rwkv_mu[l], rwkv_w0[l], _lora_window_weight(rwkv_w2[l], C_WL - C_WL_WIN),
                                      rwkv_a0[l], _lora_window_weight(rwkv_a2[l], C_AL - C_AL_WIN), rwkv_k_k[l],
                                      rwkv_k_a[l], rwkv_r_k[l].reshape(C), rwkv_g2[l].astype(BF16), ones2)
    y_dir = _wkv(r, k, v, kk, lw, a, rwkv_k_a[l])

    cos, sin = _rope_tables()
    uq = mla_w_uq[l].reshape(Q_LORA, MLA_HEADS, QK_NOPE + QK_ROPE)
    wq = jnp.concatenate([uq, _rope_partner(uq[..., QK_NOPE:])], axis=-1).transpose(1, 0, 2).astype(BF16)
    wkv = mla_w_ukv[l].reshape(KV_LORA, MLA_HEADS, QK_NOPE + V_HEAD).transpose(1, 0, 2).astype(BF16)
    q = _qproj(p, mla_q_norm[l], wq, cos[:SEQ], sin[:SEQ])
    k_all, v_all = _kvproj(p, mla_kv_norm[l], wkv, cos, sin)
    o, (wg_b, wv_b, wd_b, wrp_b, wmp_b, wo_b) = _attention(
        q, k_all, v_all, (ffn_w_gate[l], ffn_w_val[l], ffn_w_down[l], w_rwkv_proj[l], w_mla_proj[l], w_out[l]))

    merged = _merge(y_dir, g, bg, rwkv_lnx_w[l], rwkv_lnx_b[l], avg2, o, wrp_b, wmp_b, gates)
    x1, h2 = _outproj(merged, wo_b, x, norm_mix_post[l], g1, norm_ffn_pre[l], mods2)
    return _ffn(h2, wg_b, wv_b, wd_b, ffn_conv_w[l], ffn_conv_b[l], x1, g2, norm_ffn_post[l])
```

```python
import functools
import math

import jax
import jax.numpy as jnp
from jax import lax
from jax.experimental import pallas as pl
from jax.experimental.pallas import tpu as pltpu

D_MODEL = 2048
BATCH = 2
SEQ = 4096
GRID_W = 64
CTX_LEN = 256
S_ALL = SEQ + CTX_LEN
EPS = 1e-6
RWKV_HEADS = 16
RWKV_HEAD = 64
RWKV_DIM = RWKV_HEADS * RWKV_HEAD
DECAY_LORA = 96
AAA_LORA = 96
GATE_LORA = 256
N_DIR = 2
DECAY_SCALE = math.exp(-0.5)
LNX_EPS = 64e-5
MLA_HEADS = 16
Q_LORA = 512
KV_LORA = 512
QK_NOPE = 128
QK_ROPE = 64
V_HEAD = 128
ROPE_THETA = 10000.0
ATTN_SCALE = (QK_NOPE + QK_ROPE) ** -0.5
LOG2_E = math.log2(math.e)
D_FF = 5632
RWKV_IN = 3 * RWKV_DIM + N_DIR * DECAY_LORA + N_DIR * AAA_LORA + GATE_LORA
MLA_IN = Q_LORA + KV_LORA + QK_ROPE

LANES = 128
SUBLANES = 8
VMEM_LIMIT = 56 * 1024 * 1024

C_R, C_K, C_V = 0, RWKV_DIM, 2 * RWKV_DIM
C_WL = 3 * RWKV_DIM
C_AL = C_WL + N_DIR * DECAY_LORA
C_GL = C_AL + N_DIR * AAA_LORA
RWKV_COLS = C_GL + GATE_LORA
C_KPE = RWKV_COLS
C_CQ = -(-(C_KPE + 2 * QK_ROPE) // Q_LORA) * Q_LORA
C_CKV = C_CQ + Q_LORA
P_COLS = C_CKV + KV_LORA
LORA_WIN = 2 * LANES
C_WL_WIN = C_WL // LANES * LANES
C_AL_WIN = C_AL // LANES * LANES

WKV_CHUNK = 64
WKV_SUB = 4
HEAD_PAIRS = RWKV_DIM // LANES

F32 = jnp.float32
BF16 = jnp.bfloat16


def _cparams(sem):
    return pltpu.CompilerParams(dimension_semantics=sem, vmem_limit_bytes=VMEM_LIMIT)


def _bdot(a, b):
    return jnp.dot(a.astype(BF16), b.astype(BF16), preferred_element_type=F32)


def _hilo_dot(x, w):
    hi = x.astype(BF16)
    lo = (x - hi.astype(F32)).astype(BF16)
    return jnp.dot(hi, w, preferred_element_type=F32) + jnp.dot(lo, w, preferred_element_type=F32)


def _head_sum(x, ones2):
    return jnp.concatenate([_hilo_dot(x[:, c * LANES:(c + 1) * LANES], ones2) for c in range(HEAD_PAIRS)], axis=-1)


def _row_vectors_times(s_ref, w_ref, b_ref, o_ref):
    w = w_ref[...]
    for r in range(s_ref.shape[0]):
        o_ref[r:r + 1, :] = jnp.sum(s_ref[r] * w, axis=0, keepdims=True) + b_ref[...]


def _ada_kernel(s_ref, w_ref, b_ref, o_ref):
    _row_vectors_times(s_ref, w_ref, b_ref, o_ref)


def _ada(s, w, b, n_cols, tn=1024):
    R, K = s.shape
    return pl.pallas_call(
        _ada_kernel, grid=(n_cols // tn,),
        in_specs=[pl.BlockSpec((R, K, 1), lambda j: (0, 0, 0)),
                  pl.BlockSpec((K, tn), lambda j: (0, j)),
                  pl.BlockSpec((1, tn), lambda j: (0, j))],
        out_specs=pl.BlockSpec((R, tn), lambda j: (0, j)),
        out_shape=jax.ShapeDtypeStruct((R, n_cols), F32),
        compiler_params=_cparams(("parallel",)), name="ada",
    )(s[:, :, None], w, b[None, :])


def _norm_mod_kernel(x_ref, c_ref, g_ref, m_ref, o_ref, *, lat_tiles):
    def emit(x):
        n = x * lax.rsqrt(jnp.mean(x * x, axis=-1, keepdims=True) + EPS) * g_ref[...]
        m = m_ref[0, 0]
        o_ref[0] = (n * (1.0 + m[1:2]) + m[0:1]).astype(o_ref.dtype)

    is_latent = pl.program_id(1) < lat_tiles

    @pl.when(is_latent)
    def _():
        emit(x_ref[0])

    @pl.when(jnp.logical_not(is_latent))
    def _():
        emit(c_ref[0])


def _norm_mod(x, ctx, gain, mods, tm=256):
    B, _, D = x.shape
    assert ctx.shape[1] == tm
    lat_tiles = SEQ // tm
    return pl.pallas_call(
        functools.partial(_norm_mod_kernel, lat_tiles=lat_tiles), grid=(B, S_ALL // tm),
        in_specs=[pl.BlockSpec((1, tm, D), lambda b, i: (b, jnp.minimum(i, lat_tiles - 1), 0)),
                  pl.BlockSpec((1, tm, D), lambda b, i: (b, 0, 0)),
                  pl.BlockSpec((1, D), lambda b, i: (0, 0)),
                  pl.BlockSpec((1, 1, 2, D), lambda b, i: (b, i // lat_tiles, 0, 0))],
        out_specs=pl.BlockSpec((1, tm, D), lambda b, i: (b, i, 0)),
        out_shape=jax.ShapeDtypeStruct((B, S_ALL, D), BF16),
        compiler_params=_cparams(("parallel", "parallel")), name="norm_mod",
    )(x, ctx, gain[None, :], mods)


def _dot_nt(a, wt):
    return lax.dot_general(a, wt, (((1,), (1,)), ((), ())), preferred_element_type=F32)


def _mm_kernel(a_ref, wt_ref, s_ref, wa_ref, ba_ref, o_ref, oa_ref):
    o_ref[...] = _dot_nt(a_ref[...], wt_ref[...]).astype(o_ref.dtype)
    _row_vectors_times(s_ref, wa_ref, ba_ref, oa_ref)


def _matmul_nt(a, wt, tm, tn, s, w_ada, b_ada, col0, ta=512, out_dtype=F32, name="matmul"):
    M, K = a.shape
    N = wt.shape[0]
    R, Ka = s.shape
    n_i, n_j = M // tm, N // tn
    n_blk = (w_ada.shape[1] - col0) // ta
    assert n_blk <= n_i * n_j
    ablk = lambda i, j: (i * n_j + j) * n_blk // (n_i * n_j)
    return pl.pallas_call(
        _mm_kernel, grid=(n_i, n_j),
        in_specs=[pl.BlockSpec((tm, K), lambda i, j: (i, 0)),
                  pl.BlockSpec((tn, K), lambda i, j: (j, 0)),
                  pl.BlockSpec((R, Ka, 1), lambda i, j: (0, 0, 0)),
                  pl.BlockSpec((Ka, ta), lambda i, j: (0, col0 // ta + ablk(i, j))),
                  pl.BlockSpec((1, ta), lambda i, j: (0, col0 // ta + ablk(i, j)))],
        out_specs=[pl.BlockSpec((tm, tn), lambda i, j: (i, j)),
                   pl.BlockSpec((R, ta), lambda i, j: (0, ablk(i, j)))],
        out_shape=[jax.ShapeDtypeStruct((M, N), out_dtype), jax.ShapeDtypeStruct((R, n_blk * ta), F32)],
        compiler_params=_cparams(("arbitrary", "arbitrary")), name=name,
    )(a, wt, s[:, :, None], w_ada, b_ada[None, :])


def _prep_kernel(p_ref, pp_ref, pn_ref, mu_ref, w0_ref, w2_ref, a0_ref, a2_ref, kk_ref, ka_ref, rk_ref, g2_ref,
                 ones_ref, r_o, k_o, v_o, kkn_o, lw_o, a_o, g_o, bg_o):
    i = pl.program_id(1)
    x = p_ref[0]
    tm = x.shape[0]
    lat_tiles = SEQ // tm
    first = jnp.logical_or(i == 0, i == lat_tiles)
    last = jnp.logical_or(i == lat_tiles - 1, i == pl.num_programs(1) - 1)
    x_before = jnp.where(first, 0.0, pp_ref[0][SUBLANES - 1:SUBLANES])
    x_after = jnp.where(last, 0.0, pn_ref[0][0:1])
    rows = lax.broadcasted_iota(jnp.int32, (tm, 1), 0)
    prev = jnp.where(rows == 0, x_before, pltpu.roll(x, 1, 0))
    nxt = jnp.where(rows == tm - 1, x_after, pltpu.roll(x, tm - 1, 0))
    xs = x + mu_ref[...] * (0.5 * (prev + nxt) - x)

    C = RWKV_DIM
    r, k, v = xs[:, C_R:C_R + C], xs[:, C_K:C_K + C], xs[:, C_V:C_V + C]
    w_lora = _bdot(jnp.tanh(xs[:, C_WL_WIN:C_WL_WIN + LORA_WIN]), w2_ref[...])
    a_lora = _bdot(xs[:, C_AL_WIN:C_AL_WIN + LORA_WIN], a2_ref[...])
    a_sum = None
    for d in range(N_DIR):
        w_raw = w0_ref[d:d + 1] + w_lora[:, d * C:(d + 1) * C]
        lw_o[0, d] = -DECAY_SCALE * jax.nn.sigmoid(w_raw)
        a_d = jax.nn.sigmoid(a0_ref[d:d + 1] + a_lora[:, d * C:(d + 1) * C])
        a_o[0, d] = a_d
        a_sum = a_d if a_sum is None else a_sum + a_d
    ones2 = ones_ref[...]
    kkx = k * kk_ref[...]
    kkn_o[0] = kkx / jnp.maximum(jnp.sqrt(_head_sum(kkx * kkx, ones2)), 1e-12)
    g = _bdot(jax.nn.sigmoid(xs[:, C_GL:C_GL + GATE_LORA]), g2_ref[...])
    k_bar = k * (1.0 + (0.5 * a_sum - 1.0) * ka_ref[...])
    bonus = _head_sum(r * k_bar * rk_ref[...], ones2) * v
    r_o[0] = r
    k_o[0] = k
    v_o[0] = v
    g_o[0] = g.astype(g_o.dtype)
    bg_o[0] = (bonus * g).astype(bg_o.dtype)


def _prep(p, mu, w0, w2p, a0, a2p, k_k, k_a, r_k, g2, ones2, tm=256):
    B = p.shape[0]
    C = RWKV_DIM
    hb = tm // SUBLANES
    last = S_ALL // SUBLANES - 1
    row = lambda n: pl.BlockSpec((1, n), lambda b, i: (0, 0))
    tok = pl.BlockSpec((1, tm, C), lambda b, i: (b, i, 0))
    tok_dir = pl.BlockSpec((1, N_DIR, tm, C), lambda b, i: (b, 0, i, 0))
    shp = jax.ShapeDtypeStruct((B, S_ALL, C), F32)
    shp_dir = jax.ShapeDtypeStruct((B, N_DIR, S_ALL, C), F32)
    shp_b = jax.ShapeDtypeStruct((B, S_ALL, C), BF16)
    return pl.pallas_call(
        _prep_kernel, grid=(B, S_ALL // tm),
        in_specs=[pl.BlockSpec((1, tm, RWKV_COLS), lambda b, i: (b, i, 0)),
                  pl.BlockSpec((1, SUBLANES, RWKV_COLS), lambda b, i: (b, jnp.maximum(i * hb - 1, 0), 0)),
                  pl.BlockSpec((1, SUBLANES, RWKV_COLS), lambda b, i: (b, jnp.minimum((i + 1) * hb, last), 0)),
                  row(RWKV_COLS),
                  pl.BlockSpec((N_DIR, C), lambda b, i: (0, 0)),
                  pl.BlockSpec((LORA_WIN, N_DIR * C), lambda b, i: (0, 0)),
                  pl.BlockSpec((N_DIR, C), lambda b, i: (0, 0)),
                  pl.BlockSpec((LORA_WIN, N_DIR * C), lambda b, i: (0, 0)),
                  row(C), row(C), row(C),
                  pl.BlockSpec((GATE_LORA, C), lambda b, i: (0, 0)),
                  pl.BlockSpec((LANES, LANES), lambda b, i: (0, 0))],
        out_specs=[tok, tok, tok, tok, tok_dir, tok_dir, tok, tok],
        out_shape=[shp, shp, shp, shp, shp_dir, shp_dir, shp_b, shp_b],
        compiler_params=_cparams(("parallel", "parallel")), name="rwkv_prep",
    )(p, p, p, mu[None, :], w0, w2p, a0, a2p, k_k[None, :], k_a[None, :], r_k[None, :], g2, ones2)


def _wkv_kernel(r_ref, k_ref, v_ref, kk_ref, lw_ref, a_ref, ka_ref, y_ref, s_ref):
    d = pl.program_id(1)
    C = WKV_CHUNK
    C2 = 2 * C
    P = HEAD_PAIRS
    sign = 1 - 2 * d
    fwd = d == 0

    @pl.when(pl.program_id(2) == 0)
    def _():
        s_ref[...] = jnp.zeros_like(s_ref)

    def chunk_tri(n):
        trow = lax.broadcasted_iota(jnp.int32, (n * C, n * C), 0)
        tcol = lax.broadcasted_iota(jnp.int32, (n * C, n * C), 1)
        return jnp.logical_and((trow // C) == (tcol // C), (trow - tcol) * sign >= 0).astype(BF16)

    lane = lax.broadcasted_iota(jnp.int32, (C, LANES), 1)
    head0 = lane < RWKV_HEAD
    zero = jnp.zeros((), F32)

    def expand(x):
        x = x.astype(BF16)
        zb = jnp.zeros((), BF16)
        tiles = [x[c:c + C, j * LANES:(j + 1) * LANES] for c in range(0, x.shape[0], C) for j in range(P)]
        return jnp.stack([jnp.concatenate([jnp.where(head0, t, zb), jnp.where(head0, zb, t)], axis=0)
                          for t in tiles])

    def bmm(x, y):
        return jnp.einsum('hab,hbc->hac', x.astype(BF16), y.astype(BF16), preferred_element_type=F32)

    def bmm_nt(x, y):
        return jnp.einsum('han,hbn->hab', x.astype(BF16), y.astype(BF16), preferred_element_type=F32)

    def bmm_tn(x, y):
        return jnp.einsum('hca,hcb->hab', x.astype(BF16), y.astype(BF16), preferred_element_type=F32)

    row = lax.broadcasted_iota(jnp.int32, (C2, C2), 0)
    col = lax.broadcasted_iota(jnp.int32, (C2, C2), 1)
    same = (row // C) == (col // C)
    lag = (row - col) * sign
    incl = jnp.logical_and(same, lag >= 0)[None]
    strict = jnp.logical_and(same, lag > 0)[None]

    def blk(m):
        return (row // m) == (col // m)

    def half_rows(t, m, second):
        return jnp.concatenate([t[:, g + second * m:g + (second + 1) * m] for g in range(0, C2, 2 * m)], axis=1)

    def state_free(chunk_rows):
        def load(ref, *lead):
            return jnp.concatenate([ref[(*lead, rows, slice(None))] for rows in chunk_rows], axis=0)

        r = load(r_ref, 0)
        v = load(v_ref, 0)
        kk = load(kk_ref, 0)
        lw = load(lw_ref, 0, 0)
        a = load(a_ref, 0, 0)
        k = load(k_ref, 0) * (1.0 + (a - 1.0) * ka_ref[...])
        n = len(chunk_rows)

        cum = _hilo_dot_left(chunk_tri(n), lw)
        tot = jnp.concatenate(
            [jnp.broadcast_to(jnp.where(fwd, cum[(c + 1) * C - 1:(c + 1) * C], cum[c * C:c * C + 1]), (C, cum.shape[1]))
             for c in range(n)], axis=0)
        e_cum = jnp.exp(cum)
        e_neg = jnp.exp(-cum)
        e_rem = jnp.exp(tot - cum)
        b = kk * a
        at = -kk * jnp.exp(cum - lw)
        dec = jnp.exp(tot)
        AT, RT, BT, KT, BD, KD, V = (expand(t) for t in (at, r * e_cum, b * e_neg, k * e_neg, b * e_rem,
                                                         k * e_rem, v))
        AR = jnp.concatenate([AT, RT], axis=1)
        G = bmm_nt(AR, jnp.concatenate([BT, KT], axis=1))
        a_ab, a_ak = G[:, :C2, :C2], G[:, :C2, C2:]
        a_rb, a_rk = G[:, C2:, :C2], G[:, C2:, C2:]

        L = jnp.where(strict, a_ab, zero)
        Lb = jnp.where(blk(8)[None], L, zero)
        L2 = bmm(Lb, Lb)
        L4 = bmm(L2, L2)
        S1 = Lb + L2 + bmm(Lb, L2)
        Nm = S1 + L4 + bmm(L4, S1)
        for m in (8, 16, 32):
            Lo = jnp.where(jnp.logical_and(blk(2 * m), jnp.logical_not(blk(m)))[None], L, zero)
            n_first, n_second = half_rows(Nm, m, 0), half_rows(Nm, m, 1)
            n_late = jnp.where(fwd, n_second, n_first)
            p_late = jnp.where(fwd, half_rows(Lo, m, 1), half_rows(Lo, m, 0)) + bmm(n_late, Lo)
            n_late = n_late + p_late + bmm(p_late, Nm)
            n_first, n_second = jnp.where(fwd, n_first, n_late), jnp.where(fwd, n_late, n_second)
            Nm = jnp.concatenate([piece for g in range(C2 // (2 * m))
                                  for piece in (n_first[:, g * m:(g + 1) * m], n_second[:, g * m:(g + 1) * m])],
                                 axis=1)
        rhs_v = bmm(jnp.where(strict, a_ak, zero), V)
        a_y = jnp.concatenate([jnp.where(incl, a_rb, zero), jnp.where(incl, a_rk, zero)], axis=2).astype(BF16)
        dec3 = jnp.stack([dec[c * C:c * C + 1, j * LANES:(j + 1) * LANES]
                          for c in range(n) for j in range(P)])
        whole = (AR, rhs_v, Nm.astype(BF16), V, a_y, jnp.concatenate([BD, KD], axis=1), dec3)
        return [tuple(t[c * P:(c + 1) * P] for t in whole) for c in range(n)]

    def state_step(part, s0):
        AR, rhs_v, Nm, V, a_y, BKD, dec3 = part
        ars0 = bmm_nt(AR, s0)
        rhs = ars0[:, :C2] + rhs_v
        u = rhs + bmm(Nm, rhs)
        uv = jnp.concatenate([u.astype(BF16), V], axis=1)
        y2 = ars0[:, C2:] + bmm(a_y, uv)
        y = y2[:, :C] + y2[:, C:]
        return jnp.concatenate([y[j] for j in range(P)], axis=-1), s0 * dec3 + bmm_tn(uv, BKD)

    chunk_rows = [pl.ds(pl.multiple_of(jnp.where(fwd, j, WKV_SUB - 1 - j) * C, C), C) for j in range(WKV_SUB)]
    half = WKV_SUB // 2
    s = s_ref[...]
    for group in (chunk_rows[:half], chunk_rows[half:]):
        for rows, part in zip(group, state_free(group)):
            y, s = state_step(part, s)
            y_ref[0, 0, rows, :] = y
    s_ref[...] = s


def _hilo_dot_left(w, x):
    hi = x.astype(BF16)
    lo = (x - hi.astype(F32)).astype(BF16)
    return jnp.dot(w, hi, preferred_element_type=F32) + jnp.dot(w, lo, preferred_element_type=F32)


def _wkv(r, k, v, kk, lw, a, k_a):
    B, S, Cd = r.shape
    R = WKV_SUB * WKV_CHUNK
    n_blocks = S // R
    lat_blocks = SEQ // R

    def bidx(d, i):
        return jnp.where(d == 0, (i + lat_blocks) % n_blocks, n_blocks - 1 - i)

    shared = pl.BlockSpec((1, R, Cd), lambda b, d, i: (b, bidx(d, i), 0))
    perdir = pl.BlockSpec((1, 1, R, Cd), lambda b, d, i: (b, d, bidx(d, i), 0))
    return pl.pallas_call(
        _wkv_kernel, grid=(B, N_DIR, n_blocks),
        in_specs=[shared, shared, shared, shared, perdir, perdir, pl.BlockSpec((1, Cd), lambda b, d, i: (0, 0))],
        out_specs=perdir,
        out_shape=jax.ShapeDtypeStruct((B, N_DIR, S, Cd), F32),
        scratch_shapes=[pltpu.VMEM((HEAD_PAIRS, LANES, LANES), F32)],
        compiler_params=_cparams(("parallel", "parallel", "arbitrary")), name="wkv7",
    )(r, k, v, kk, lw, a, k_a[None, :])


def _rms(x, g):
    return x * lax.rsqrt(jnp.mean(x * x, axis=-1, keepdims=True) + EPS) * g


def _qproj_kernel(p_ref, g_ref, w_ref, cos_ref, sin_ref, q_ref):
    cq = _rms(p_ref[0], g_ref[...]).astype(BF16)
    cs, sn = cos_ref[...], sin_ref[...]
    for h in range(MLA_HEADS):
        y = jnp.dot(cq, w_ref[h], preferred_element_type=F32)
        rope = y[:, QK_NOPE:QK_NOPE + QK_ROPE] * cs + y[:, QK_NOPE + QK_ROPE:] * sn
        q = jnp.concatenate([y[:, :QK_NOPE], rope], axis=-1) * (ATTN_SCALE * LOG2_E)
        q_ref[0, h] = q.astype(q_ref.dtype)


def _qproj(p, gain, wq, cos, sin, tm=1024):
    B = p.shape[0]
    return pl.pallas_call(
        _qproj_kernel, grid=(B, SEQ // tm),
        in_specs=[pl.BlockSpec((1, tm, Q_LORA), lambda b, i: (b, i, C_CQ // Q_LORA)),
                  pl.BlockSpec((1, Q_LORA), lambda b, i: (0, 0)),
                  pl.BlockSpec((MLA_HEADS, Q_LORA, 2 * LANES), lambda b, i: (0, 0, 0)),
                  pl.BlockSpec((tm, QK_ROPE), lambda b, i: (i, 0)),
                  pl.BlockSpec((tm, QK_ROPE), lambda b, i: (i, 0))],
        out_specs=pl.BlockSpec((1, MLA_HEADS, tm, QK_NOPE + QK_ROPE), lambda b, i: (b, 0, i, 0)),
        out_shape=jax.ShapeDtypeStruct((B, MLA_HEADS, SEQ, QK_NOPE + QK_ROPE), BF16),
        compiler_params=_cparams(("parallel", "parallel")), name="mla_q",
    )(p, gain[None, :], wq, cos, sin)


def _kvproj_kernel(p_ref, pe_ref, g_ref, w_ref, cos_ref, sin_ref, k_ref, v_ref):
    ckv = _rms(p_ref[0], g_ref[...]).astype(BF16)
    pe = pe_ref[0]
    kpe = pe[:, :QK_ROPE] * cos_ref[...] + pe[:, QK_ROPE:] * sin_ref[...]
    ones = jnp.ones((ckv.shape[0], V_HEAD), F32)
    for h in range(MLA_HEADS):
        y = jnp.dot(ckv, w_ref[h], preferred_element_type=F32)
        k_ref[0, h] = jnp.concatenate([y[:, :QK_NOPE], kpe], axis=-1).astype(k_ref.dtype)
        v_ref[0, h] = jnp.concatenate([y[:, QK_NOPE:], ones], axis=-1).astype(v_ref.dtype)


def _kvproj(p, gain, wkv, cos, sin, tm=256):
    B = p.shape[0]
    return pl.pallas_call(
        _kvproj_kernel, grid=(B, S_ALL // tm),
        in_specs=[pl.BlockSpec((1, tm, KV_LORA), lambda b, i: (b, i, C_CKV // KV_LORA)),
                  pl.BlockSpec((1, tm, 2 * QK_ROPE), lambda b, i: (b, i, C_KPE // (2 * QK_ROPE))),
                  pl.BlockSpec((1, KV_LORA), lambda b, i: (0, 0)),
                  pl.BlockSpec((MLA_HEADS, KV_LORA, 2 * LANES), lambda b, i: (0, 0, 0)),
                  pl.BlockSpec((tm, QK_ROPE), lambda b, i: (i, 0)),
                  pl.BlockSpec((tm, QK_ROPE), lambda b, i: (i, 0))],
        out_specs=[pl.BlockSpec((1, MLA_HEADS, tm, QK_NOPE + QK_ROPE), lambda b, i: (b, 0, i, 0)),
                   pl.BlockSpec((1, MLA_HEADS, tm, 2 * V_HEAD), lambda b, i: (b, 0, i, 0))],
        out_shape=[jax.ShapeDtypeStruct((B, MLA_HEADS, S_ALL, QK_NOPE + QK_ROPE), BF16),
                   jax.ShapeDtypeStruct((B, MLA_HEADS, S_ALL, 2 * V_HEAD), BF16)],
        compiler_params=_cparams(("parallel", "parallel")), name="mla_kv",
    )(p, p, gain[None, :], wkv, cos, sin)


def _attn_kernel(*refs, tk, n_cast):
    q_ref, k_ref, v_ref = refs[:3]
    cast_in = refs[3:3 + n_cast]
    o_ref = refs[3 + n_cast]
    cast_out = refs[4 + n_cast:4 + 2 * n_cast]
    sa, sb, pa, pb, acc, m_ref, al_ref = refs[4 + 2 * n_cast:]
    for w_ref, wb_ref in zip(cast_in, cast_out):
        wb_ref[...] = w_ref[...].astype(wb_ref.dtype)

    q = q_ref[0, 0]
    n_keys = k_ref.shape[2]
    blocks = [(off, min(tk, n_keys - off)) for off in range(0, n_keys, tk)]
    n_kv = len(blocks)

    def put_scores(s_ref, j):
        off, w = blocks[j]
        kj = k_ref[0, 0, off:off + w, :]
        s_ref[:, :w] = lax.dot_general(q, kj, (((1,), (1,)), ((), ())), preferred_element_type=F32)

    def accumulate(p_ref, j):
        off, w = blocks[j]
        pv = jnp.dot(p_ref[:, :w], v_ref[0, 0, off:off + w, :], preferred_element_type=F32)
        al = al_ref[...]
        for c in range(0, 2 * V_HEAD, LANES):
            acc[:, c:c + LANES] = al * acc[:, c:c + LANES] + pv[:, c:c + LANES]

    def softmax(s_ref, p_ref, j):
        w = blocks[j][1]
        s = s_ref[:, :w]
        m_old = m_ref[...]
        m_new = jnp.maximum(m_old, jnp.max(s, axis=-1, keepdims=True))
        al_ref[...] = jnp.exp2(m_old - m_new)
        m_ref[...] = m_new
        for c in range(0, w, LANES):
            p_ref[:, c:c + LANES] = jnp.exp2(s[:, c:c + LANES] - m_new).astype(p_ref.dtype)

    s_bufs, p_bufs = (sa, sb), (pa, pb)
    put_scores(sa, 0)
    acc[...] = jnp.zeros_like(acc)
    m_ref[...] = jnp.full_like(m_ref, -jnp.inf)
    for j in range(n_kv):
        if j + 1 < n_kv:
            put_scores(s_bufs[(j + 1) % 2], j + 1)
        if j >= 1:
            accumulate(p_bufs[(j - 1) % 2], j - 1)
        softmax(s_bufs[j % 2], p_bufs[j % 2], j)
    accumulate(p_bufs[(n_kv - 1) % 2], n_kv - 1)
    o_ref[0] = (acc[:, :V_HEAD] / acc[:, V_HEAD:]).astype(o_ref.dtype)


BF16_SUBLANES = 16


def _attention(q, k, v, to_cast, tq=1024, tk=1024):
    B, H, T, Dk = q.shape
    S = k.shape[2]
    nt = T // tq
    n_steps = B * H * nt

    def cast_spec(w):
        rows, cols = w.shape
        rb = next(r for r in range(BF16_SUBLANES, rows + 1, BF16_SUBLANES)
                  if rows % r == 0 and rows // r <= n_steps)
        n_blk = rows // rb
        return pl.BlockSpec((rb, cols), lambda b, h, i: (((b * H + h) * nt + i) * n_blk // n_steps, 0))

    cast_specs = [cast_spec(w) for w in to_cast]
    outs = pl.pallas_call(
        functools.partial(_attn_kernel, tk=tk, n_cast=len(to_cast)), grid=(B, H, nt),
        in_specs=[pl.BlockSpec((1, 1, tq, Dk), lambda b, h, i: (b, h, i, 0)),
                  pl.BlockSpec((1, 1, S, Dk), lambda b, h, i: (b, h, 0, 0)),
                  pl.BlockSpec((1, 1, S, 2 * V_HEAD), lambda b, h, i: (b, h, 0, 0))] + cast_specs,
        out_specs=[pl.BlockSpec((1, tq, V_HEAD), lambda b, h, i: (b, i, h))] + cast_specs,
        out_shape=[jax.ShapeDtypeStruct((B, T, H * V_HEAD), BF16)]
                  + [jax.ShapeDtypeStruct(w.shape, BF16) for w in to_cast],
        scratch_shapes=[pltpu.VMEM((tq, tk), F32), pltpu.VMEM((tq, tk), F32),
                        pltpu.VMEM((tq, tk), BF16), pltpu.VMEM((tq, tk), BF16),
                        pltpu.VMEM((tq, 2 * V_HEAD), F32), pltpu.VMEM((tq, LANES), F32), pltpu.VMEM((tq, LANES), F32)],
        compiler_params=_cparams(("arbitrary", "arbitrary", "arbitrary")), name="mla_attn",
    )(q, k, v, *to_cast)
    return outs[0], outs[1:]


def _gates_kernel(h_ref, wt_ref, b_ref, o_ref):
    o_ref[0] = jax.nn.sigmoid(_dot_nt(h_ref[0], wt_ref[...]) + b_ref[...]).astype(o_ref.dtype)


def _gates(h, wt_gate, gate_b, tm=1024, tn=1024):
    B, _, D = h.shape
    N = wt_gate.shape[0]
    return pl.pallas_call(
        _gates_kernel, grid=(B, SEQ // tm, N // tn),
        in_specs=[pl.BlockSpec((1, tm, D), lambda b, i, j: (b, i, 0)),
                  pl.BlockSpec((tn, D), lambda b, i, j: (j, 0)),
                  pl.BlockSpec((1, tn), lambda b, i, j: (0, j))],
        out_specs=pl.BlockSpec((1, tm, tn), lambda b, i, j: (b, i, j)),
        out_shape=jax.ShapeDtypeStruct((B, SEQ, N), BF16),
        compiler_params=_cparams(("parallel", "parallel", "parallel")), name="gates",
    )(h, wt_gate, gate_b[None, :])


MERGE_TN = 512


def _merge_kernel(y0_ref, y1_ref, g_ref, bg_ref, lnw_ref, lnb_ref, avg_ref, o_ref, wr_ref, wm_ref, gate_ref, out_ref):
    avg = avg_ref[...]
    y = y0_ref[0, 0] + y1_ref[0, 0]
    yc = y - _head_sum(y, avg)
    yn = yc * lax.rsqrt(_head_sum(yc * yc, avg) + LNX_EPS) * lnw_ref[...] + lnb_ref[...]
    yr = (yn * g_ref[0].astype(F32) + bg_ref[0].astype(F32)).astype(BF16)
    o = o_ref[0]
    D = D_MODEL
    for n in range(0, D, MERGE_TN):
        gr = gate_ref[0, :, n:n + MERGE_TN].astype(F32)
        gm = gate_ref[0, :, D + n:D + n + MERGE_TN].astype(F32)
        out_ref[0, :, n:n + MERGE_TN] = (
            gr * jnp.dot(yr, wr_ref[:, n:n + MERGE_TN], preferred_element_type=F32)
            + gm * jnp.dot(o, wm_ref[:, n:n + MERGE_TN], preferred_element_type=F32)).astype(out_ref.dtype)


def _merge(y_dir, g, bg, lnx_w, lnx_b, avg2, o, w_rp, w_mp, gates, tm=512):
    B = o.shape[0]
    D = D_MODEL
    C = RWKV_DIM
    tok = pl.BlockSpec((1, tm, C), lambda b, i: (b, i, 0))
    row = lambda n: pl.BlockSpec((1, n), lambda b, i: (0, 0))
    return pl.pallas_call(
        _merge_kernel, grid=(B, SEQ // tm),
        in_specs=[pl.BlockSpec((1, 1, tm, C), lambda b, i: (b, 0, i, 0)),
                  pl.BlockSpec((1, 1, tm, C), lambda b, i: (b, 1, i, 0)),
                  tok, tok, row(C), row(C),
                  pl.BlockSpec((LANES, LANES), lambda b, i: (0, 0)),
                  pl.BlockSpec((1, tm, D), lambda b, i: (b, i, 0)),
                  pl.BlockSpec((C, D), lambda b, i: (0, 0), pipeline_mode=pl.Buffered(1)),
                  pl.BlockSpec((D, D), lambda b, i: (0, 0), pipeline_mode=pl.Buffered(1)),
                  pl.BlockSpec((1, tm, 2 * D), lambda b, i: (b, i, 0))],
        out_specs=pl.BlockSpec((1, tm, D), lambda b, i: (b, i, 0)),
        out_shape=jax.ShapeDtypeStruct((B, SEQ, D), BF16),
        compiler_params=_cparams(("parallel", "parallel")), name="merge",
    )(y_dir, y_dir, g, bg, lnx_w[None, :], lnx_b[None, :], avg2, o, w_rp, w_mp, gates)


def _outproj_kernel(m_ref, w_ref, x_ref, gpost_ref, gate_ref, gpre_ref, mod_ref, x1_ref, h_ref):
    mod = mod_ref[0]
    tm = m_ref.shape[1]
    for rows in (slice(0, tm // 2), slice(tm // 2, tm)):
        out = jnp.dot(m_ref[0, rows], w_ref[...], preferred_element_type=F32)
        x1 = x_ref[0, rows] + gate_ref[0] * _rms(out, gpost_ref[...])
        x1_ref[0, rows] = x1
        h_ref[0, rows] = (_rms(x1, gpre_ref[...]) * (1.0 + mod[1:2]) + mod[0:1]).astype(h_ref.dtype)


def _outproj(merged, w_out, x, g_post, gate, g_pre, mod2, tm=512):
    B = x.shape[0]
    D = D_MODEL
    return pl.pallas_call(
        _outproj_kernel, grid=(B, SEQ // tm),
        in_specs=[pl.BlockSpec((1, tm, D), lambda b, i: (b, i, 0)),
                  pl.BlockSpec((D, D), lambda b, i: (0, 0)),
                  pl.BlockSpec((1, tm, D), lambda b, i: (b, i, 0)),
                  pl.BlockSpec((1, D), lambda b, i: (0, 0)),
                  pl.BlockSpec((1, 1, D), lambda b, i: (b, 0, 0)),
                  pl.BlockSpec((1, D), lambda b, i: (0, 0)),
                  pl.BlockSpec((1, 2, D), lambda b, i: (b, 0, 0))],
        out_specs=[pl.BlockSpec((1, tm, D), lambda b, i: (b, i, 0)),
                   pl.BlockSpec((1, tm, D), lambda b, i: (b, i, 0))],
        out_shape=[jax.ShapeDtypeStruct((B, SEQ, D), F32),
                   jax.ShapeDtypeStruct((B, SEQ, D), BF16)],
        compiler_params=_cparams(("parallel", "parallel")), name="outproj",
    )(merged, w_out, x, g_post[None, :], gate[:, None, :], g_pre[None, :], mod2)


HALO = 16


def _ffn_kernel(h_ref, hp_ref, hn_ref, wg_ref, wv_ref, wd_ref, cw_ref, cb_ref, x_ref, gate_ref, gpost_ref,
                o_ref, acc_ref):
    i = pl.program_id(1)
    f = pl.program_id(2)
    tm = h_ref.shape[1]

    @pl.when(f == 0)
    def _():
        acc_ref[...] = jnp.zeros_like(acc_ref)

    h = h_ref[0]
    wg = wg_ref[...]
    g_ext = jnp.dot(jnp.concatenate([hp_ref[0], h, hn_ref[0]], axis=0), wg, preferred_element_type=F32)
    g = g_ext[HALO:HALO + tm]
    g_prev = g_ext[HALO - 1:HALO]
    g_next = g_ext[HALO + tm:HALO + tm + 1]
    g_prev = jnp.where(i == 0, 0.0, g_prev)
    g_next = jnp.where(i == pl.num_programs(1) - 1, 0.0, g_next)
    rows = lax.broadcasted_iota(jnp.int32, g.shape, 0)
    up = jnp.where(rows == 0, g_prev, pltpu.roll(g, 1, 0))
    dn = jnp.where(rows == tm - 1, g_next, pltpu.roll(g, tm - 1, 0))
    cw = cw_ref[...]
    u = cb_ref[...] + up * cw[0:1] + g * cw[1:2] + dn * cw[2:3]
    val = jnp.dot(h, wv_ref[...], preferred_element_type=F32)
    act = jax.nn.gelu(u, approximate=True) * val
    acc_ref[...] += jnp.dot(act.astype(BF16), wd_ref[...], preferred_element_type=F32)

    @pl.when(f == pl.num_programs(2) - 1)
    def _():
        o_ref[0] = x_ref[0] + gate_ref[0] * _rms(acc_ref[...], gpost_ref[...])


def _ffn(h, wg, wv, wd, cw, cb, x1, gate, g_post, tm=512, tf=512):
    B = h.shape[0]
    D = D_MODEL
    hb = tm // HALO
    last = SEQ // HALO - 1
    return pl.pallas_call(
        _ffn_kernel, grid=(B, SEQ // tm, D_FF // tf),
        in_specs=[pl.BlockSpec((1, tm, D), lambda b, i, f: (b, i, 0)),
                  pl.BlockSpec((1, HALO, D), lambda b, i, f: (b, jnp.maximum(i * hb - 1, 0), 0)),
                  pl.BlockSpec((1, HALO, D), lambda b, i, f: (b, jnp.minimum((i + 1) * hb, last), 0)),
                  pl.BlockSpec((D, tf), lambda b, i, f: (0, f)),
                  pl.BlockSpec((D, tf), lambda b, i, f: (0, f)),
                  pl.BlockSpec((tf, D), lambda b, i, f: (f, 0)),
                  pl.BlockSpec((3, tf), lambda b, i, f: (0, f)),
                  pl.BlockSpec((1, tf), lambda b, i, f: (0, f)),
                  pl.BlockSpec((1, tm, D), lambda b, i, f: (b, i, 0)),
                  pl.BlockSpec((1, 1, D), lambda b, i, f: (b, 0, 0)),
                  pl.BlockSpec((1, D), lambda b, i, f: (0, 0))],
        out_specs=pl.BlockSpec((1, tm, D), lambda b, i, f: (b, i, 0)),
        out_shape=jax.ShapeDtypeStruct((B, SEQ, D), F32),
        scratch_shapes=[pltpu.VMEM((tm, D), F32)],
        compiler_params=_cparams(("parallel", "parallel", "arbitrary")), name="convffn",
    )(h, h, h, wg, wv, wd, cw, cb[None, :], x1, gate[:, None, :], g_post[None, :])


def _rope_partner(w):
    q = QK_ROPE // 4
    return jnp.concatenate([-w[..., q:2 * q], w[..., :q], -w[..., 3 * q:], w[..., 2 * q:3 * q]], axis=-1)


def _pack_w_in_kernel(wt_ref, perm_ref, main_ref, gate_ref):
    latents = RWKV_IN + Q_LORA + KV_LORA
    main_ref[:RWKV_IN] = wt_ref[:RWKV_IN].astype(BF16)
    kpe = wt_ref[latents:latents + QK_ROPE].astype(BF16)
    partner = jnp.dot(perm_ref[...], kpe, preferred_element_type=F32).astype(BF16)
    main_ref[C_KPE:C_KPE + QK_ROPE] = kpe
    main_ref[C_KPE + QK_ROPE:C_KPE + 2 * QK_ROPE] = partner
    main_ref[C_KPE + 2 * QK_ROPE:C_CQ] = jnp.zeros((C_CQ - C_KPE - 2 * QK_ROPE, kpe.shape[1]), BF16)
    main_ref[C_CQ:] = wt_ref[RWKV_IN:latents].astype(BF16)
    gate_ref[...] = wt_ref[RWKV_IN + MLA_IN:].astype(BF16)


def _pack_w_in(wt, tk=256):
    n_in, K = wt.shape
    perm_t = _rope_partner(jnp.eye(QK_ROPE, dtype=F32)).T.astype(BF16)
    n_gate = n_in - RWKV_IN - MLA_IN
    return pl.pallas_call(
        _pack_w_in_kernel, grid=(K // tk,),
        in_specs=[pl.BlockSpec((n_in, tk), lambda i: (0, i)),
                  pl.BlockSpec((QK_ROPE, QK_ROPE), lambda i: (0, 0))],
        out_specs=[pl.BlockSpec((P_COLS, tk), lambda i: (0, i)),
                   pl.BlockSpec((n_gate, tk), lambda i: (0, i))],
        out_shape=[jax.ShapeDtypeStruct((P_COLS, K), BF16), jax.ShapeDtypeStruct((n_gate, K), BF16)],
        compiler_params=_cparams(("parallel",)), name="pack_w_in",
    )(wt, perm_t)


def _lora_window_weight(w2, first_row):
    n_dir, R, C = w2.shape
    out = jnp.zeros((LORA_WIN, n_dir, C), BF16)
    for d in range(n_dir):
        out = out.at[first_row + d * R:first_row + (d + 1) * R, d].set(w2[d].astype(BF16))
    return out.reshape(LORA_WIN, n_dir * C)


def _rope_tables():
    rows = SEQ // GRID_W
    half = QK_ROPE // 2
    freqs = ROPE_THETA ** (-jnp.arange(0, half, 2, dtype=F32) / half)
    ar = jnp.arange(rows, dtype=F32)[:, None] * freqs
    ac = jnp.arange(GRID_W, dtype=F32)[:, None] * freqs
    by_row = lambda t: jnp.repeat(t, GRID_W, axis=0)
    by_col = lambda t: jnp.tile(t, (rows, 1))
    cos = jnp.concatenate([by_row(jnp.cos(ar))] * 2 + [by_col(jnp.cos(ac))] * 2, axis=-1)
    sin = jnp.concatenate([by_row(jnp.sin(ar))] * 2 + [by_col(jnp.sin(ac))] * 2, axis=-1)
    cos_all = jnp.concatenate([cos, jnp.ones((CTX_LEN, QK_ROPE), F32)], axis=0)
    sin_all = jnp.concatenate([sin, jnp.zeros((CTX_LEN, QK_ROPE), F32)], axis=0)
    return cos_all, sin_all


def kernel(x, c, ctx, c_ctx, w_ada, b_ada, norm_mix_pre, norm_mix_post, norm_ffn_pre, norm_ffn_post, w_in, rwkv_mu, rwkv_w0, rwkv_w2, rwkv_a0, rwkv_a2, rwkv_k_k, rwkv_k_a, rwkv_r_k, rwkv_lnx_w, rwkv_lnx_b, rwkv_g2, w_rwkv_proj, mla_q_norm, mla_kv_norm, mla_w_uq, mla_w_ukv, w_mla_proj, gate_b, w_out, ffn_w_gate, ffn_w_val, ffn_conv_w, ffn_conv_b, ffn_w_down):
    B = x.shape[0]
    D = D_MODEL
    C = RWKV_DIM
    l = 0

    s = jnp.concatenate([jax.nn.silu(c), jax.nn.silu(c_ctx)[None, :]], axis=0)
    early = _ada(s, w_ada[l], b_ada[l], 2 * D)
    sh1, sc1 = early[:B, :D], early[:B, D:]
    mods1 = jnp.stack([jnp.stack([sh1, sc1], axis=1),
                       jnp.broadcast_to(early[B].reshape(2, D)[None], (B, 2, D))], axis=1)

    h = _norm_mod(x, ctx, norm_mix_pre[l], mods1)
    wt_main, wt_gate = _pack_w_in(w_in[l].T)
    p, late = _matmul_nt(h.reshape(B * S_ALL, D), wt_main, 1088, 1024, s, w_ada[l], b_ada[l], 2 * D, name="w_in")
    p = p.reshape(B, S_ALL, P_COLS)
    g1, sh2, sc2, g2 = [late[:B, j * D:(j + 1) * D] for j in range(4)]
    mods2 = jnp.stack([sh2, sc2], axis=1)
    gates = _gates(h, wt_gate, gate_b[l])

    lane_head = jnp.arange(LANES) // RWKV_HEAD
    ones2 = (lane_head[:, None] == lane_head[None, :]).astype(BF16)
    avg2 = ones2 * (1.0 / RWKV_HEAD)
    r, k, v, kk, lw, a, g, bg = _prep(p, rwkv_mu[l], rwkv_w0[l], _lora_window_weight(rwkv_w2[l], C_WL - C_WL_WIN),
                                      rwkv_a0[l], _lora_window_weight(rwkv_a2[l], C_AL - C_AL_WIN), rwkv_k_k[l],
                                      rwkv_k_a[l], rwkv_r_k[l].reshape(C), rwkv_g2[l].astype(BF16), ones2)
    y_dir = _wkv(r, k, v, kk, lw, a, rwkv_k_a[l])

    cos, sin = _rope_tables()
    uq = mla_w_uq[l].reshape(Q_LORA, MLA_HEADS, QK_NOPE + QK_ROPE)
    wq = jnp.concatenate([uq, _rope_partner(uq[..., QK_NOPE:])], axis=-1).transpose(1, 0, 2).astype(BF16)
    wkv = mla_w_ukv[l].reshape(KV_LORA, MLA_HEADS, QK_NOPE + V_HEAD).transpose(1, 0, 2).astype(BF16)
    q = _qproj(p, mla_q_norm[l], wq, cos[:SEQ], sin[:SEQ])
    k_all, v_all = _kvproj(p, mla_kv_norm[l], wkv, cos, sin)
    o, (wg_b, wv_b, wd_b, wrp_b, wmp_b, wo_b) = _attention(
        q, k_all, v_all, (ffn_w_gate[l], ffn_w_val[l], ffn_w_down[l], w_rwkv_proj[l], w_mla_proj[l], w_out[l]))

    merged = _merge(y_dir, g, bg, rwkv_lnx_w[l], rwkv_lnx_b[l], avg2, o, wrp_b, wmp_b, gates)
    x1, h2 = _outproj(merged, wo_b, x, norm_mix_post[l], g1, norm_ffn_pre[l], mods2)
    return _ffn(h2, wg_b, wv_b, wd_b, ffn_conv_w[l], ffn_conv_b[l], x1, g2, norm_ffn_post[l])
```

```python
import functools
import math

import jax
import jax.numpy as jnp
from jax import lax
from jax.experimental import pallas as pl
from jax.experimental.pallas import tpu as pltpu

D_MODEL = 2048
BATCH = 2
SEQ = 4096
GRID_W = 64
CTX_LEN = 256
S_ALL = SEQ + CTX_LEN
EPS = 1e-6
RWKV_HEADS = 16
RWKV_HEAD = 64
RWKV_DIM = RWKV_HEADS * RWKV_HEAD
DECAY_LORA = 96
AAA_LORA = 96
GATE_LORA = 256
N_DIR = 2
DECAY_SCALE = math.exp(-0.5)
LNX_EPS = 64e-5
MLA_HEADS = 16
Q_LORA = 512
KV_LORA = 512
QK_NOPE = 128
QK_ROPE = 64
V_HEAD = 128
ROPE_THETA = 10000.0
ATTN_SCALE = (QK_NOPE + QK_ROPE) ** -0.5
LOG2_E = math.log2(math.e)
D_FF = 5632
RWKV_IN = 3 * RWKV_DIM + N_DIR * DECAY_LORA + N_DIR * AAA_LORA + GATE_LORA
MLA_IN = Q_LORA + KV_LORA + QK_ROPE

LANES = 128
SUBLANES = 8
VMEM_LIMIT = 56 * 1024 * 1024

C_R, C_K, C_V = 0, RWKV_DIM, 2 * RWKV_DIM
C_WL = 3 * RWKV_DIM
C_AL = C_WL + N_DIR * DECAY_LORA
C_GL = C_AL + N_DIR * AAA_LORA
RWKV_COLS = C_GL + GATE_LORA
C_KPE = RWKV_COLS
C_CQ = -(-(C_KPE + 2 * QK_ROPE) // Q_LORA) * Q_LORA
C_CKV = C_CQ + Q_LORA
P_COLS = C_CKV + KV_LORA
LORA_WIN = 2 * LANES
C_WL_WIN = C_WL // LANES * LANES
C_AL_WIN = C_AL // LANES * LANES

WKV_CHUNK = 64
WKV_SUB = 4
HEAD_PAIRS = RWKV_DIM // LANES

F32 = jnp.float32
BF16 = jnp.bfloat16


def _cparams(sem):
    return pltpu.CompilerParams(dimension_semantics=sem, vmem_limit_bytes=VMEM_LIMIT)


def _bdot(a, b):
    return jnp.dot(a.astype(BF16), b.astype(BF16), preferred_element_type=F32)


def _hilo_dot(x, w):
    hi = x.astype(BF16)
    lo = (x - hi.astype(F32)).astype(BF16)
    return jnp.dot(hi, w, preferred_element_type=F32) + jnp.dot(lo, w, preferred_element_type=F32)


def _head_sum(x, ones2):
    return jnp.concatenate([_hilo_dot(x[:, c * LANES:(c + 1) * LANES], ones2) for c in range(HEAD_PAIRS)], axis=-1)


def _ada_kernel(s_ref, w_ref, b_ref, o_ref):
    w = w_ref[...]
    for r in range(s_ref.shape[0]):
        o_ref[r:r + 1, :] = jnp.sum(s_ref[r] * w, axis=0, keepdims=True) + b_ref[...]


def _ada(s, w, b, tn=1024):
    R, K = s.shape
    N = w.shape[1]
    return pl.pallas_call(
        _ada_kernel, grid=(N // tn,),
        in_specs=[pl.BlockSpec((R, K, 1), lambda j: (0, 0, 0)),
                  pl.BlockSpec((K, tn), lambda j: (0, j)),
                  pl.BlockSpec((1, tn), lambda j: (0, j))],
        out_specs=pl.BlockSpec((R, tn), lambda j: (0, j)),
        out_shape=jax.ShapeDtypeStruct((R, N), F32),
        compiler_params=_cparams(("parallel",)), name="ada",
    )(s[:, :, None], w, b[None, :])


def _norm_mod_kernel(x_ref, c_ref, g_ref, m_ref, o_ref, *, lat_tiles):
    def emit(x):
        n = x * lax.rsqrt(jnp.mean(x * x, axis=-1, keepdims=True) + EPS) * g_ref[...]
        m = m_ref[0, 0]
        o_ref[0] = (n * (1.0 + m[1:2]) + m[0:1]).astype(o_ref.dtype)

    is_latent = pl.program_id(1) < lat_tiles

    @pl.when(is_latent)
    def _():
        emit(x_ref[0])

    @pl.when(jnp.logical_not(is_latent))
    def _():
        emit(c_ref[0])


def _norm_mod(x, ctx, gain, mods, tm=256):
    B, _, D = x.shape
    assert ctx.shape[1] == tm
    lat_tiles = SEQ // tm
    return pl.pallas_call(
        functools.partial(_norm_mod_kernel, lat_tiles=lat_tiles), grid=(B, S_ALL // tm),
        in_specs=[pl.BlockSpec((1, tm, D), lambda b, i: (b, jnp.minimum(i, lat_tiles - 1), 0)),
                  pl.BlockSpec((1, tm, D), lambda b, i: (b, 0, 0)),
                  pl.BlockSpec((1, D), lambda b, i: (0, 0)),
                  pl.BlockSpec((1, 1, 2, D), lambda b, i: (b, i // lat_tiles, 0, 0))],
        out_specs=pl.BlockSpec((1, tm, D), lambda b, i: (b, i, 0)),
        out_shape=jax.ShapeDtypeStruct((B, S_ALL, D), BF16),
        compiler_params=_cparams(("parallel", "parallel")), name="norm_mod",
    )(x, ctx, gain[None, :], mods)


def _dot_nt(a, wt):
    return lax.dot_general(a, wt, (((1,), (1,)), ((), ())), preferred_element_type=F32)


def _mm_kernel(a_ref, wt_ref, o_ref):
    o_ref[...] = _dot_nt(a_ref[...], wt_ref[...]).astype(o_ref.dtype)


def _matmul_nt(a, wt, tm, tn, out_dtype=F32, name="matmul"):
    M, K = a.shape
    N = wt.shape[0]
    return pl.pallas_call(
        _mm_kernel, grid=(M // tm, N // tn),
        in_specs=[pl.BlockSpec((tm, K), lambda i, j: (i, 0)),
                  pl.BlockSpec((tn, K), lambda i, j: (j, 0))],
        out_specs=pl.BlockSpec((tm, tn), lambda i, j: (i, j)),
        out_shape=jax.ShapeDtypeStruct((M, N), out_dtype),
        compiler_params=_cparams(("parallel", "parallel")), name=name,
    )(a, wt)


def _prep_kernel(p_ref, pp_ref, pn_ref, mu_ref, w0_ref, w2_ref, a0_ref, a2_ref, kk_ref, ka_ref, rk_ref, g2_ref,
                 ones_ref, r_o, k_o, v_o, kkn_o, lw_o, a_o, g_o, bg_o):
    i = pl.program_id(1)
    x = p_ref[0]
    tm = x.shape[0]
    lat_tiles = SEQ // tm
    first = jnp.logical_or(i == 0, i == lat_tiles)
    last = jnp.logical_or(i == lat_tiles - 1, i == pl.num_programs(1) - 1)
    x_before = jnp.where(first, 0.0, pp_ref[0][SUBLANES - 1:SUBLANES])
    x_after = jnp.where(last, 0.0, pn_ref[0][0:1])
    rows = lax.broadcasted_iota(jnp.int32, (tm, 1), 0)
    prev = jnp.where(rows == 0, x_before, pltpu.roll(x, 1, 0))
    nxt = jnp.where(rows == tm - 1, x_after, pltpu.roll(x, tm - 1, 0))
    xs = x + mu_ref[...] * (0.5 * (prev + nxt) - x)

    C = RWKV_DIM
    r, k, v = xs[:, C_R:C_R + C], xs[:, C_K:C_K + C], xs[:, C_V:C_V + C]
    w_lora = _bdot(jnp.tanh(xs[:, C_WL_WIN:C_WL_WIN + LORA_WIN]), w2_ref[...])
    a_lora = _bdot(xs[:, C_AL_WIN:C_AL_WIN + LORA_WIN], a2_ref[...])
    a_sum = None
    for d in range(N_DIR):
        w_raw = w0_ref[d:d + 1] + w_lora[:, d * C:(d + 1) * C]
        lw_o[0, d] = -DECAY_SCALE * jax.nn.sigmoid(w_raw)
        a_d = jax.nn.sigmoid(a0_ref[d:d + 1] + a_lora[:, d * C:(d + 1) * C])
        a_o[0, d] = a_d
        a_sum = a_d if a_sum is None else a_sum + a_d
    ones2 = ones_ref[...]
    kkx = k * kk_ref[...]
    kkn_o[0] = kkx / jnp.maximum(jnp.sqrt(_head_sum(kkx * kkx, ones2)), 1e-12)
    g = _bdot(jax.nn.sigmoid(xs[:, C_GL:C_GL + GATE_LORA]), g2_ref[...])
    k_bar = k * (1.0 + (0.5 * a_sum - 1.0) * ka_ref[...])
    bonus = _head_sum(r * k_bar * rk_ref[...], ones2) * v
    r_o[0] = r
    k_o[0] = k
    v_o[0] = v
    g_o[0] = g.astype(g_o.dtype)
    bg_o[0] = (bonus * g).astype(bg_o.dtype)


def _prep(p, mu, w0, w2p, a0, a2p, k_k, k_a, r_k, g2, ones2, tm=256):
    B = p.shape[0]
    C = RWKV_DIM
    hb = tm // SUBLANES
    last = S_ALL // SUBLANES - 1
    row = lambda n: pl.BlockSpec((1, n), lambda b, i: (0, 0))
    tok = pl.BlockSpec((1, tm, C), lambda b, i: (b, i, 0))
    tok_dir = pl.BlockSpec((1, N_DIR, tm, C), lambda b, i: (b, 0, i, 0))
    shp = jax.ShapeDtypeStruct((B, S_ALL, C), F32)
    shp_dir = jax.ShapeDtypeStruct((B, N_DIR, S_ALL, C), F32)
    shp_b = jax.ShapeDtypeStruct((B, S_ALL, C), BF16)
    return pl.pallas_call(
        _prep_kernel, grid=(B, S_ALL // tm),
        in_specs=[pl.BlockSpec((1, tm, RWKV_COLS), lambda b, i: (b, i, 0)),
                  pl.BlockSpec((1, SUBLANES, RWKV_COLS), lambda b, i: (b, jnp.maximum(i * hb - 1, 0), 0)),
                  pl.BlockSpec((1, SUBLANES, RWKV_COLS), lambda b, i: (b, jnp.minimum((i + 1) * hb, last), 0)),
                  row(RWKV_COLS),
                  pl.BlockSpec((N_DIR, C), lambda b, i: (0, 0)),
                  pl.BlockSpec((LORA_WIN, N_DIR * C), lambda b, i: (0, 0)),
                  pl.BlockSpec((N_DIR, C), lambda b, i: (0, 0)),
                  pl.BlockSpec((LORA_WIN, N_DIR * C), lambda b, i: (0, 0)),
                  row(C), row(C), row(C),
                  pl.BlockSpec((GATE_LORA, C), lambda b, i: (0, 0)),
                  pl.BlockSpec((LANES, LANES), lambda b, i: (0, 0))],
        out_specs=[tok, tok, tok, tok, tok_dir, tok_dir, tok, tok],
        out_shape=[shp, shp, shp, shp, shp_dir, shp_dir, shp_b, shp_b],
        compiler_params=_cparams(("parallel", "parallel")), name="rwkv_prep",
    )(p, p, p, mu[None, :], w0, w2p, a0, a2p, k_k[None, :], k_a[None, :], r_k[None, :], g2, ones2)


def _wkv_kernel(r_ref, k_ref, v_ref, kk_ref, lw_ref, a_ref, ka_ref, y_ref, s_ref):
    d = pl.program_id(1)
    C = WKV_CHUNK
    C2 = 2 * C
    P = HEAD_PAIRS
    sign = 1 - 2 * d
    fwd = d == 0

    @pl.when(pl.program_id(2) == 0)
    def _():
        s_ref[...] = jnp.zeros_like(s_ref)

    def chunk_tri(n):
        trow = lax.broadcasted_iota(jnp.int32, (n * C, n * C), 0)
        tcol = lax.broadcasted_iota(jnp.int32, (n * C, n * C), 1)
        return jnp.logical_and((trow // C) == (tcol // C), (trow - tcol) * sign >= 0).astype(BF16)

    lane = lax.broadcasted_iota(jnp.int32, (C, LANES), 1)
    head0 = lane < RWKV_HEAD
    zero = jnp.zeros((), F32)

    def expand(x):
        x = x.astype(BF16)
        zb = jnp.zeros((), BF16)
        tiles = [x[c:c + C, j * LANES:(j + 1) * LANES] for c in range(0, x.shape[0], C) for j in range(P)]
        return jnp.stack([jnp.concatenate([jnp.where(head0, t, zb), jnp.where(head0, zb, t)], axis=0)
                          for t in tiles])

    def bmm(x, y):
        return jnp.einsum('hab,hbc->hac', x.astype(BF16), y.astype(BF16), preferred_element_type=F32)

    def bmm_nt(x, y):
        return jnp.einsum('han,hbn->hab', x.astype(BF16), y.astype(BF16), preferred_element_type=F32)

    def bmm_tn(x, y):
        return jnp.einsum('hca,hcb->hab', x.astype(BF16), y.astype(BF16), preferred_element_type=F32)

    row = lax.broadcasted_iota(jnp.int32, (C2, C2), 0)
    col = lax.broadcasted_iota(jnp.int32, (C2, C2), 1)
    same = (row // C) == (col // C)
    lag = (row - col) * sign
    incl = jnp.logical_and(same, lag >= 0)[None]
    strict = jnp.logical_and(same, lag > 0)[None]

    def blk(m):
        return (row // m) == (col // m)

    def half_rows(t, m, second):
        return jnp.concatenate([t[:, g + second * m:g + (second + 1) * m] for g in range(0, C2, 2 * m)], axis=1)

    def state_free(chunk_rows):
        def load(ref, *lead):
            return jnp.concatenate([ref[(*lead, rows, slice(None))] for rows in chunk_rows], axis=0)

        r = load(r_ref, 0)
        v = load(v_ref, 0)
        kk = load(kk_ref, 0)
        lw = load(lw_ref, 0, 0)
        a = load(a_ref, 0, 0)
        k = load(k_ref, 0) * (1.0 + (a - 1.0) * ka_ref[...])
        n = len(chunk_rows)

        cum = _hilo_dot_left(chunk_tri(n), lw)
        tot = jnp.concatenate(
            [jnp.broadcast_to(jnp.where(fwd, cum[(c + 1) * C - 1:(c + 1) * C], cum[c * C:c * C + 1]), (C, cum.shape[1]))
             for c in range(n)], axis=0)
        e_cum = jnp.exp(cum)
        e_neg = jnp.exp(-cum)
        e_rem = jnp.exp(tot - cum)
        b = kk * a
        at = -kk * jnp.exp(cum - lw)
        dec = jnp.exp(tot)
        AT, RT, BT, KT, BD, KD, V = (expand(t) for t in (at, r * e_cum, b * e_neg, k * e_neg, b * e_rem,
                                                         k * e_rem, v))
        AR = jnp.concatenate([AT, RT], axis=1)
        G = bmm_nt(AR, jnp.concatenate([BT, KT], axis=1))
        a_ab, a_ak = G[:, :C2, :C2], G[:, :C2, C2:]
        a_rb, a_rk = G[:, C2:, :C2], G[:, C2:, C2:]

        L = jnp.where(strict, a_ab, zero)
        Lb = jnp.where(blk(8)[None], L, zero)
        L2 = bmm(Lb, Lb)
        L4 = bmm(L2, L2)
        S1 = Lb + L2 + bmm(Lb, L2)
        Nm = S1 + L4 + bmm(L4, S1)
        for m in (8, 16, 32):
            Lo = jnp.where(jnp.logical_and(blk(2 * m), jnp.logical_not(blk(m)))[None], L, zero)
            n_first, n_second = half_rows(Nm, m, 0), half_rows(Nm, m, 1)
            n_late = jnp.where(fwd, n_second, n_first)
            p_late = jnp.where(fwd, half_rows(Lo, m, 1), half_rows(Lo, m, 0)) + bmm(n_late, Lo)
            n_late = n_late + p_late + bmm(p_late, Nm)
            n_first, n_second = jnp.where(fwd, n_first, n_late), jnp.where(fwd, n_late, n_second)
            Nm = jnp.concatenate([piece for g in range(C2 // (2 * m))
                                  for piece in (n_first[:, g * m:(g + 1) * m], n_second[:, g * m:(g + 1) * m])],
                                 axis=1)
        rhs_v = bmm(jnp.where(strict, a_ak, zero), V)
        a_y = jnp.concatenate([jnp.where(incl, a_rb, zero), jnp.where(incl, a_rk, zero)], axis=2).astype(BF16)
        dec3 = jnp.stack([dec[c * C:c * C + 1, j * LANES:(j + 1) * LANES]
                          for c in range(n) for j in range(P)])
        whole = (AR, rhs_v, Nm.astype(BF16), V, a_y, jnp.concatenate([BD, KD], axis=1), dec3)
        return [tuple(t[c * P:(c + 1) * P] for t in whole) for c in range(n)]

    def state_step(part, s0):
        AR, rhs_v, Nm, V, a_y, BKD, dec3 = part
        ars0 = bmm_nt(AR, s0)
        rhs = ars0[:, :C2] + rhs_v
        u = rhs + bmm(Nm, rhs)
        uv = jnp.concatenate([u.astype(BF16), V], axis=1)
        y2 = ars0[:, C2:] + bmm(a_y, uv)
        y = y2[:, :C] + y2[:, C:]
        return jnp.concatenate([y[j] for j in range(P)], axis=-1), s0 * dec3 + bmm_tn(uv, BKD)

    chunk_rows = [pl.ds(pl.multiple_of(jnp.where(fwd, j, WKV_SUB - 1 - j) * C, C), C) for j in range(WKV_SUB)]
    half = WKV_SUB // 2
    s = s_ref[...]
    for group in (chunk_rows[:half], chunk_rows[half:]):
        for rows, part in zip(group, state_free(group)):
            y, s = state_step(part, s)
            y_ref[0, 0, rows, :] = y
    s_ref[...] = s


def _hilo_dot_left(w, x):
    hi = x.astype(BF16)
    lo = (x - hi.astype(F32)).astype(BF16)
    return jnp.dot(w, hi, preferred_element_type=F32) + jnp.dot(w, lo, preferred_element_type=F32)


def _wkv(r, k, v, kk, lw, a, k_a):
    B, S, Cd = r.shape
    R = WKV_SUB * WKV_CHUNK
    n_blocks = S // R
    lat_blocks = SEQ // R

    def bidx(d, i):
        return jnp.where(d == 0, (i + lat_blocks) % n_blocks, n_blocks - 1 - i)

    shared = pl.BlockSpec((1, R, Cd), lambda b, d, i: (b, bidx(d, i), 0))
    perdir = pl.BlockSpec((1, 1, R, Cd), lambda b, d, i: (b, d, bidx(d, i), 0))
    return pl.pallas_call(
        _wkv_kernel, grid=(B, N_DIR, n_blocks),
        in_specs=[shared, shared, shared, shared, perdir, perdir, pl.BlockSpec((1, Cd), lambda b, d, i: (0, 0))],
        out_specs=perdir,
        out_shape=jax.ShapeDtypeStruct((B, N_DIR, S, Cd), F32),
        scratch_shapes=[pltpu.VMEM((HEAD_PAIRS, LANES, LANES), F32)],
        compiler_params=_cparams(("parallel", "parallel", "arbitrary")), name="wkv7",
    )(r, k, v, kk, lw, a, k_a[None, :])


def _rms(x, g):
    return x * lax.rsqrt(jnp.mean(x * x, axis=-1, keepdims=True) + EPS) * g


def _qproj_kernel(p_ref, g_ref, w_ref, cos_ref, sin_ref, q_ref):
    cq = _rms(p_ref[0], g_ref[...]).astype(BF16)
    cs, sn = cos_ref[...], sin_ref[...]
    for h in range(MLA_HEADS):
        y = jnp.dot(cq, w_ref[h], preferred_element_type=F32)
        rope = y[:, QK_NOPE:QK_NOPE + QK_ROPE] * cs + y[:, QK_NOPE + QK_ROPE:] * sn
        q = jnp.concatenate([y[:, :QK_NOPE], rope], axis=-1) * (ATTN_SCALE * LOG2_E)
        q_ref[0, h] = q.astype(q_ref.dtype)


def _qproj(p, gain, wq, cos, sin, tm=1024):
    B = p.shape[0]
    return pl.pallas_call(
        _qproj_kernel, grid=(B, SEQ // tm),
        in_specs=[pl.BlockSpec((1, tm, Q_LORA), lambda b, i: (b, i, C_CQ // Q_LORA)),
                  pl.BlockSpec((1, Q_LORA), lambda b, i: (0, 0)),
                  pl.BlockSpec((MLA_HEADS, Q_LORA, 2 * LANES), lambda b, i: (0, 0, 0)),
                  pl.BlockSpec((tm, QK_ROPE), lambda b, i: (i, 0)),
                  pl.BlockSpec((tm, QK_ROPE), lambda b, i: (i, 0))],
        out_specs=pl.BlockSpec((1, MLA_HEADS, tm, QK_NOPE + QK_ROPE), lambda b, i: (b, 0, i, 0)),
        out_shape=jax.ShapeDtypeStruct((B, MLA_HEADS, SEQ, QK_NOPE + QK_ROPE), BF16),
        compiler_params=_cparams(("parallel", "parallel")), name="mla_q",
    )(p, gain[None, :], wq, cos, sin)


def _kvproj_kernel(p_ref, pe_ref, g_ref, w_ref, cos_ref, sin_ref, k_ref, v_ref):
    ckv = _rms(p_ref[0], g_ref[...]).astype(BF16)
    pe = pe_ref[0]
    kpe = pe[:, :QK_ROPE] * cos_ref[...] + pe[:, QK_ROPE:] * sin_ref[...]
    ones = jnp.ones((ckv.shape[0], V_HEAD), F32)
    for h in range(MLA_HEADS):
        y = jnp.dot(ckv, w_ref[h], preferred_element_type=F32)
        k_ref[0, h] = jnp.concatenate([y[:, :QK_NOPE], kpe], axis=-1).astype(k_ref.dtype)
        v_ref[0, h] = jnp.concatenate([y[:, QK_NOPE:], ones], axis=-1).astype(v_ref.dtype)


def _kvproj(p, gain, wkv, cos, sin, tm=256):
    B = p.shape[0]
    return pl.pallas_call(
        _kvproj_kernel, grid=(B, S_ALL // tm),
        in_specs=[pl.BlockSpec((1, tm, KV_LORA), lambda b, i: (b, i, C_CKV // KV_LORA)),
                  pl.BlockSpec((1, tm, 2 * QK_ROPE), lambda b, i: (b, i, C_KPE // (2 * QK_ROPE))),
                  pl.BlockSpec((1, KV_LORA), lambda b, i: (0, 0)),
                  pl.BlockSpec((MLA_HEADS, KV_LORA, 2 * LANES), lambda b, i: (0, 0, 0)),
                  pl.BlockSpec((tm, QK_ROPE), lambda b, i: (i, 0)),
                  pl.BlockSpec((tm, QK_ROPE), lambda b, i: (i, 0))],
        out_specs=[pl.BlockSpec((1, MLA_HEADS, tm, QK_NOPE + QK_ROPE), lambda b, i: (b, 0, i, 0)),
                   pl.BlockSpec((1, MLA_HEADS, tm, 2 * V_HEAD), lambda b, i: (b, 0, i, 0))],
        out_shape=[jax.ShapeDtypeStruct((B, MLA_HEADS, S_ALL, QK_NOPE + QK_ROPE), BF16),
                   jax.ShapeDtypeStruct((B, MLA_HEADS, S_ALL, 2 * V_HEAD), BF16)],
        compiler_params=_cparams(("parallel", "parallel")), name="mla_kv",
    )(p, p, gain[None, :], wkv, cos, sin)


def _attn_kernel(*refs, tk, n_cast):
    q_ref, k_ref, v_ref = refs[:3]
    cast_in = refs[3:3 + n_cast]
    o_ref = refs[3 + n_cast]
    cast_out = refs[4 + n_cast:4 + 2 * n_cast]
    sa, sb, pa, pb, acc, m_ref, al_ref = refs[4 + 2 * n_cast:]
    for w_ref, wb_ref in zip(cast_in, cast_out):
        wb_ref[...] = w_ref[...].astype(wb_ref.dtype)

    q = q_ref[0, 0]
    n_keys = k_ref.shape[2]
    blocks = [(off, min(tk, n_keys - off)) for off in range(0, n_keys, tk)]
    n_kv = len(blocks)

    def put_scores(s_ref, j):
        off, w = blocks[j]
        kj = k_ref[0, 0, off:off + w, :]
        s_ref[:, :w] = lax.dot_general(q, kj, (((1,), (1,)), ((), ())), preferred_element_type=F32)

    def accumulate(p_ref, j):
        off, w = blocks[j]
        pv = jnp.dot(p_ref[:, :w], v_ref[0, 0, off:off + w, :], preferred_element_type=F32)
        al = al_ref[...]
        for c in range(0, 2 * V_HEAD, LANES):
            acc[:, c:c + LANES] = al * acc[:, c:c + LANES] + pv[:, c:c + LANES]

    def softmax(s_ref, p_ref, j):
        w = blocks[j][1]
        s = s_ref[:, :w]
        m_old = m_ref[...]
        m_new = jnp.maximum(m_old, jnp.max(s, axis=-1, keepdims=True))
        al_ref[...] = jnp.exp2(m_old - m_new)
        m_ref[...] = m_new
        for c in range(0, w, LANES):
            p_ref[:, c:c + LANES] = jnp.exp2(s[:, c:c + LANES] - m_new).astype(p_ref.dtype)

    s_bufs, p_bufs = (sa, sb), (pa, pb)
    put_scores(sa, 0)
    acc[...] = jnp.zeros_like(acc)
    m_ref[...] = jnp.full_like(m_ref, -jnp.inf)
    for j in range(n_kv):
        if j + 1 < n_kv:
            put_scores(s_bufs[(j + 1) % 2], j + 1)
        if j >= 1:
            accumulate(p_bufs[(j - 1) % 2], j - 1)
        softmax(s_bufs[j % 2], p_bufs[j % 2], j)
    accumulate(p_bufs[(n_kv - 1) % 2], n_kv - 1)
    o_ref[0] = (acc[:, :V_HEAD] / acc[:, V_HEAD:]).astype(o_ref.dtype)


BF16_SUBLANES = 16


def _attention(q, k, v, to_cast, tq=1024, tk=2048):
    B, H, T, Dk = q.shape
    S = k.shape[2]
    nt = T // tq
    n_steps = B * H * nt

    def cast_spec(w):
        rows, cols = w.shape
        rb = next(r for r in range(BF16_SUBLANES, rows + 1, BF16_SUBLANES)
                  if rows % r == 0 and rows // r <= n_steps)
        n_blk = rows // rb
        return pl.BlockSpec((rb, cols), lambda b, h, i: (((b * H + h) * nt + i) * n_blk // n_steps, 0))

    cast_specs = [cast_spec(w) for w in to_cast]
    outs = pl.pallas_call(
        functools.partial(_attn_kernel, tk=tk, n_cast=len(to_cast)), grid=(B, H, nt),
        in_specs=[pl.BlockSpec((1, 1, tq, Dk), lambda b, h, i: (b, h, i, 0)),
                  pl.BlockSpec((1, 1, S, Dk), lambda b, h, i: (b, h, 0, 0)),
                  pl.BlockSpec((1, 1, S, 2 * V_HEAD), lambda b, h, i: (b, h, 0, 0))] + cast_specs,
        out_specs=[pl.BlockSpec((1, tq, V_HEAD), lambda b, h, i: (b, i, h))] + cast_specs,
        out_shape=[jax.ShapeDtypeStruct((B, T, H * V_HEAD), BF16)]
                  + [jax.ShapeDtypeStruct(w.shape, BF16) for w in to_cast],
        scratch_shapes=[pltpu.VMEM((tq, tk), F32), pltpu.VMEM((tq, tk), F32),
                        pltpu.VMEM((tq, tk), BF16), pltpu.VMEM((tq, tk), BF16),
                        pltpu.VMEM((tq, 2 * V_HEAD), F32), pltpu.VMEM((tq, LANES), F32), pltpu.VMEM((tq, LANES), F32)],
        compiler_params=_cparams(("arbitrary", "arbitrary", "arbitrary")), name="mla_attn",
    )(q, k, v, *to_cast)
    return outs[0], outs[1:]


def _gates_kernel(h_ref, wt_ref, b_ref, o_ref):
    o_ref[0] = jax.nn.sigmoid(_dot_nt(h_ref[0], wt_ref[...]) + b_ref[...]).astype(o_ref.dtype)


def _gates(h, wt_gate, gate_b, tm=1024, tn=1024):
    B, _, D = h.shape
    N = wt_gate.shape[0]
    return pl.pallas_call(
        _gates_kernel, grid=(B, SEQ // tm, N // tn),
        in_specs=[pl.BlockSpec((1, tm, D), lambda b, i, j: (b, i, 0)),
                  pl.BlockSpec((tn, D), lambda b, i, j: (j, 0)),
                  pl.BlockSpec((1, tn), lambda b, i, j: (0, j))],
        out_specs=pl.BlockSpec((1, tm, tn), lambda b, i, j: (b, i, j)),
        out_shape=jax.ShapeDtypeStruct((B, SEQ, N), BF16),
        compiler_params=_cparams(("parallel", "parallel", "parallel")), name="gates",
    )(h, wt_gate, gate_b[None, :])


MERGE_TN = 512


def _merge_kernel(y0_ref, y1_ref, g_ref, bg_ref, lnw_ref, lnb_ref, avg_ref, o_ref, wr_ref, wm_ref, gate_ref, out_ref):
    avg = avg_ref[...]
    y = y0_ref[0, 0] + y1_ref[0, 0]
    yc = y - _head_sum(y, avg)
    yn = yc * lax.rsqrt(_head_sum(yc * yc, avg) + LNX_EPS) * lnw_ref[...] + lnb_ref[...]
    yr = (yn * g_ref[0].astype(F32) + bg_ref[0].astype(F32)).astype(BF16)
    o = o_ref[0]
    D = D_MODEL
    for n in range(0, D, MERGE_TN):
        gr = gate_ref[0, :, n:n + MERGE_TN].astype(F32)
        gm = gate_ref[0, :, D + n:D + n + MERGE_TN].astype(F32)
        out_ref[0, :, n:n + MERGE_TN] = (
            gr * jnp.dot(yr, wr_ref[:, n:n + MERGE_TN], preferred_element_type=F32)
            + gm * jnp.dot(o, wm_ref[:, n:n + MERGE_TN], preferred_element_type=F32)).astype(out_ref.dtype)


def _merge(y_dir, g, bg, lnx_w, lnx_b, avg2, o, w_rp, w_mp, gates, tm=512):
    B = o.shape[0]
    D = D_MODEL
    C = RWKV_DIM
    tok = pl.BlockSpec((1, tm, C), lambda b, i: (b, i, 0))
    row = lambda n: pl.BlockSpec((1, n), lambda b, i: (0, 0))
    return pl.pallas_call(
        _merge_kernel, grid=(B, SEQ // tm),
        in_specs=[pl.BlockSpec((1, 1, tm, C), lambda b, i: (b, 0, i, 0)),
                  pl.BlockSpec((1, 1, tm, C), lambda b, i: (b, 1, i, 0)),
                  tok, tok, row(C), row(C),
                  pl.BlockSpec((LANES, LANES), lambda b, i: (0, 0)),
                  pl.BlockSpec((1, tm, D), lambda b, i: (b, i, 0)),
                  pl.BlockSpec((C, D), lambda b, i: (0, 0), pipeline_mode=pl.Buffered(1)),
                  pl.BlockSpec((D, D), lambda b, i: (0, 0), pipeline_mode=pl.Buffered(1)),
                  pl.BlockSpec((1, tm, 2 * D), lambda b, i: (b, i, 0))],
        out_specs=pl.BlockSpec((1, tm, D), lambda b, i: (b, i, 0)),
        out_shape=jax.ShapeDtypeStruct((B, SEQ, D), BF16),
        compiler_params=_cparams(("parallel", "parallel")), name="merge",
    )(y_dir, y_dir, g, bg, lnx_w[None, :], lnx_b[None, :], avg2, o, w_rp, w_mp, gates)


def _outproj_kernel(m_ref, w_ref, x_ref, gpost_ref, gate_ref, gpre_ref, mod_ref, x1_ref, h_ref):
    mod = mod_ref[0]
    tm = m_ref.shape[1]
    for rows in (slice(0, tm // 2), slice(tm // 2, tm)):
        out = jnp.dot(m_ref[0, rows], w_ref[...], preferred_element_type=F32)
        x1 = x_ref[0, rows] + gate_ref[0] * _rms(out, gpost_ref[...])
        x1_ref[0, rows] = x1
        h_ref[0, rows] = (_rms(x1, gpre_ref[...]) * (1.0 + mod[1:2]) + mod[0:1]).astype(h_ref.dtype)


def _outproj(merged, w_out, x, g_post, gate, g_pre, mod2, tm=512):
    B = x.shape[0]
    D = D_MODEL
    return pl.pallas_call(
        _outproj_kernel, grid=(B, SEQ // tm),
        in_specs=[pl.BlockSpec((1, tm, D), lambda b, i: (b, i, 0)),
                  pl.BlockSpec((D, D), lambda b, i: (0, 0)),
                  pl.BlockSpec((1, tm, D), lambda b, i: (b, i, 0)),
                  pl.BlockSpec((1, D), lambda b, i: (0, 0)),
                  pl.BlockSpec((1, 1, D), lambda b, i: (b, 0, 0)),
                  pl.BlockSpec((1, D), lambda b, i: (0, 0)),
                  pl.BlockSpec((1, 2, D), lambda b, i: (b, 0, 0))],
        out_specs=[pl.BlockSpec((1, tm, D), lambda b, i: (b, i, 0)),
                   pl.BlockSpec((1, tm, D), lambda b, i: (b, i, 0))],
        out_shape=[jax.ShapeDtypeStruct((B, SEQ, D), F32),
                   jax.ShapeDtypeStruct((B, SEQ, D), BF16)],
        compiler_params=_cparams(("parallel", "parallel")), name="outproj",
    )(merged, w_out, x, g_post[None, :], gate[:, None, :], g_pre[None, :], mod2)


HALO = 16


def _ffn_kernel(h_ref, hp_ref, hn_ref, wg_ref, wv_ref, wd_ref, cw_ref, cb_ref, x_ref, gate_ref, gpost_ref,
                o_ref, acc_ref):
    i = pl.program_id(1)
    f = pl.program_id(2)
    tm = h_ref.shape[1]

    @pl.when(f == 0)
    def _():
        acc_ref[...] = jnp.zeros_like(acc_ref)

    h = h_ref[0]
    wg = wg_ref[...]
    g_ext = jnp.dot(jnp.concatenate([hp_ref[0], h, hn_ref[0]], axis=0), wg, preferred_element_type=F32)
    g = g_ext[HALO:HALO + tm]
    g_prev = g_ext[HALO - 1:HALO]
    g_next = g_ext[HALO + tm:HALO + tm + 1]
    g_prev = jnp.where(i == 0, 0.0, g_prev)
    g_next = jnp.where(i == pl.num_programs(1) - 1, 0.0, g_next)
    rows = lax.broadcasted_iota(jnp.int32, g.shape, 0)
    up = jnp.where(rows == 0, g_prev, pltpu.roll(g, 1, 0))
    dn = jnp.where(rows == tm - 1, g_next, pltpu.roll(g, tm - 1, 0))
    cw = cw_ref[...]
    u = cb_ref[...] + up * cw[0:1] + g * cw[1:2] + dn * cw[2:3]
    val = jnp.dot(h, wv_ref[...], preferred_element_type=F32)
    act = jax.nn.gelu(u, approximate=True) * val
    acc_ref[...] += jnp.dot(act.astype(BF16), wd_ref[...], preferred_element_type=F32)

    @pl.when(f == pl.num_programs(2) - 1)
    def _():
        o_ref[0] = x_ref[0] + gate_ref[0] * _rms(acc_ref[...], gpost_ref[...])


def _ffn(h, wg, wv, wd, cw, cb, x1, gate, g_post, tm=512, tf=512):
    B = h.shape[0]
    D = D_MODEL
    hb = tm // HALO
    last = SEQ // HALO - 1
    return pl.pallas_call(
        _ffn_kernel, grid=(B, SEQ // tm, D_FF // tf),
        in_specs=[pl.BlockSpec((1, tm, D), lambda b, i, f: (b, i, 0)),
                  pl.BlockSpec((1, HALO, D), lambda b, i, f: (b, jnp.maximum(i * hb - 1, 0), 0)),
                  pl.BlockSpec((1, HALO, D), lambda b, i, f: (b, jnp.minimum((i + 1) * hb, last), 0)),
                  pl.BlockSpec((D, tf), lambda b, i, f: (0, f)),
                  pl.BlockSpec((D, tf), lambda b, i, f: (0, f)),
                  pl.BlockSpec((tf, D), lambda b, i, f: (f, 0)),
                  pl.BlockSpec((3, tf), lambda b, i, f: (0, f)),
                  pl.BlockSpec((1, tf), lambda b, i, f: (0, f)),
                  pl.BlockSpec((1, tm, D), lambda b, i, f: (b, i, 0)),
                  pl.BlockSpec((1, 1, D), lambda b, i, f: (b, 0, 0)),
                  pl.BlockSpec((1, D), lambda b, i, f: (0, 0))],
        out_specs=pl.BlockSpec((1, tm, D), lambda b, i, f: (b, i, 0)),
        out_shape=jax.ShapeDtypeStruct((B, SEQ, D), F32),
        scratch_shapes=[pltpu.VMEM((tm, D), F32)],
        compiler_params=_cparams(("parallel", "parallel", "arbitrary")), name="convffn",
    )(h, h, h, wg, wv, wd, cw, cb[None, :], x1, gate[:, None, :], g_post[None, :])


def _rope_partner(w):
    q = QK_ROPE // 4
    return jnp.concatenate([-w[..., q:2 * q], w[..., :q], -w[..., 3 * q:], w[..., 2 * q:3 * q]], axis=-1)


def _pack_w_in_kernel(wt_ref, perm_ref, main_ref, gate_ref):
    latents = RWKV_IN + Q_LORA + KV_LORA
    main_ref[:RWKV_IN] = wt_ref[:RWKV_IN].astype(BF16)
    kpe = wt_ref[latents:latents + QK_ROPE].astype(BF16)
    partner = jnp.dot(perm_ref[...], kpe, preferred_element_type=F32).astype(BF16)
    main_ref[C_KPE:C_KPE + QK_ROPE] = kpe
    main_ref[C_KPE + QK_ROPE:C_KPE + 2 * QK_ROPE] = partner
    main_ref[C_KPE + 2 * QK_ROPE:C_CQ] = jnp.zeros((C_CQ - C_KPE - 2 * QK_ROPE, kpe.shape[1]), BF16)
    main_ref[C_CQ:] = wt_ref[RWKV_IN:latents].astype(BF16)
    gate_ref[...] = wt_ref[RWKV_IN + MLA_IN:].astype(BF16)


def _pack_w_in(wt, tk=256):
    n_in, K = wt.shape
    perm_t = _rope_partner(jnp.eye(QK_ROPE, dtype=F32)).T.astype(BF16)
    n_gate = n_in - RWKV_IN - MLA_IN
    return pl.pallas_call(
        _pack_w_in_kernel, grid=(K // tk,),
        in_specs=[pl.BlockSpec((n_in, tk), lambda i: (0, i)),
                  pl.BlockSpec((QK_ROPE, QK_ROPE), lambda i: (0, 0))],
        out_specs=[pl.BlockSpec((P_COLS, tk), lambda i: (0, i)),
                   pl.BlockSpec((n_gate, tk), lambda i: (0, i))],
        out_shape=[jax.ShapeDtypeStruct((P_COLS, K), BF16), jax.ShapeDtypeStruct((n_gate, K), BF16)],
        compiler_params=_cparams(("parallel",)), name="pack_w_in",
    )(wt, perm_t)


def _lora_window_weight(w2, first_row):
    n_dir, R, C = w2.shape
    out = jnp.zeros((LORA_WIN, n_dir, C), BF16)
    for d in range(n_dir):
        out = out.at[first_row + d * R:first_row + (d + 1) * R, d].set(w2[d].astype(BF16))
    return out.reshape(LORA_WIN, n_dir * C)


def _rope_tables():
    rows = SEQ // GRID_W
    half = QK_ROPE // 2
    freqs = ROPE_THETA ** (-jnp.arange(0, half, 2, dtype=F32) / half)
    ar = jnp.arange(rows, dtype=F32)[:, None] * freqs
    ac = jnp.arange(GRID_W, dtype=F32)[:, None] * freqs
    by_row = lambda t: jnp.repeat(t, GRID_W, axis=0)
    by_col = lambda t: jnp.tile(t, (rows, 1))
    cos = jnp.concatenate([by_row(jnp.cos(ar))] * 2 + [by_col(jnp.cos(ac))] * 2, axis=-1)
    sin = jnp.concatenate([by_row(jnp.sin(ar))] * 2 + [by_col(jnp.sin(ac))] * 2, axis=-1)
    cos_all = jnp.concatenate([cos, jnp.ones((CTX_LEN, QK_ROPE), F32)], axis=0)
    sin_all = jnp.concatenate([sin, jnp.zeros((CTX_LEN, QK_ROPE), F32)], axis=0)
    return cos_all, sin_all


def kernel(x, c, ctx, c_ctx, w_ada, b_ada, norm_mix_pre, norm_mix_post, norm_ffn_pre, norm_ffn_post, w_in, rwkv_mu, rwkv_w0, rwkv_w2, rwkv_a0, rwkv_a2, rwkv_k_k, rwkv_k_a, rwkv_r_k, rwkv_lnx_w, rwkv_lnx_b, rwkv_g2, w_rwkv_proj, mla_q_norm, mla_kv_norm, mla_w_uq, mla_w_ukv, w_mla_proj, gate_b, w_out, ffn_w_gate, ffn_w_val, ffn_conv_w, ffn_conv_b, ffn_w_down):
    B = x.shape[0]
    D = D_MODEL
    C = RWKV_DIM
    l = 0

    s = jnp.concatenate([jax.nn.silu(c), jax.nn.silu(c_ctx)[None, :]], axis=0)
    mods = _ada(s, w_ada[l], b_ada[l])
    lat = mods[:B].reshape(B, 6, D)
    cm = mods[B].reshape(6, D)
    sh1, sc1, g1, sh2, sc2, g2 = [lat[:, j] for j in range(6)]
    mods1 = jnp.stack([jnp.stack([sh1, sc1], axis=1),
                       jnp.broadcast_to(jnp.stack([cm[0], cm[1]])[None], (B, 2, D))], axis=1)
    mods2 = jnp.stack([sh2, sc2], axis=1)

    h = _norm_mod(x, ctx, norm_mix_pre[l], mods1)
    wt_main, wt_gate = _pack_w_in(w_in[l].T)
    p = _matmul_nt(h.reshape(B * S_ALL, D), wt_main, tm=2176, tn=1024, name="w_in").reshape(B, S_ALL, P_COLS)
    gates = _gates(h, wt_gate, gate_b[l])

    lane_head = jnp.arange(LANES) // RWKV_HEAD
    ones2 = (lane_head[:, None] == lane_head[None, :]).astype(BF16)
    avg2 = ones2 * (1.0 / RWKV_HEAD)
    r, k, v, kk, lw, a, g, bg = _prep(p, rwkv_mu[l], rwkv_w0[l], _lora_window_weight(rwkv_w2[l], C_WL - C_WL_WIN),
                                      rwkv_a0[l], _lora_window_weight(rwkv_a2[l], C_AL - C_AL_WIN), rwkv_k_k[l],
                                      rwkv_k_a[l], rwkv_r_k[l].reshape(C), rwkv_g2[l].astype(BF16), ones2)
    y_dir = _wkv(r, k, v, kk, lw, a, rwkv_k_a[l])

    cos, sin = _rope_tables()
    uq = mla_w_uq[l].reshape(Q_LORA, MLA_HEADS, QK_NOPE + QK_ROPE)
    wq = jnp.concatenate([uq, _rope_partner(uq[..., QK_NOPE:])], axis=-1).transpose(1, 0, 2).astype(BF16)
    wkv = mla_w_ukv[l].reshape(KV_LORA, MLA_HEADS, QK_NOPE + V_HEAD).transpose(1, 0, 2).astype(BF16)
    q = _qproj(p, mla_q_norm[l], wq, cos[:SEQ], sin[:SEQ])
    k_all, v_all = _kvproj(p, mla_kv_norm[l], wkv, cos, sin)
    o, (wg_b, wv_b, wd_b, wrp_b, wmp_b, wo_b) = _attention(
        q, k_all, v_all, (ffn_w_gate[l], ffn_w_val[l], ffn_w_down[l], w_rwkv_proj[l], w_mla_proj[l], w_out[l]))

    merged = _merge(y_dir, g, bg, rwkv_lnx_w[l], rwkv_lnx_b[l], avg2, o, wrp_b, wmp_b, gates)
    x1, h2 = _outproj(merged, wo_b, x, norm_mix_post[l], g1, norm_ffn_pre[l], mods2)
    return _ffn(h2, wg_b, wv_b, wd_b, ffn_conv_w[l], ffn_conv_b[l], x1, g2, norm_ffn_post[l])
```

```python
import functools
import math

import jax
import jax.numpy as jnp
from jax import lax
from jax.experimental import pallas as pl
from jax.experimental.pallas import tpu as pltpu

D_MODEL = 2048
BATCH = 2
SEQ = 4096
GRID_W = 64
CTX_LEN = 256
S_ALL = SEQ + CTX_LEN
EPS = 1e-6
RWKV_HEADS = 16
RWKV_HEAD = 64
RWKV_DIM = RWKV_HEADS * RWKV_HEAD
DECAY_LORA = 96
AAA_LORA = 96
GATE_LORA = 256
N_DIR = 2
DECAY_SCALE = math.exp(-0.5)
LNX_EPS = 64e-5
MLA_HEADS = 16
Q_LORA = 512
KV_LORA = 512
QK_NOPE = 128
QK_ROPE = 64
V_HEAD = 128
ROPE_THETA = 10000.0
ATTN_SCALE = (QK_NOPE + QK_ROPE) ** -0.5
LOG2_E = math.log2(math.e)
D_FF = 5632
RWKV_IN = 3 * RWKV_DIM + N_DIR * DECAY_LORA + N_DIR * AAA_LORA + GATE_LORA
MLA_IN = Q_LORA + KV_LORA + QK_ROPE

LANES = 128
SUBLANES = 8
VMEM_LIMIT = 56 * 1024 * 1024

C_R, C_K, C_V = 0, RWKV_DIM, 2 * RWKV_DIM
C_WL = 3 * RWKV_DIM
C_AL = C_WL + N_DIR * DECAY_LORA
C_GL = C_AL + N_DIR * AAA_LORA
RWKV_COLS = C_GL + GATE_LORA
C_KPE = RWKV_COLS
C_CQ = -(-(C_KPE + 2 * QK_ROPE) // Q_LORA) * Q_LORA
C_CKV = C_CQ + Q_LORA
P_COLS = C_CKV + KV_LORA
LORA_WIN = 2 * LANES
C_WL_WIN = C_WL // LANES * LANES
C_AL_WIN = C_AL // LANES * LANES

WKV_CHUNK = 64
WKV_SUB = 4
HEAD_PAIRS = RWKV_DIM // LANES

F32 = jnp.float32
BF16 = jnp.bfloat16


def _cparams(sem):
    return pltpu.CompilerParams(dimension_semantics=sem, vmem_limit_bytes=VMEM_LIMIT)


def _bdot(a, b):
    return jnp.dot(a.astype(BF16), b.astype(BF16), preferred_element_type=F32)


def _hilo_dot(x, w):
    hi = x.astype(BF16)
    lo = (x - hi.astype(F32)).astype(BF16)
    return jnp.dot(hi, w, preferred_element_type=F32) + jnp.dot(lo, w, preferred_element_type=F32)


def _head_sum(x, ones2):
    return jnp.concatenate([_hilo_dot(x[:, c * LANES:(c + 1) * LANES], ones2) for c in range(HEAD_PAIRS)], axis=-1)


def _ada_kernel(s_ref, w_ref, b_ref, o_ref):
    w = w_ref[...]
    for r in range(s_ref.shape[0]):
        o_ref[r:r + 1, :] = jnp.sum(s_ref[r] * w, axis=0, keepdims=True) + b_ref[...]


def _ada(s, w, b, tn=1024):
    R, K = s.shape
    N = w.shape[1]
    return pl.pallas_call(
        _ada_kernel, grid=(N // tn,),
        in_specs=[pl.BlockSpec((R, K, 1), lambda j: (0, 0, 0)),
                  pl.BlockSpec((K, tn), lambda j: (0, j)),
                  pl.BlockSpec((1, tn), lambda j: (0, j))],
        out_specs=pl.BlockSpec((R, tn), lambda j: (0, j)),
        out_shape=jax.ShapeDtypeStruct((R, N), F32),
        compiler_params=_cparams(("parallel",)), name="ada",
    )(s[:, :, None], w, b[None, :])


def _norm_mod_kernel(x_ref, c_ref, g_ref, m_ref, o_ref, *, lat_tiles):
    def emit(x):
        n = x * lax.rsqrt(jnp.mean(x * x, axis=-1, keepdims=True) + EPS) * g_ref[...]
        m = m_ref[0, 0]
        o_ref[0] = (n * (1.0 + m[1:2]) + m[0:1]).astype(o_ref.dtype)

    is_latent = pl.program_id(1) < lat_tiles

    @pl.when(is_latent)
    def _():
        emit(x_ref[0])

    @pl.when(jnp.logical_not(is_latent))
    def _():
        emit(c_ref[0])


def _norm_mod(x, ctx, gain, mods, tm=256):
    B, _, D = x.shape
    assert ctx.shape[1] == tm
    lat_tiles = SEQ // tm
    return pl.pallas_call(
        functools.partial(_norm_mod_kernel, lat_tiles=lat_tiles), grid=(B, S_ALL // tm),
        in_specs=[pl.BlockSpec((1, tm, D), lambda b, i: (b, jnp.minimum(i, lat_tiles - 1), 0)),
                  pl.BlockSpec((1, tm, D), lambda b, i: (b, 0, 0)),
                  pl.BlockSpec((1, D), lambda b, i: (0, 0)),
                  pl.BlockSpec((1, 1, 2, D), lambda b, i: (b, i // lat_tiles, 0, 0))],
        out_specs=pl.BlockSpec((1, tm, D), lambda b, i: (b, i, 0)),
        out_shape=jax.ShapeDtypeStruct((B, S_ALL, D), BF16),
        compiler_params=_cparams(("parallel", "parallel")), name="norm_mod",
    )(x, ctx, gain[None, :], mods)


NORM_BUFS = 3


def _norm_mod_ring_kernel(x_hbm, c_hbm, g_ref, m_ref, o_hbm, xbuf, obuf, in_sem, out_sem, *, tm, lat_tiles, n_batch):
    per = lat_tiles + 1
    n = n_batch * per

    def in_start(t, slot):
        b, i = t // per, t % per

        @pl.when(i < lat_tiles)
        def _():
            pltpu.make_async_copy(x_hbm.at[b, pl.ds(i * tm, tm)], xbuf.at[slot], in_sem.at[slot]).start()

        @pl.when(i == lat_tiles)
        def _():
            pltpu.make_async_copy(c_hbm.at[b], xbuf.at[slot], in_sem.at[slot]).start()

    def in_wait(slot):
        pltpu.make_async_copy(c_hbm.at[0], xbuf.at[slot], in_sem.at[slot]).wait()

    def out_copy(t, slot):
        b, i = t // per, t % per
        return pltpu.make_async_copy(obuf.at[slot], o_hbm.at[b, pl.ds(i * tm, tm)], out_sem.at[slot])

    for s in range(NORM_BUFS - 1):
        in_start(jnp.int32(s), s)

    def body(t, carry):
        slot = t % NORM_BUFS
        ahead = t + NORM_BUFS - 1

        @pl.when(ahead < n)
        def _():
            in_start(ahead, ahead % NORM_BUFS)

        in_wait(slot)
        oslot = t % 2

        @pl.when(t >= 2)
        def _():
            out_copy(t - 2, oslot).wait()

        x = xbuf[slot]
        m = m_ref[t // per, (t % per) // lat_tiles]
        y = x * lax.rsqrt(jnp.mean(x * x, axis=-1, keepdims=True) + EPS) * g_ref[...]
        obuf[oslot] = (y * (1.0 + m[1:2]) + m[0:1]).astype(obuf.dtype)
        out_copy(t, oslot).start()
        return carry

    lax.fori_loop(0, n, body, 0)
    out_copy(n - 2, (n - 2) % 2).wait()
    out_copy(n - 1, (n - 1) % 2).wait()


def _norm_mod_ring(x, ctx, gain, mods, tm=256):
    B, _, D = x.shape
    assert ctx.shape[1] == tm
    lat_tiles = SEQ // tm
    return pl.pallas_call(
        functools.partial(_norm_mod_ring_kernel, tm=tm, lat_tiles=lat_tiles, n_batch=B),
        in_specs=[pl.BlockSpec(memory_space=pl.ANY), pl.BlockSpec(memory_space=pl.ANY),
                  pl.BlockSpec(memory_space=pltpu.VMEM), pl.BlockSpec(memory_space=pltpu.VMEM)],
        out_specs=pl.BlockSpec(memory_space=pl.ANY),
        out_shape=jax.ShapeDtypeStruct((B, S_ALL, D), BF16),
        scratch_shapes=[pltpu.VMEM((NORM_BUFS, tm, D), F32), pltpu.VMEM((2, tm, D), BF16),
                        pltpu.SemaphoreType.DMA((NORM_BUFS,)), pltpu.SemaphoreType.DMA((2,))],
        compiler_params=pltpu.CompilerParams(vmem_limit_bytes=VMEM_LIMIT), name="norm_mod",
    )(x, ctx, gain[None, :], mods)


def _dot_nt(a, wt):
    return lax.dot_general(a, wt, (((1,), (1,)), ((), ())), preferred_element_type=F32)


def _mm_kernel(a_ref, wt_ref, o_ref):
    o_ref[...] = _dot_nt(a_ref[...], wt_ref[...]).astype(o_ref.dtype)


def _matmul_nt(a, wt, tm, tn, out_dtype=F32, name="matmul"):
    M, K = a.shape
    N = wt.shape[0]
    return pl.pallas_call(
        _mm_kernel, grid=(M // tm, N // tn),
        in_specs=[pl.BlockSpec((tm, K), lambda i, j: (i, 0)),
                  pl.BlockSpec((tn, K), lambda i, j: (j, 0))],
        out_specs=pl.BlockSpec((tm, tn), lambda i, j: (i, j)),
        out_shape=jax.ShapeDtypeStruct((M, N), out_dtype),
        compiler_params=_cparams(("parallel", "parallel")), name=name,
    )(a, wt)


def _prep_kernel(p_ref, pp_ref, pn_ref, mu_ref, w0_ref, w2_ref, a0_ref, a2_ref, kk_ref, ka_ref, rk_ref, g2_ref,
                 ones_ref, r_o, k_o, v_o, kkn_o, lw_o, a_o, g_o, bg_o):
    i = pl.program_id(1)
    x = p_ref[0]
    tm = x.shape[0]
    lat_tiles = SEQ // tm
    first = jnp.logical_or(i == 0, i == lat_tiles)
    last = jnp.logical_or(i == lat_tiles - 1, i == pl.num_programs(1) - 1)
    x_before = jnp.where(first, 0.0, pp_ref[0][SUBLANES - 1:SUBLANES])
    x_after = jnp.where(last, 0.0, pn_ref[0][0:1])
    rows = lax.broadcasted_iota(jnp.int32, (tm, 1), 0)
    prev = jnp.where(rows == 0, x_before, pltpu.roll(x, 1, 0))
    nxt = jnp.where(rows == tm - 1, x_after, pltpu.roll(x, tm - 1, 0))
    xs = x + mu_ref[...] * (0.5 * (prev + nxt) - x)

    C = RWKV_DIM
    r, k, v = xs[:, C_R:C_R + C], xs[:, C_K:C_K + C], xs[:, C_V:C_V + C]
    w_lora = _bdot(jnp.tanh(xs[:, C_WL_WIN:C_WL_WIN + LORA_WIN]), w2_ref[...])
    a_lora = _bdot(xs[:, C_AL_WIN:C_AL_WIN + LORA_WIN], a2_ref[...])
    a_sum = None
    for d in range(N_DIR):
        w_raw = w0_ref[d:d + 1] + w_lora[:, d * C:(d + 1) * C]
        lw_o[0, d] = -DECAY_SCALE * jax.nn.sigmoid(w_raw)
        a_d = jax.nn.sigmoid(a0_ref[d:d + 1] + a_lora[:, d * C:(d + 1) * C])
        a_o[0, d] = a_d
        a_sum = a_d if a_sum is None else a_sum + a_d
    ones2 = ones_ref[...]
    kkx = k * kk_ref[...]
    kkn_o[0] = kkx / jnp.maximum(jnp.sqrt(_head_sum(kkx * kkx, ones2)), 1e-12)
    g = _bdot(jax.nn.sigmoid(xs[:, C_GL:C_GL + GATE_LORA]), g2_ref[...])
    k_bar = k * (1.0 + (0.5 * a_sum - 1.0) * ka_ref[...])
    bonus = _head_sum(r * k_bar * rk_ref[...], ones2) * v
    r_o[0] = r
    k_o[0] = k
    v_o[0] = v
    g_o[0] = g.astype(g_o.dtype)
    bg_o[0] = (bonus * g).astype(bg_o.dtype)


def _prep(p, mu, w0, w2p, a0, a2p, k_k, k_a, r_k, g2, ones2, tm=256):
    B = p.shape[0]
    C = RWKV_DIM
    hb = tm // SUBLANES
    last = S_ALL // SUBLANES - 1
    row = lambda n: pl.BlockSpec((1, n), lambda b, i: (0, 0))
    tok = pl.BlockSpec((1, tm, C), lambda b, i: (b, i, 0))
    tok_dir = pl.BlockSpec((1, N_DIR, tm, C), lambda b, i: (b, 0, i, 0))
    shp = jax.ShapeDtypeStruct((B, S_ALL, C), F32)
    shp_dir = jax.ShapeDtypeStruct((B, N_DIR, S_ALL, C), F32)
    shp_b = jax.ShapeDtypeStruct((B, S_ALL, C), BF16)
    return pl.pallas_call(
        _prep_kernel, grid=(B, S_ALL // tm),
        in_specs=[pl.BlockSpec((1, tm, RWKV_COLS), lambda b, i: (b, i, 0)),
                  pl.BlockSpec((1, SUBLANES, RWKV_COLS), lambda b, i: (b, jnp.maximum(i * hb - 1, 0), 0)),
                  pl.BlockSpec((1, SUBLANES, RWKV_COLS), lambda b, i: (b, jnp.minimum((i + 1) * hb, last), 0)),
                  row(RWKV_COLS),
                  pl.BlockSpec((N_DIR, C), lambda b, i: (0, 0)),
                  pl.BlockSpec((LORA_WIN, N_DIR * C), lambda b, i: (0, 0)),
                  pl.BlockSpec((N_DIR, C), lambda b, i: (0, 0)),
                  pl.BlockSpec((LORA_WIN, N_DIR * C), lambda b, i: (0, 0)),
                  row(C), row(C), row(C),
                  pl.BlockSpec((GATE_LORA, C), lambda b, i: (0, 0)),
                  pl.BlockSpec((LANES, LANES), lambda b, i: (0, 0))],
        out_specs=[tok, tok, tok, tok, tok_dir, tok_dir, tok, tok],
        out_shape=[shp, shp, shp, shp, shp_dir, shp_dir, shp_b, shp_b],
        compiler_params=_cparams(("parallel", "parallel")), name="rwkv_prep",
    )(p, p, p, mu[None, :], w0, w2p, a0, a2p, k_k[None, :], k_a[None, :], r_k[None, :], g2, ones2)


def _wkv_kernel(r_ref, k_ref, v_ref, kk_ref, lw_ref, a_ref, ka_ref, y_ref, s_ref):
    d = pl.program_id(1)
    C = WKV_CHUNK
    C2 = 2 * C
    P = HEAD_PAIRS
    sign = 1 - 2 * d
    fwd = d == 0

    @pl.when(pl.program_id(2) == 0)
    def _():
        s_ref[...] = jnp.zeros_like(s_ref)

    def chunk_tri(n):
        trow = lax.broadcasted_iota(jnp.int32, (n * C, n * C), 0)
        tcol = lax.broadcasted_iota(jnp.int32, (n * C, n * C), 1)
        return jnp.logical_and((trow // C) == (tcol // C), (trow - tcol) * sign >= 0).astype(BF16)

    lane = lax.broadcasted_iota(jnp.int32, (C, LANES), 1)
    head0 = lane < RWKV_HEAD
    zero = jnp.zeros((), F32)

    def expand(x):
        x = x.astype(BF16)
        zb = jnp.zeros((), BF16)
        tiles = [x[c:c + C, j * LANES:(j + 1) * LANES] for c in range(0, x.shape[0], C) for j in range(P)]
        return jnp.stack([jnp.concatenate([jnp.where(head0, t, zb), jnp.where(head0, zb, t)], axis=0)
                          for t in tiles])

    def bmm(x, y):
        return jnp.einsum('hab,hbc->hac', x.astype(BF16), y.astype(BF16), preferred_element_type=F32)

    def bmm_nt(x, y):
        return jnp.einsum('han,hbn->hab', x.astype(BF16), y.astype(BF16), preferred_element_type=F32)

    def bmm_tn(x, y):
        return jnp.einsum('hca,hcb->hab', x.astype(BF16), y.astype(BF16), preferred_element_type=F32)

    row = lax.broadcasted_iota(jnp.int32, (C2, C2), 0)
    col = lax.broadcasted_iota(jnp.int32, (C2, C2), 1)
    same = (row // C) == (col // C)
    lag = (row - col) * sign
    incl = jnp.logical_and(same, lag >= 0)[None]
    strict = jnp.logical_and(same, lag > 0)[None]

    def blk(m):
        return (row // m) == (col // m)

    def half_rows(t, m, second):
        return jnp.concatenate([t[:, g + second * m:g + (second + 1) * m] for g in range(0, C2, 2 * m)], axis=1)

    def state_free(chunk_rows):
        def load(ref, *lead):
            return jnp.concatenate([ref[(*lead, rows, slice(None))] for rows in chunk_rows], axis=0)

        r = load(r_ref, 0)
        v = load(v_ref, 0)
        kk = load(kk_ref, 0)
        lw = load(lw_ref, 0, 0)
        a = load(a_ref, 0, 0)
        k = load(k_ref, 0) * (1.0 + (a - 1.0) * ka_ref[...])
        n = len(chunk_rows)

        cum = _hilo_dot_left(chunk_tri(n), lw)
        tot = jnp.concatenate(
            [jnp.broadcast_to(jnp.where(fwd, cum[(c + 1) * C - 1:(c + 1) * C], cum[c * C:c * C + 1]), (C, cum.shape[1]))
             for c in range(n)], axis=0)
        e_cum = jnp.exp(cum)
        e_neg = jnp.exp(-cum)
        e_rem = jnp.exp(tot - cum)
        b = kk * a
        at = -kk * jnp.exp(cum - lw)
        dec = jnp.exp(tot)
        AT, RT, BT, KT, BD, KD, V = (expand(t) for t in (at, r * e_cum, b * e_neg, k * e_neg, b * e_rem,
                                                         k * e_rem, v))
        AR = jnp.concatenate([AT, RT], axis=1)
        G = bmm_nt(AR, jnp.concatenate([BT, KT], axis=1))
        a_ab, a_ak = G[:, :C2, :C2], G[:, :C2, C2:]
        a_rb, a_rk = G[:, C2:, :C2], G[:, C2:, C2:]

        L = jnp.where(strict, a_ab, zero)
        Lb = jnp.where(blk(8)[None], L, zero)
        L2 = bmm(Lb, Lb)
        L4 = bmm(L2, L2)
        S1 = Lb + L2 + bmm(Lb, L2)
        Nm = S1 + L4 + bmm(L4, S1)
        for m in (8, 16, 32):
            Lo = jnp.where(jnp.logical_and(blk(2 * m), jnp.logical_not(blk(m)))[None], L, zero)
            n_first, n_second = half_rows(Nm, m, 0), half_rows(Nm, m, 1)
            n_late = jnp.where(fwd, n_second, n_first)
            p_late = jnp.where(fwd, half_rows(Lo, m, 1), half_rows(Lo, m, 0)) + bmm(n_late, Lo)
            n_late = n_late + p_late + bmm(p_late, Nm)
            n_first, n_second = jnp.where(fwd, n_first, n_late), jnp.where(fwd, n_late, n_second)
            Nm = jnp.concatenate([piece for g in range(C2 // (2 * m))
                                  for piece in (n_first[:, g * m:(g + 1) * m], n_second[:, g * m:(g + 1) * m])],
                                 axis=1)
        rhs_v = bmm(jnp.where(strict, a_ak, zero), V)
        a_y = jnp.concatenate([jnp.where(incl, a_rb, zero), jnp.where(incl, a_rk, zero)], axis=2).astype(BF16)
        dec3 = jnp.stack([dec[c * C:c * C + 1, j * LANES:(j + 1) * LANES]
                          for c in range(n) for j in range(P)])
        whole = (AR, rhs_v, Nm.astype(BF16), V, a_y, jnp.concatenate([BD, KD], axis=1), dec3)
        return [tuple(t[c * P:(c + 1) * P] for t in whole) for c in range(n)]

    def state_step(part, s0):
        AR, rhs_v, Nm, V, a_y, BKD, dec3 = part
        ars0 = bmm_nt(AR, s0)
        rhs = ars0[:, :C2] + rhs_v
        u = rhs + bmm(Nm, rhs)
        uv = jnp.concatenate([u.astype(BF16), V], axis=1)
        y2 = ars0[:, C2:] + bmm(a_y, uv)
        y = y2[:, :C] + y2[:, C:]
        return jnp.concatenate([y[j] for j in range(P)], axis=-1), s0 * dec3 + bmm_tn(uv, BKD)

    chunk_rows = [pl.ds(pl.multiple_of(jnp.where(fwd, j, WKV_SUB - 1 - j) * C, C), C) for j in range(WKV_SUB)]
    half = WKV_SUB // 2
    s = s_ref[...]
    for group in (chunk_rows[:half], chunk_rows[half:]):
        for rows, part in zip(group, state_free(group)):
            y, s = state_step(part, s)
            y_ref[0, 0, rows, :] = y
    s_ref[...] = s


def _hilo_dot_left(w, x):
    hi = x.astype(BF16)
    lo = (x - hi.astype(F32)).astype(BF16)
    return jnp.dot(w, hi, preferred_element_type=F32) + jnp.dot(w, lo, preferred_element_type=F32)


def _wkv(r, k, v, kk, lw, a, k_a):
    B, S, Cd = r.shape
    R = WKV_SUB * WKV_CHUNK
    n_blocks = S // R
    lat_blocks = SEQ // R

    def bidx(d, i):
        return jnp.where(d == 0, (i + lat_blocks) % n_blocks, n_blocks - 1 - i)

    shared = pl.BlockSpec((1, R, Cd), lambda b, d, i: (b, bidx(d, i), 0))
    perdir = pl.BlockSpec((1, 1, R, Cd), lambda b, d, i: (b, d, bidx(d, i), 0))
    return pl.pallas_call(
        _wkv_kernel, grid=(B, N_DIR, n_blocks),
        in_specs=[shared, shared, shared, shared, perdir, perdir, pl.BlockSpec((1, Cd), lambda b, d, i: (0, 0))],
        out_specs=perdir,
        out_shape=jax.ShapeDtypeStruct((B, N_DIR, S, Cd), F32),
        scratch_shapes=[pltpu.VMEM((HEAD_PAIRS, LANES, LANES), F32)],
        compiler_params=_cparams(("parallel", "parallel", "arbitrary")), name="wkv7",
    )(r, k, v, kk, lw, a, k_a[None, :])


def _rms(x, g):
    return x * lax.rsqrt(jnp.mean(x * x, axis=-1, keepdims=True) + EPS) * g


def _qproj_kernel(p_ref, g_ref, w_ref, cos_ref, sin_ref, q_ref):
    cq = _rms(p_ref[0], g_ref[...]).astype(BF16)
    cs, sn = cos_ref[...], sin_ref[...]
    for h in range(MLA_HEADS):
        y = jnp.dot(cq, w_ref[h], preferred_element_type=F32)
        rope = y[:, QK_NOPE:QK_NOPE + QK_ROPE] * cs + y[:, QK_NOPE + QK_ROPE:] * sn
        q = jnp.concatenate([y[:, :QK_NOPE], rope], axis=-1) * (ATTN_SCALE * LOG2_E)
        q_ref[0, h] = q.astype(q_ref.dtype)


def _qproj(p, gain, wq, cos, sin, tm=1024):
    B = p.shape[0]
    return pl.pallas_call(
        _qproj_kernel, grid=(B, SEQ // tm),
        in_specs=[pl.BlockSpec((1, tm, Q_LORA), lambda b, i: (b, i, C_CQ // Q_LORA)),
                  pl.BlockSpec((1, Q_LORA), lambda b, i: (0, 0)),
                  pl.BlockSpec((MLA_HEADS, Q_LORA, 2 * LANES), lambda b, i: (0, 0, 0)),
                  pl.BlockSpec((tm, QK_ROPE), lambda b, i: (i, 0)),
                  pl.BlockSpec((tm, QK_ROPE), lambda b, i: (i, 0))],
        out_specs=pl.BlockSpec((1, MLA_HEADS, tm, QK_NOPE + QK_ROPE), lambda b, i: (b, 0, i, 0)),
        out_shape=jax.ShapeDtypeStruct((B, MLA_HEADS, SEQ, QK_NOPE + QK_ROPE), BF16),
        compiler_params=_cparams(("parallel", "parallel")), name="mla_q",
    )(p, gain[None, :], wq, cos, sin)


def _kvproj_kernel(p_ref, pe_ref, g_ref, w_ref, cos_ref, sin_ref, k_ref, v_ref):
    ckv = _rms(p_ref[0], g_ref[...]).astype(BF16)
    pe = pe_ref[0]
    kpe = pe[:, :QK_ROPE] * cos_ref[...] + pe[:, QK_ROPE:] * sin_ref[...]
    ones = jnp.ones((ckv.shape[0], V_HEAD), F32)
    for h in range(MLA_HEADS):
        y = jnp.dot(ckv, w_ref[h], preferred_element_type=F32)
        k_ref[0, h] = jnp.concatenate([y[:, :QK_NOPE], kpe], axis=-1).astype(k_ref.dtype)
        v_ref[0, h] = jnp.concatenate([y[:, QK_NOPE:], ones], axis=-1).astype(v_ref.dtype)


def _kvproj(p, gain, wkv, cos, sin, tm=256):
    B = p.shape[0]
    return pl.pallas_call(
        _kvproj_kernel, grid=(B, S_ALL // tm),
        in_specs=[pl.BlockSpec((1, tm, KV_LORA), lambda b, i: (b, i, C_CKV // KV_LORA)),
                  pl.BlockSpec((1, tm, 2 * QK_ROPE), lambda b, i: (b, i, C_KPE // (2 * QK_ROPE))),
                  pl.BlockSpec((1, KV_LORA), lambda b, i: (0, 0)),
                  pl.BlockSpec((MLA_HEADS, KV_LORA, 2 * LANES), lambda b, i: (0, 0, 0)),
                  pl.BlockSpec((tm, QK_ROPE), lambda b, i: (i, 0)),
                  pl.BlockSpec((tm, QK_ROPE), lambda b, i: (i, 0))],
        out_specs=[pl.BlockSpec((1, MLA_HEADS, tm, QK_NOPE + QK_ROPE), lambda b, i: (b, 0, i, 0)),
                   pl.BlockSpec((1, MLA_HEADS, tm, 2 * V_HEAD), lambda b, i: (b, 0, i, 0))],
        out_shape=[jax.ShapeDtypeStruct((B, MLA_HEADS, S_ALL, QK_NOPE + QK_ROPE), BF16),
                   jax.ShapeDtypeStruct((B, MLA_HEADS, S_ALL, 2 * V_HEAD), BF16)],
        compiler_params=_cparams(("parallel", "parallel")), name="mla_kv",
    )(p, p, gain[None, :], wkv, cos, sin)


def _attn_kernel(*refs, tk, n_cast):
    q_ref, k_ref, v_ref = refs[:3]
    cast_in = refs[3:3 + n_cast]
    o_ref = refs[3 + n_cast]
    cast_out = refs[4 + n_cast:4 + 2 * n_cast]
    sa, sb, pa, pb, acc, m_ref, al_ref = refs[4 + 2 * n_cast:]
    for w_ref, wb_ref in zip(cast_in, cast_out):
        wb_ref[...] = w_ref[...].astype(wb_ref.dtype)

    q = q_ref[0, 0]
    n_keys = k_ref.shape[2]
    blocks = [(off, min(tk, n_keys - off)) for off in range(0, n_keys, tk)]
    n_kv = len(blocks)

    def put_scores(s_ref, j):
        off, w = blocks[j]
        kj = k_ref[0, 0, off:off + w, :]
        s_ref[:, :w] = lax.dot_general(q, kj, (((1,), (1,)), ((), ())), preferred_element_type=F32)

    def accumulate(p_ref, j):
        off, w = blocks[j]
        pv = jnp.dot(p_ref[:, :w], v_ref[0, 0, off:off + w, :], preferred_element_type=F32)
        al = al_ref[...]
        for c in range(0, 2 * V_HEAD, LANES):
            acc[:, c:c + LANES] = al * acc[:, c:c + LANES] + pv[:, c:c + LANES]

    def softmax(s_ref, p_ref, j):
        w = blocks[j][1]
        s = s_ref[:, :w]
        m_old = m_ref[...]
        m_new = jnp.maximum(m_old, jnp.max(s, axis=-1, keepdims=True))
        al_ref[...] = jnp.exp2(m_old - m_new)
        m_ref[...] = m_new
        for c in range(0, w, LANES):
            p_ref[:, c:c + LANES] = jnp.exp2(s[:, c:c + LANES] - m_new).astype(p_ref.dtype)

    s_bufs, p_bufs = (sa, sb), (pa, pb)
    put_scores(sa, 0)
    acc[...] = jnp.zeros_like(acc)
    m_ref[...] = jnp.full_like(m_ref, -jnp.inf)
    for j in range(n_kv):
        if j + 1 < n_kv:
            put_scores(s_bufs[(j + 1) % 2], j + 1)
        if j >= 1:
            accumulate(p_bufs[(j - 1) % 2], j - 1)
        softmax(s_bufs[j % 2], p_bufs[j % 2], j)
    accumulate(p_bufs[(n_kv - 1) % 2], n_kv - 1)
    o_ref[0] = (acc[:, :V_HEAD] / acc[:, V_HEAD:]).astype(o_ref.dtype)


BF16_SUBLANES = 16


def _attention(q, k, v, to_cast, tq=1024, tk=1024):
    B, H, T, Dk = q.shape
    S = k.shape[2]
    nt = T // tq
    n_steps = B * H * nt

    def cast_spec(w):
        rows, cols = w.shape
        rb = next(r for r in range(BF16_SUBLANES, rows + 1, BF16_SUBLANES)
                  if rows % r == 0 and rows // r <= n_steps)
        n_blk = rows // rb
        return pl.BlockSpec((rb, cols), lambda b, h, i: (((b * H + h) * nt + i) * n_blk // n_steps, 0))

    cast_specs = [cast_spec(w) for w in to_cast]
    outs = pl.pallas_call(
        functools.partial(_attn_kernel, tk=tk, n_cast=len(to_cast)), grid=(B, H, nt),
        in_specs=[pl.BlockSpec((1, 1, tq, Dk), lambda b, h, i: (b, h, i, 0)),
                  pl.BlockSpec((1, 1, S, Dk), lambda b, h, i: (b, h, 0, 0)),
                  pl.BlockSpec((1, 1, S, 2 * V_HEAD), lambda b, h, i: (b, h, 0, 0))] + cast_specs,
        out_specs=[pl.BlockSpec((1, tq, V_HEAD), lambda b, h, i: (b, i, h))] + cast_specs,
        out_shape=[jax.ShapeDtypeStruct((B, T, H * V_HEAD), BF16)]
                  + [jax.ShapeDtypeStruct(w.shape, BF16) for w in to_cast],
        scratch_shapes=[pltpu.VMEM((tq, tk), F32), pltpu.VMEM((tq, tk), F32),
                        pltpu.VMEM((tq, tk), BF16), pltpu.VMEM((tq, tk), BF16),
                        pltpu.VMEM((tq, 2 * V_HEAD), F32), pltpu.VMEM((tq, LANES), F32), pltpu.VMEM((tq, LANES), F32)],
        compiler_params=_cparams(("arbitrary", "arbitrary", "arbitrary")), name="mla_attn",
    )(q, k, v, *to_cast)
    return outs[0], outs[1:]


def _gates_kernel(h_ref, wt_ref, b_ref, o_ref):
    o_ref[0] = jax.nn.sigmoid(_dot_nt(h_ref[0], wt_ref[...]) + b_ref[...]).astype(o_ref.dtype)


def _gates(h, wt_gate, gate_b, tm=1024, tn=1024):
    B, _, D = h.shape
    N = wt_gate.shape[0]
    return pl.pallas_call(
        _gates_kernel, grid=(B, SEQ // tm, N // tn),
        in_specs=[pl.BlockSpec((1, tm, D), lambda b, i, j: (b, i, 0)),
                  pl.BlockSpec((tn, D), lambda b, i, j: (j, 0)),
                  pl.BlockSpec((1, tn), lambda b, i, j: (0, j))],
        out_specs=pl.BlockSpec((1, tm, tn), lambda b, i, j: (b, i, j)),
        out_shape=jax.ShapeDtypeStruct((B, SEQ, N), BF16),
        compiler_params=_cparams(("parallel", "parallel", "parallel")), name="gates",
    )(h, wt_gate, gate_b[None, :])


MERGE_TN = 512


def _merge_kernel(y0_ref, y1_ref, g_ref, bg_ref, lnw_ref, lnb_ref, avg_ref, o_ref, wr_ref, wm_ref, gate_ref, out_ref):
    avg = avg_ref[...]
    y = y0_ref[0, 0] + y1_ref[0, 0]
    yc = y - _head_sum(y, avg)
    yn = yc * lax.rsqrt(_head_sum(yc * yc, avg) + LNX_EPS) * lnw_ref[...] + lnb_ref[...]
    yr = (yn * g_ref[0].astype(F32) + bg_ref[0].astype(F32)).astype(BF16)
    o = o_ref[0]
    D = D_MODEL
    for n in range(0, D, MERGE_TN):
        gr = gate_ref[0, :, n:n + MERGE_TN].astype(F32)
        gm = gate_ref[0, :, D + n:D + n + MERGE_TN].astype(F32)
        out_ref[0, :, n:n + MERGE_TN] = (
            gr * jnp.dot(yr, wr_ref[:, n:n + MERGE_TN], preferred_element_type=F32)
            + gm * jnp.dot(o, wm_ref[:, n:n + MERGE_TN], preferred_element_type=F32)).astype(out_ref.dtype)


def _merge(y_dir, g, bg, lnx_w, lnx_b, avg2, o, w_rp, w_mp, gates, tm=512):
    B = o.shape[0]
    D = D_MODEL
    C = RWKV_DIM
    tok = pl.BlockSpec((1, tm, C), lambda b, i: (b, i, 0))
    row = lambda n: pl.BlockSpec((1, n), lambda b, i: (0, 0))
    return pl.pallas_call(
        _merge_kernel, grid=(B, SEQ // tm),
        in_specs=[pl.BlockSpec((1, 1, tm, C), lambda b, i: (b, 0, i, 0)),
                  pl.BlockSpec((1, 1, tm, C), lambda b, i: (b, 1, i, 0)),
                  tok, tok, row(C), row(C),
                  pl.BlockSpec((LANES, LANES), lambda b, i: (0, 0)),
                  pl.BlockSpec((1, tm, D), lambda b, i: (b, i, 0)),
                  pl.BlockSpec((C, D), lambda b, i: (0, 0), pipeline_mode=pl.Buffered(1)),
                  pl.BlockSpec((D, D), lambda b, i: (0, 0), pipeline_mode=pl.Buffered(1)),
                  pl.BlockSpec((1, tm, 2 * D), lambda b, i: (b, i, 0))],
        out_specs=pl.BlockSpec((1, tm, D), lambda b, i: (b, i, 0)),
        out_shape=jax.ShapeDtypeStruct((B, SEQ, D), BF16),
        compiler_params=_cparams(("parallel", "parallel")), name="merge",
    )(y_dir, y_dir, g, bg, lnx_w[None, :], lnx_b[None, :], avg2, o, w_rp, w_mp, gates)


def _outproj_kernel(m_ref, w_ref, x_ref, gpost_ref, gate_ref, gpre_ref, mod_ref, x1_ref, h_ref):
    mod = mod_ref[0]
    tm = m_ref.shape[1]
    for rows in (slice(0, tm // 2), slice(tm // 2, tm)):
        out = jnp.dot(m_ref[0, rows], w_ref[...], preferred_element_type=F32)
        x1 = x_ref[0, rows] + gate_ref[0] * _rms(out, gpost_ref[...])
        x1_ref[0, rows] = x1
        h_ref[0, rows] = (_rms(x1, gpre_ref[...]) * (1.0 + mod[1:2]) + mod[0:1]).astype(h_ref.dtype)


def _outproj(merged, w_out, x, g_post, gate, g_pre, mod2, tm=512):
    B = x.shape[0]
    D = D_MODEL
    return pl.pallas_call(
        _outproj_kernel, grid=(B, SEQ // tm),
        in_specs=[pl.BlockSpec((1, tm, D), lambda b, i: (b, i, 0)),
                  pl.BlockSpec((D, D), lambda b, i: (0, 0)),
                  pl.BlockSpec((1, tm, D), lambda b, i: (b, i, 0)),
                  pl.BlockSpec((1, D), lambda b, i: (0, 0)),
                  pl.BlockSpec((1, 1, D), lambda b, i: (b, 0, 0)),
                  pl.BlockSpec((1, D), lambda b, i: (0, 0)),
                  pl.BlockSpec((1, 2, D), lambda b, i: (b, 0, 0))],
        out_specs=[pl.BlockSpec((1, tm, D), lambda b, i: (b, i, 0)),
                   pl.BlockSpec((1, tm, D), lambda b, i: (b, i, 0))],
        out_shape=[jax.ShapeDtypeStruct((B, SEQ, D), F32),
                   jax.ShapeDtypeStruct((B, SEQ, D), BF16)],
        compiler_params=_cparams(("parallel", "parallel")), name="outproj",
    )(merged, w_out, x, g_post[None, :], gate[:, None, :], g_pre[None, :], mod2)


HALO = 16


def _ffn_kernel(h_ref, hp_ref, hn_ref, wg_ref, wv_ref, wd_ref, cw_ref, cb_ref, x_ref, gate_ref, gpost_ref,
                o_ref, acc_ref):
    i = pl.program_id(1)
    f = pl.program_id(2)
    tm = h_ref.shape[1]

    @pl.when(f == 0)
    def _():
        acc_ref[...] = jnp.zeros_like(acc_ref)

    h = h_ref[0]
    wg = wg_ref[...]
    g_ext = jnp.dot(jnp.concatenate([hp_ref[0], h, hn_ref[0]], axis=0), wg, preferred_element_type=F32)
    g = g_ext[HALO:HALO + tm]
    g_prev = g_ext[HALO - 1:HALO]
    g_next = g_ext[HALO + tm:HALO + tm + 1]
    g_prev = jnp.where(i == 0, 0.0, g_prev)
    g_next = jnp.where(i == pl.num_programs(1) - 1, 0.0, g_next)
    rows = lax.broadcasted_iota(jnp.int32, g.shape, 0)
    up = jnp.where(rows == 0, g_prev, pltpu.roll(g, 1, 0))
    dn = jnp.where(rows == tm - 1, g_next, pltpu.roll(g, tm - 1, 0))
    cw = cw_ref[...]
    u = cb_ref[...] + up * cw[0:1] + g * cw[1:2] + dn * cw[2:3]
    val = jnp.dot(h, wv_ref[...], preferred_element_type=F32)
    act = jax.nn.gelu(u, approximate=True) * val
    acc_ref[...] += jnp.dot(act.astype(BF16), wd_ref[...], preferred_element_type=F32)

    @pl.when(f == pl.num_programs(2) - 1)
    def _():
        o_ref[0] = x_ref[0] + gate_ref[0] * _rms(acc_ref[...], gpost_ref[...])


def _ffn(h, wg, wv, wd, cw, cb, x1, gate, g_post, tm=512, tf=512):
    B = h.shape[0]
    D = D_MODEL
    hb = tm // HALO
    last = SEQ // HALO - 1
    return pl.pallas_call(
        _ffn_kernel, grid=(B, SEQ // tm, D_FF // tf),
        in_specs=[pl.BlockSpec((1, tm, D), lambda b, i, f: (b, i, 0)),
                  pl.BlockSpec((1, HALO, D), lambda b, i, f: (b, jnp.maximum(i * hb - 1, 0), 0)),
                  pl.BlockSpec((1, HALO, D), lambda b, i, f: (b, jnp.minimum((i + 1) * hb, last), 0)),
                  pl.BlockSpec((D, tf), lambda b, i, f: (0, f)),
                  pl.BlockSpec((D, tf), lambda b, i, f: (0, f)),
                  pl.BlockSpec((tf, D), lambda b, i, f: (f, 0)),
                  pl.BlockSpec((3, tf), lambda b, i, f: (0, f)),
                  pl.BlockSpec((1, tf), lambda b, i, f: (0, f)),
                  pl.BlockSpec((1, tm, D), lambda b, i, f: (b, i, 0)),
                  pl.BlockSpec((1, 1, D), lambda b, i, f: (b, 0, 0)),
                  pl.BlockSpec((1, D), lambda b, i, f: (0, 0))],
        out_specs=pl.BlockSpec((1, tm, D), lambda b, i, f: (b, i, 0)),
        out_shape=jax.ShapeDtypeStruct((B, SEQ, D), F32),
        scratch_shapes=[pltpu.VMEM((tm, D), F32)],
        compiler_params=_cparams(("parallel", "parallel", "arbitrary")), name="convffn",
    )(h, h, h, wg, wv, wd, cw, cb[None, :], x1, gate[:, None, :], g_post[None, :])


def _rope_partner(w):
    q = QK_ROPE // 4
    return jnp.concatenate([-w[..., q:2 * q], w[..., :q], -w[..., 3 * q:], w[..., 2 * q:3 * q]], axis=-1)


def _pack_w_in_kernel(wt_ref, perm_ref, main_ref, gate_ref):
    latents = RWKV_IN + Q_LORA + KV_LORA
    main_ref[:RWKV_IN] = wt_ref[:RWKV_IN].astype(BF16)
    kpe = wt_ref[latents:latents + QK_ROPE].astype(BF16)
    partner = jnp.dot(perm_ref[...], kpe, preferred_element_type=F32).astype(BF16)
    main_ref[C_KPE:C_KPE + QK_ROPE] = kpe
    main_ref[C_KPE + QK_ROPE:C_KPE + 2 * QK_ROPE] = partner
    main_ref[C_KPE + 2 * QK_ROPE:C_CQ] = jnp.zeros((C_CQ - C_KPE - 2 * QK_ROPE, kpe.shape[1]), BF16)
    main_ref[C_CQ:] = wt_ref[RWKV_IN:latents].astype(BF16)
    gate_ref[...] = wt_ref[RWKV_IN + MLA_IN:].astype(BF16)


def _pack_w_in(wt, tk=256):
    n_in, K = wt.shape
    perm_t = _rope_partner(jnp.eye(QK_ROPE, dtype=F32)).T.astype(BF16)
    n_gate = n_in - RWKV_IN - MLA_IN
    return pl.pallas_call(
        _pack_w_in_kernel, grid=(K // tk,),
        in_specs=[pl.BlockSpec((n_in, tk), lambda i: (0, i)),
                  pl.BlockSpec((QK_ROPE, QK_ROPE), lambda i: (0, 0))],
        out_specs=[pl.BlockSpec((P_COLS, tk), lambda i: (0, i)),
                   pl.BlockSpec((n_gate, tk), lambda i: (0, i))],
        out_shape=[jax.ShapeDtypeStruct((P_COLS, K), BF16), jax.ShapeDtypeStruct((n_gate, K), BF16)],
        compiler_params=_cparams(("parallel",)), name="pack_w_in",
    )(wt, perm_t)


def _lora_window_weight(w2, first_row):
    n_dir, R, C = w2.shape
    out = jnp.zeros((LORA_WIN, n_dir, C), BF16)
    for d in range(n_dir):
        out = out.at[first_row + d * R:first_row + (d + 1) * R, d].set(w2[d].astype(BF16))
    return out.reshape(LORA_WIN, n_dir * C)


def _rope_tables():
    rows = SEQ // GRID_W
    half = QK_ROPE // 2
    freqs = ROPE_THETA ** (-jnp.arange(0, half, 2, dtype=F32) / half)
    ar = jnp.arange(rows, dtype=F32)[:, None] * freqs
    ac = jnp.arange(GRID_W, dtype=F32)[:, None] * freqs
    by_row = lambda t: jnp.repeat(t, GRID_W, axis=0)
    by_col = lambda t: jnp.tile(t, (rows, 1))
    cos = jnp.concatenate([by_row(jnp.cos(ar))] * 2 + [by_col(jnp.cos(ac))] * 2, axis=-1)
    sin = jnp.concatenate([by_row(jnp.sin(ar))] * 2 + [by_col(jnp.sin(ac))] * 2, axis=-1)
    cos_all = jnp.concatenate([cos, jnp.ones((CTX_LEN, QK_ROPE), F32)], axis=0)
    sin_all = jnp.concatenate([sin, jnp.zeros((CTX_LEN, QK_ROPE), F32)], axis=0)
    return cos_all, sin_all


def kernel(x, c, ctx, c_ctx, w_ada, b_ada, norm_mix_pre, norm_mix_post, norm_ffn_pre, norm_ffn_post, w_in, rwkv_mu, rwkv_w0, rwkv_w2, rwkv_a0, rwkv_a2, rwkv_k_k, rwkv_k_a, rwkv_r_k, rwkv_lnx_w, rwkv_lnx_b, rwkv_g2, w_rwkv_proj, mla_q_norm, mla_kv_norm, mla_w_uq, mla_w_ukv, w_mla_proj, gate_b, w_out, ffn_w_gate, ffn_w_val, ffn_conv_w, ffn_conv_b, ffn_w_down):
    B = x.shape[0]
    D = D_MODEL
    C = RWKV_DIM
    l = 0

    s = jnp.concatenate([jax.nn.silu(c), jax.nn.silu(c_ctx)[None, :]], axis=0)
    mods = _ada(s, w_ada[l], b_ada[l])
    lat = mods[:B].reshape(B, 6, D)
    cm = mods[B].reshape(6, D)
    sh1, sc1, g1, sh2, sc2, g2 = [lat[:, j] for j in range(6)]
    mods1 = jnp.stack([jnp.stack([sh1, sc1], axis=1),
                       jnp.broadcast_to(jnp.stack([cm[0], cm[1]])[None], (B, 2, D))], axis=1)
    mods2 = jnp.stack([sh2, sc2], axis=1)

    h = _norm_mod_ring(x, ctx, norm_mix_pre[l], mods1)
    wt_main, wt_gate = _pack_w_in(w_in[l].T)
    p = _matmul_nt(h.reshape(B * S_ALL, D), wt_main, tm=2176, tn=1024, name="w_in").reshape(B, S_ALL, P_COLS)
    gates = _gates(h, wt_gate, gate_b[l])

    lane_head = jnp.arange(LANES) // RWKV_HEAD
    ones2 = (lane_head[:, None] == lane_head[None, :]).astype(BF16)
    avg2 = ones2 * (1.0 / RWKV_HEAD)
    r, k, v, kk, lw, a, g, bg = _prep(p, rwkv_mu[l], rwkv_w0[l], _lora_window_weight(rwkv_w2[l], C_WL - C_WL_WIN),
                                      rwkv_a0[l], _lora_window_weight(rwkv_a2[l], C_AL - C_AL_WIN), rwkv_k_k[l],
                                      rwkv_k_a[l], rwkv_r_k[l].reshape(C), rwkv_g2[l].astype(BF16), ones2)
    y_dir = _wkv(r, k, v, kk, lw, a, rwkv_k_a[l])

    cos, sin = _rope_tables()
    uq = mla_w_uq[l].reshape(Q_LORA, MLA_HEADS, QK_NOPE + QK_ROPE)
    wq = jnp.concatenate([uq, _rope_partner(uq[..., QK_NOPE:])], axis=-1).transpose(1, 0, 2).astype(BF16)
    wkv = mla_w_ukv[l].reshape(KV_LORA, MLA_HEADS, QK_NOPE + V_HEAD).transpose(1, 0, 2).astype(BF16)
    q = _qproj(p, mla_q_norm[l], wq, cos[:SEQ], sin[:SEQ])
    k_all, v_all = _kvproj(p, mla_kv_norm[l], wkv, cos, sin)
    o, (wg_b, wv_b, wd_b, wrp_b, wmp_b, wo_b) = _attention(
        q, k_all, v_all, (ffn_w_gate[l], ffn_w_val[l], ffn_w_down[l], w_rwkv_proj[l], w_mla_proj[l], w_out[l]))

    merged = _merge(y_dir, g, bg, rwkv_lnx_w[l], rwkv_lnx_b[l], avg2, o, wrp_b, wmp_b, gates)
    x1, h2 = _outproj(merged, wo_b, x, norm_mix_post[l], g1, norm_ffn_pre[l], mods2)
    return _ffn(h2, wg_b, wv_b, wd_b, ffn_conv_w[l], ffn_conv_b[l], x1, g2, norm_ffn_post[l])
```
